```python
import jax, jax.numpy as jnp
from jax import lax
import numpy as np

D_MODEL = 4096
BATCH = 8
SEQ = 4096
DEPTH = 1

N_Q_HEADS = 64
N_KV_HEADS = 8
HEAD_DIM = 64
Q_PER_KV = N_Q_HEADS // N_KV_HEADS
ATTN_WIDTH = N_Q_HEADS * HEAD_DIM
KV_WIDTH = N_KV_HEADS * HEAD_DIM
WINDOW = 128
BLOCK = 128
ROPE_THETA = 500000.0
ROPE_DIM = HEAD_DIM // 4

GMLP_WIDTH = D_MODEL
GMLP_GROUPS = 8
GMLP_GROUP_DIM = GMLP_WIDTH // GMLP_GROUPS
GMLP_CHUNK = 128

NORM_EPS = 1e-5
LN_EPS = 1e-5

PROJ_SIZES = (ATTN_WIDTH, KV_WIDTH, KV_WIDTH, ATTN_WIDTH,
              GMLP_WIDTH, GMLP_WIDTH, GMLP_WIDTH, D_MODEL, D_MODEL)
PROJ_WIDTH = int(sum(PROJ_SIZES))
SPLIT_POINTS = tuple(int(s) for s in np.cumsum(PROJ_SIZES)[:-1])

kernel_name = "hybrid_swa_sink_gmlp_gated_merge"


def rms_norm(x, g):
    x32 = x.astype(jnp.float32)
    y = x32 * lax.rsqrt(jnp.mean(x32 * x32, axis=-1, keepdims=True) + NORM_EPS)
    return (y * g.astype(jnp.float32)).astype(x.dtype)


def layer_norm(x, g, b):
    x32 = x.astype(jnp.float32)
    mu = jnp.mean(x32, axis=-1, keepdims=True)
    xc = x32 - mu
    y = xc * lax.rsqrt(jnp.mean(xc * xc, axis=-1, keepdims=True) + LN_EPS)
    return (y * g.astype(jnp.float32) + b.astype(jnp.float32)).astype(x.dtype)


def rope_tables(positions, dtype):
    half = ROPE_DIM // 2
    inv_freq = ROPE_THETA ** (-jnp.arange(half, dtype=jnp.float32) * 2.0 / ROPE_DIM)
    ang = positions.astype(jnp.float32)[..., None] * inv_freq
    return jnp.cos(ang)[:, :, None, :].astype(dtype), jnp.sin(ang)[:, :, None, :].astype(dtype)


def partial_rope(t, cos, sin):
    half = ROPE_DIM // 2
    t1 = t[..., :half]
    t2 = t[..., half:ROPE_DIM]
    return jnp.concatenate([t1 * cos - t2 * sin, t2 * cos + t1 * sin, t[..., ROPE_DIM:]], axis=-1)


def sliding_window_sink_attention(q, k, v, sink):
    B, S = q.shape[0], q.shape[1]
    nb = S // BLOCK
    qb = q.reshape(B, nb, BLOCK, N_KV_HEADS, Q_PER_KV, HEAD_DIM)
    kb = k.reshape(B, nb, BLOCK, N_KV_HEADS, HEAD_DIM)
    vb = v.reshape(B, nb, BLOCK, N_KV_HEADS, HEAD_DIM)

    def with_prev(t):
        prev = jnp.concatenate([jnp.zeros_like(t[:, :1]), t[:, :-1]], axis=1)
        return jnp.concatenate([prev, t], axis=2)

    kband, vband = with_prev(kb), with_prev(vb)
    sink_g = sink.astype(jnp.float32).reshape(1, N_KV_HEADS, Q_PER_KV, 1, 1)
    qi = jnp.arange(BLOCK)[:, None]
    si = jnp.arange(2 * BLOCK)[None, :]
    band = (si <= qi + BLOCK) & (si > qi + BLOCK - WINDOW)
    scale = HEAD_DIM ** -0.5

    def one_block(args):
        idx, qx, kx, vx = args
        s = jnp.einsum('bqhgd,bshd->bhgqs', qx, kx,
                       preferred_element_type=jnp.float32) * scale
        mask = band & ((idx > 0) | (si >= BLOCK))
        s = jnp.where(mask, s, -jnp.inf)
        m = jnp.maximum(jnp.max(s, axis=-1, keepdims=True), sink_g)
        p = jnp.exp(s - m)
        denom = jnp.sum(p, axis=-1, keepdims=True) + jnp.exp(sink_g - m)
        return jnp.einsum('bhgqs,bshd->bqhgd', (p / denom).astype(vx.dtype), vx)

    xs = (jnp.arange(nb), jnp.moveaxis(qb, 1, 0), jnp.moveaxis(kband, 1, 0), jnp.moveaxis(vband, 1, 0))
    out = lax.map(one_block, xs)
    return jnp.moveaxis(out, 0, 1).reshape(B, S, ATTN_WIDTH)


def chunked_spatial_gating(u, v, w_s, b_s, ln_g, ln_b):
    B, S, W = v.shape
    nc = S // GMLP_CHUNK
    vn = layer_norm(v, ln_g, ln_b)
    vc = vn.reshape(B, nc, GMLP_CHUNK, GMLP_GROUPS, GMLP_GROUP_DIM)
    causal = jnp.tril(jnp.ones((GMLP_CHUNK, GMLP_CHUNK), dtype=bool))
    w = jnp.where(causal[None], w_s, jnp.zeros_like(w_s)).astype(v.dtype)
    mixed = jnp.einsum('gts,bnsgc->bntgc', w, vc) + b_s.T[:, :, None].astype(v.dtype)
    return u * mixed.reshape(B, S, W)


def _fwd_setup_inputs(seed: int = 0) -> dict:
    key = jax.random.key(seed)
    ks = jax.random.split(key, 16)
    f32 = jnp.float32
    x = jax.random.normal(ks[0], (BATCH, SEQ, D_MODEL), f32)
    offsets = jax.random.randint(ks[1], (BATCH, 1), 0, 4096, dtype=jnp.int32)
    positions = offsets + jnp.arange(SEQ, dtype=jnp.int32)[None, :]
    norm_g = 1.0 + 0.02 * jax.random.normal(ks[2], (DEPTH, D_MODEL), f32)
    w_in = jax.random.normal(ks[3], (DEPTH, D_MODEL, PROJ_WIDTH), f32) * D_MODEL ** -0.5
    attn_sink = 0.5 * jax.random.normal(ks[4], (DEPTH, N_Q_HEADS), f32)
    gmlp_ln_g = 1.0 + 0.02 * jax.random.normal(ks[5], (DEPTH, GMLP_WIDTH), f32)
    gmlp_ln_b = 0.02 * jax.random.normal(ks[6], (DEPTH, GMLP_WIDTH), f32)
    w_spatial = jax.random.normal(ks[7], (DEPTH, GMLP_GROUPS, GMLP_CHUNK, GMLP_CHUNK), f32) * GMLP_CHUNK ** -0.5
    b_spatial = 1.0 + 0.1 * jax.random.normal(ks[8], (DEPTH, GMLP_GROUPS, GMLP_CHUNK), f32)
    w_up_attn = jax.random.normal(ks[9], (DEPTH, ATTN_WIDTH, D_MODEL), f32) * ATTN_WIDTH ** -0.5
    w_up_gmlp = jax.random.normal(ks[10], (DEPTH, GMLP_WIDTH, D_MODEL), f32) * GMLP_WIDTH ** -0.5
    w_out = jax.random.normal(ks[11], (DEPTH, D_MODEL, D_MODEL), f32) * D_MODEL ** -0.5
    final_norm_g = 1.0 + 0.02 * jax.random.normal(ks[12], (D_MODEL,), f32)
    return {"x": x, "positions": positions, "norm_g": norm_g, "w_in": w_in,
            "attn_sink": attn_sink, "gmlp_ln_g": gmlp_ln_g, "gmlp_ln_b": gmlp_ln_b,
            "w_spatial": w_spatial, "b_spatial": b_spatial, "w_up_attn": w_up_attn,
            "w_up_gmlp": w_up_gmlp, "w_out": w_out, "final_norm_g": final_norm_g}


def _fwd_reference(x, positions, norm_g, w_in, attn_sink, gmlp_ln_g, gmlp_ln_b, w_spatial,
              b_spatial, w_up_attn, w_up_gmlp, w_out, final_norm_g):
    B, S = x.shape[0], x.shape[1]
    cos, sin = rope_tables(positions, x.dtype)
    for l in range(DEPTH):
        h = rms_norm(x, norm_g[l])
        proj = jnp.einsum('bsd,dp->bsp', h, w_in[l])
        q, k, v, gate_a, u, vg, gate_b, mg_a, mg_b = jnp.split(proj, SPLIT_POINTS, axis=-1)
        q = partial_rope(q.reshape(B, S, N_Q_HEADS, HEAD_DIM), cos, sin)
        k = partial_rope(k.reshape(B, S, N_KV_HEADS, HEAD_DIM), cos, sin)
        v = v.reshape(B, S, N_KV_HEADS, HEAD_DIM)
        attn = sliding_window_sink_attention(q, k, v, attn_sink[l])
        y_a = jnp.einsum('bsw,wd->bsd', attn * jax.nn.silu(gate_a), w_up_attn[l])
        sg = chunked_spatial_gating(jax.nn.gelu(u), jax.nn.gelu(vg), w_spatial[l], b_spatial[l],
                                    gmlp_ln_g[l], gmlp_ln_b[l])
        y_b = jnp.einsum('bsw,wd->bsd', sg * jax.nn.silu(gate_b), w_up_gmlp[l])
        merged = jax.nn.sigmoid(mg_a) * y_a + jax.nn.sigmoid(mg_b) * y_b
        x = x + jnp.einsum('bsd,de->bse', merged, w_out[l])
    return rms_norm(x, final_norm_g)


import jax as _jax
import jax.numpy as _jnp

TWIN_FORMAT = 'train_step'
FWD_PARAMS = ['x', 'positions', 'norm_g', 'w_in', 'attn_sink', 'gmlp_ln_g', 'gmlp_ln_b', 'w_spatial', 'b_spatial', 'w_up_attn', 'w_up_gmlp', 'w_out', 'final_norm_g']
TWIN_WEIGHTS = ['norm_g', 'w_in', 'attn_sink', 'gmlp_ln_g', 'gmlp_ln_b', 'w_spatial', 'b_spatial', 'w_up_attn', 'w_up_gmlp', 'w_out', 'final_norm_g']
TWIN_DIFF_INPUT = 'x'
TWIN_INPUTS = ['x', 'positions', 'norm_g', 'w_in', 'attn_sink', 'gmlp_ln_g', 'gmlp_ln_b', 'w_spatial', 'b_spatial', 'w_up_attn', 'w_up_gmlp', 'w_out', 'final_norm_g', 'loss_target', 'm_norm_g', 'm_w_in', 'm_attn_sink', 'm_gmlp_ln_g', 'm_gmlp_ln_b', 'm_w_spatial', 'm_b_spatial', 'm_w_up_attn', 'm_w_up_gmlp', 'm_w_out', 'm_final_norm_g', 'v_norm_g', 'v_w_in', 'v_attn_sink', 'v_gmlp_ln_g', 'v_gmlp_ln_b', 'v_w_spatial', 'v_b_spatial', 'v_w_up_attn', 'v_w_up_gmlp', 'v_w_out', 'v_final_norm_g']
TWIN_OUTPUTS = ['loss', 'grad_x', 'grad_norm_g', 'grad_w_in', 'grad_attn_sink', 'grad_gmlp_ln_g', 'grad_gmlp_ln_b', 'grad_w_spatial', 'grad_b_spatial', 'grad_w_up_attn', 'grad_w_up_gmlp', 'grad_w_out', 'grad_final_norm_g', 'delta_norm_g', 'delta_w_in', 'delta_attn_sink', 'delta_gmlp_ln_g', 'delta_gmlp_ln_b', 'delta_w_spatial', 'delta_b_spatial', 'delta_w_up_attn', 'delta_w_up_gmlp', 'delta_w_out', 'delta_final_norm_g', 'new_m_norm_g', 'new_m_w_in', 'new_m_attn_sink', 'new_m_gmlp_ln_g', 'new_m_gmlp_ln_b', 'new_m_w_spatial', 'new_m_b_spatial', 'new_m_w_up_attn', 'new_m_w_up_gmlp', 'new_m_w_out', 'new_m_final_norm_g', 'new_v_norm_g', 'new_v_w_in', 'new_v_attn_sink', 'new_v_gmlp_ln_g', 'new_v_gmlp_ln_b', 'new_v_w_spatial', 'new_v_b_spatial', 'new_v_w_up_attn', 'new_v_w_up_gmlp', 'new_v_w_out', 'new_v_final_norm_g']
TWIN_LEAF_KINDS = {'loss': 'loss', 'grad_x': 'grad_x', 'grad_norm_g': 'grad_w', 'grad_w_in': 'grad_w', 'grad_attn_sink': 'grad_w', 'grad_gmlp_ln_g': 'grad_w', 'grad_gmlp_ln_b': 'grad_w', 'grad_w_spatial': 'grad_w', 'grad_b_spatial': 'grad_w', 'grad_w_up_attn': 'grad_w', 'grad_w_up_gmlp': 'grad_w', 'grad_w_out': 'grad_w', 'grad_final_norm_g': 'grad_w', 'delta_norm_g': 'delta_w', 'delta_w_in': 'delta_w', 'delta_attn_sink': 'delta_w', 'delta_gmlp_ln_g': 'delta_w', 'delta_gmlp_ln_b': 'delta_w', 'delta_w_spatial': 'delta_w', 'delta_b_spatial': 'delta_w', 'delta_w_up_attn': 'delta_w', 'delta_w_up_gmlp': 'delta_w', 'delta_w_out': 'delta_w', 'delta_final_norm_g': 'delta_w', 'new_m_norm_g': 'new_m', 'new_m_w_in': 'new_m', 'new_m_attn_sink': 'new_m', 'new_m_gmlp_ln_g': 'new_m', 'new_m_gmlp_ln_b': 'new_m', 'new_m_w_spatial': 'new_m', 'new_m_b_spatial': 'new_m', 'new_m_w_up_attn': 'new_m', 'new_m_w_up_gmlp': 'new_m', 'new_m_w_out': 'new_m', 'new_m_final_norm_g': 'new_m', 'new_v_norm_g': 'new_v', 'new_v_w_in': 'new_v', 'new_v_attn_sink': 'new_v', 'new_v_gmlp_ln_g': 'new_v', 'new_v_gmlp_ln_b': 'new_v', 'new_v_w_spatial': 'new_v', 'new_v_b_spatial': 'new_v', 'new_v_w_up_attn': 'new_v', 'new_v_w_up_gmlp': 'new_v', 'new_v_w_out': 'new_v', 'new_v_final_norm_g': 'new_v'}


def _forward(args):
    return _fwd_reference(*[args[k] for k in FWD_PARAMS])


def _output_shape():
    out = _jax.eval_shape(lambda: _forward(_fwd_setup_inputs(0)))
    return out.shape, out.dtype

N_MICROBATCH = 1
ADAM_LR = 0.001
ADAM_B1 = 0.9
ADAM_B2 = 0.999
ADAM_EPS = 1e-08
ADAM_WD = 0.01
ADAM_STEP = 10
PER_EXAMPLE_BATCH_AXIS = {'x': 0, 'positions': 0, 'loss_target': 0}
SHARED_INPUTS = []
_WEIGHT_DTYPES = {'norm_g': _jnp.float32, 'w_in': _jnp.float32, 'attn_sink': _jnp.float32, 'gmlp_ln_g': _jnp.float32, 'gmlp_ln_b': _jnp.float32, 'w_spatial': _jnp.float32, 'b_spatial': _jnp.float32, 'w_up_attn': _jnp.float32, 'w_up_gmlp': _jnp.float32, 'w_out': _jnp.float32, 'final_norm_g': _jnp.float32}
MOMENT_SCALE = {'norm_g': 1.969857e-02, 'w_in': 7.114819e-03, 'attn_sink': 2.168175e-03, 'gmlp_ln_g': 6.594765e-03, 'gmlp_ln_b': 6.533583e-03, 'w_spatial': 1.295371e-02, 'b_spatial': 1.798195e-02, 'w_up_attn': 2.829642e-03, 'w_up_gmlp': 1.117456e-02, 'w_out': 1.145762e-02, 'final_norm_g': 7.986477e+00}


def _to_microbatches(a, axis):
    t = _jnp.moveaxis(a, axis, 0)
    t = t.reshape((N_MICROBATCH, t.shape[0] // N_MICROBATCH) + t.shape[1:])
    return _jnp.moveaxis(t, 1, axis + 1)


def setup_inputs(seed: int = 0) -> dict:
    inp = _fwd_setup_inputs(seed)
    key = _jax.random.fold_in(_jax.random.key(seed), 7919)
    shape, _ = _output_shape()
    out = dict(inp)
    out["loss_target"] = _jax.random.normal(_jax.random.fold_in(key, 0), shape, _jnp.float32)
    for i, name in enumerate(TWIN_WEIGHTS):
        w = inp[name].astype(_jnp.float32)
        if MOMENT_SCALE is None:
            s = _jnp.sqrt(_jnp.mean(_jnp.square(w)) + 1e-30)
        else:
            s = MOMENT_SCALE[name]
        km, kv = _jax.random.split(_jax.random.fold_in(key, i + 1))
        out[name] = w
        out["m_" + name] = s * _jax.random.normal(km, w.shape, _jnp.float32)
        out["v_" + name] = (s * s) * _jax.random.uniform(kv, w.shape, _jnp.float32, 0.5, 1.5)
    if N_MICROBATCH > 1:
        for name, axis in PER_EXAMPLE_BATCH_AXIS.items():
            out[name] = _to_microbatches(out[name], axis)
    return {'x': out['x'], 'positions': out['positions'], 'norm_g': out['norm_g'], 'w_in': out['w_in'], 'attn_sink': out['attn_sink'], 'gmlp_ln_g': out['gmlp_ln_g'], 'gmlp_ln_b': out['gmlp_ln_b'], 'w_spatial': out['w_spatial'], 'b_spatial': out['b_spatial'], 'w_up_attn': out['w_up_attn'], 'w_up_gmlp': out['w_up_gmlp'], 'w_out': out['w_out'], 'final_norm_g': out['final_norm_g'], 'loss_target': out['loss_target'], 'm_norm_g': out['m_norm_g'], 'm_w_in': out['m_w_in'], 'm_attn_sink': out['m_attn_sink'], 'm_gmlp_ln_g': out['m_gmlp_ln_g'], 'm_gmlp_ln_b': out['m_gmlp_ln_b'], 'm_w_spatial': out['m_w_spatial'], 'm_b_spatial': out['m_b_spatial'], 'm_w_up_attn': out['m_w_up_attn'], 'm_w_up_gmlp': out['m_w_up_gmlp'], 'm_w_out': out['m_w_out'], 'm_final_norm_g': out['m_final_norm_g'], 'v_norm_g': out['v_norm_g'], 'v_w_in': out['v_w_in'], 'v_attn_sink': out['v_attn_sink'], 'v_gmlp_ln_g': out['v_gmlp_ln_g'], 'v_gmlp_ln_b': out['v_gmlp_ln_b'], 'v_w_spatial': out['v_w_spatial'], 'v_b_spatial': out['v_b_spatial'], 'v_w_up_attn': out['v_w_up_attn'], 'v_w_up_gmlp': out['v_w_up_gmlp'], 'v_w_out': out['v_w_out'], 'v_final_norm_g': out['v_final_norm_g']}


def _loss(weights, diff, rest, loss_target):
    with _jax.named_scope("forward"):
        args = {**rest, TWIN_DIFF_INPUT: diff, **{k: w.astype(_WEIGHT_DTYPES[k]) for k, w in weights.items()}}
        y = _forward(args)
    with _jax.named_scope("loss_head"):
        err = _jnp.square(y.astype(_jnp.float32) - loss_target)
        return 0.5 * _jnp.sum(_jnp.mean(err, axis=-1)) if err.ndim else 0.5 * err


def _adamw(w, g, m, v):
    m = ADAM_B1 * m + (1.0 - ADAM_B1) * g
    v = ADAM_B2 * v + (1.0 - ADAM_B2) * _jnp.square(g)
    m_hat = m / (1.0 - ADAM_B1 ** ADAM_STEP)
    v_hat = v / (1.0 - ADAM_B2 ** ADAM_STEP)
    delta = -ADAM_LR * (m_hat / (_jnp.sqrt(v_hat) + ADAM_EPS) + ADAM_WD * w)
    return delta, m, v


def reference(x, positions, norm_g, w_in, attn_sink, gmlp_ln_g, gmlp_ln_b, w_spatial, b_spatial, w_up_attn, w_up_gmlp, w_out, final_norm_g, loss_target, m_norm_g, m_w_in, m_attn_sink, m_gmlp_ln_g, m_gmlp_ln_b, m_w_spatial, m_b_spatial, m_w_up_attn, m_w_up_gmlp, m_w_out, m_final_norm_g, v_norm_g, v_w_in, v_attn_sink, v_gmlp_ln_g, v_gmlp_ln_b, v_w_spatial, v_b_spatial, v_w_up_attn, v_w_up_gmlp, v_w_out, v_final_norm_g):
    given = dict(x=x, positions=positions, norm_g=norm_g, w_in=w_in, attn_sink=attn_sink, gmlp_ln_g=gmlp_ln_g, gmlp_ln_b=gmlp_ln_b, w_spatial=w_spatial, b_spatial=b_spatial, w_up_attn=w_up_attn, w_up_gmlp=w_up_gmlp, w_out=w_out, final_norm_g=final_norm_g, loss_target=loss_target, m_norm_g=m_norm_g, m_w_in=m_w_in, m_attn_sink=m_attn_sink, m_gmlp_ln_g=m_gmlp_ln_g, m_gmlp_ln_b=m_gmlp_ln_b, m_w_spatial=m_w_spatial, m_b_spatial=m_b_spatial, m_w_up_attn=m_w_up_attn, m_w_up_gmlp=m_w_up_gmlp, m_w_out=m_w_out, m_final_norm_g=m_final_norm_g, v_norm_g=v_norm_g, v_w_in=v_w_in, v_attn_sink=v_attn_sink, v_gmlp_ln_g=v_gmlp_ln_g, v_gmlp_ln_b=v_gmlp_ln_b, v_w_spatial=v_w_spatial, v_b_spatial=v_b_spatial, v_w_up_attn=v_w_up_attn, v_w_up_gmlp=v_w_up_gmlp, v_w_out=v_w_out, v_final_norm_g=v_final_norm_g)
    weights = {n: given[n] for n in TWIN_WEIGHTS}
    shared = {n: given[n] for n in SHARED_INPUTS}
    per_example = {n: given[n] for n in ['x', 'positions']}
    grad_fn = _jax.value_and_grad(_loss, argnums=(0, 1))

    def one_microbatch(ex, loss_target):
        ex = dict(ex)
        diff = ex.pop(TWIN_DIFF_INPUT)
        return grad_fn(weights, diff, {**shared, **ex}, loss_target)

    if N_MICROBATCH == 1:
        loss, (grad_w, grad_x) = one_microbatch(per_example, given["loss_target"])
    else:
        def body(carry, xs):
            loss_sum, grad_sum = carry
            l_k, (gw_k, gx_k) = one_microbatch(xs[0], xs[1])
            with _jax.named_scope("update"):
                return (loss_sum + l_k, _jax.tree.map(_jnp.add, grad_sum, gw_k)), gx_k

        init = (_jnp.zeros((), _jnp.float32), _jax.tree.map(_jnp.zeros_like, weights))
        (loss, grad_w), grad_x = _jax.lax.scan(body, init, (per_example, given["loss_target"]))
    with _jax.named_scope("update"):
        delta_w, new_m, new_v = {}, {}, {}
        for n in TWIN_WEIGHTS:
            delta_w[n], new_m[n], new_v[n] = _adamw(weights[n], grad_w[n], given["m_" + n], given["v_" + n])
    return (loss, grad_x, *[grad_w[n] for n in TWIN_WEIGHTS], *[delta_w[n] for n in TWIN_WEIGHTS],
            *[new_m[n] for n in TWIN_WEIGHTS], *[new_v[n] for n in TWIN_WEIGHTS])
```

```python
import functools
import math

import jax
import jax.numpy as jnp
from jax import lax
from jax.experimental import pallas as pl
from jax.experimental.pallas import tpu as pltpu

F32 = jnp.float32
BF16 = jnp.bfloat16
MESH = pl.DeviceIdType.MESH

N_DEV = 8
HEAD_DIM = 64
BLOCK = 128
ROPE_DIM = 16
ROPE_HALF = ROPE_DIM // 2
ROPE_THETA = 500000.0
GMLP_GROUPS = 8
NORM_EPS = 1e-5
LN_EPS = 1e-5
ATTN_SCALE = HEAD_DIM ** -0.5
LANES = 128
SUBLANES = 8
VMEM_LIMIT = 48 * 1024 * 1024

ADAM_LR = 0.001
ADAM_B1 = 0.9
ADAM_B2 = 0.999
ADAM_EPS = 1e-08
ADAM_WD = 0.01
ADAM_STEP = 10

GELU_C = math.sqrt(2.0 / math.pi)
GELU_K = 0.044715


def _sds(shape, dtype):
    return jax.ShapeDtypeStruct(shape, dtype)


def _params(*sem):
    return pltpu.CompilerParams(dimension_semantics=sem or None, vmem_limit_bytes=VMEM_LIMIT)


def _gelu(x):
    return 0.5 * x * (1.0 + jnp.tanh(GELU_C * (x + GELU_K * x * x * x)))


def _gelu_grad(x):
    t = jnp.tanh(GELU_C * (x + GELU_K * x * x * x))
    return 0.5 * (1.0 + t) + 0.5 * x * (1.0 - t * t) * GELU_C * (1.0 + 3.0 * GELU_K * x * x)


def _silu_and_grad(x):
    s = jax.nn.sigmoid(x)
    return x * s, s * (1.0 + x * (1.0 - s))


def _adamw(w, g, m, v):
    m = ADAM_B1 * m + (1.0 - ADAM_B1) * g
    v = ADAM_B2 * v + (1.0 - ADAM_B2) * (g * g)
    m_hat = m / (1.0 - ADAM_B1 ** ADAM_STEP)
    v_hat = v / (1.0 - ADAM_B2 ** ADAM_STEP)
    delta = -ADAM_LR * (m_hat / (jnp.sqrt(v_hat) + ADAM_EPS) + ADAM_WD * w)
    return delta, m, v


def _mesh_pos():
    return lax.axis_index("x"), lax.axis_index("y"), lax.axis_index("c")


def _flat(px, py, pc):
    return 4 * px + 2 * py + pc


def _block_of(ref, idx, size, axis):
    start = pl.multiple_of(idx * size, size)
    if axis == 0:
        return ref.at[pl.ds(start, size), :]
    return ref.at[:, pl.ds(start, size)]


def _cast_bf16(w, name):
    rows, cols = w.shape
    tr = min(rows, 256)

    def body(w_ref, o_ref):
        o_ref[...] = w_ref[...].astype(BF16)

    return pl.pallas_call(
        body, name=name, grid=(rows // tr,),
        in_specs=[pl.BlockSpec((tr, cols), lambda i: (i, 0))],
        out_specs=pl.BlockSpec((tr, cols), lambda i: (i, 0)),
        out_shape=_sds((rows, cols), BF16), compiler_params=_params("parallel"),
    )(w)


def _all_gather(shards, axes, name):
    n = len(shards)

    def body(*refs):
        x_refs, out_refs = refs[:n], refs[n:2 * n]
        send_sems, recv_sems, local_sems = refs[2 * n:]
        x, y, c = _mesh_pos()
        me, sibling = (x, y, c), (x, y, 1 - c)
        chips = [(1 - x, y), (x, 1 - y), (1 - x, 1 - y)]
        first, passed, local = [], [], []
        for a in range(n):
            size = x_refs[a].shape[axes[a]]

            def block(pos, a=a, size=size):
                return _block_of(out_refs[a], _flat(*pos), size, axes[a])

            def copy(k, pos, to, src=None, a=a, block=block):
                return pltpu.make_async_remote_copy(
                    src_ref=block(pos) if src is None else src, dst_ref=block(pos),
                    send_sem=send_sems.at[7 * a + k], recv_sem=recv_sems.at[7 * a + k],
                    device_id=to, device_id_type=MESH)

            mine = pltpu.make_async_copy(x_refs[a], block(me), local_sems.at[a])
            mine.start()
            local.append(mine)
            first.append(copy(0, me, sibling, src=x_refs[a]))
            first += [copy(1 + j, me, (*chip, c), src=x_refs[a]) for j, chip in enumerate(chips)]
        for cp in first:
            cp.start()
        for a in range(n):
            size = x_refs[a].shape[axes[a]]

            def block(pos, a=a, size=size):
                return _block_of(out_refs[a], _flat(*pos), size, axes[a])

            def copy(k, pos, to, a=a, block=block):
                return pltpu.make_async_remote_copy(
                    src_ref=block(pos), dst_ref=block(pos),
                    send_sem=send_sems.at[7 * a + k], recv_sem=recv_sems.at[7 * a + k],
                    device_id=to, device_id_type=MESH)

            for j, chip in enumerate(chips):
                copy(1 + j, (*chip, c), me).wait_recv()
                fwd = copy(4 + j, (*chip, c), sibling)
                fwd.start()
                passed.append(fwd)
        for a in range(n):
            size = x_refs[a].shape[axes[a]]

            def block(pos, a=a, size=size):
                return _block_of(out_refs[a], _flat(*pos), size, axes[a])

            def copy(k, pos, to, a=a, block=block):
                return pltpu.make_async_remote_copy(
                    src_ref=block(pos), dst_ref=block(pos),
                    send_sem=send_sems.at[7 * a + k], recv_sem=recv_sems.at[7 * a + k],
                    device_id=to, device_id_type=MESH)

            copy(0, sibling, me).wait_recv()
            for j, chip in enumerate(chips):
                copy(4 + j, (*chip, 1 - c), me).wait_recv()
        for cp in first + passed:
            cp.wait_send()
        for cp in local:
            cp.wait()

    out_shape = []
    for s, ax in zip(shards, axes):
        shape = list(s.shape)
        shape[ax] *= N_DEV
        out_shape.append(_sds(tuple(shape), s.dtype))
    any_spec = pl.BlockSpec(memory_space=pl.ANY)
    return pl.pallas_call(
        body, name=name, out_shape=tuple(out_shape),
        in_specs=[any_spec] * n, out_specs=tuple([any_spec] * n),
        scratch_shapes=[pltpu.SemaphoreType.DMA((7 * n,)), pltpu.SemaphoreType.DMA((7 * n,)),
                        pltpu.SemaphoreType.DMA((n,))],
    )(*shards)


def _scatter_blocks(fulls, axes, name):
    n = len(fulls)

    def body(*refs):
        g_refs, out_refs = refs[:n], refs[n:2 * n]
        send_sems, recv_sems, local_sems = refs[2 * n:]
        x, y, c = _mesh_pos()
        me = _flat(x, y, c)
        copies, local = [], []
        for a in range(n):
            size = g_refs[a].shape[axes[a]] // N_DEV
            mine = pltpu.make_async_copy(_block_of(g_refs[a], me, size, axes[a]), out_refs[a].at[me],
                                         local_sems.at[a])
            mine.start()
            local.append(mine)
            for k in range(1, N_DEV):
                peer = (x ^ (k >> 2), y ^ ((k >> 1) & 1), c ^ (k & 1))
                copies.append(pltpu.make_async_remote_copy(
                    src_ref=_block_of(g_refs[a], _flat(*peer), size, axes[a]), dst_ref=out_refs[a].at[me],
                    send_sem=send_sems.at[7 * a + k - 1], recv_sem=recv_sems.at[7 * a + k - 1],
                    device_id=peer, device_id_type=MESH))
        for cp in copies:
            cp.start()
        for cp in copies:
            cp.wait_recv()
        for cp in copies:
            cp.wait_send()
        for cp in local:
            cp.wait()

    out_shape = []
    for g, ax in zip(fulls, axes):
        shape = list(g.shape)
        shape[ax] //= N_DEV
        out_shape.append(_sds((N_DEV, *shape), g.dtype))
    any_spec = pl.BlockSpec(memory_space=pl.ANY)
    return pl.pallas_call(
        body, name=name, out_shape=tuple(out_shape),
        in_specs=[any_spec] * n, out_specs=tuple([any_spec] * n),
        scratch_shapes=[pltpu.SemaphoreType.DMA((7 * n,)), pltpu.SemaphoreType.DMA((7 * n,)),
                        pltpu.SemaphoreType.DMA((n,))],
    )(*fulls)


def _reduce_adamw(parts, w, m, v, name):
    rows, cols = w.shape
    tr = min(rows, 64)

    def body(p_ref, w_ref, m_ref, v_ref, g_ref, d_ref, nm_ref, nv_ref):
        g = p_ref[0].astype(F32)
        for s in range(1, N_DEV):
            g = g + p_ref[s].astype(F32)
        delta, nm, nv = _adamw(w_ref[...], g, m_ref[...], v_ref[...])
        g_ref[...] = g
        d_ref[...] = delta
        nm_ref[...] = nm
        nv_ref[...] = nv

    spec = pl.BlockSpec((tr, cols), lambda i: (i, 0))
    return pl.pallas_call(
        body, name=name, grid=(rows // tr,),
        in_specs=[pl.BlockSpec((N_DEV, tr, cols), lambda i: (0, i, 0)), spec, spec, spec],
        out_specs=(spec, spec, spec, spec), out_shape=tuple([_sds((rows, cols), F32)] * 4),
        compiler_params=_params("parallel"),
    )(parts, w, m, v)


def _small_allreduce_adamw(g, w, m, v):
    rows = g.shape[0]

    def body(g_ref, w_ref, m_ref, v_ref, gs_ref, d_ref, nm_ref, nv_ref, all_ref, send_sems, recv_sems):
        x, y, c = _mesh_pos()
        me = _flat(x, y, c)
        copies = []
        for k in range(1, N_DEV):
            peer = (x ^ (k >> 2), y ^ ((k >> 1) & 1), c ^ (k & 1))
            copies.append(pltpu.make_async_remote_copy(
                src_ref=g_ref, dst_ref=all_ref.at[me], send_sem=send_sems.at[k - 1],
                recv_sem=recv_sems.at[k - 1], device_id=peer, device_id_type=MESH))
        for cp in copies:
            cp.start()
        all_ref[me] = g_ref[...]
        for cp in copies:
            cp.wait_recv()
        total = all_ref[0]
        for s in range(1, N_DEV):
            total = total + all_ref[s]
        delta, nm, nv = _adamw(w_ref[...], total, m_ref[...], v_ref[...])
        gs_ref[...] = total
        d_ref[...] = delta
        nm_ref[...] = nm
        nv_ref[...] = nv
        for cp in copies:
            cp.wait_send()

    vmem = pl.BlockSpec(memory_space=pltpu.VMEM)
    return pl.pallas_call(
        body, name="small_allreduce_adamw", out_shape=tuple([_sds((rows, LANES), F32)] * 4),
        in_specs=[vmem] * 4, out_specs=tuple([vmem] * 4),
        scratch_shapes=[pltpu.VMEM((N_DEV, rows, LANES), F32), pltpu.SemaphoreType.DMA((7,)),
                        pltpu.SemaphoreType.DMA((7,))],
    )(g, w, m, v)


_DOT_DIMS = {"nn": ((1,), (0,)), "nt": ((1,), (1,)), "tn": ((0,), (0,))}


def _matmul(a, b, mode, out_dtype, name, *, n=None, b_off=0, res=None, tm=2048, tn=1024, tk=512):
    if mode == "tn":
        kdim, mdim = a.shape
    else:
        mdim, kdim = a.shape
    ndim = n if n is not None else (b.shape[0] if mode == "nt" else b.shape[1])
    tm, tn, tk = min(tm, mdim), min(tn, ndim), min(tk, kdim)
    assert mdim % tm == 0 and ndim % tn == 0 and kdim % tk == 0, (name, mdim, ndim, kdim)
    nk = kdim // tk

    def body(*refs):
        if res is None:
            a_ref, b_ref, o_ref, acc_ref = refs
        else:
            a_ref, b_ref, r_ref, o_ref, acc_ref = refs
        k = pl.program_id(2)

        @pl.when(k == 0)
        def _():
            acc_ref[...] = jnp.zeros_like(acc_ref)

        acc_ref[...] += lax.dot_general(a_ref[...], b_ref[...], (_DOT_DIMS[mode], ((), ())),
                                        preferred_element_type=F32)

        @pl.when(k == nk - 1)
        def _():
            out = acc_ref[...]
            if res is not None:
                out = out + r_ref[...]
            o_ref[...] = out.astype(out_dtype)

    if mode == "tn":
        a_spec = pl.BlockSpec((tk, tm), lambda i, j, k: (k, i))
    else:
        a_spec = pl.BlockSpec((tm, tk), lambda i, j, k: (i, k))
    if mode == "nt":
        b_spec = pl.BlockSpec((tn, tk), lambda i, j, k: (j + b_off, k))
    else:
        b_spec = pl.BlockSpec((tk, tn), lambda i, j, k: (k, j + b_off))
    o_spec = pl.BlockSpec((tm, tn), lambda i, j, k: (i, j))
    in_specs, args = [a_spec, b_spec], [a, b]
    if res is not None:
        in_specs.append(o_spec)
        args.append(res)
    return pl.pallas_call(
        body, name=name, grid=(mdim // tm, ndim // tn, nk), in_specs=in_specs, out_specs=o_spec,
        out_shape=_sds((mdim, ndim), out_dtype), scratch_shapes=[pltpu.VMEM((tm, tn), F32)],
        compiler_params=_params("parallel", "parallel", "arbitrary"),
    )(*args)


def _row_tile(rows):
    return min(rows, 128)


def _rmsnorm_fwd(x, g):
    s, d = x.shape
    tr = _row_tile(s)

    def body(x_ref, g_ref, h_ref):
        xv = x_ref[...]
        r = lax.rsqrt(jnp.mean(xv * xv, axis=-1, keepdims=True) + NORM_EPS)
        h_ref[...] = (xv * r * g_ref[...]).astype(BF16)

    row = pl.BlockSpec((tr, d), lambda i: (i, 0))
    vec = pl.BlockSpec((1, d), lambda i: (0, 0))
    return pl.pallas_call(body, name="rmsnorm_fwd", grid=(s // tr,), in_specs=[row, vec], out_specs=row,
                          out_shape=_sds((s, d), BF16), compiler_params=_params("parallel"))(x, g)


def _merge_fwd(y_a, y_b, proj_m):
    s, d = y_a.shape
    tr = _row_tile(s)

    def body(ya_ref, yb_ref, ma_ref, mb_ref, o_ref):
        o_ref[...] = (jax.nn.sigmoid(ma_ref[...]) * ya_ref[...]
                      + jax.nn.sigmoid(mb_ref[...]) * yb_ref[...]).astype(BF16)

    row = pl.BlockSpec((tr, d), lambda i: (i, 0))
    row1 = pl.BlockSpec((tr, d), lambda i: (i, 1))
    return pl.pallas_call(body, name="merge_fwd", grid=(s // tr,), in_specs=[row, row, row, row1],
                          out_specs=row, out_shape=_sds((s, d), BF16),
                          compiler_params=_params("parallel"))(y_a, y_b, proj_m, proj_m)


def _loss_and_final_norm_bwd(x2, target, g):
    s, d = x2.shape
    tr = _row_tile(s)

    def body(x_ref, t_ref, g_ref, loss_ref, dg_ref, dx_ref, dxb_ref):
        @pl.when(pl.program_id(0) == 0)
        def _():
            loss_ref[...] = jnp.zeros_like(loss_ref)
            dg_ref[...] = jnp.zeros_like(dg_ref)

        xv, gv = x_ref[...], g_ref[...]
        r = lax.rsqrt(jnp.mean(xv * xv, axis=-1, keepdims=True) + NORM_EPS)
        xhat = xv * r
        err = xhat * gv - t_ref[...]
        loss_ref[...] += 0.5 * jnp.sum(jnp.mean(err * err, axis=-1, keepdims=True))
        dy = err / d
        dg_ref[...] += jnp.sum(dy * xhat, axis=0, keepdims=True)
        dyg = dy * gv
        dx = r * (dyg - xhat * jnp.mean(dyg * xhat, axis=-1, keepdims=True))
        dx_ref[...] = dx
        dxb_ref[...] = dx.astype(BF16)

    row = pl.BlockSpec((tr, d), lambda i: (i, 0))
    vec = pl.BlockSpec((1, d), lambda i: (0, 0))
    return pl.pallas_call(
        body, name="loss_final_norm_bwd", grid=(s // tr,), in_specs=[row, row, vec],
        out_specs=(pl.BlockSpec((SUBLANES, LANES), lambda i: (0, 0)), vec, row, row),
        out_shape=(_sds((SUBLANES, LANES), F32), _sds((1, d), F32), _sds((s, d), F32), _sds((s, d), BF16)),
        compiler_params=_params("arbitrary"))(x2, target, g)


def _merge_bwd(d_merged, y_a, y_b, proj_m):
    s, d = y_a.shape
    tr = _row_tile(s)

    def body(dm_ref, ya_ref, yb_ref, ma_ref, mb_ref, dya_ref, dyb_ref, dmg_ref):
        dm = dm_ref[...]
        sa = jax.nn.sigmoid(ma_ref[...])
        sb = jax.nn.sigmoid(mb_ref[...])
        dya_ref[...] = (dm * sa).astype(BF16)
        dyb_ref[...] = (dm * sb).astype(BF16)
        dmg_ref[:, :d] = (dm * ya_ref[...] * (sa * (1.0 - sa))).astype(BF16)
        dmg_ref[:, d:] = (dm * yb_ref[...] * (sb * (1.0 - sb))).astype(BF16)

    row = pl.BlockSpec((tr, d), lambda i: (i, 0))
    row1 = pl.BlockSpec((tr, d), lambda i: (i, 1))
    wide = pl.BlockSpec((tr, 2 * d), lambda i: (i, 0))
    return pl.pallas_call(
        body, name="merge_bwd", grid=(s // tr,), in_specs=[row, row, row, row, row1],
        out_specs=(row, row, wide),
        out_shape=(_sds((s, d), BF16), _sds((s, d), BF16), _sds((s, 2 * d), BF16)),
        compiler_params=_params("parallel"))(d_merged, y_a, y_b, proj_m, proj_m)


def _input_grad(d_h, x, g, dx2):
    s, d = x.shape
    tr = _row_tile(s)

    def body(dh_ref, x_ref, g_ref, dx2_ref, gx_ref, dg_ref):
        @pl.when(pl.program_id(0) == 0)
        def _():
            dg_ref[...] = jnp.zeros_like(dg_ref)

        xv, dh = x_ref[...], dh_ref[...]
        r = lax.rsqrt(jnp.mean(xv * xv, axis=-1, keepdims=True) + NORM_EPS)
        xhat = xv * r
        dg_ref[...] += jnp.sum(dh * xhat, axis=0, keepdims=True)
        dyg = dh * g_ref[...]
        gx_ref[...] = dx2_ref[...] + r * (dyg - xhat * jnp.mean(dyg * xhat, axis=-1, keepdims=True))

    row = pl.BlockSpec((tr, d), lambda i: (i, 0))
    vec = pl.BlockSpec((1, d), lambda i: (0, 0))
    return pl.pallas_call(
        body, name="input_grad", grid=(s // tr,), in_specs=[row, row, vec, row], out_specs=(row, vec),
        out_shape=(_sds((s, d), F32), _sds((1, d), F32)), compiler_params=_params("arbitrary"))(d_h, x, g, dx2)


def _rope_tables(positions):
    inv_freq = ROPE_THETA ** (-jnp.arange(ROPE_HALF, dtype=F32) * 2.0 / ROPE_DIM)
    ang = positions.astype(F32)[:, None] * inv_freq
    cos, sin = jnp.cos(ang), jnp.sin(ang)
    zero = jnp.zeros((positions.shape[0], HEAD_DIM - ROPE_DIM), F32)
    zero_h = jnp.zeros_like(sin)
    c = jnp.concatenate([cos, cos, zero + 1.0], axis=1)
    up = jnp.concatenate([-sin, zero_h, zero], axis=1)
    down = jnp.concatenate([zero_h, sin, zero], axis=1)
    reps = LANES // HEAD_DIM
    return jnp.stack([jnp.tile(c, (1, reps)), jnp.tile(up, (1, reps)), jnp.tile(down, (1, reps))])


def _lane_tiles(x):
    return [x[:, t * LANES:(t + 1) * LANES] for t in range(x.shape[1] // LANES)]


def _rope(x, tab):
    out = [xt * tab[0] + pltpu.roll(xt, LANES - ROPE_HALF, 1) * tab[1] + pltpu.roll(xt, ROPE_HALF, 1) * tab[2]
           for xt in _lane_tiles(x)]
    return out[0] if len(out) == 1 else jnp.concatenate(out, axis=1)


def _rope_bwd(g, tab):
    out = [gt * tab[0] + pltpu.roll(gt * tab[1], ROPE_HALF, 1) + pltpu.roll(gt * tab[2], LANES - ROPE_HALF, 1)
           for gt in _lane_tiles(g)]
    return out[0] if len(out) == 1 else jnp.concatenate(out, axis=1)


def _head(x, h):
    return x[:, h * HEAD_DIM:(h + 1) * HEAD_DIM]


def _stack_heads(x, first, count):
    return jnp.concatenate([_head(x, first + h) for h in range(count)], axis=0)


def _band_scores(q, k_prev, k_cur, sink_ref, pair, a, group, blk):
    kb = jnp.concatenate([_head(k_prev, a), _head(k_cur, a)], axis=0).astype(BF16)
    qs = _stack_heads(q, a * group, group).astype(BF16)
    s = lax.dot_general(qs, kb, (((1,), (1,)), ((), ())), preferred_element_type=F32) * ATTN_SCALE
    qi = lax.broadcasted_iota(jnp.int32, s.shape, 0) % BLOCK
    si = lax.broadcasted_iota(jnp.int32, s.shape, 1)
    mask = (si <= qi + BLOCK) & (si > qi) & ((blk > 0) | (si >= BLOCK))
    s = jnp.where(mask, s, -jnp.inf)
    sink = jnp.concatenate(
        [jnp.full((BLOCK, 1), sink_ref[(2 * pair + a) * group + h], F32) for h in range(group)], axis=0)
    m = jnp.maximum(jnp.max(s, axis=-1, keepdims=True), sink)
    p = jnp.exp(s - m)
    p_sink = jnp.exp(sink - m)
    denom = jnp.sum(p, axis=-1, keepdims=True) + p_sink
    return qs, kb, p / denom, p_sink / denom


def _attn_dims(proj_a, n_q_heads):
    s = proj_a.shape[0]
    d = n_q_heads * HEAD_DIM
    kv = (proj_a.shape[1] - 2 * d) // 2
    n_kv = kv // HEAD_DIM
    group = n_q_heads // n_kv
    qw = 2 * group * HEAD_DIM
    assert n_kv % 2 == 0 and (d + 2 * kv) % qw == 0 and s % BLOCK == 0
    return s, d, kv, group, qw, n_kv // 2, s // BLOCK


def _attention_fwd(proj_a, tables, sink):
    s, d, kv, group, qw, n_pairs, nb = _attn_dims(proj_a, sink.shape[0])

    def body(sink_ref, q_ref, kc_ref, kp_ref, vc_ref, vp_ref, ga_ref, tc_ref, tp_ref, attn_ref, ain_ref):
        pair, blk = pl.program_id(0), pl.program_id(1)
        tab_c, tab_p = tc_ref[...], tp_ref[...]
        q = _rope(q_ref[...], tab_c)
        k_cur, k_prev = _rope(kc_ref[...], tab_c), _rope(kp_ref[...], tab_p)
        v_cur, v_prev = vc_ref[...], vp_ref[...]
        outs = []
        for a in range(2):
            _, _, p, _ = _band_scores(q, k_prev, k_cur, sink_ref, pair, a, group, blk)
            vb = jnp.concatenate([_head(v_prev, a), _head(v_cur, a)], axis=0).astype(BF16)
            o = jnp.dot(p.astype(BF16), vb, preferred_element_type=F32)
            outs += [o[h * BLOCK:(h + 1) * BLOCK] for h in range(group)]
        attn = jnp.concatenate(outs, axis=1)
        attn_ref[...] = attn
        silu, _ = _silu_and_grad(ga_ref[...])
        ain_ref[...] = (attn * silu).astype(BF16)

    k0, v0, g0 = d // LANES, (d + kv) // LANES, (d + 2 * kv) // qw
    prev = lambda i: jnp.maximum(i - 1, 0)
    in_specs = [
        pl.BlockSpec(memory_space=pltpu.SMEM),
        pl.BlockSpec((BLOCK, qw), lambda p, i: (i, p)),
        pl.BlockSpec((BLOCK, LANES), lambda p, i: (i, k0 + p)),
        pl.BlockSpec((BLOCK, LANES), lambda p, i: (prev(i), k0 + p)),
        pl.BlockSpec((BLOCK, LANES), lambda p, i: (i, v0 + p)),
        pl.BlockSpec((BLOCK, LANES), lambda p, i: (prev(i), v0 + p)),
        pl.BlockSpec((BLOCK, qw), lambda p, i: (i, g0 + p)),
        pl.BlockSpec((3, BLOCK, LANES), lambda p, i: (0, i, 0)),
        pl.BlockSpec((3, BLOCK, LANES), lambda p, i: (0, prev(i), 0)),
    ]
    out = pl.BlockSpec((BLOCK, qw), lambda p, i: (i, p))
    return pl.pallas_call(
        body, name="attention_fwd", grid=(n_pairs, nb), in_specs=in_specs, out_specs=(out, out),
        out_shape=(_sds((s, d), F32), _sds((s, d), BF16)), compiler_params=_params("parallel", "parallel"),
    )(sink, proj_a, proj_a, proj_a, proj_a, proj_a, proj_a, tables, tables)


def _attention_bwd(proj_a, tables, sink, attn, d_ain):
    s, d, kv, group, qw, n_pairs, nb = _attn_dims(proj_a, sink.shape[0])

    def body(sink_ref, q_ref, kc_ref, kp_ref, vc_ref, vp_ref, ga_ref, tc_ref, tp_ref, attn_ref, dain_ref,
             dq_ref, dk_ref, dv_ref, dga_ref, dsink_ref, carry_k, carry_v):
        pair, blk = pl.program_id(0), pl.program_id(1)

        @pl.when(blk == 0)
        def _():
            carry_k[...] = jnp.zeros_like(carry_k)
            carry_v[...] = jnp.zeros_like(carry_v)
            dsink_ref[...] = jnp.zeros_like(dsink_ref)

        @pl.when(blk < nb)
        def _():
            tab_c, tab_p = tc_ref[...], tp_ref[...]
            q = _rope(q_ref[...], tab_c)
            k_cur, k_prev = _rope(kc_ref[...], tab_c), _rope(kp_ref[...], tab_p)
            v_cur, v_prev = vc_ref[...], vp_ref[...]
            silu, silu_grad = _silu_and_grad(ga_ref[...])
            d_ain_v = dain_ref[...]
            dga_ref[...] = (d_ain_v * attn_ref[...] * silu_grad).astype(BF16)
            d_attn = d_ain_v * silu
            dq_parts, dk_parts, dv_parts = [], [], []
            lane = lax.broadcasted_iota(jnp.int32, (1, LANES), 1)
            dsink = jnp.zeros((1, LANES), F32)
            for a in range(2):
                qs, kb, p, p_sink = _band_scores(q, k_prev, k_cur, sink_ref, pair, a, group, blk)
                vb = jnp.concatenate([_head(v_prev, a), _head(v_cur, a)], axis=0).astype(BF16)
                do = _stack_heads(d_attn, a * group, group).astype(BF16)
                dp = lax.dot_general(do, vb, (((1,), (1,)), ((), ())), preferred_element_type=F32)
                delta = jnp.sum(p * dp, axis=-1, keepdims=True)
                ds = ((p * (dp - delta)) * ATTN_SCALE).astype(BF16)
                dqs = jnp.dot(ds, kb, preferred_element_type=F32)
                dq_parts += [dqs[h * BLOCK:(h + 1) * BLOCK] for h in range(group)]
                dk_parts.append(lax.dot_general(ds, qs, (((0,), (0,)), ((), ())), preferred_element_type=F32))
                dv_parts.append(lax.dot_general(p.astype(BF16), do, (((0,), (0,)), ((), ())),
                                                preferred_element_type=F32))
                ds_sink = -(p_sink * delta)
                for h in range(group):
                    dsink = dsink + jnp.where(lane == a * group + h,
                                              jnp.sum(ds_sink[h * BLOCK:(h + 1) * BLOCK]), 0.0)
            dsink_ref[0] += dsink
            dq_ref[...] = _rope_bwd(jnp.concatenate(dq_parts, axis=1), tab_c).astype(BF16)
            dk_band = jnp.concatenate(dk_parts, axis=1)
            dv_band = jnp.concatenate(dv_parts, axis=1)
            dk_ref[...] = (carry_k[...] + _rope_bwd(dk_band[:BLOCK], tab_p)).astype(BF16)
            dv_ref[...] = (carry_v[...] + dv_band[:BLOCK]).astype(BF16)
            carry_k[...] = _rope_bwd(dk_band[BLOCK:], tab_c)
            carry_v[...] = dv_band[BLOCK:]

        @pl.when(blk == nb)
        def _():
            dk_ref[...] = carry_k[...].astype(BF16)
            dv_ref[...] = carry_v[...].astype(BF16)

    k0, v0, g0 = d // LANES, (d + kv) // LANES, (d + 2 * kv) // qw
    cur = lambda i: jnp.minimum(i, nb - 1)
    prev = lambda i: jnp.maximum(cur(i) - 1, 0)
    back = lambda i: jnp.maximum(i - 1, 0)
    q_spec = pl.BlockSpec((BLOCK, qw), lambda p, i: (cur(i), p))
    in_specs = [
        pl.BlockSpec(memory_space=pltpu.SMEM),
        q_spec,
        pl.BlockSpec((BLOCK, LANES), lambda p, i: (cur(i), k0 + p)),
        pl.BlockSpec((BLOCK, LANES), lambda p, i: (prev(i), k0 + p)),
        pl.BlockSpec((BLOCK, LANES), lambda p, i: (cur(i), v0 + p)),
        pl.BlockSpec((BLOCK, LANES), lambda p, i: (prev(i), v0 + p)),
        pl.BlockSpec((BLOCK, qw), lambda p, i: (cur(i), g0 + p)),
        pl.BlockSpec((3, BLOCK, LANES), lambda p, i: (0, cur(i), 0)),
        pl.BlockSpec((3, BLOCK, LANES), lambda p, i: (0, prev(i), 0)),
        q_spec,
        q_spec,
    ]
    kv_out = pl.BlockSpec((BLOCK, LANES), lambda p, i: (back(i), p))
    return pl.pallas_call(
        body, name="attention_bwd", grid=(n_pairs, nb + 1), in_specs=in_specs,
        out_specs=(q_spec, kv_out, kv_out, q_spec, pl.BlockSpec((1, 1, LANES), lambda p, i: (p, 0, 0))),
        out_shape=(_sds((s, d), BF16), _sds((s, kv), BF16), _sds((s, kv), BF16), _sds((s, d), BF16),
                   _sds((n_pairs, 1, LANES), F32)),
        scratch_shapes=[pltpu.VMEM((BLOCK, LANES), F32), pltpu.VMEM((BLOCK, LANES), F32)],
        compiler_params=_params("parallel", "arbitrary"),
    )(sink, proj_a, proj_a, proj_a, proj_a, proj_a, proj_a, tables, tables, attn, d_ain)


def _gmlp_core(u, vg, ln_g, ln_b, w_ref, bias_t):
    gu = _gelu(u)
    gv = _gelu(vg)
    xc = gv - jnp.mean(gv, axis=-1, keepdims=True)
    rstd = lax.rsqrt(jnp.mean(xc * xc, axis=-1, keepdims=True) + LN_EPS)
    xhat = xc * rstd
    vn = (xhat * ln_g + ln_b).astype(BF16)
    gd = u.shape[1] // GMLP_GROUPS
    tri = (lax.broadcasted_iota(jnp.int32, (BLOCK, BLOCK), 0) >= lax.broadcasted_iota(jnp.int32, (BLOCK, BLOCK), 1))
    w_tri = [jnp.where(tri, w_ref[g], 0.0).astype(BF16) for g in range(GMLP_GROUPS)]
    mixed = jnp.concatenate(
        [jnp.dot(w_tri[g], vn[:, g * gd:(g + 1) * gd], preferred_element_type=F32) + bias_t[:, g:g + 1]
         for g in range(GMLP_GROUPS)], axis=1)
    return gu, xhat, rstd, vn, w_tri, tri, mixed


def _gmlp_fwd(proj_g, w_s, bias_t, ln_g, ln_b):
    s, d = proj_g.shape[0], proj_g.shape[1] // 3

    def body(u_ref, v_ref, gb_ref, w_ref, bt_ref, lg_ref, lb_ref, o_ref):
        gu, _, _, _, _, _, mixed = _gmlp_core(u_ref[...], v_ref[...], lg_ref[...], lb_ref[...], w_ref, bt_ref[...])
        silu, _ = _silu_and_grad(gb_ref[...])
        o_ref[...] = ((gu * mixed) * silu).astype(BF16)

    col = lambda j: pl.BlockSpec((BLOCK, d), lambda i: (i, j))
    whole = lambda shape: pl.BlockSpec(shape, lambda i: tuple(0 for _ in shape))
    return pl.pallas_call(
        body, name="gmlp_fwd", grid=(s // BLOCK,),
        in_specs=[col(0), col(1), col(2), whole(w_s.shape), whole(bias_t.shape), whole((1, d)), whole((1, d))],
        out_specs=col(0), out_shape=_sds((s, d), BF16), compiler_params=_params("parallel"),
    )(proj_g, proj_g, proj_g, w_s, bias_t, ln_g, ln_b)


def _gmlp_bwd(proj_g, w_s, bias_t, ln_g, ln_b, d_bin):
    s, d = proj_g.shape[0], proj_g.shape[1] // 3
    gd = d // GMLP_GROUPS

    def body(u_ref, v_ref, gb_ref, w_ref, bt_ref, lg_ref, lb_ref, dbin_ref, dg_ref, dw_ref, dbt_ref, dlg_ref, dlb_ref):
        @pl.when(pl.program_id(0) == 0)
        def _():
            dw_ref[...] = jnp.zeros_like(dw_ref)
            dbt_ref[...] = jnp.zeros_like(dbt_ref)
            dlg_ref[...] = jnp.zeros_like(dlg_ref)
            dlb_ref[...] = jnp.zeros_like(dlb_ref)

        u, vg, ln_g = u_ref[...], v_ref[...], lg_ref[...]
        gu, xhat, rstd, vn, w_tri, tri, mixed = _gmlp_core(u, vg, ln_g, lb_ref[...], w_ref, bt_ref[...])
        silu, silu_grad = _silu_and_grad(gb_ref[...])
        d_bin_v = dbin_ref[...]
        d_sg = d_bin_v * silu
        dg_ref[:, 2 * d:] = (d_bin_v * (gu * mixed) * silu_grad).astype(BF16)
        dg_ref[:, :d] = (d_sg * mixed * _gelu_grad(u)).astype(BF16)
        d_mixed = d_sg * gu
        d_mixed_b = d_mixed.astype(BF16)
        d_vn, d_bias = [], []
        for g in range(GMLP_GROUPS):
            dm_g = d_mixed_b[:, g * gd:(g + 1) * gd]
            d_bias.append(jnp.sum(d_mixed[:, g * gd:(g + 1) * gd], axis=-1, keepdims=True))
            dw = lax.dot_general(dm_g, vn[:, g * gd:(g + 1) * gd], (((1,), (1,)), ((), ())),
                                 preferred_element_type=F32)
            dw_ref[g] += jnp.where(tri, dw, 0.0)
            d_vn.append(lax.dot_general(w_tri[g], dm_g, (((0,), (0,)), ((), ())), preferred_element_type=F32))
        dbt_ref[...] += jnp.concatenate(d_bias, axis=1)
        d_vn = jnp.concatenate(d_vn, axis=1)
        dlg_ref[...] += jnp.sum(d_vn * xhat, axis=0, keepdims=True)
        dlb_ref[...] += jnp.sum(d_vn, axis=0, keepdims=True)
        d_xhat = d_vn * ln_g
        d_gv = rstd * (d_xhat - jnp.mean(d_xhat, axis=-1, keepdims=True)
                       - xhat * jnp.mean(d_xhat * xhat, axis=-1, keepdims=True))
        dg_ref[:, d:2 * d] = (d_gv * _gelu_grad(vg)).astype(BF16)

    col = lambda j: pl.BlockSpec((BLOCK, d), lambda i: (i, j))
    whole = lambda shape: pl.BlockSpec(shape, lambda i: tuple(0 for _ in shape))
    return pl.pallas_call(
        body, name="gmlp_bwd", grid=(s // BLOCK,),
        in_specs=[col(0), col(1), col(2), whole(w_s.shape), whole(bias_t.shape), whole((1, d)), whole((1, d)), col(0)],
        out_specs=(pl.BlockSpec((BLOCK, 3 * d), lambda i: (i, 0)), whole(w_s.shape), whole(bias_t.shape),
                   whole((1, d)), whole((1, d))),
        out_shape=(_sds((s, 3 * d), BF16), _sds(w_s.shape, F32), _sds(bias_t.shape, F32), _sds((1, d), F32),
                   _sds((1, d), F32)),
        compiler_params=_params("arbitrary"),
    )(proj_g, proj_g, proj_g, w_s, bias_t, ln_g, ln_b, d_bin)


def _pack(parts):
    rows = []
    for p in parts:
        flat = p.astype(F32).reshape(-1)
        tile = SUBLANES * LANES
        padded = -(-flat.shape[0] // tile) * tile
        rows.append(jnp.pad(flat, (0, padded - flat.shape[0])).reshape(-1, LANES))
    return jnp.concatenate(rows, axis=0)


def _unpack(packed, shapes):
    out, row = [], 0
    for shape in shapes:
        size = math.prod(shape)
        tile = SUBLANES * LANES
        n_rows = -(-size // tile) * SUBLANES
        out.append(packed[row:row + n_rows].reshape(-1)[:size].reshape(shape))
        row += n_rows
    return out


def kernel(x, positions, norm_g, w_in, attn_sink, gmlp_ln_g, gmlp_ln_b, w_spatial, b_spatial, w_up_attn, w_up_gmlp, w_out, final_norm_g, loss_target, m_norm_g, m_w_in, m_attn_sink, m_gmlp_ln_g, m_gmlp_ln_b, m_w_spatial, m_b_spatial, m_w_up_attn, m_w_up_gmlp, m_w_out, m_final_norm_g, v_norm_g, v_w_in, v_attn_sink, v_gmlp_ln_g, v_gmlp_ln_b, v_w_spatial, v_b_spatial, v_w_up_attn, v_w_up_gmlp, v_w_out, v_final_norm_g):
    x2d, target = x[0], loss_target[0]
    s, d = x2d.shape
    n_q_heads = attn_sink.shape[1]
    proj_w = w_in.shape[2] * N_DEV
    kv = (proj_w - 7 * d) // 2
    wa, wg = 2 * d + 2 * kv, 3 * d
    final_g = final_norm_g.reshape(1, d)
    sink = attn_sink[0]
    w_s = w_spatial[0]
    bias_t = b_spatial[0].T

    (w_in_f,) = _all_gather([_cast_bf16(w_in[0], "cast_w_in")], [1], "gather_w_in")
    w_ua_f, w_ug_f, w_out_f = _all_gather(
        [_cast_bf16(w_up_attn[0], "cast_w_up_attn"), _cast_bf16(w_up_gmlp[0], "cast_w_up_gmlp"),
         _cast_bf16(w_out[0], "cast_w_out")], [0, 0, 0], "gather_w_up_out")

    tn = min(1024, d)
    h = _rmsnorm_fwd(x2d, norm_g)
    proj_a = _matmul(h, w_in_f, "nn", F32, "proj_attn", n=wa, b_off=0, tn=tn)
    proj_g = _matmul(h, w_in_f, "nn", F32, "proj_gmlp", n=wg, b_off=wa // tn, tn=tn)
    proj_m = _matmul(h, w_in_f, "nn", F32, "proj_merge", n=2 * d, b_off=(wa + wg) // tn, tn=tn)
    tables = _rope_tables(positions[0])
    attn, a_in = _attention_fwd(proj_a, tables, sink)
    b_in = _gmlp_fwd(proj_g, w_s, bias_t, gmlp_ln_g, gmlp_ln_b)
    y_a = _matmul(a_in, w_ua_f, "nn", F32, "up_attn")
    y_b = _matmul(b_in, w_ug_f, "nn", F32, "up_gmlp")
    merged = _merge_fwd(y_a, y_b, proj_m)
    x_out = _matmul(merged, w_out_f, "nn", F32, "out_proj", res=x2d)
    loss_p, d_final_g, dx2, dx2_b = _loss_and_final_norm_bwd(x_out, target, final_g)

    d_merged = _matmul(dx2_b, w_out_f, "nt", F32, "d_merged")
    g_w_out = _matmul(merged, dx2_b, "tn", BF16, "g_w_out")
    d_ya, d_yb, d_mg = _merge_bwd(d_merged, y_a, y_b, proj_m)
    d_ain = _matmul(d_ya, w_ua_f, "nt", F32, "d_a_in")
    g_w_ua = _matmul(a_in, d_ya, "tn", BF16, "g_w_up_attn")
    d_bin = _matmul(d_yb, w_ug_f, "nt", F32, "d_b_in")
    g_w_ug = _matmul(b_in, d_yb, "tn", BF16, "g_w_up_gmlp")
    d_q, d_k, d_v, d_ga, d_sink = _attention_bwd(proj_a, tables, sink, attn, d_ain)
    d_g, d_w_s, d_bias_t, d_ln_g, d_ln_b = _gmlp_bwd(proj_g, w_s, bias_t, gmlp_ln_g, gmlp_ln_b, d_bin)
    d_proj = jnp.concatenate([d_q, d_k, d_v, d_ga, d_g, d_mg], axis=1)
    d_h = _matmul(d_proj, w_in_f, "nt", F32, "d_h")
    g_w_in = _matmul(h, d_proj, "tn", BF16, "g_w_in")
    grad_x, d_norm_g = _input_grad(d_h, x2d, norm_g, dx2)

    p_in, p_ua, p_ug, p_out = _scatter_blocks([g_w_in, g_w_ua, g_w_ug, g_w_out], [1, 0, 0, 0], "scatter_grads")
    big = {}
    for name, parts, w, m, v in (("w_in", p_in, w_in, m_w_in, v_w_in), ("w_up_attn", p_ua, w_up_attn, m_w_up_attn, v_w_up_attn),
                                 ("w_up_gmlp", p_ug, w_up_gmlp, m_w_up_gmlp, v_w_up_gmlp), ("w_out", p_out, w_out, m_w_out, v_w_out)):
        big[name] = [r[None] for r in _reduce_adamw(parts, w[0], m[0], v[0], "adamw_" + name)]

    heads_per_pair = 2 * n_q_heads // (kv // HEAD_DIM)
    g_sink = d_sink[:, 0, :heads_per_pair].reshape(1, n_q_heads)
    small_w = [norm_g, attn_sink, gmlp_ln_g, gmlp_ln_b, w_spatial, b_spatial, final_norm_g]
    small_m = [m_norm_g, m_attn_sink, m_gmlp_ln_g, m_gmlp_ln_b, m_w_spatial, m_b_spatial, m_final_norm_g]
    small_v = [v_norm_g, v_attn_sink, v_gmlp_ln_g, v_gmlp_ln_b, v_w_spatial, v_b_spatial, v_final_norm_g]
    small_g = [d_norm_g, g_sink, d_ln_g, d_ln_b, d_w_s[None], d_bias_t.T[None], d_final_g.reshape(d)]
    loss_pad = jnp.zeros((1,), F32)
    shapes = [w.shape for w in small_w] + [(1,)]
    packed = _small_allreduce_adamw(_pack(small_g + [loss_p[0, :1]]), _pack(small_w + [loss_pad]),
                                    _pack(small_m + [loss_pad]), _pack(small_v + [loss_pad]))
    sg, sd, sm, sv = [_unpack(p, shapes) for p in packed]
    loss = sg[-1][0]

    names = ["norm_g", "w_in", "attn_sink", "gmlp_ln_g", "gmlp_ln_b", "w_spatial", "b_spatial", "w_up_attn",
             "w_up_gmlp", "w_out", "final_norm_g"]
    small_names = ["norm_g", "attn_sink", "gmlp_ln_g", "gmlp_ln_b", "w_spatial", "b_spatial", "final_norm_g"]
    outs = [[], [], [], []]
    for nm in names:
        for k in range(4):
            if nm in big:
                outs[k].append(big[nm][k])
            else:
                outs[k].append((sg, sd, sm, sv)[k][small_names.index(nm)])
    return (loss, grad_x[None], *outs[0], *outs[1], *outs[2], *outs[3])
```

```python
import math
from typing import Callable, NamedTuple

import jax
import jax.numpy as jnp
from jax import lax
from jax.experimental import pallas as pl
from jax.experimental.pallas import tpu as pltpu

F32 = jnp.float32
BF16 = jnp.bfloat16
MESH = pl.DeviceIdType.MESH

N_DEV = 8
N_CHIPS = 4
HEAD_DIM = 64
BLOCK = 128
ROPE_DIM = 16
ROPE_HALF = ROPE_DIM // 2
ROPE_THETA = 500000.0
GMLP_GROUPS = 8
NORM_EPS = 1e-5
LN_EPS = 1e-5
ATTN_SCALE = HEAD_DIM ** -0.5
LANES = 128
SUBLANES = 8
VMEM_LIMIT = 48 * 1024 * 1024
VMEM_LIMIT_WIDE = 56 * 1024 * 1024
DOT_COLS = 1024
SEGMENT_SPLIT = 4

ADAM_LR = 0.001
ADAM_B1 = 0.9
ADAM_B2 = 0.999
ADAM_EPS = 1e-08
ADAM_WD = 0.01
ADAM_STEP = 10

GELU_C = math.sqrt(2.0 / math.pi)
GELU_K = 0.044715

HBM_SPEC = pl.BlockSpec(memory_space=pltpu.HBM)
ANY_SPEC = pl.BlockSpec(memory_space=pl.ANY)
SEM_SPEC = pl.BlockSpec(memory_space=pltpu.SEMAPHORE)
VMEM_SPEC = pl.BlockSpec(memory_space=pltpu.VMEM)
SMEM_SPEC = pl.BlockSpec(memory_space=pltpu.SMEM)


def _sds(shape, dtype):
    return jax.ShapeDtypeStruct(shape, dtype)


def _params(*sem, vmem=VMEM_LIMIT):
    return pltpu.CompilerParams(dimension_semantics=sem or None, vmem_limit_bytes=vmem)


def _gelu(x):
    return 0.5 * x * (1.0 + jnp.tanh(GELU_C * (x + GELU_K * x * x * x)))


def _gelu_grad(x):
    t = jnp.tanh(GELU_C * (x + GELU_K * x * x * x))
    return 0.5 * (1.0 + t) + 0.5 * x * (1.0 - t * t) * GELU_C * (1.0 + 3.0 * GELU_K * x * x)


def _silu_and_grad(x):
    s = jax.nn.sigmoid(x)
    return x * s, s * (1.0 + x * (1.0 - s))


def _adamw(w, g, m, v):
    m = ADAM_B1 * m + (1.0 - ADAM_B1) * g
    v = ADAM_B2 * v + (1.0 - ADAM_B2) * (g * g)
    m_hat = m / (1.0 - ADAM_B1 ** ADAM_STEP)
    v_hat = v / (1.0 - ADAM_B2 ** ADAM_STEP)
    delta = -ADAM_LR * (m_hat / (jnp.sqrt(v_hat) + ADAM_EPS) + ADAM_WD * w)
    return delta, m, v


def _mesh_pos():
    return lax.axis_index("x"), lax.axis_index("y"), lax.axis_index("c")


def _slot(x, y, c):
    return 4 * x + 2 * y + c


def _chip(x, y):
    return 2 * x + y


def _sibling(x, y, c):
    return (x, y, 1 - c)


_OTHER_CHIPS = (lambda x, y: (1 - x, y), lambda x, y: (x, 1 - y), lambda x, y: (1 - x, 1 - y))


class _Copy(NamedTuple):
    src: int
    src_slot: Callable
    dst: int
    dst_slot: Callable
    peer: Callable


def _descriptor(refs, send_sems, recv_sems, k, cp):
    pos = _mesh_pos()
    return pltpu.make_async_remote_copy(
        src_ref=refs[cp.src].at[cp.src_slot(*pos)], dst_ref=refs[cp.dst].at[cp.dst_slot(*pos)],
        send_sem=send_sems.at[k], recv_sem=recv_sems.at[k], device_id=cp.peer(*pos), device_id_type=MESH)


def _gather_first_copies(n_arrays):
    copies = []
    for a in range(n_arrays):
        copies.append(_Copy(a, _slot, a, _slot, _sibling))
        for chip in _OTHER_CHIPS:
            copies.append(_Copy(a, _slot, a, _slot, lambda x, y, c, chip=chip: (*chip(x, y), c)))
    return copies


def _gather_pass_copies(n_arrays):
    copies = []
    for a in range(n_arrays):
        for chip in _OTHER_CHIPS:
            src = lambda x, y, c, chip=chip: _slot(*chip(x, y), c)
            copies.append(_Copy(a, src, a, src, _sibling))
    return copies


def _pair_copies(n_sets):
    copies = []
    for a in range(n_sets):
        for q in range(N_CHIPS):
            copies.append(_Copy(2 * a, lambda x, y, c, q=q: q, 2 * a + 1, lambda x, y, c, q=q: q, _sibling))
    return copies


def _pair_copies_strided(n_sets):
    copies = []
    for a in range(n_sets):
        for q in range(N_CHIPS):
            copies.append(_Copy(2 * a, lambda x, y, c, q=q: 2 * q + 1 - c, 2 * a + 1, lambda x, y, c, q=q: q, _sibling))
    return copies


def _chip_sum_copies(n_sets):
    copies = []
    for a in range(n_sets):
        for k, chip in enumerate(_OTHER_CHIPS):
            copies.append(_Copy(2 * a, lambda x, y, c, chip=chip: _chip(*chip(x, y)), 2 * a + 1,
                                lambda x, y, c, k=k: k, lambda x, y, c, chip=chip: (*chip(x, y), c)))
    return copies


def _rdma_start(name, arrays, copies):
    n, nc = len(arrays), len(copies)

    def body(*refs):
        in_refs = refs[:n]
        send_sems, recv_sems = refs[n], refs[n + 1]
        token = refs[2 * n + 2]
        for k, cp in enumerate(copies):
            _descriptor(in_refs, send_sems, recv_sems, k, cp).start()
        token[...] = jnp.zeros_like(token)

    out = pl.pallas_call(
        body, name=name,
        out_shape=(pltpu.SemaphoreType.DMA((nc,)), pltpu.SemaphoreType.DMA((nc,)),
                   *[pltpu.HBM(a.shape, a.dtype) for a in arrays], _sds((SUBLANES, LANES), F32)),
        in_specs=[HBM_SPEC] * n, out_specs=(SEM_SPEC, SEM_SPEC, *([HBM_SPEC] * n), VMEM_SPEC),
        input_output_aliases={i: i + 2 for i in range(n)},
        compiler_params=pltpu.CompilerParams(has_side_effects=pltpu.SideEffectType.DATAFLOW_SIDE_EFFECTING),
    )(*[pltpu.with_memory_space_constraint(a, pltpu.HBM) for a in arrays])
    return out[0], out[1], list(out[2:2 + n]), out[2 + n]


def _rdma_wait(name, arrays, send_sems, recv_sems, copies, after):
    n = len(arrays)

    def body(*refs):
        in_refs = refs[:n]
        send_ref, recv_ref = refs[n], refs[n + 1]
        for k, cp in enumerate(copies):
            d = _descriptor(in_refs, send_ref, recv_ref, k, cp)
            d.wait_send()
            d.wait_recv()

    out = pl.pallas_call(
        body, name=name, out_shape=tuple(pltpu.HBM(a.shape, a.dtype) for a in arrays),
        in_specs=[HBM_SPEC] * n + [SEM_SPEC, SEM_SPEC, ANY_SPEC], out_specs=tuple([HBM_SPEC] * n),
        input_output_aliases={i: i for i in range(n)},
        compiler_params=pltpu.CompilerParams(has_side_effects=pltpu.SideEffectType.DATAFLOW_SIDE_EFFECTING),
    )(*arrays, send_sems, recv_sems, after)
    return list(out)


def _cast_into_slot(w, pos, name):
    rows, cols = w.shape
    tr = min(rows, 256)

    def body(pos_ref, w_ref, o_ref):
        o_ref[...] = w_ref[...].astype(BF16)

    return pl.pallas_call(
        body, name=name,
        grid_spec=pltpu.PrefetchScalarGridSpec(
            num_scalar_prefetch=1, grid=(rows // tr,),
            in_specs=[pl.BlockSpec((tr, cols), lambda i, p: (i, 0))],
            out_specs=pl.BlockSpec((None, tr, cols), lambda i, p: (_slot(p[0], p[1], p[2]), i, 0))),
        out_shape=_sds((N_DEV, rows, cols), BF16), compiler_params=_params("parallel"),
    )(pos, w)


def _all_gather_slots(arrays, name):
    n = len(arrays)
    first, passed = _gather_first_copies(n), _gather_pass_copies(n)

    def body(*refs):
        in_refs = refs[:n]
        send_sems, recv_sems = refs[2 * n], refs[2 * n + 1]
        nf = len(first)
        for k, cp in enumerate(first):
            _descriptor(in_refs, send_sems, recv_sems, k, cp).start()
        for j, cp in enumerate(passed):
            a, rel = divmod(j, 3)
            _descriptor(in_refs, send_sems, recv_sems, 4 * a + 1 + rel, first[4 * a + 1 + rel]).wait_recv()
            _descriptor(in_refs, send_sems, recv_sems, nf + j, cp).start()
        for a in range(n):
            _descriptor(in_refs, send_sems, recv_sems, 4 * a, first[4 * a]).wait_recv()
        for j, cp in enumerate(passed):
            _descriptor(in_refs, send_sems, recv_sems, nf + j, cp).wait_recv()
        for k, cp in enumerate(first + passed):
            _descriptor(in_refs, send_sems, recv_sems, k, cp).wait_send()

    nsem = len(first) + len(passed)
    out = pl.pallas_call(
        body, name=name, out_shape=tuple(_sds(a.shape, a.dtype) for a in arrays),
        in_specs=[ANY_SPEC] * n, out_specs=tuple([ANY_SPEC] * n),
        input_output_aliases={i: i for i in range(n)},
        scratch_shapes=[pltpu.SemaphoreType.DMA((nsem,)), pltpu.SemaphoreType.DMA((nsem,))],
    )(*arrays)
    return list(out)


def _pair_sum(g, land, pos, name):
    _, rows, cols = land.shape
    tr = min(rows, 128)
    strided = g.shape[0] == N_DEV

    def body(pos_ref, g_ref, l_ref, o_ref):
        o_ref[...] = (g_ref[...].astype(F32) + l_ref[...].astype(F32)).astype(BF16)

    g_map = (lambda q, i, p: (2 * q + p[2], i, 0)) if strided else (lambda q, i, p: (q, i, 0))
    blk = pl.BlockSpec((None, tr, cols), lambda q, i, p: (q, i, 0))
    return pl.pallas_call(
        body, name=name,
        grid_spec=pltpu.PrefetchScalarGridSpec(
            num_scalar_prefetch=1, grid=(N_CHIPS, rows // tr),
            in_specs=[pl.BlockSpec((None, tr, cols), g_map), blk], out_specs=blk),
        out_shape=_sds((N_CHIPS, rows, cols), BF16), compiler_params=_params("parallel", "parallel"),
    )(pos, g, land)


def _reduce_adamw(sums, land, w, m, v, pos, name):
    rows, cols = w.shape
    tr = min(rows, 64)

    def body(pos_ref, s_ref, l_ref, w_ref, m_ref, v_ref, g_ref, d_ref, nm_ref, nv_ref):
        g = s_ref[...].astype(F32)
        for k in range(N_CHIPS - 1):
            g = g + l_ref[k].astype(F32)
        delta, nm, nv = _adamw(w_ref[...], g, m_ref[...], v_ref[...])
        g_ref[...] = g
        d_ref[...] = delta
        nm_ref[...] = nm
        nv_ref[...] = nv

    spec = pl.BlockSpec((tr, cols), lambda i, p: (i, 0))
    return pl.pallas_call(
        body, name=name,
        grid_spec=pltpu.PrefetchScalarGridSpec(
            num_scalar_prefetch=1, grid=(rows // tr,),
            in_specs=[pl.BlockSpec((None, tr, cols), lambda i, p: (_chip(p[0], p[1]), i, 0)),
                      pl.BlockSpec((N_CHIPS - 1, tr, cols), lambda i, p: (0, i, 0)), spec, spec, spec],
            out_specs=(spec, spec, spec, spec)),
        out_shape=tuple([_sds((rows, cols), F32)] * 4), compiler_params=_params("parallel"),
    )(pos, sums, land, w, m, v)


def _small_allreduce_adamw(g, w, m, v):
    rows = g.shape[0]

    def body(g_ref, w_ref, m_ref, v_ref, gs_ref, d_ref, nm_ref, nv_ref, all_ref, send_sems, recv_sems):
        x, y, c = _mesh_pos()
        me = _slot(x, y, c)
        copies = []
        for k in range(1, N_DEV):
            peer = (x ^ (k >> 2), y ^ ((k >> 1) & 1), c ^ (k & 1))
            copies.append(pltpu.make_async_remote_copy(
                src_ref=g_ref, dst_ref=all_ref.at[me], send_sem=send_sems.at[k - 1],
                recv_sem=recv_sems.at[k - 1], device_id=peer, device_id_type=MESH))
        for cp in copies:
            cp.start()
        all_ref[me] = g_ref[...]
        for cp in copies:
            cp.wait_recv()
        total = all_ref[0]
        for s in range(1, N_DEV):
            total = total + all_ref[s]
        delta, nm, nv = _adamw(w_ref[...], total, m_ref[...], v_ref[...])
        gs_ref[...] = total
        d_ref[...] = delta
        nm_ref[...] = nm
        nv_ref[...] = nv
        for cp in copies:
            cp.wait_send()

    return pl.pallas_call(
        body, name="small_allreduce_adamw", out_shape=tuple([_sds((rows, LANES), F32)] * 4),
        in_specs=[VMEM_SPEC] * 4, out_specs=tuple([VMEM_SPEC] * 4),
        scratch_shapes=[pltpu.VMEM((N_DEV, rows, LANES), F32), pltpu.SemaphoreType.DMA((7,)),
                        pltpu.SemaphoreType.DMA((7,))],
    )(g, w, m, v)


_DOT_DIMS = {"nn": ((1,), (0,)), "nt": ((1,), (1,)), "tn": ((0,), (0,))}


def _dot(a, b, mode):
    return lax.dot_general(a, b, (_DOT_DIMS[mode], ((), ())), preferred_element_type=F32)


def _col_chunks(cols):
    return [(c0, min(c0 + DOT_COLS, cols)) for c0 in range(0, cols, DOT_COLS)]


def _matmul(a, b, mode, out_dtype, name, *, res=None, tm=2048, tn=1024, tk=512):
    if mode == "tn":
        kdim, mdim = a.shape
    else:
        mdim, kdim = a.shape
    ndim = b.shape[0] if mode == "nt" else b.shape[1]
    tm, tn, tk = min(tm, mdim), min(tn, ndim), min(tk, kdim)
    assert mdim % tm == 0 and ndim % tn == 0 and kdim % tk == 0, (name, mdim, ndim, kdim)
    nk = kdim // tk

    def body(*refs):
        if res is None:
            a_ref, b_ref, o_ref, acc_ref = refs
        else:
            a_ref, b_ref, r_ref, o_ref, acc_ref = refs
        k = pl.program_id(2)

        @pl.when(k == 0)
        def _():
            acc_ref[...] = jnp.zeros_like(acc_ref)

        acc_ref[...] += _dot(a_ref[...], b_ref[...], mode)

        @pl.when(k == nk - 1)
        def _():
            out = acc_ref[...]
            if res is not None:
                out = out + r_ref[...]
            o_ref[...] = out.astype(out_dtype)

    if mode == "tn":
        a_spec = pl.BlockSpec((tk, tm), lambda i, j, k: (k, i))
    else:
        a_spec = pl.BlockSpec((tm, tk), lambda i, j, k: (i, k))
    if mode == "nt":
        b_spec = pl.BlockSpec((tn, tk), lambda i, j, k: (j, k))
    else:
        b_spec = pl.BlockSpec((tk, tn), lambda i, j, k: (k, j))
    o_spec = pl.BlockSpec((tm, tn), lambda i, j, k: (i, j))
    in_specs, args = [a_spec, b_spec], [a, b]
    if res is not None:
        in_specs.append(o_spec)
        args.append(res)
    return pl.pallas_call(
        body, name=name, grid=(mdim // tm, ndim // tn, nk), in_specs=in_specs, out_specs=o_spec,
        out_shape=_sds((mdim, ndim), out_dtype), scratch_shapes=[pltpu.VMEM((tm, tn), F32)],
        compiler_params=_params("parallel", "parallel", "arbitrary"),
    )(*args)


def _project(h, w_blocks, block_ids, name, *, proj=None, tm=1024, tk=512):
    s, d = h.shape
    _, _, cw = w_blocks.shape
    n = block_ids.shape[0]
    tm, tk = min(tm, s), min(tk, d)

    def body(ids_ref, h_ref, w_ref, *rest):
        o_ref = rest[-1]
        k = pl.program_id(2)

        @pl.when(k == 0)
        def _():
            o_ref[...] = jnp.zeros_like(o_ref)

        for c0, c1 in _col_chunks(cw):
            o_ref[:, c0:c1] += _dot(h_ref[...], w_ref[:, c0:c1], "nn")

    in_specs = [pl.BlockSpec((tm, tk), lambda j, i, k, ids: (i, k)),
                pl.BlockSpec((None, tk, cw), lambda j, i, k, ids: (ids[j], k, 0))]
    args = [block_ids, h, w_blocks]
    aliases = {}
    if proj is not None:
        in_specs.append(ANY_SPEC)
        args.append(proj)
        aliases = {3: 0}
    return pl.pallas_call(
        body, name=name,
        grid_spec=pltpu.PrefetchScalarGridSpec(
            num_scalar_prefetch=1, grid=(n, s // tm, d // tk), in_specs=in_specs,
            out_specs=pl.BlockSpec((tm, cw), lambda j, i, k, ids: (i, ids[j]))),
        out_shape=_sds((s, N_DEV * cw), F32), input_output_aliases=aliases,
        compiler_params=_params("arbitrary", "arbitrary", "arbitrary", vmem=VMEM_LIMIT_WIDE),
    )(*args)


def _grad_w_in_blocks(h, d_proj, block_ids, cw, name, after=None, *, tm=1024, tk=512):
    s, d = h.shape
    n = block_ids.shape[0]
    tm, tk = min(tm, d), min(tk, s)
    nk = s // tk

    def body(ids_ref, h_ref, g_ref, *rest):
        o_ref, acc_ref = rest[-2], rest[-1]
        k = pl.program_id(2)

        @pl.when(k == 0)
        def _():
            acc_ref[...] = jnp.zeros_like(acc_ref)

        for c0, c1 in _col_chunks(cw):
            acc_ref[:, c0:c1] += _dot(h_ref[...], g_ref[:, c0:c1], "tn")

        @pl.when(k == nk - 1)
        def _():
            o_ref[...] = acc_ref[...].astype(BF16)

    in_specs = [pl.BlockSpec((tk, tm), lambda q, i, k, ids: (k, i)),
                pl.BlockSpec((tk, cw), lambda q, i, k, ids: (k, ids[q]))]
    args = [block_ids, h, d_proj]
    if after is not None:
        in_specs.append(ANY_SPEC)
        args.append(after)
    return pl.pallas_call(
        body, name=name,
        grid_spec=pltpu.PrefetchScalarGridSpec(
            num_scalar_prefetch=1, grid=(n, d // tm, nk), in_specs=in_specs,
            out_specs=pl.BlockSpec((None, tm, cw), lambda q, i, k, ids: (q, i, 0)),
            scratch_shapes=[pltpu.VMEM((tm, cw), F32)]),
        out_shape=_sds((n, d, cw), BF16),
        compiler_params=_params("parallel", "parallel", "arbitrary", vmem=VMEM_LIMIT_WIDE),
    )(*args)


def _d_hidden(d_proj, w_blocks, after=None, *, tm=1024, tn=1024):
    s = d_proj.shape[0]
    nb, d, cw = w_blocks.shape
    tm, tn = min(tm, s), min(tn, d)

    def body(g_ref, w_ref, *rest):
        o_ref = rest[-1]
        k = pl.program_id(2)

        @pl.when(k == 0)
        def _():
            o_ref[...] = jnp.zeros_like(o_ref)

        o_ref[...] += _dot(g_ref[...], w_ref[...], "nt")

    in_specs = [pl.BlockSpec((tm, cw), lambda i, j, k: (i, k)),
                pl.BlockSpec((None, tn, cw), lambda i, j, k: (k, j, 0))]
    args = [d_proj, w_blocks]
    if after is not None:
        in_specs.append(ANY_SPEC)
        args.append(after)
    return pl.pallas_call(
        body, name="d_h", grid=(s // tm, d // tn, nb), in_specs=in_specs,
        out_specs=pl.BlockSpec((tm, tn), lambda i, j, k: (i, j)), out_shape=_sds((s, d), F32),
        compiler_params=_params("parallel", "parallel", "arbitrary", vmem=VMEM_LIMIT_WIDE),
    )(*args)


def _row_tile(rows):
    return min(rows, 128)


def _segment_specs(rows, d, col0):
    w = d // SEGMENT_SPLIT
    assert col0 % w == 0
    return [pl.BlockSpec((rows, w), lambda i, t=t: (i, col0 // w + t)) for t in range(SEGMENT_SPLIT)]


def _cat(refs):
    return jnp.concatenate([r[...] for r in refs], axis=1)


def _rmsnorm_fwd(x, g, after):
    s, d = x.shape
    tr = _row_tile(s)

    def body(x_ref, g_ref, after_ref, h_ref):
        xv = x_ref[...]
        r = lax.rsqrt(jnp.mean(xv * xv, axis=-1, keepdims=True) + NORM_EPS)
        h_ref[...] = (xv * r * g_ref[...]).astype(BF16)

    row = pl.BlockSpec((tr, d), lambda i: (i, 0))
    vec = pl.BlockSpec((1, d), lambda i: (0, 0))
    return pl.pallas_call(body, name="rmsnorm_fwd", grid=(s // tr,), in_specs=[row, vec, ANY_SPEC], out_specs=row,
                          out_shape=_sds((s, d), BF16), compiler_params=_params("parallel"))(x, g, after)


def _merge_fwd(y_a, y_b, proj, col_m):
    s, d = y_a.shape
    tr = _row_tile(s)
    ns = SEGMENT_SPLIT

    def body(ya_ref, yb_ref, *rest):
        ma, mb, o_ref = _cat(rest[:ns]), _cat(rest[ns:2 * ns]), rest[2 * ns]
        o_ref[...] = (jax.nn.sigmoid(ma) * ya_ref[...] + jax.nn.sigmoid(mb) * yb_ref[...]).astype(BF16)

    row = pl.BlockSpec((tr, d), lambda i: (i, 0))
    return pl.pallas_call(
        body, name="merge_fwd", grid=(s // tr,),
        in_specs=[row, row, *_segment_specs(tr, d, col_m), *_segment_specs(tr, d, col_m + d)],
        out_specs=row, out_shape=_sds((s, d), BF16), compiler_params=_params("parallel"),
    )(y_a, y_b, *([proj] * (2 * ns)))


def _loss_and_final_norm_bwd(x2, target, g):
    s, d = x2.shape
    tr = _row_tile(s)

    def body(x_ref, t_ref, g_ref, loss_ref, dg_ref, dx_ref, dxb_ref):
        @pl.when(pl.program_id(0) == 0)
        def _():
            loss_ref[...] = jnp.zeros_like(loss_ref)
            dg_ref[...] = jnp.zeros_like(dg_ref)

        xv, gv = x_ref[...], g_ref[...]
        r = lax.rsqrt(jnp.mean(xv * xv, axis=-1, keepdims=True) + NORM_EPS)
        xhat = xv * r
        err = xhat * gv - t_ref[...]
        loss_ref[...] += 0.5 * jnp.sum(jnp.mean(err * err, axis=-1, keepdims=True))
        dy = err / d
        dg_ref[...] += jnp.sum(dy * xhat, axis=0, keepdims=True)
        dyg = dy * gv
        dx = r * (dyg - xhat * jnp.mean(dyg * xhat, axis=-1, keepdims=True))
        dx_ref[...] = dx
        dxb_ref[...] = dx.astype(BF16)

    row = pl.BlockSpec((tr, d), lambda i: (i, 0))
    vec = pl.BlockSpec((1, d), lambda i: (0, 0))
    return pl.pallas_call(
        body, name="loss_final_norm_bwd", grid=(s // tr,), in_specs=[row, row, vec],
        out_specs=(pl.BlockSpec((SUBLANES, LANES), lambda i: (0, 0)), vec, row, row),
        out_shape=(_sds((SUBLANES, LANES), F32), _sds((1, d), F32), _sds((s, d), F32), _sds((s, d), BF16)),
        compiler_params=_params("arbitrary"))(x2, target, g)


def _merge_bwd(d_merged, y_a, y_b, proj, col_m):
    s, d = y_a.shape
    tr = _row_tile(s)
    ns = SEGMENT_SPLIT

    def body(dm_ref, ya_ref, yb_ref, *rest):
        ma, mb = _cat(rest[:ns]), _cat(rest[ns:2 * ns])
        dya_ref, dyb_ref, dmg_ref = rest[2 * ns:]
        dm = dm_ref[...]
        sa = jax.nn.sigmoid(ma)
        sb = jax.nn.sigmoid(mb)
        dya_ref[...] = (dm * sa).astype(BF16)
        dyb_ref[...] = (dm * sb).astype(BF16)
        dmg_ref[:, :d] = (dm * ya_ref[...] * (sa * (1.0 - sa))).astype(BF16)
        dmg_ref[:, d:] = (dm * yb_ref[...] * (sb * (1.0 - sb))).astype(BF16)

    row = pl.BlockSpec((tr, d), lambda i: (i, 0))
    wide = pl.BlockSpec((tr, 2 * d), lambda i: (i, 0))
    return pl.pallas_call(
        body, name="merge_bwd", grid=(s // tr,),
        in_specs=[row, row, row, *_segment_specs(tr, d, col_m), *_segment_specs(tr, d, col_m + d)],
        out_specs=(row, row, wide),
        out_shape=(_sds((s, d), BF16), _sds((s, d), BF16), _sds((s, 2 * d), BF16)),
        compiler_params=_params("parallel"))(d_merged, y_a, y_b, *([proj] * (2 * ns)))


def _input_grad(d_h, x, g, dx2):
    s, d = x.shape
    tr = _row_tile(s)

    def body(dh_ref, x_ref, g_ref, dx2_ref, gx_ref, dg_ref):
        @pl.when(pl.program_id(0) == 0)
        def _():
            dg_ref[...] = jnp.zeros_like(dg_ref)

        xv, dh = x_ref[...], dh_ref[...]
        r = lax.rsqrt(jnp.mean(xv * xv, axis=-1, keepdims=True) + NORM_EPS)
        xhat = xv * r
        dg_ref[...] += jnp.sum(dh * xhat, axis=0, keepdims=True)
        dyg = dh * g_ref[...]
        gx_ref[...] = dx2_ref[...] + r * (dyg - xhat * jnp.mean(dyg * xhat, axis=-1, keepdims=True))

    row = pl.BlockSpec((tr, d), lambda i: (i, 0))
    vec = pl.BlockSpec((1, d), lambda i: (0, 0))
    return pl.pallas_call(
        body, name="input_grad", grid=(s // tr,), in_specs=[row, row, vec, row], out_specs=(row, vec),
        out_shape=(_sds((s, d), F32), _sds((1, d), F32)), compiler_params=_params("arbitrary"))(d_h, x, g, dx2)


def _rope_tables(positions):
    inv_freq = ROPE_THETA ** (-jnp.arange(ROPE_HALF, dtype=F32) * 2.0 / ROPE_DIM)
    ang = positions.astype(F32)[:, None] * inv_freq
    cos, sin = jnp.cos(ang), jnp.sin(ang)
    zero = jnp.zeros((positions.shape[0], HEAD_DIM - ROPE_DIM), F32)
    zero_h = jnp.zeros_like(sin)
    c = jnp.concatenate([cos, cos, zero + 1.0], axis=1)
    up = jnp.concatenate([-sin, zero_h, zero], axis=1)
    down = jnp.concatenate([zero_h, sin, zero], axis=1)
    reps = LANES // HEAD_DIM
    return jnp.stack([jnp.tile(c, (1, reps)), jnp.tile(up, (1, reps)), jnp.tile(down, (1, reps))])


def _lane_tiles(x):
    return [x[:, t * LANES:(t + 1) * LANES] for t in range(x.shape[1] // LANES)]


def _rope(x, tab):
    out = [xt * tab[0] + pltpu.roll(xt, LANES - ROPE_HALF, 1) * tab[1] + pltpu.roll(xt, ROPE_HALF, 1) * tab[2]
           for xt in _lane_tiles(x)]
    return out[0] if len(out) == 1 else jnp.concatenate(out, axis=1)


def _rope_bwd(g, tab):
    out = [gt * tab[0] + pltpu.roll(gt * tab[1], ROPE_HALF, 1) + pltpu.roll(gt * tab[2], LANES - ROPE_HALF, 1)
           for gt in _lane_tiles(g)]
    return out[0] if len(out) == 1 else jnp.concatenate(out, axis=1)


def _head(x, h):
    return x[:, h * HEAD_DIM:(h + 1) * HEAD_DIM]


def _stack_heads(x, first, count):
    return jnp.concatenate([_head(x, first + h) for h in range(count)], axis=0)


def _band_scores(q, k_prev, k_cur, sink_ref, pair, a, group, blk):
    kb = jnp.concatenate([_head(k_prev, a), _head(k_cur, a)], axis=0).astype(BF16)
    qs = _stack_heads(q, a * group, group).astype(BF16)
    s = lax.dot_general(qs, kb, (((1,), (1,)), ((), ())), preferred_element_type=F32) * ATTN_SCALE
    qi = lax.broadcasted_iota(jnp.int32, s.shape, 0) % BLOCK
    si = lax.broadcasted_iota(jnp.int32, s.shape, 1)
    mask = (si <= qi + BLOCK) & (si > qi) & ((blk > 0) | (si >= BLOCK))
    s = jnp.where(mask, s, -jnp.inf)
    sink = jnp.concatenate(
        [jnp.full((BLOCK, 1), sink_ref[(2 * pair + a) * group + h], F32) for h in range(group)], axis=0)
    m = jnp.maximum(jnp.max(s, axis=-1, keepdims=True), sink)
    p = jnp.exp(s - m)
    p_sink = jnp.exp(sink - m)
    denom = jnp.sum(p, axis=-1, keepdims=True) + p_sink
    return qs, kb, p / denom, p_sink / denom


def _attn_dims(s, d, kv):
    n_kv = kv // HEAD_DIM
    group = d // kv
    qw = 2 * group * HEAD_DIM
    assert n_kv % 2 == 0 and (d + 2 * kv) % qw == 0 and s % BLOCK == 0
    return group, qw, n_kv // 2, s // BLOCK


def _attention_fwd(proj, tables, sink, kv, after):
    s, d = proj.shape[0], sink.shape[0] * HEAD_DIM
    group, qw, n_pairs, nb = _attn_dims(s, d, kv)

    def body(sink_ref, q_ref, kc_ref, kp_ref, vc_ref, vp_ref, ga_ref, tc_ref, tp_ref, after_ref, attn_ref, ain_ref):
        pair, blk = pl.program_id(0), pl.program_id(1)
        tab_c, tab_p = tc_ref[...], tp_ref[...]
        q = _rope(q_ref[...], tab_c)
        k_cur, k_prev = _rope(kc_ref[...], tab_c), _rope(kp_ref[...], tab_p)
        v_cur, v_prev = vc_ref[...], vp_ref[...]
        outs = []
        for a in range(2):
            _, _, p, _ = _band_scores(q, k_prev, k_cur, sink_ref, pair, a, group, blk)
            vb = jnp.concatenate([_head(v_prev, a), _head(v_cur, a)], axis=0).astype(BF16)
            o = jnp.dot(p.astype(BF16), vb, preferred_element_type=F32)
            outs += [o[h * BLOCK:(h + 1) * BLOCK] for h in range(group)]
        attn = jnp.concatenate(outs, axis=1)
        attn_ref[...] = attn
        silu, _ = _silu_and_grad(ga_ref[...])
        ain_ref[...] = (attn * silu).astype(BF16)

    k0, v0, g0 = d // LANES, (d + kv) // LANES, (d + 2 * kv) // qw
    prev = lambda i: jnp.maximum(i - 1, 0)
    in_specs = [
        SMEM_SPEC,
        pl.BlockSpec((BLOCK, qw), lambda p, i: (i, p)),
        pl.BlockSpec((BLOCK, LANES), lambda p, i: (i, k0 + p)),
        pl.BlockSpec((BLOCK, LANES), lambda p, i: (prev(i), k0 + p)),
        pl.BlockSpec((BLOCK, LANES), lambda p, i: (i, v0 + p)),
        pl.BlockSpec((BLOCK, LANES), lambda p, i: (prev(i), v0 + p)),
        pl.BlockSpec((BLOCK, qw), lambda p, i: (i, g0 + p)),
        pl.BlockSpec((3, BLOCK, LANES), lambda p, i: (0, i, 0)),
        pl.BlockSpec((3, BLOCK, LANES), lambda p, i: (0, prev(i), 0)),
        ANY_SPEC,
    ]
    out = pl.BlockSpec((BLOCK, qw), lambda p, i: (i, p))
    return pl.pallas_call(
        body, name="attention_fwd", grid=(n_pairs, nb), in_specs=in_specs, out_specs=(out, out),
        out_shape=(_sds((s, d), F32), _sds((s, d), BF16)), compiler_params=_params("parallel", "parallel"),
    )(sink, proj, proj, proj, proj, proj, proj, tables, tables, after)


def _attention_bwd(proj, tables, sink, kv, attn, d_ain, after):
    s, d = proj.shape[0], sink.shape[0] * HEAD_DIM
    group, qw, n_pairs, nb = _attn_dims(s, d, kv)

    def body(sink_ref, q_ref, kc_ref, kp_ref, vc_ref, vp_ref, ga_ref, tc_ref, tp_ref, attn_ref, dain_ref, after_ref,
             dq_ref, dk_ref, dv_ref, dga_ref, dsink_ref, carry_k, carry_v):
        pair, blk = pl.program_id(0), pl.program_id(1)

        @pl.when(blk == 0)
        def _():
            carry_k[...] = jnp.zeros_like(carry_k)
            carry_v[...] = jnp.zeros_like(carry_v)
            dsink_ref[...] = jnp.zeros_like(dsink_ref)

        @pl.when(blk < nb)
        def _():
            tab_c, tab_p = tc_ref[...], tp_ref[...]
            q = _rope(q_ref[...], tab_c)
            k_cur, k_prev = _rope(kc_ref[...], tab_c), _rope(kp_ref[...], tab_p)
            v_cur, v_prev = vc_ref[...], vp_ref[...]
            silu, silu_grad = _silu_and_grad(ga_ref[...])
            d_ain_v = dain_ref[...]
            dga_ref[...] = (d_ain_v * attn_ref[...] * silu_grad).astype(BF16)
            d_attn = d_ain_v * silu
            dq_parts, dk_parts, dv_parts = [], [], []
            lane = lax.broadcasted_iota(jnp.int32, (1, LANES), 1)
            dsink = jnp.zeros((1, LANES), F32)
            for a in range(2):
                qs, kb, p, p_sink = _band_scores(q, k_prev, k_cur, sink_ref, pair, a, group, blk)
                vb = jnp.concatenate([_head(v_prev, a), _head(v_cur, a)], axis=0).astype(BF16)
                do = _stack_heads(d_attn, a * group, group).astype(BF16)
                dp = lax.dot_general(do, vb, (((1,), (1,)), ((), ())), preferred_element_type=F32)
                delta = jnp.sum(p * dp, axis=-1, keepdims=True)
                ds = ((p * (dp - delta)) * ATTN_SCALE).astype(BF16)
                dqs = jnp.dot(ds, kb, preferred_element_type=F32)
                dq_parts += [dqs[h * BLOCK:(h + 1) * BLOCK] for h in range(group)]
                dk_parts.append(lax.dot_general(ds, qs, (((0,), (0,)), ((), ())), preferred_element_type=F32))
                dv_parts.append(lax.dot_general(p.astype(BF16), do, (((0,), (0,)), ((), ())),
                                                preferred_element_type=F32))
                ds_sink = -(p_sink * delta)
                for h in range(group):
                    dsink = dsink + jnp.where(lane == a * group + h,
                                              jnp.sum(ds_sink[h * BLOCK:(h + 1) * BLOCK]), 0.0)
            dsink_ref[0] += dsink
            dq_ref[...] = _rope_bwd(jnp.concatenate(dq_parts, axis=1), tab_c).astype(BF16)
            dk_band = jnp.concatenate(dk_parts, axis=1)
            dv_band = jnp.concatenate(dv_parts, axis=1)
            dk_ref[...] = (carry_k[...] + _rope_bwd(dk_band[:BLOCK], tab_p)).astype(BF16)
            dv_ref[...] = (carry_v[...] + dv_band[:BLOCK]).astype(BF16)
            carry_k[...] = _rope_bwd(dk_band[BLOCK:], tab_c)
            carry_v[...] = dv_band[BLOCK:]

        @pl.when(blk == nb)
        def _():
            dk_ref[...] = carry_k[...].astype(BF16)
            dv_ref[...] = carry_v[...].astype(BF16)

    k0, v0, g0 = d // LANES, (d + kv) // LANES, (d + 2 * kv) // qw
    cur = lambda i: jnp.minimum(i, nb - 1)
    prev = lambda i: jnp.maximum(cur(i) - 1, 0)
    back = lambda i: jnp.maximum(i - 1, 0)
    q_spec = pl.BlockSpec((BLOCK, qw), lambda p, i: (cur(i), p))
    in_specs = [
        SMEM_SPEC,
        q_spec,
        pl.BlockSpec((BLOCK, LANES), lambda p, i: (cur(i), k0 + p)),
        pl.BlockSpec((BLOCK, LANES), lambda p, i: (prev(i), k0 + p)),
        pl.BlockSpec((BLOCK, LANES), lambda p, i: (cur(i), v0 + p)),
        pl.BlockSpec((BLOCK, LANES), lambda p, i: (prev(i), v0 + p)),
        pl.BlockSpec((BLOCK, qw), lambda p, i: (cur(i), g0 + p)),
        pl.BlockSpec((3, BLOCK, LANES), lambda p, i: (0, cur(i), 0)),
        pl.BlockSpec((3, BLOCK, LANES), lambda p, i: (0, prev(i), 0)),
        q_spec,
        q_spec,
        ANY_SPEC,
    ]
    kv_out = pl.BlockSpec((BLOCK, LANES), lambda p, i: (back(i), p))
    return pl.pallas_call(
        body, name="attention_bwd", grid=(n_pairs, nb + 1), in_specs=in_specs,
        out_specs=(q_spec, kv_out, kv_out, q_spec, pl.BlockSpec((1, 1, LANES), lambda p, i: (p, 0, 0))),
        out_shape=(_sds((s, d), BF16), _sds((s, kv), BF16), _sds((s, kv), BF16), _sds((s, d), BF16),
                   _sds((n_pairs, 1, LANES), F32)),
        scratch_shapes=[pltpu.VMEM((BLOCK, LANES), F32), pltpu.VMEM((BLOCK, LANES), F32)],
        compiler_params=_params("parallel", "arbitrary"),
    )(sink, proj, proj, proj, proj, proj, proj, tables, tables, attn, d_ain, after)


def _gmlp_core(u, vg, ln_g, ln_b, w_ref, bias_t):
    gu = _gelu(u)
    gv = _gelu(vg)
    xc = gv - jnp.mean(gv, axis=-1, keepdims=True)
    rstd = lax.rsqrt(jnp.mean(xc * xc, axis=-1, keepdims=True) + LN_EPS)
    xhat = xc * rstd
    vn = (xhat * ln_g + ln_b).astype(BF16)
    gd = u.shape[1] // GMLP_GROUPS
    tri = (lax.broadcasted_iota(jnp.int32, (BLOCK, BLOCK), 0) >= lax.broadcasted_iota(jnp.int32, (BLOCK, BLOCK), 1))
    w_tri = [jnp.where(tri, w_ref[g], 0.0).astype(BF16) for g in range(GMLP_GROUPS)]
    mixed = jnp.concatenate(
        [jnp.dot(w_tri[g], vn[:, g * gd:(g + 1) * gd], preferred_element_type=F32) + bias_t[:, g:g + 1]
         for g in range(GMLP_GROUPS)], axis=1)
    return gu, xhat, rstd, vn, w_tri, tri, mixed


def _whole(shape):
    return pl.BlockSpec(shape, lambda i: tuple(0 for _ in shape))


def _gmlp_fwd(proj, col_u, d, w_s, bias_t, ln_g, ln_b):
    s = proj.shape[0]
    ns = SEGMENT_SPLIT

    def body(*refs):
        u, vg, gb = _cat(refs[:ns]), _cat(refs[ns:2 * ns]), _cat(refs[2 * ns:3 * ns])
        w_ref, bt_ref, lg_ref, lb_ref, o_ref = refs[3 * ns:]
        gu, _, _, _, _, _, mixed = _gmlp_core(u, vg, lg_ref[...], lb_ref[...], w_ref, bt_ref[...])
        silu, _ = _silu_and_grad(gb)
        o_ref[...] = ((gu * mixed) * silu).astype(BF16)

    segs = [sp for j in range(3) for sp in _segment_specs(BLOCK, d, col_u + j * d)]
    return pl.pallas_call(
        body, name="gmlp_fwd", grid=(s // BLOCK,),
        in_specs=[*segs, _whole(w_s.shape), _whole(bias_t.shape), _whole((1, d)), _whole((1, d))],
        out_specs=pl.BlockSpec((BLOCK, d), lambda i: (i, 0)), out_shape=_sds((s, d), BF16),
        compiler_params=_params("parallel"),
    )(*([proj] * (3 * ns)), w_s, bias_t, ln_g, ln_b)


def _gmlp_bwd(proj, col_u, d, w_s, bias_t, ln_g, ln_b, d_bin, after):
    s = proj.shape[0]
    gd = d // GMLP_GROUPS
    ns = SEGMENT_SPLIT

    def body(*refs):
        u, vg, gb = _cat(refs[:ns]), _cat(refs[ns:2 * ns]), _cat(refs[2 * ns:3 * ns])
        w_ref, bt_ref, lg_ref, lb_ref, dbin_ref, after_ref, dg_ref, dw_ref, dbt_ref, dlg_ref, dlb_ref = refs[3 * ns:]

        @pl.when(pl.program_id(0) == 0)
        def _():
            dw_ref[...] = jnp.zeros_like(dw_ref)
            dbt_ref[...] = jnp.zeros_like(dbt_ref)
            dlg_ref[...] = jnp.zeros_like(dlg_ref)
            dlb_ref[...] = jnp.zeros_like(dlb_ref)

        ln_g = lg_ref[...]
        gu, xhat, rstd, vn, w_tri, tri, mixed = _gmlp_core(u, vg, ln_g, lb_ref[...], w_ref, bt_ref[...])
        silu, silu_grad = _silu_and_grad(gb)
        d_bin_v = dbin_ref[...]
        d_sg = d_bin_v * silu
        dg_ref[:, 2 * d:] = (d_bin_v * (gu * mixed) * silu_grad).astype(BF16)
        dg_ref[:, :d] = (d_sg * mixed * _gelu_grad(u)).astype(BF16)
        d_mixed = d_sg * gu
        d_mixed_b = d_mixed.astype(BF16)
        d_vn, d_bias = [], []
        for g in range(GMLP_GROUPS):
            dm_g = d_mixed_b[:, g * gd:(g + 1) * gd]
            d_bias.append(jnp.sum(d_mixed[:, g * gd:(g + 1) * gd], axis=-1, keepdims=True))
            dw = lax.dot_general(dm_g, vn[:, g * gd:(g + 1) * gd], (((1,), (1,)), ((), ())),
                                 preferred_element_type=F32)
            dw_ref[g] += jnp.where(tri, dw, 0.0)
            d_vn.append(lax.dot_general(w_tri[g], dm_g, (((0,), (0,)), ((), ())), preferred_element_type=F32))
        dbt_ref[...] += jnp.concatenate(d_bias, axis=1)
        d_vn = jnp.concatenate(d_vn, axis=1)
        dlg_ref[...] += jnp.sum(d_vn * xhat, axis=0, keepdims=True)
        dlb_ref[...] += jnp.sum(d_vn, axis=0, keepdims=True)
        d_xhat = d_vn * ln_g
        d_gv = rstd * (d_xhat - jnp.mean(d_xhat, axis=-1, keepdims=True)
                       - xhat * jnp.mean(d_xhat * xhat, axis=-1, keepdims=True))
        dg_ref[:, d:2 * d] = (d_gv * _gelu_grad(vg)).astype(BF16)

    segs = [sp for j in range(3) for sp in _segment_specs(BLOCK, d, col_u + j * d)]
    return pl.pallas_call(
        body, name="gmlp_bwd", grid=(s // BLOCK,),
        in_specs=[*segs, _whole(w_s.shape), _whole(bias_t.shape), _whole((1, d)), _whole((1, d)),
                  pl.BlockSpec((BLOCK, d), lambda i: (i, 0)), ANY_SPEC],
        out_specs=(pl.BlockSpec((BLOCK, 3 * d), lambda i: (i, 0)), _whole(w_s.shape), _whole(bias_t.shape),
                   _whole((1, d)), _whole((1, d))),
        out_shape=(_sds((s, 3 * d), BF16), _sds(w_s.shape, F32), _sds(bias_t.shape, F32), _sds((1, d), F32),
                   _sds((1, d), F32)),
        compiler_params=_params("arbitrary"),
    )(*([proj] * (3 * ns)), w_s, bias_t, ln_g, ln_b, d_bin, after)


def _pack(parts):
    rows = []
    tile = SUBLANES * LANES
    for p in parts:
        flat = p.astype(F32).reshape(-1)
        padded = -(-flat.shape[0] // tile) * tile
        rows.append(jnp.pad(flat, (0, padded - flat.shape[0])).reshape(-1, LANES))
    return jnp.concatenate(rows, axis=0)


def _unpack(packed, shapes):
    out, row = [], 0
    tile = SUBLANES * LANES
    for shape in shapes:
        size = math.prod(shape)
        n_rows = -(-size // tile) * SUBLANES
        out.append(packed[row:row + n_rows].reshape(-1)[:size].reshape(shape))
        row += n_rows
    return out


def kernel(x, positions, norm_g, w_in, attn_sink, gmlp_ln_g, gmlp_ln_b, w_spatial, b_spatial, w_up_attn, w_up_gmlp, w_out, final_norm_g, loss_target, m_norm_g, m_w_in, m_attn_sink, m_gmlp_ln_g, m_gmlp_ln_b, m_w_spatial, m_b_spatial, m_w_up_attn, m_w_up_gmlp, m_w_out, m_final_norm_g, v_norm_g, v_w_in, v_attn_sink, v_gmlp_ln_g, v_gmlp_ln_b, v_w_spatial, v_b_spatial, v_w_up_attn, v_w_up_gmlp, v_w_out, v_final_norm_g):
    x2d, target = x[0], loss_target[0]
    s, d = x2d.shape
    n_q_heads = attn_sink.shape[1]
    cw = w_in.shape[2]
    rw = w_up_attn.shape[1]
    kv = (cw * N_DEV - 7 * d) // 2
    col_u, col_m = 2 * d + 2 * kv, 5 * d + 2 * kv
    final_g = final_norm_g.reshape(1, d)
    sink = attn_sink[0]
    w_s = w_spatial[0]
    bias_t = b_spatial[0].T
    mx, my, mc = _mesh_pos()
    pos = jnp.stack([mx, my, mc]).astype(jnp.int32)
    chips = jnp.arange(N_CHIPS, dtype=jnp.int32)

    (w_in_b,) = _all_gather_slots([_cast_into_slot(w_in[0], pos, "cast_w_in")], "gather_w_in")
    squares = [_cast_into_slot(w[0], pos, "cast_" + nm)
               for nm, w in (("w_up_attn", w_up_attn), ("w_up_gmlp", w_up_gmlp), ("w_out", w_out))]
    first = _gather_first_copies(3)
    send1, recv1, squares, token = _rdma_start("gather_squares_start", squares, first)

    h = _rmsnorm_fwd(x2d, norm_g, token)
    proj = _project(h, w_in_b, jnp.arange(N_DEV, dtype=jnp.int32), "projection")
    squares = _rdma_wait("gather_squares_wait", squares, send1, recv1, first, proj)
    passed = _gather_pass_copies(3)
    send2, recv2, squares, token = _rdma_start("pass_squares_start", squares, passed)
    tables = _rope_tables(positions[0])
    attn, a_in = _attention_fwd(proj, tables, sink, kv, token)
    b_in = _gmlp_fwd(proj, col_u, d, w_s, bias_t, gmlp_ln_g, gmlp_ln_b)
    squares = _rdma_wait("pass_squares_wait", squares, send2, recv2, passed, b_in)
    w_ua, w_ug, w_o = [w.reshape(N_DEV * rw, d) for w in squares]
    y_a = _matmul(a_in, w_ua, "nn", F32, "up_attn")
    y_b = _matmul(b_in, w_ug, "nn", F32, "up_gmlp")
    merged = _merge_fwd(y_a, y_b, proj, col_m)
    x_out = _matmul(merged, w_o, "nn", F32, "out_proj", res=x2d)
    loss_p, d_final_g, dx2, dx2_b = _loss_and_final_norm_bwd(x_out, target, final_g)

    d_merged = _matmul(dx2_b, w_o, "nt", F32, "d_merged")
    g_w_out = _matmul(merged, dx2_b, "tn", BF16, "g_w_out")
    d_ya, d_yb, d_mg = _merge_bwd(d_merged, y_a, y_b, proj, col_m)
    d_ain = _matmul(d_ya, w_ua, "nt", F32, "d_a_in")
    g_w_ua = _matmul(a_in, d_ya, "tn", BF16, "g_w_up_attn")
    d_bin = _matmul(d_yb, w_ug, "nt", F32, "d_b_in")
    g_w_ug = _matmul(b_in, d_yb, "tn", BF16, "g_w_up_gmlp")
    sq_grads = [g.reshape(N_DEV, rw, d) for g in (g_w_ua, g_w_ug, g_w_out)]
    sq_land = [lax.empty((N_CHIPS, rw, d), BF16) for _ in sq_grads]
    pairs_sq = _pair_copies_strided(3)
    arrays = [a for gl in zip(sq_grads, sq_land) for a in gl]
    send3, recv3, arrays, token = _rdma_start("pair_squares_start", arrays, pairs_sq)
    d_q, d_k, d_v, d_ga, d_sink = _attention_bwd(proj, tables, sink, kv, attn, d_ain, token)
    arrays = _rdma_wait("pair_squares_wait", arrays, send3, recv3, pairs_sq, d_q)
    sq_sums = [_pair_sum(arrays[2 * a], arrays[2 * a + 1], pos, "pair_sum_%d" % a) for a in range(3)]
    sq_land2 = [lax.empty((N_CHIPS - 1, rw, d), BF16) for _ in sq_sums]
    chip_sq = _chip_sum_copies(3)
    arrays = [a for gl in zip(sq_sums, sq_land2) for a in gl]
    send4, recv4, sq_arrays, token = _rdma_start("chip_squares_start", arrays, chip_sq)
    d_g, d_w_s, d_bias_t, d_ln_g, d_ln_b = _gmlp_bwd(proj, col_u, d, w_s, bias_t, gmlp_ln_g, gmlp_ln_b, d_bin, token)
    d_proj = jnp.concatenate([d_q, d_k, d_v, d_ga, d_g, d_mg], axis=1)

    g_sib = _grad_w_in_blocks(h, d_proj, 2 * chips + 1 - mc, cw, "g_w_in_sibling")
    pairs_in = _pair_copies(1)
    send5, recv5, arrays, token = _rdma_start("pair_w_in_start", [g_sib, lax.empty((N_CHIPS, d, cw), BF16)], pairs_in)
    g_own = _grad_w_in_blocks(h, d_proj, 2 * chips + mc, cw, "g_w_in_own", after=token)
    arrays = _rdma_wait("pair_w_in_wait", arrays, send5, recv5, pairs_in, g_own)
    in_sums = _pair_sum(g_own, arrays[1], pos, "pair_sum_w_in")
    chip_in = _chip_sum_copies(1)
    send6, recv6, in_arrays, token = _rdma_start(
        "chip_w_in_start", [in_sums, lax.empty((N_CHIPS - 1, d, cw), BF16)], chip_in)
    d_h = _d_hidden(d_proj, w_in_b, after=token)
    grad_x, d_norm_g = _input_grad(d_h, x2d, norm_g, dx2)

    sq_arrays = _rdma_wait("chip_squares_wait", sq_arrays, send4, recv4, chip_sq, grad_x)
    big = {}
    for a, (name, w, m, v) in enumerate((("w_up_attn", w_up_attn, m_w_up_attn, v_w_up_attn),
                                         ("w_up_gmlp", w_up_gmlp, m_w_up_gmlp, v_w_up_gmlp),
                                         ("w_out", w_out, m_w_out, v_w_out))):
        big[name] = [r[None] for r in _reduce_adamw(sq_arrays[2 * a], sq_arrays[2 * a + 1], w[0], m[0], v[0], pos,
                                                    "adamw_" + name)]
    in_arrays = _rdma_wait("chip_w_in_wait", in_arrays, send6, recv6, chip_in, big["w_out"][0])
    big["w_in"] = [r[None] for r in _reduce_adamw(in_arrays[0], in_arrays[1], w_in[0], m_w_in[0], v_w_in[0], pos,
                                                  "adamw_w_in")]

    heads_per_pair = 2 * n_q_heads // (kv // HEAD_DIM)
    g_sink = d_sink[:, 0, :heads_per_pair].reshape(1, n_q_heads)
    small_w = [norm_g, attn_sink, gmlp_ln_g, gmlp_ln_b, w_spatial, b_spatial, final_norm_g]
    small_m = [m_norm_g, m_attn_sink, m_gmlp_ln_g, m_gmlp_ln_b, m_w_spatial, m_b_spatial, m_final_norm_g]
    small_v = [v_norm_g, v_attn_sink, v_gmlp_ln_g, v_gmlp_ln_b, v_w_spatial, v_b_spatial, v_final_norm_g]
    small_g = [d_norm_g, g_sink, d_ln_g, d_ln_b, d_w_s[None], d_bias_t.T[None], d_final_g.reshape(d)]
    loss_pad = jnp.zeros((1,), F32)
    shapes = [w.shape for w in small_w] + [(1,)]
    packed = _small_allreduce_adamw(_pack(small_g + [loss_p[0, :1]]), _pack(small_w + [loss_pad]),
                                    _pack(small_m + [loss_pad]), _pack(small_v + [loss_pad]))
    sg, sd, sm, sv = [_unpack(p, shapes) for p in packed]
    loss = sg[-1][0]

    names = ["norm_g", "w_in", "attn_sink", "gmlp_ln_g", "gmlp_ln_b", "w_spatial", "b_spatial", "w_up_attn",
             "w_up_gmlp", "w_out", "final_norm_g"]
    small_names = ["norm_g", "attn_sink", "gmlp_ln_g", "gmlp_ln_b", "w_spatial", "b_spatial", "final_norm_g"]
    outs = [[], [], [], []]
    for nm in names:
        for k in range(4):
            if nm in big:
                outs[k].append(big[nm][k])
            else:
                outs[k].append((sg, sd, sm, sv)[k][small_names.index(nm)])
    return (loss, grad_x[None], *outs[0], *outs[1], *outs[2], *outs[3])
```

```python
import math
from typing import Callable, NamedTuple

import jax
import jax.numpy as jnp
from jax import lax
from jax.experimental import pallas as pl
from jax.experimental.pallas import tpu as pltpu

F32 = jnp.float32
BF16 = jnp.bfloat16
MESH = pl.DeviceIdType.MESH

N_DEV = 8
N_CHIPS = 4
HEAD_DIM = 64
BLOCK = 128
ROPE_DIM = 16
ROPE_HALF = ROPE_DIM // 2
ROPE_THETA = 500000.0
GMLP_GROUPS = 8
NORM_EPS = 1e-5
LN_EPS = 1e-5
ATTN_SCALE = HEAD_DIM ** -0.5
LANES = 128
SUBLANES = 8
VMEM_LIMIT = 48 * 1024 * 1024
VMEM_LIMIT_WIDE = 56 * 1024 * 1024
DOT_COLS = 1024
SEGMENT_SPLIT = 4

ADAM_LR = 0.001
ADAM_B1 = 0.9
ADAM_B2 = 0.999
ADAM_EPS = 1e-08
ADAM_WD = 0.01
ADAM_STEP = 10

GELU_C = math.sqrt(2.0 / math.pi)
GELU_K = 0.044715

HBM_SPEC = pl.BlockSpec(memory_space=pltpu.HBM)
ANY_SPEC = pl.BlockSpec(memory_space=pl.ANY)
SEM_SPEC = pl.BlockSpec(memory_space=pltpu.SEMAPHORE)
VMEM_SPEC = pl.BlockSpec(memory_space=pltpu.VMEM)
SMEM_SPEC = pl.BlockSpec(memory_space=pltpu.SMEM)


def _sds(shape, dtype):
    return jax.ShapeDtypeStruct(shape, dtype)


def _params(*sem, vmem=VMEM_LIMIT):
    return pltpu.CompilerParams(dimension_semantics=sem or None, vmem_limit_bytes=vmem)


def _gelu(x):
    return 0.5 * x * (1.0 + jnp.tanh(GELU_C * (x + GELU_K * x * x * x)))


def _gelu_grad(x):
    t = jnp.tanh(GELU_C * (x + GELU_K * x * x * x))
    return 0.5 * (1.0 + t) + 0.5 * x * (1.0 - t * t) * GELU_C * (1.0 + 3.0 * GELU_K * x * x)


def _silu_and_grad(x):
    s = jax.nn.sigmoid(x)
    return x * s, s * (1.0 + x * (1.0 - s))


def _adamw(w, g, m, v):
    m = ADAM_B1 * m + (1.0 - ADAM_B1) * g
    v = ADAM_B2 * v + (1.0 - ADAM_B2) * (g * g)
    m_hat = m / (1.0 - ADAM_B1 ** ADAM_STEP)
    v_hat = v / (1.0 - ADAM_B2 ** ADAM_STEP)
    delta = -ADAM_LR * (m_hat / (jnp.sqrt(v_hat) + ADAM_EPS) + ADAM_WD * w)
    return delta, m, v


def _mesh_pos():
    return lax.axis_index("x"), lax.axis_index("y"), lax.axis_index("c")


def _slot(x, y, c):
    return 4 * x + 2 * y + c


def _chip(x, y):
    return 2 * x + y


def _sibling(x, y, c):
    return (x, y, 1 - c)


_OTHER_CHIPS = (lambda x, y: (1 - x, y), lambda x, y: (x, 1 - y), lambda x, y: (1 - x, 1 - y))
_ICI_STAGES = (lambda x, y, c: (x ^ c, y ^ (1 - c)), lambda x, y, c: (x ^ (1 - c), y ^ c),
               lambda x, y, c: (1 - x, 1 - y))


class _Copy(NamedTuple):
    src: int
    src_slot: Callable
    dst: int
    dst_slot: Callable
    peer: Callable


def _descriptor(refs, send_sems, recv_sems, k, cp):
    pos = _mesh_pos()
    return pltpu.make_async_remote_copy(
        src_ref=refs[cp.src].at[cp.src_slot(*pos)], dst_ref=refs[cp.dst].at[cp.dst_slot(*pos)],
        send_sem=send_sems.at[k], recv_sem=recv_sems.at[k], device_id=cp.peer(*pos), device_id_type=MESH)


def _gather_first_copies(n_arrays):
    copies = []
    for a in range(n_arrays):
        copies.append(_Copy(a, _slot, a, _slot, _sibling))
        for chip in _OTHER_CHIPS:
            copies.append(_Copy(a, _slot, a, _slot, lambda x, y, c, chip=chip: (*chip(x, y), c)))
    return copies


def _gather_pass_copies(n_arrays):
    copies = []
    for a in range(n_arrays):
        for chip in _OTHER_CHIPS:
            src = lambda x, y, c, chip=chip: _slot(*chip(x, y), c)
            copies.append(_Copy(a, src, a, src, _sibling))
    return copies


def _pair_copies(n_sets):
    copies = []
    for a in range(n_sets):
        for q in range(N_CHIPS):
            copies.append(_Copy(2 * a, lambda x, y, c, q=q: q, 2 * a + 1, lambda x, y, c, q=q: q, _sibling))
    return copies


def _pair_copies_strided(n_sets):
    copies = []
    for a in range(n_sets):
        for q in range(N_CHIPS):
            copies.append(_Copy(2 * a, lambda x, y, c, q=q: 2 * q + 1 - c, 2 * a + 1, lambda x, y, c, q=q: q, _sibling))
    return copies


def _chip_sum_copies(n_sets):
    copies = []
    for a in range(n_sets):
        for k, chip in enumerate(_OTHER_CHIPS):
            copies.append(_Copy(2 * a, lambda x, y, c, chip=chip: _chip(*chip(x, y)), 2 * a + 1,
                                lambda x, y, c, k=k: k, lambda x, y, c, chip=chip: (*chip(x, y), c)))
    return copies


def _rdma_start(name, arrays, copies):
    n, nc = len(arrays), len(copies)

    def body(*refs):
        in_refs = refs[:n]
        send_sems, recv_sems = refs[n], refs[n + 1]
        token = refs[2 * n + 2]
        for k, cp in enumerate(copies):
            _descriptor(in_refs, send_sems, recv_sems, k, cp).start()
        token[...] = jnp.zeros_like(token)

    out = pl.pallas_call(
        body, name=name,
        out_shape=(pltpu.SemaphoreType.DMA((nc,)), pltpu.SemaphoreType.DMA((nc,)),
                   *[pltpu.HBM(a.shape, a.dtype) for a in arrays], _sds((SUBLANES, LANES), F32)),
        in_specs=[HBM_SPEC] * n, out_specs=(SEM_SPEC, SEM_SPEC, *([HBM_SPEC] * n), VMEM_SPEC),
        input_output_aliases={i: i + 2 for i in range(n)},
        compiler_params=pltpu.CompilerParams(has_side_effects=pltpu.SideEffectType.DATAFLOW_SIDE_EFFECTING),
    )(*[pltpu.with_memory_space_constraint(a, pltpu.HBM) for a in arrays])
    return out[0], out[1], list(out[2:2 + n]), out[2 + n]


def _rdma_wait(name, arrays, send_sems, recv_sems, copies, after):
    n = len(arrays)

    def body(*refs):
        in_refs = refs[:n]
        send_ref, recv_ref = refs[n], refs[n + 1]
        for k, cp in enumerate(copies):
            d = _descriptor(in_refs, send_ref, recv_ref, k, cp)
            d.wait_send()
            d.wait_recv()

    out = pl.pallas_call(
        body, name=name, out_shape=tuple(pltpu.HBM(a.shape, a.dtype) for a in arrays),
        in_specs=[HBM_SPEC] * n + [SEM_SPEC, SEM_SPEC, ANY_SPEC], out_specs=tuple([HBM_SPEC] * n),
        input_output_aliases={i: i for i in range(n)},
        compiler_params=pltpu.CompilerParams(has_side_effects=pltpu.SideEffectType.DATAFLOW_SIDE_EFFECTING),
    )(*arrays, send_sems, recv_sems, after)
    return list(out)


def _cast_into_slot(w, pos, name):
    rows, cols = w.shape
    tr = min(rows, 256)

    def body(pos_ref, w_ref, o_ref):
        o_ref[...] = w_ref[...].astype(BF16)

    return pl.pallas_call(
        body, name=name,
        grid_spec=pltpu.PrefetchScalarGridSpec(
            num_scalar_prefetch=1, grid=(rows // tr,),
            in_specs=[pl.BlockSpec((tr, cols), lambda i, p: (i, 0))],
            out_specs=pl.BlockSpec((None, tr, cols), lambda i, p: (_slot(p[0], p[1], p[2]), i, 0))),
        out_shape=_sds((N_DEV, rows, cols), BF16), compiler_params=_params("parallel"),
    )(pos, w)


def _all_gather_slots(arrays, name):
    n = len(arrays)
    first, passed = _gather_first_copies(n), _gather_pass_copies(n)

    def body(*refs):
        in_refs = refs[:n]
        send_sems, recv_sems = refs[2 * n], refs[2 * n + 1]
        nf = len(first)
        for k, cp in enumerate(first):
            _descriptor(in_refs, send_sems, recv_sems, k, cp).start()
        for j, cp in enumerate(passed):
            a, rel = divmod(j, 3)
            _descriptor(in_refs, send_sems, recv_sems, 4 * a + 1 + rel, first[4 * a + 1 + rel]).wait_recv()
            _descriptor(in_refs, send_sems, recv_sems, nf + j, cp).start()
        for a in range(n):
            _descriptor(in_refs, send_sems, recv_sems, 4 * a, first[4 * a]).wait_recv()
        for j, cp in enumerate(passed):
            _descriptor(in_refs, send_sems, recv_sems, nf + j, cp).wait_recv()
        for k, cp in enumerate(first + passed):
            _descriptor(in_refs, send_sems, recv_sems, k, cp).wait_send()

    nsem = len(first) + len(passed)
    out = pl.pallas_call(
        body, name=name, out_shape=tuple(_sds(a.shape, a.dtype) for a in arrays),
        in_specs=[ANY_SPEC] * n, out_specs=tuple([ANY_SPEC] * n),
        input_output_aliases={i: i for i in range(n)},
        scratch_shapes=[pltpu.SemaphoreType.DMA((nsem,)), pltpu.SemaphoreType.DMA((nsem,))],
    )(*arrays)
    return list(out)


def _pair_sum(g, land, pos, name):
    _, rows, cols = land.shape
    tr = min(rows, 128)
    strided = g.shape[0] == N_DEV

    def body(pos_ref, g_ref, l_ref, o_ref):
        o_ref[...] = (g_ref[...].astype(F32) + l_ref[...].astype(F32)).astype(BF16)

    g_map = (lambda q, i, p: (2 * q + p[2], i, 0)) if strided else (lambda q, i, p: (q, i, 0))
    blk = pl.BlockSpec((None, tr, cols), lambda q, i, p: (q, i, 0))
    return pl.pallas_call(
        body, name=name,
        grid_spec=pltpu.PrefetchScalarGridSpec(
            num_scalar_prefetch=1, grid=(N_CHIPS, rows // tr),
            in_specs=[pl.BlockSpec((None, tr, cols), g_map), blk], out_specs=blk),
        out_shape=_sds((N_CHIPS, rows, cols), BF16), compiler_params=_params("parallel", "parallel"),
    )(pos, g, land)


def _reduce_adamw(sums, land, w, m, v, pos, name):
    rows, cols = w.shape
    tr = min(rows, 64)

    def body(pos_ref, s_ref, l_ref, w_ref, m_ref, v_ref, g_ref, d_ref, nm_ref, nv_ref):
        g = s_ref[...].astype(F32)
        for k in range(N_CHIPS - 1):
            g = g + l_ref[k].astype(F32)
        delta, nm, nv = _adamw(w_ref[...], g, m_ref[...], v_ref[...])
        g_ref[...] = g
        d_ref[...] = delta
        nm_ref[...] = nm
        nv_ref[...] = nv

    spec = pl.BlockSpec((tr, cols), lambda i, p: (i, 0))
    return pl.pallas_call(
        body, name=name,
        grid_spec=pltpu.PrefetchScalarGridSpec(
            num_scalar_prefetch=1, grid=(rows // tr,),
            in_specs=[pl.BlockSpec((None, tr, cols), lambda i, p: (_chip(p[0], p[1]), i, 0)),
                      pl.BlockSpec((N_CHIPS - 1, tr, cols), lambda i, p: (0, i, 0)), spec, spec, spec],
            out_specs=(spec, spec, spec, spec)),
        out_shape=tuple([_sds((rows, cols), F32)] * 4), compiler_params=_params("parallel"),
    )(pos, sums, land, w, m, v)


def _small_allreduce_adamw(g, w, m, v):
    rows = g.shape[0]

    def body(g_ref, w_ref, m_ref, v_ref, gs_ref, d_ref, nm_ref, nv_ref, all_ref, send_sems, recv_sems):
        x, y, c = _mesh_pos()
        me = _slot(x, y, c)
        copies = []
        for k in range(1, N_DEV):
            peer = (x ^ (k >> 2), y ^ ((k >> 1) & 1), c ^ (k & 1))
            copies.append(pltpu.make_async_remote_copy(
                src_ref=g_ref, dst_ref=all_ref.at[me], send_sem=send_sems.at[k - 1],
                recv_sem=recv_sems.at[k - 1], device_id=peer, device_id_type=MESH))
        for cp in copies:
            cp.start()
        all_ref[me] = g_ref[...]
        for cp in copies:
            cp.wait_recv()
        total = all_ref[0]
        for s in range(1, N_DEV):
            total = total + all_ref[s]
        delta, nm, nv = _adamw(w_ref[...], total, m_ref[...], v_ref[...])
        gs_ref[...] = total
        d_ref[...] = delta
        nm_ref[...] = nm
        nv_ref[...] = nv
        for cp in copies:
            cp.wait_send()

    return pl.pallas_call(
        body, name="small_allreduce_adamw", out_shape=tuple([_sds((rows, LANES), F32)] * 4),
        in_specs=[VMEM_SPEC] * 4, out_specs=tuple([VMEM_SPEC] * 4),
        scratch_shapes=[pltpu.VMEM((N_DEV, rows, LANES), F32), pltpu.SemaphoreType.DMA((7,)),
                        pltpu.SemaphoreType.DMA((7,))],
    )(g, w, m, v)


_DOT_DIMS = {"nn": ((1,), (0,)), "nt": ((1,), (1,)), "tn": ((0,), (0,))}


def _dot(a, b, mode):
    return lax.dot_general(a, b, (_DOT_DIMS[mode], ((), ())), preferred_element_type=F32)


def _col_chunks(cols):
    return [(c0, min(c0 + DOT_COLS, cols)) for c0 in range(0, cols, DOT_COLS)]


def _matmul(a, b, mode, out_dtype, name, *, res=None, tm=2048, tn=1024, tk=512):
    if mode == "tn":
        kdim, mdim = a.shape
    else:
        mdim, kdim = a.shape
    ndim = b.shape[0] if mode == "nt" else b.shape[1]
    tm, tn, tk = min(tm, mdim), min(tn, ndim), min(tk, kdim)
    assert mdim % tm == 0 and ndim % tn == 0 and kdim % tk == 0, (name, mdim, ndim, kdim)
    nk = kdim // tk

    def body(*refs):
        if res is None:
            a_ref, b_ref, o_ref, acc_ref = refs
        else:
            a_ref, b_ref, r_ref, o_ref, acc_ref = refs
        k = pl.program_id(2)

        @pl.when(k == 0)
        def _():
            acc_ref[...] = jnp.zeros_like(acc_ref)

        acc_ref[...] += _dot(a_ref[...], b_ref[...], mode)

        @pl.when(k == nk - 1)
        def _():
            out = acc_ref[...]
            if res is not None:
                out = out + r_ref[...]
            o_ref[...] = out.astype(out_dtype)

    if mode == "tn":
        a_spec = pl.BlockSpec((tk, tm), lambda i, j, k: (k, i))
    else:
        a_spec = pl.BlockSpec((tm, tk), lambda i, j, k: (i, k))
    if mode == "nt":
        b_spec = pl.BlockSpec((tn, tk), lambda i, j, k: (j, k))
    else:
        b_spec = pl.BlockSpec((tk, tn), lambda i, j, k: (k, j))
    o_spec = pl.BlockSpec((tm, tn), lambda i, j, k: (i, j))
    in_specs, args = [a_spec, b_spec], [a, b]
    if res is not None:
        in_specs.append(o_spec)
        args.append(res)
    return pl.pallas_call(
        body, name=name, grid=(mdim // tm, ndim // tn, nk), in_specs=in_specs, out_specs=o_spec,
        out_shape=_sds((mdim, ndim), out_dtype), scratch_shapes=[pltpu.VMEM((tm, tn), F32)],
        compiler_params=_params("parallel", "parallel", "arbitrary"),
    )(*args)


def _project(h, w_blocks, block_ids, name, *, proj=None, after=None, tm=1024, tk=512):
    s, d = h.shape
    _, _, cw = w_blocks.shape
    n = block_ids.shape[0]
    tm, tk = min(tm, s), min(tk, d)

    n_extra = (proj is not None) + (after is not None)

    def body(ids_ref, h_ref, w_ref, *rest):
        o_ref = rest[n_extra]
        k = pl.program_id(2)

        @pl.when(k == 0)
        def _():
            o_ref[...] = jnp.zeros_like(o_ref)

        for c0, c1 in _col_chunks(cw):
            o_ref[:, c0:c1] += _dot(h_ref[...], w_ref[:, c0:c1], "nn")

    in_specs = [pl.BlockSpec((tm, tk), lambda j, i, k, ids: (i, k)),
                pl.BlockSpec((None, tk, cw), lambda j, i, k, ids: (ids[j], k, 0))]
    args = [block_ids, h, w_blocks]
    aliases = {}
    if proj is not None:
        in_specs.append(ANY_SPEC)
        args.append(proj)
        aliases = {3: 0}
    if after is not None:
        in_specs.append(ANY_SPEC)
        args.append(after)
    return pl.pallas_call(
        body, name=name,
        grid_spec=pltpu.PrefetchScalarGridSpec(
            num_scalar_prefetch=1, grid=(n, s // tm, d // tk), in_specs=in_specs,
            out_specs=pl.BlockSpec((tm, cw), lambda j, i, k, ids: (i, ids[j]))),
        out_shape=_sds((s, N_DEV * cw), F32), input_output_aliases=aliases,
        compiler_params=_params("arbitrary", "arbitrary", "arbitrary", vmem=VMEM_LIMIT_WIDE),
    )(*args)


def _grad_w_in_blocks(h, d_proj, block_ids, cw, name, after=None, *, tm=1024, tk=512):
    s, d = h.shape
    n = block_ids.shape[0]
    tm, tk = min(tm, d), min(tk, s)
    nk = s // tk

    def body(ids_ref, h_ref, g_ref, *rest):
        o_ref, acc_ref = rest[-2], rest[-1]
        k = pl.program_id(2)

        @pl.when(k == 0)
        def _():
            acc_ref[...] = jnp.zeros_like(acc_ref)

        for c0, c1 in _col_chunks(cw):
            acc_ref[:, c0:c1] += _dot(h_ref[...], g_ref[:, c0:c1], "tn")

        @pl.when(k == nk - 1)
        def _():
            o_ref[...] = acc_ref[...].astype(BF16)

    in_specs = [pl.BlockSpec((tk, tm), lambda q, i, k, ids: (k, i)),
                pl.BlockSpec((tk, cw), lambda q, i, k, ids: (k, ids[q]))]
    args = [block_ids, h, d_proj]
    if after is not None:
        in_specs.append(ANY_SPEC)
        args.append(after)
    return pl.pallas_call(
        body, name=name,
        grid_spec=pltpu.PrefetchScalarGridSpec(
            num_scalar_prefetch=1, grid=(n, d // tm, nk), in_specs=in_specs,
            out_specs=pl.BlockSpec((None, tm, cw), lambda q, i, k, ids: (q, i, 0)),
            scratch_shapes=[pltpu.VMEM((tm, cw), F32)]),
        out_shape=_sds((n, d, cw), BF16),
        compiler_params=_params("parallel", "parallel", "arbitrary", vmem=VMEM_LIMIT_WIDE),
    )(*args)


def _d_hidden(d_proj, w_blocks, after=None, *, tm=1024, tn=1024):
    s = d_proj.shape[0]
    nb, d, cw = w_blocks.shape
    tm, tn = min(tm, s), min(tn, d)

    def body(g_ref, w_ref, *rest):
        o_ref = rest[-1]
        k = pl.program_id(2)

        @pl.when(k == 0)
        def _():
            o_ref[...] = jnp.zeros_like(o_ref)

        o_ref[...] += _dot(g_ref[...], w_ref[...], "nt")

    in_specs = [pl.BlockSpec((tm, cw), lambda i, j, k: (i, k)),
                pl.BlockSpec((None, tn, cw), lambda i, j, k: (k, j, 0))]
    args = [d_proj, w_blocks]
    if after is not None:
        in_specs.append(ANY_SPEC)
        args.append(after)
    return pl.pallas_call(
        body, name="d_h", grid=(s // tm, d // tn, nb), in_specs=in_specs,
        out_specs=pl.BlockSpec((tm, tn), lambda i, j, k: (i, j)), out_shape=_sds((s, d), F32),
        compiler_params=_params("parallel", "parallel", "arbitrary", vmem=VMEM_LIMIT_WIDE),
    )(*args)


def _row_tile(rows):
    return min(rows, 128)


def _segment_specs(rows, d, col0):
    w = d // SEGMENT_SPLIT
    assert col0 % w == 0
    return [pl.BlockSpec((rows, w), lambda i, t=t: (i, col0 // w + t)) for t in range(SEGMENT_SPLIT)]


def _cat(refs):
    return jnp.concatenate([r[...] for r in refs], axis=1)


def _rmsnorm_fwd(x, g, after):
    s, d = x.shape
    tr = _row_tile(s)

    def body(x_ref, g_ref, after_ref, h_ref):
        xv = x_ref[...]
        r = lax.rsqrt(jnp.mean(xv * xv, axis=-1, keepdims=True) + NORM_EPS)
        h_ref[...] = (xv * r * g_ref[...]).astype(BF16)

    row = pl.BlockSpec((tr, d), lambda i: (i, 0))
    vec = pl.BlockSpec((1, d), lambda i: (0, 0))
    return pl.pallas_call(body, name="rmsnorm_fwd", grid=(s // tr,), in_specs=[row, vec, ANY_SPEC], out_specs=row,
                          out_shape=_sds((s, d), BF16), compiler_params=_params("parallel"))(x, g, after)


def _merge_fwd(y_a, y_b, proj, col_m):
    s, d = y_a.shape
    tr = _row_tile(s)
    ns = SEGMENT_SPLIT

    def body(ya_ref, yb_ref, *rest):
        ma, mb, o_ref = _cat(rest[:ns]), _cat(rest[ns:2 * ns]), rest[2 * ns]
        o_ref[...] = (jax.nn.sigmoid(ma) * ya_ref[...] + jax.nn.sigmoid(mb) * yb_ref[...]).astype(BF16)

    row = pl.BlockSpec((tr, d), lambda i: (i, 0))
    return pl.pallas_call(
        body, name="merge_fwd", grid=(s // tr,),
        in_specs=[row, row, *_segment_specs(tr, d, col_m), *_segment_specs(tr, d, col_m + d)],
        out_specs=row, out_shape=_sds((s, d), BF16), compiler_params=_params("parallel"),
    )(y_a, y_b, *([proj] * (2 * ns)))


def _loss_and_final_norm_bwd(x2, target, g):
    s, d = x2.shape
    tr = _row_tile(s)

    def body(x_ref, t_ref, g_ref, loss_ref, dg_ref, dx_ref, dxb_ref):
        @pl.when(pl.program_id(0) == 0)
        def _():
            loss_ref[...] = jnp.zeros_like(loss_ref)
            dg_ref[...] = jnp.zeros_like(dg_ref)

        xv, gv = x_ref[...], g_ref[...]
        r = lax.rsqrt(jnp.mean(xv * xv, axis=-1, keepdims=True) + NORM_EPS)
        xhat = xv * r
        err = xhat * gv - t_ref[...]
        loss_ref[...] += 0.5 * jnp.sum(jnp.mean(err * err, axis=-1, keepdims=True))
        dy = err / d
        dg_ref[...] += jnp.sum(dy * xhat, axis=0, keepdims=True)
        dyg = dy * gv
        dx = r * (dyg - xhat * jnp.mean(dyg * xhat, axis=-1, keepdims=True))
        dx_ref[...] = dx
        dxb_ref[...] = dx.astype(BF16)

    row = pl.BlockSpec((tr, d), lambda i: (i, 0))
    vec = pl.BlockSpec((1, d), lambda i: (0, 0))
    return pl.pallas_call(
        body, name="loss_final_norm_bwd", grid=(s // tr,), in_specs=[row, row, vec],
        out_specs=(pl.BlockSpec((SUBLANES, LANES), lambda i: (0, 0)), vec, row, row),
        out_shape=(_sds((SUBLANES, LANES), F32), _sds((1, d), F32), _sds((s, d), F32), _sds((s, d), BF16)),
        compiler_params=_params("arbitrary"))(x2, target, g)


def _merge_bwd(d_merged, y_a, y_b, proj, col_m):
    s, d = y_a.shape
    tr = _row_tile(s)
    ns = SEGMENT_SPLIT

    def body(dm_ref, ya_ref, yb_ref, *rest):
        ma, mb = _cat(rest[:ns]), _cat(rest[ns:2 * ns])
        dya_ref, dyb_ref, dmg_ref = rest[2 * ns:]
        dm = dm_ref[...]
        sa = jax.nn.sigmoid(ma)
        sb = jax.nn.sigmoid(mb)
        dya_ref[...] = (dm * sa).astype(BF16)
        dyb_ref[...] = (dm * sb).astype(BF16)
        dmg_ref[:, :d] = (dm * ya_ref[...] * (sa * (1.0 - sa))).astype(BF16)
        dmg_ref[:, d:] = (dm * yb_ref[...] * (sb * (1.0 - sb))).astype(BF16)

    row = pl.BlockSpec((tr, d), lambda i: (i, 0))
    wide = pl.BlockSpec((tr, 2 * d), lambda i: (i, 0))
    return pl.pallas_call(
        body, name="merge_bwd", grid=(s // tr,),
        in_specs=[row, row, row, *_segment_specs(tr, d, col_m), *_segment_specs(tr, d, col_m + d)],
        out_specs=(row, row, wide),
        out_shape=(_sds((s, d), BF16), _sds((s, d), BF16), _sds((s, 2 * d), BF16)),
        compiler_params=_params("parallel"))(d_merged, y_a, y_b, *([proj] * (2 * ns)))


def _input_grad(d_h, x, g, dx2):
    s, d = x.shape
    tr = _row_tile(s)

    def body(dh_ref, x_ref, g_ref, dx2_ref, gx_ref, dg_ref):
        @pl.when(pl.program_id(0) == 0)
        def _():
            dg_ref[...] = jnp.zeros_like(dg_ref)

        xv, dh = x_ref[...], dh_ref[...]
        r = lax.rsqrt(jnp.mean(xv * xv, axis=-1, keepdims=True) + NORM_EPS)
        xhat = xv * r
        dg_ref[...] += jnp.sum(dh * xhat, axis=0, keepdims=True)
        dyg = dh * g_ref[...]
        gx_ref[...] = dx2_ref[...] + r * (dyg - xhat * jnp.mean(dyg * xhat, axis=-1, keepdims=True))

    row = pl.BlockSpec((tr, d), lambda i: (i, 0))
    vec = pl.BlockSpec((1, d), lambda i: (0, 0))
    return pl.pallas_call(
        body, name="input_grad", grid=(s // tr,), in_specs=[row, row, vec, row], out_specs=(row, vec),
        out_shape=(_sds((s, d), F32), _sds((1, d), F32)), compiler_params=_params("arbitrary"))(d_h, x, g, dx2)


def _rope_tables(positions):
    inv_freq = ROPE_THETA ** (-jnp.arange(ROPE_HALF, dtype=F32) * 2.0 / ROPE_DIM)
    ang = positions.astype(F32)[:, None] * inv_freq
    cos, sin = jnp.cos(ang), jnp.sin(ang)
    zero = jnp.zeros((positions.shape[0], HEAD_DIM - ROPE_DIM), F32)
    zero_h = jnp.zeros_like(sin)
    c = jnp.concatenate([cos, cos, zero + 1.0], axis=1)
    up = jnp.concatenate([-sin, zero_h, zero], axis=1)
    down = jnp.concatenate([zero_h, sin, zero], axis=1)
    reps = LANES // HEAD_DIM
    return jnp.stack([jnp.tile(c, (1, reps)), jnp.tile(up, (1, reps)), jnp.tile(down, (1, reps))])


def _lane_tiles(x):
    return [x[:, t * LANES:(t + 1) * LANES] for t in range(x.shape[1] // LANES)]


def _rope(x, tab):
    out = [xt * tab[0] + pltpu.roll(xt, LANES - ROPE_HALF, 1) * tab[1] + pltpu.roll(xt, ROPE_HALF, 1) * tab[2]
           for xt in _lane_tiles(x)]
    return out[0] if len(out) == 1 else jnp.concatenate(out, axis=1)


def _rope_bwd(g, tab):
    out = [gt * tab[0] + pltpu.roll(gt * tab[1], ROPE_HALF, 1) + pltpu.roll(gt * tab[2], LANES - ROPE_HALF, 1)
           for gt in _lane_tiles(g)]
    return out[0] if len(out) == 1 else jnp.concatenate(out, axis=1)


def _head(x, h):
    return x[:, h * HEAD_DIM:(h + 1) * HEAD_DIM]


def _stack_heads(x, first, count):
    return jnp.concatenate([_head(x, first + h) for h in range(count)], axis=0)


def _band_scores(q, k_prev, k_cur, sink_ref, pair, a, group, blk):
    kb = jnp.concatenate([_head(k_prev, a), _head(k_cur, a)], axis=0).astype(BF16)
    qs = _stack_heads(q, a * group, group).astype(BF16)
    s = lax.dot_general(qs, kb, (((1,), (1,)), ((), ())), preferred_element_type=F32) * ATTN_SCALE
    qi = lax.broadcasted_iota(jnp.int32, s.shape, 0) % BLOCK
    si = lax.broadcasted_iota(jnp.int32, s.shape, 1)
    mask = (si <= qi + BLOCK) & (si > qi) & ((blk > 0) | (si >= BLOCK))
    s = jnp.where(mask, s, -jnp.inf)
    sink = jnp.concatenate(
        [jnp.full((BLOCK, 1), sink_ref[(2 * pair + a) * group + h], F32) for h in range(group)], axis=0)
    m = jnp.maximum(jnp.max(s, axis=-1, keepdims=True), sink)
    p = jnp.exp(s - m)
    p_sink = jnp.exp(sink - m)
    denom = jnp.sum(p, axis=-1, keepdims=True) + p_sink
    return qs, kb, p / denom, p_sink / denom


def _attn_dims(s, d, kv):
    n_kv = kv // HEAD_DIM
    group = d // kv
    qw = 2 * group * HEAD_DIM
    assert n_kv % 2 == 0 and (d + 2 * kv) % qw == 0 and s % BLOCK == 0
    return group, qw, n_kv // 2, s // BLOCK


def _attention_fwd(proj, tables, sink, kv, after):
    s, d = proj.shape[0], sink.shape[0] * HEAD_DIM
    group, qw, n_pairs, nb = _attn_dims(s, d, kv)

    def body(sink_ref, q_ref, kc_ref, kp_ref, vc_ref, vp_ref, ga_ref, tc_ref, tp_ref, after_ref, attn_ref, ain_ref):
        pair, blk = pl.program_id(0), pl.program_id(1)
        tab_c, tab_p = tc_ref[...], tp_ref[...]
        q = _rope(q_ref[...], tab_c)
        k_cur, k_prev = _rope(kc_ref[...], tab_c), _rope(kp_ref[...], tab_p)
        v_cur, v_prev = vc_ref[...], vp_ref[...]
        outs = []
        for a in range(2):
            _, _, p, _ = _band_scores(q, k_prev, k_cur, sink_ref, pair, a, group, blk)
            vb = jnp.concatenate([_head(v_prev, a), _head(v_cur, a)], axis=0).astype(BF16)
            o = jnp.dot(p.astype(BF16), vb, preferred_element_type=F32)
            outs += [o[h * BLOCK:(h + 1) * BLOCK] for h in range(group)]
        attn = jnp.concatenate(outs, axis=1)
        attn_ref[...] = attn
        silu, _ = _silu_and_grad(ga_ref[...])
        ain_ref[...] = (attn * silu).astype(BF16)

    k0, v0, g0 = d // LANES, (d + kv) // LANES, (d + 2 * kv) // qw
    prev = lambda i: jnp.maximum(i - 1, 0)
    in_specs = [
        SMEM_SPEC,
        pl.BlockSpec((BLOCK, qw), lambda p, i: (i, p)),
        pl.BlockSpec((BLOCK, LANES), lambda p, i: (i, k0 + p)),
        pl.BlockSpec((BLOCK, LANES), lambda p, i: (prev(i), k0 + p)),
        pl.BlockSpec((BLOCK, LANES), lambda p, i: (i, v0 + p)),
        pl.BlockSpec((BLOCK, LANES), lambda p, i: (prev(i), v0 + p)),
        pl.BlockSpec((BLOCK, qw), lambda p, i: (i, g0 + p)),
        pl.BlockSpec((3, BLOCK, LANES), lambda p, i: (0, i, 0)),
        pl.BlockSpec((3, BLOCK, LANES), lambda p, i: (0, prev(i), 0)),
        ANY_SPEC,
    ]
    out = pl.BlockSpec((BLOCK, qw), lambda p, i: (i, p))
    return pl.pallas_call(
        body, name="attention_fwd", grid=(n_pairs, nb), in_specs=in_specs, out_specs=(out, out),
        out_shape=(_sds((s, d), F32), _sds((s, d), BF16)), compiler_params=_params("parallel", "parallel"),
    )(sink, proj, proj, proj, proj, proj, proj, tables, tables, after)


def _attention_bwd(proj, tables, sink, kv, attn, d_ain, after):
    s, d = proj.shape[0], sink.shape[0] * HEAD_DIM
    group, qw, n_pairs, nb = _attn_dims(s, d, kv)

    def body(sink_ref, q_ref, kc_ref, kp_ref, vc_ref, vp_ref, ga_ref, tc_ref, tp_ref, attn_ref, dain_ref, after_ref,
             dq_ref, dk_ref, dv_ref, dga_ref, dsink_ref, carry_k, carry_v):
        pair, blk = pl.program_id(0), pl.program_id(1)

        @pl.when(blk == 0)
        def _():
            carry_k[...] = jnp.zeros_like(carry_k)
            carry_v[...] = jnp.zeros_like(carry_v)
            dsink_ref[...] = jnp.zeros_like(dsink_ref)

        @pl.when(blk < nb)
        def _():
            tab_c, tab_p = tc_ref[...], tp_ref[...]
            q = _rope(q_ref[...], tab_c)
            k_cur, k_prev = _rope(kc_ref[...], tab_c), _rope(kp_ref[...], tab_p)
            v_cur, v_prev = vc_ref[...], vp_ref[...]
            silu, silu_grad = _silu_and_grad(ga_ref[...])
            d_ain_v = dain_ref[...]
            dga_ref[...] = (d_ain_v * attn_ref[...] * silu_grad).astype(BF16)
            d_attn = d_ain_v * silu
            dq_parts, dk_parts, dv_parts = [], [], []
            lane = lax.broadcasted_iota(jnp.int32, (1, LANES), 1)
            dsink = jnp.zeros((1, LANES), F32)
            for a in range(2):
                qs, kb, p, p_sink = _band_scores(q, k_prev, k_cur, sink_ref, pair, a, group, blk)
                vb = jnp.concatenate([_head(v_prev, a), _head(v_cur, a)], axis=0).astype(BF16)
                do = _stack_heads(d_attn, a * group, group).astype(BF16)
                dp = lax.dot_general(do, vb, (((1,), (1,)), ((), ())), preferred_element_type=F32)
                delta = jnp.sum(p * dp, axis=-1, keepdims=True)
                ds = ((p * (dp - delta)) * ATTN_SCALE).astype(BF16)
                dqs = jnp.dot(ds, kb, preferred_element_type=F32)
                dq_parts += [dqs[h * BLOCK:(h + 1) * BLOCK] for h in range(group)]
                dk_parts.append(lax.dot_general(ds, qs, (((0,), (0,)), ((), ())), preferred_element_type=F32))
                dv_parts.append(lax.dot_general(p.astype(BF16), do, (((0,), (0,)), ((), ())),
                                                preferred_element_type=F32))
                ds_sink = -(p_sink * delta)
                for h in range(group):
                    dsink = dsink + jnp.where(lane == a * group + h,
                                              jnp.sum(ds_sink[h * BLOCK:(h + 1) * BLOCK]), 0.0)
            dsink_ref[0] += dsink
            dq_ref[...] = _rope_bwd(jnp.concatenate(dq_parts, axis=1), tab_c).astype(BF16)
            dk_band = jnp.concatenate(dk_parts, axis=1)
            dv_band = jnp.concatenate(dv_parts, axis=1)
            dk_ref[...] = (carry_k[...] + _rope_bwd(dk_band[:BLOCK], tab_p)).astype(BF16)
            dv_ref[...] = (carry_v[...] + dv_band[:BLOCK]).astype(BF16)
            carry_k[...] = _rope_bwd(dk_band[BLOCK:], tab_c)
            carry_v[...] = dv_band[BLOCK:]

        @pl.when(blk == nb)
        def _():
            dk_ref[...] = carry_k[...].astype(BF16)
            dv_ref[...] = carry_v[...].astype(BF16)

    k0, v0, g0 = d // LANES, (d + kv) // LANES, (d + 2 * kv) // qw
    cur = lambda i: jnp.minimum(i, nb - 1)
    prev = lambda i: jnp.maximum(cur(i) - 1, 0)
    back = lambda i: jnp.maximum(i - 1, 0)
    q_spec = pl.BlockSpec((BLOCK, qw), lambda p, i: (cur(i), p))
    in_specs = [
        SMEM_SPEC,
        q_spec,
        pl.BlockSpec((BLOCK, LANES), lambda p, i: (cur(i), k0 + p)),
        pl.BlockSpec((BLOCK, LANES), lambda p, i: (prev(i), k0 + p)),
        pl.BlockSpec((BLOCK, LANES), lambda p, i: (cur(i), v0 + p)),
        pl.BlockSpec((BLOCK, LANES), lambda p, i: (prev(i), v0 + p)),
        pl.BlockSpec((BLOCK, qw), lambda p, i: (cur(i), g0 + p)),
        pl.BlockSpec((3, BLOCK, LANES), lambda p, i: (0, cur(i), 0)),
        pl.BlockSpec((3, BLOCK, LANES), lambda p, i: (0, prev(i), 0)),
        q_spec,
        q_spec,
        ANY_SPEC,
    ]
    kv_out = pl.BlockSpec((BLOCK, LANES), lambda p, i: (back(i), p))
    return pl.pallas_call(
        body, name="attention_bwd", grid=(n_pairs, nb + 1), in_specs=in_specs,
        out_specs=(q_spec, kv_out, kv_out, q_spec, pl.BlockSpec((1, 1, LANES), lambda p, i: (p, 0, 0))),
        out_shape=(_sds((s, d), BF16), _sds((s, kv), BF16), _sds((s, kv), BF16), _sds((s, d), BF16),
                   _sds((n_pairs, 1, LANES), F32)),
        scratch_shapes=[pltpu.VMEM((BLOCK, LANES), F32), pltpu.VMEM((BLOCK, LANES), F32)],
        compiler_params=_params("parallel", "arbitrary"),
    )(sink, proj, proj, proj, proj, proj, proj, tables, tables, attn, d_ain, after)


def _gmlp_core(u, vg, ln_g, ln_b, w_ref, bias_t):
    gu = _gelu(u)
    gv = _gelu(vg)
    xc = gv - jnp.mean(gv, axis=-1, keepdims=True)
    rstd = lax.rsqrt(jnp.mean(xc * xc, axis=-1, keepdims=True) + LN_EPS)
    xhat = xc * rstd
    vn = (xhat * ln_g + ln_b).astype(BF16)
    gd = u.shape[1] // GMLP_GROUPS
    tri = (lax.broadcasted_iota(jnp.int32, (BLOCK, BLOCK), 0) >= lax.broadcasted_iota(jnp.int32, (BLOCK, BLOCK), 1))
    w_tri = [jnp.where(tri, w_ref[g], 0.0).astype(BF16) for g in range(GMLP_GROUPS)]
    mixed = jnp.concatenate(
        [jnp.dot(w_tri[g], vn[:, g * gd:(g + 1) * gd], preferred_element_type=F32) + bias_t[:, g:g + 1]
         for g in range(GMLP_GROUPS)], axis=1)
    return gu, xhat, rstd, vn, w_tri, tri, mixed


def _whole(shape):
    return pl.BlockSpec(shape, lambda i: tuple(0 for _ in shape))


def _gmlp_fwd(proj, col_u, d, w_s, bias_t, ln_g, ln_b, after):
    s = proj.shape[0]
    ns = SEGMENT_SPLIT

    def body(*refs):
        u, vg, gb = _cat(refs[:ns]), _cat(refs[ns:2 * ns]), _cat(refs[2 * ns:3 * ns])
        w_ref, bt_ref, lg_ref, lb_ref, after_ref, o_ref = refs[3 * ns:]
        gu, _, _, _, _, _, mixed = _gmlp_core(u, vg, lg_ref[...], lb_ref[...], w_ref, bt_ref[...])
        silu, _ = _silu_and_grad(gb)
        o_ref[...] = ((gu * mixed) * silu).astype(BF16)

    segs = [sp for j in range(3) for sp in _segment_specs(BLOCK, d, col_u + j * d)]
    return pl.pallas_call(
        body, name="gmlp_fwd", grid=(s // BLOCK,),
        in_specs=[*segs, _whole(w_s.shape), _whole(bias_t.shape), _whole((1, d)), _whole((1, d)), ANY_SPEC],
        out_specs=pl.BlockSpec((BLOCK, d), lambda i: (i, 0)), out_shape=_sds((s, d), BF16),
        compiler_params=_params("parallel"),
    )(*([proj] * (3 * ns)), w_s, bias_t, ln_g, ln_b, after)


def _gmlp_bwd(proj, col_u, d, w_s, bias_t, ln_g, ln_b, d_bin, after):
    s = proj.shape[0]
    gd = d // GMLP_GROUPS
    ns = SEGMENT_SPLIT

    def body(*refs):
        u, vg, gb = _cat(refs[:ns]), _cat(refs[ns:2 * ns]), _cat(refs[2 * ns:3 * ns])
        w_ref, bt_ref, lg_ref, lb_ref, dbin_ref, after_ref, dg_ref, dw_ref, dbt_ref, dlg_ref, dlb_ref = refs[3 * ns:]

        @pl.when(pl.program_id(0) == 0)
        def _():
            dw_ref[...] = jnp.zeros_like(dw_ref)
            dbt_ref[...] = jnp.zeros_like(dbt_ref)
            dlg_ref[...] = jnp.zeros_like(dlg_ref)
            dlb_ref[...] = jnp.zeros_like(dlb_ref)

        ln_g = lg_ref[...]
        gu, xhat, rstd, vn, w_tri, tri, mixed = _gmlp_core(u, vg, ln_g, lb_ref[...], w_ref, bt_ref[...])
        silu, silu_grad = _silu_and_grad(gb)
        d_bin_v = dbin_ref[...]
        d_sg = d_bin_v * silu
        dg_ref[:, 2 * d:] = (d_bin_v * (gu * mixed) * silu_grad).astype(BF16)
        dg_ref[:, :d] = (d_sg * mixed * _gelu_grad(u)).astype(BF16)
        d_mixed = d_sg * gu
        d_mixed_b = d_mixed.astype(BF16)
        d_vn, d_bias = [], []
        for g in range(GMLP_GROUPS):
            dm_g = d_mixed_b[:, g * gd:(g + 1) * gd]
            d_bias.append(jnp.sum(d_mixed[:, g * gd:(g + 1) * gd], axis=-1, keepdims=True))
            dw = lax.dot_general(dm_g, vn[:, g * gd:(g + 1) * gd], (((1,), (1,)), ((), ())),
                                 preferred_element_type=F32)
            dw_ref[g] += jnp.where(tri, dw, 0.0)
            d_vn.append(lax.dot_general(w_tri[g], dm_g, (((0,), (0,)), ((), ())), preferred_element_type=F32))
        dbt_ref[...] += jnp.concatenate(d_bias, axis=1)
        d_vn = jnp.concatenate(d_vn, axis=1)
        dlg_ref[...] += jnp.sum(d_vn * xhat, axis=0, keepdims=True)
        dlb_ref[...] += jnp.sum(d_vn, axis=0, keepdims=True)
        d_xhat = d_vn * ln_g
        d_gv = rstd * (d_xhat - jnp.mean(d_xhat, axis=-1, keepdims=True)
                       - xhat * jnp.mean(d_xhat * xhat, axis=-1, keepdims=True))
        dg_ref[:, d:2 * d] = (d_gv * _gelu_grad(vg)).astype(BF16)

    segs = [sp for j in range(3) for sp in _segment_specs(BLOCK, d, col_u + j * d)]
    return pl.pallas_call(
        body, name="gmlp_bwd", grid=(s // BLOCK,),
        in_specs=[*segs, _whole(w_s.shape), _whole(bias_t.shape), _whole((1, d)), _whole((1, d)),
                  pl.BlockSpec((BLOCK, d), lambda i: (i, 0)), ANY_SPEC],
        out_specs=(pl.BlockSpec((BLOCK, 3 * d), lambda i: (i, 0)), _whole(w_s.shape), _whole(bias_t.shape),
                   _whole((1, d)), _whole((1, d))),
        out_shape=(_sds((s, 3 * d), BF16), _sds(w_s.shape, F32), _sds(bias_t.shape, F32), _sds((1, d), F32),
                   _sds((1, d), F32)),
        compiler_params=_params("arbitrary"),
    )(*([proj] * (3 * ns)), w_s, bias_t, ln_g, ln_b, d_bin, after)


def _pack(parts):
    rows = []
    tile = SUBLANES * LANES
    for p in parts:
        flat = p.astype(F32).reshape(-1)
        padded = -(-flat.shape[0] // tile) * tile
        rows.append(jnp.pad(flat, (0, padded - flat.shape[0])).reshape(-1, LANES))
    return jnp.concatenate(rows, axis=0)


def _unpack(packed, shapes):
    out, row = [], 0
    tile = SUBLANES * LANES
    for shape in shapes:
        size = math.prod(shape)
        n_rows = -(-size // tile) * SUBLANES
        out.append(packed[row:row + n_rows].reshape(-1)[:size].reshape(shape))
        row += n_rows
    return out


def kernel(x, positions, norm_g, w_in, attn_sink, gmlp_ln_g, gmlp_ln_b, w_spatial, b_spatial, w_up_attn, w_up_gmlp, w_out, final_norm_g, loss_target, m_norm_g, m_w_in, m_attn_sink, m_gmlp_ln_g, m_gmlp_ln_b, m_w_spatial, m_b_spatial, m_w_up_attn, m_w_up_gmlp, m_w_out, m_final_norm_g, v_norm_g, v_w_in, v_attn_sink, v_gmlp_ln_g, v_gmlp_ln_b, v_w_spatial, v_b_spatial, v_w_up_attn, v_w_up_gmlp, v_w_out, v_final_norm_g):
    x2d, target = x[0], loss_target[0]
    s, d = x2d.shape
    n_q_heads = attn_sink.shape[1]
    cw = w_in.shape[2]
    rw = w_up_attn.shape[1]
    kv = (cw * N_DEV - 7 * d) // 2
    col_u, col_m = 2 * d + 2 * kv, 5 * d + 2 * kv
    final_g = final_norm_g.reshape(1, d)
    sink = attn_sink[0]
    w_s = w_spatial[0]
    bias_t = b_spatial[0].T
    mx, my, mc = _mesh_pos()
    pos = jnp.stack([mx, my, mc]).astype(jnp.int32)
    chips = jnp.arange(N_CHIPS, dtype=jnp.int32)

    def one_block(fn):
        return jnp.reshape(fn(mx, my, mc), (1,)).astype(jnp.int32)

    w_in_b = _cast_into_slot(w_in[0], pos, "cast_w_in")
    squares = [_cast_into_slot(w[0], pos, "cast_" + nm)
               for nm, w in (("w_up_attn", w_up_attn), ("w_up_gmlp", w_up_gmlp), ("w_out", w_out))]
    to_sibling = [_Copy(0, _slot, 0, _slot, _sibling)]
    ici = [[_Copy(0, _slot, 0, _slot, lambda x, y, c, chip=chip: (*chip(x, y, c), c))] for chip in _ICI_STAGES]
    passes = []
    for chip in _ICI_STAGES:
        landed = lambda x, y, c, chip=chip: _slot(*chip(x, y, c), c)
        passes.append([_Copy(0, landed, 0, landed, _sibling)])
    sib_sems = _rdma_start("w_in_sibling_start", [w_in_b], to_sibling)
    ici_sems = _rdma_start("w_in_ici0_start", sib_sems[2], ici[0])
    h = _rmsnorm_fwd(x2d, norm_g, ici_sems[3])
    proj = _project(h, ici_sems[2][0], one_block(_slot), "projection_own")
    w_blocks = _rdma_wait("w_in_sibling_wait", ici_sems[2], sib_sems[0], sib_sems[1], to_sibling, proj)
    proj = _project(h, w_blocks[0], one_block(lambda x, y, c: _slot(x, y, 1 - c)), "projection_sibling", proj=proj)
    w_blocks = _rdma_wait("w_in_ici0_wait", w_blocks, ici_sems[0], ici_sems[1], ici[0], proj)
    first = _gather_first_copies(3)
    for k, chip in enumerate(_ICI_STAGES):
        if k + 1 < len(_ICI_STAGES):
            ici_sems = _rdma_start("w_in_ici%d_start" % (k + 1), w_blocks, ici[k + 1])
            w_blocks = ici_sems[2]
        else:
            send1, recv1, thru, _ = _rdma_start("gather_squares_start", squares + w_blocks, first)
            squares, w_blocks = thru[:3], thru[3:]
        pass_sems = _rdma_start("w_in_pass%d_start" % k, w_blocks, passes[k])
        proj = _project(h, pass_sems[2][0], one_block(lambda x, y, c, chip=chip: _slot(*chip(x, y, c), c)),
                        "projection_ici%d" % k, proj=proj, after=pass_sems[3])
        w_blocks = _rdma_wait("w_in_pass%d_wait" % k, pass_sems[2], pass_sems[0], pass_sems[1], passes[k], proj)
        proj = _project(h, w_blocks[0], one_block(lambda x, y, c, chip=chip: _slot(*chip(x, y, 1 - c), 1 - c)),
                        "projection_pass%d" % k, proj=proj)
        if k + 1 < len(_ICI_STAGES):
            w_blocks = _rdma_wait("w_in_ici%d_wait" % (k + 1), w_blocks, ici_sems[0], ici_sems[1], ici[k + 1], proj)
    w_in_b = w_blocks[0]

    tables = _rope_tables(positions[0])
    attn, a_in = _attention_fwd(proj, tables, sink, kv, proj)
    squares = _rdma_wait("gather_squares_wait", squares, send1, recv1, first, attn)
    passed = _gather_pass_copies(3)
    send2, recv2, squares, token = _rdma_start("pass_squares_start", squares, passed)
    b_in = _gmlp_fwd(proj, col_u, d, w_s, bias_t, gmlp_ln_g, gmlp_ln_b, token)
    squares = _rdma_wait("pass_squares_wait", squares, send2, recv2, passed, b_in)
    w_ua, w_ug, w_o = [w.reshape(N_DEV * rw, d) for w in squares]
    y_a = _matmul(a_in, w_ua, "nn", F32, "up_attn")
    y_b = _matmul(b_in, w_ug, "nn", F32, "up_gmlp")
    merged = _merge_fwd(y_a, y_b, proj, col_m)
    x_out = _matmul(merged, w_o, "nn", F32, "out_proj", res=x2d)
    loss_p, d_final_g, dx2, dx2_b = _loss_and_final_norm_bwd(x_out, target, final_g)

    d_merged = _matmul(dx2_b, w_o, "nt", F32, "d_merged")
    g_w_out = _matmul(merged, dx2_b, "tn", BF16, "g_w_out")
    d_ya, d_yb, d_mg = _merge_bwd(d_merged, y_a, y_b, proj, col_m)
    d_ain = _matmul(d_ya, w_ua, "nt", F32, "d_a_in")
    g_w_ua = _matmul(a_in, d_ya, "tn", BF16, "g_w_up_attn")
    d_bin = _matmul(d_yb, w_ug, "nt", F32, "d_b_in")
    g_w_ug = _matmul(b_in, d_yb, "tn", BF16, "g_w_up_gmlp")
    sq_grads = [g.reshape(N_DEV, rw, d) for g in (g_w_ua, g_w_ug, g_w_out)]
    sq_land = [lax.empty((N_CHIPS, rw, d), BF16) for _ in sq_grads]
    pairs_sq = _pair_copies_strided(3)
    arrays = [a for gl in zip(sq_grads, sq_land) for a in gl]
    send3, recv3, arrays, token = _rdma_start("pair_squares_start", arrays, pairs_sq)
    d_q, d_k, d_v, d_ga, d_sink = _attention_bwd(proj, tables, sink, kv, attn, d_ain, token)
    arrays = _rdma_wait("pair_squares_wait", arrays, send3, recv3, pairs_sq, d_q)
    sq_sums = [_pair_sum(arrays[2 * a], arrays[2 * a + 1], pos, "pair_sum_%d" % a) for a in range(3)]
    sq_land2 = [lax.empty((N_CHIPS - 1, rw, d), BF16) for _ in sq_sums]
    chip_sq = _chip_sum_copies(3)
    arrays = [a for gl in zip(sq_sums, sq_land2) for a in gl]
    send4, recv4, sq_arrays, token = _rdma_start("chip_squares_start", arrays, chip_sq)
    d_g, d_w_s, d_bias_t, d_ln_g, d_ln_b = _gmlp_bwd(proj, col_u, d, w_s, bias_t, gmlp_ln_g, gmlp_ln_b, d_bin, token)
    d_proj = jnp.concatenate([d_q, d_k, d_v, d_ga, d_g, d_mg], axis=1)

    g_sib = _grad_w_in_blocks(h, d_proj, 2 * chips + 1 - mc, cw, "g_w_in_sibling")
    pairs_in = _pair_copies(1)
    send5, recv5, arrays, token = _rdma_start("pair_w_in_start", [g_sib, lax.empty((N_CHIPS, d, cw), BF16)], pairs_in)
    g_own = _grad_w_in_blocks(h, d_proj, 2 * chips + mc, cw, "g_w_in_own", after=token)
    arrays = _rdma_wait("pair_w_in_wait", arrays, send5, recv5, pairs_in, g_own)
    in_sums = _pair_sum(g_own, arrays[1], pos, "pair_sum_w_in")
    chip_in = _chip_sum_copies(1)
    send6, recv6, in_arrays, token = _rdma_start(
        "chip_w_in_start", [in_sums, lax.empty((N_CHIPS - 1, d, cw), BF16)], chip_in)
    d_h = _d_hidden(d_proj, w_in_b, after=token)
    grad_x, d_norm_g = _input_grad(d_h, x2d, norm_g, dx2)

    sq_arrays = _rdma_wait("chip_squares_wait", sq_arrays, send4, recv4, chip_sq, grad_x)
    big = {}
    for a, (name, w, m, v) in enumerate((("w_up_attn", w_up_attn, m_w_up_attn, v_w_up_attn),
                                         ("w_up_gmlp", w_up_gmlp, m_w_up_gmlp, v_w_up_gmlp),
                                         ("w_out", w_out, m_w_out, v_w_out))):
        big[name] = [r[None] for r in _reduce_adamw(sq_arrays[2 * a], sq_arrays[2 * a + 1], w[0], m[0], v[0], pos,
                                                    "adamw_" + name)]
    in_arrays = _rdma_wait("chip_w_in_wait", in_arrays, send6, recv6, chip_in, big["w_out"][0])
    big["w_in"] = [r[None] for r in _reduce_adamw(in_arrays[0], in_arrays[1], w_in[0], m_w_in[0], v_w_in[0], pos,
                                                  "adamw_w_in")]

    heads_per_pair = 2 * n_q_heads // (kv // HEAD_DIM)
    g_sink = d_sink[:, 0, :heads_per_pair].reshape(1, n_q_heads)
    small_w = [norm_g, attn_sink, gmlp_ln_g, gmlp_ln_b, w_spatial, b_spatial, final_norm_g]
    small_m = [m_norm_g, m_attn_sink, m_gmlp_ln_g, m_gmlp_ln_b, m_w_spatial, m_b_spatial, m_final_norm_g]
    small_v = [v_norm_g, v_attn_sink, v_gmlp_ln_g, v_gmlp_ln_b, v_w_spatial, v_b_spatial, v_final_norm_g]
    small_g = [d_norm_g, g_sink, d_ln_g, d_ln_b, d_w_s[None], d_bias_t.T[None], d_final_g.reshape(d)]
    loss_pad = jnp.zeros((1,), F32)
    shapes = [w.shape for w in small_w] + [(1,)]
    packed = _small_allreduce_adamw(_pack(small_g + [loss_p[0, :1]]), _pack(small_w + [loss_pad]),
                                    _pack(small_m + [loss_pad]), _pack(small_v + [loss_pad]))
    sg, sd, sm, sv = [_unpack(p, shapes) for p in packed]
    loss = sg[-1][0]

    names = ["norm_g", "w_in", "attn_sink", "gmlp_ln_g", "gmlp_ln_b", "w_spatial", "b_spatial", "w_up_attn",
             "w_up_gmlp", "w_out", "final_norm_g"]
    small_names = ["norm_g", "attn_sink", "gmlp_ln_g", "gmlp_ln_b", "w_spatial", "b_spatial", "final_norm_g"]
    outs = [[], [], [], []]
    for nm in names:
        for k in range(4):
            if nm in big:
                outs[k].append(big[nm][k])
            else:
                outs[k].append((sg, sd, sm, sv)[k][small_names.index(nm)])
    return (loss, grad_x[None], *outs[0], *outs[1], *outs[2], *outs[3])
```

```python
import math
from typing import Callable, NamedTuple

import jax
import jax.numpy as jnp
from jax import lax
from jax.experimental import pallas as pl
from jax.experimental.pallas import tpu as pltpu

F32 = jnp.float32
BF16 = jnp.bfloat16
MESH = pl.DeviceIdType.MESH

N_DEV = 8
N_CHIPS = 4
HEAD_DIM = 64
BLOCK = 128
ROPE_DIM = 16
ROPE_HALF = ROPE_DIM // 2
ROPE_THETA = 500000.0
GMLP_GROUPS = 8
NORM_EPS = 1e-5
LN_EPS = 1e-5
ATTN_SCALE = HEAD_DIM ** -0.5
LANES = 128
SUBLANES = 8
VMEM_LIMIT = 48 * 1024 * 1024
VMEM_LIMIT_WIDE = 56 * 1024 * 1024
DOT_COLS = 1024
SEGMENT_SPLIT = 4

ADAM_LR = 0.001
ADAM_B1 = 0.9
ADAM_B2 = 0.999
ADAM_EPS = 1e-08
ADAM_WD = 0.01
ADAM_STEP = 10

GELU_C = math.sqrt(2.0 / math.pi)
GELU_K = 0.044715

HBM_SPEC = pl.BlockSpec(memory_space=pltpu.HBM)
ANY_SPEC = pl.BlockSpec(memory_space=pl.ANY)
SEM_SPEC = pl.BlockSpec(memory_space=pltpu.SEMAPHORE)
VMEM_SPEC = pl.BlockSpec(memory_space=pltpu.VMEM)
SMEM_SPEC = pl.BlockSpec(memory_space=pltpu.SMEM)


def _sds(shape, dtype):
    return jax.ShapeDtypeStruct(shape, dtype)


def _params(*sem, vmem=VMEM_LIMIT):
    return pltpu.CompilerParams(dimension_semantics=sem or None, vmem_limit_bytes=vmem)


def _gelu(x):
    return 0.5 * x * (1.0 + jnp.tanh(GELU_C * (x + GELU_K * x * x * x)))


def _gelu_grad(x):
    t = jnp.tanh(GELU_C * (x + GELU_K * x * x * x))
    return 0.5 * (1.0 + t) + 0.5 * x * (1.0 - t * t) * GELU_C * (1.0 + 3.0 * GELU_K * x * x)


def _silu_and_grad(x):
    s = jax.nn.sigmoid(x)
    return x * s, s * (1.0 + x * (1.0 - s))


def _adamw(w, g, m, v):
    m = ADAM_B1 * m + (1.0 - ADAM_B1) * g
    v = ADAM_B2 * v + (1.0 - ADAM_B2) * (g * g)
    m_hat = m / (1.0 - ADAM_B1 ** ADAM_STEP)
    v_hat = v / (1.0 - ADAM_B2 ** ADAM_STEP)
    delta = -ADAM_LR * (m_hat / (jnp.sqrt(v_hat) + ADAM_EPS) + ADAM_WD * w)
    return delta, m, v


def _mesh_pos():
    return lax.axis_index("x"), lax.axis_index("y"), lax.axis_index("c")


def _slot(x, y, c):
    return 4 * x + 2 * y + c


def _chip(x, y):
    return 2 * x + y


def _sibling(x, y, c):
    return (x, y, 1 - c)


_OTHER_CHIPS = (lambda x, y: (1 - x, y), lambda x, y: (x, 1 - y), lambda x, y: (1 - x, 1 - y))
_ICI_STAGES = (lambda x, y, c: (x ^ c, y ^ (1 - c)), lambda x, y, c: (x ^ (1 - c), y ^ c),
               lambda x, y, c: (1 - x, 1 - y))


class _Copy(NamedTuple):
    src: int
    src_slot: Callable
    dst: int
    dst_slot: Callable
    peer: Callable


def _descriptor(refs, send_sems, recv_sems, k, cp):
    pos = _mesh_pos()
    return pltpu.make_async_remote_copy(
        src_ref=refs[cp.src].at[cp.src_slot(*pos)], dst_ref=refs[cp.dst].at[cp.dst_slot(*pos)],
        send_sem=send_sems.at[k], recv_sem=recv_sems.at[k], device_id=cp.peer(*pos), device_id_type=MESH)


def _gather_first_copies(n_arrays):
    copies = []
    for a in range(n_arrays):
        copies.append(_Copy(a, _slot, a, _slot, _sibling))
        for chip in _OTHER_CHIPS:
            copies.append(_Copy(a, _slot, a, _slot, lambda x, y, c, chip=chip: (*chip(x, y), c)))
    return copies


def _gather_pass_copies(n_arrays):
    copies = []
    for a in range(n_arrays):
        for chip in _OTHER_CHIPS:
            src = lambda x, y, c, chip=chip: _slot(*chip(x, y), c)
            copies.append(_Copy(a, src, a, src, _sibling))
    return copies


def _pair_copies(n_sets):
    copies = []
    for a in range(n_sets):
        for q in range(N_CHIPS):
            copies.append(_Copy(2 * a, lambda x, y, c, q=q: q, 2 * a + 1, lambda x, y, c, q=q: q, _sibling))
    return copies


def _pair_copies_strided(n_sets):
    copies = []
    for a in range(n_sets):
        for q in range(N_CHIPS):
            copies.append(_Copy(2 * a, lambda x, y, c, q=q: 2 * q + 1 - c, 2 * a + 1, lambda x, y, c, q=q: q, _sibling))
    return copies


def _chip_sum_copies(n_sets):
    copies = []
    for a in range(n_sets):
        for k, chip in enumerate(_OTHER_CHIPS):
            copies.append(_Copy(2 * a, lambda x, y, c, chip=chip: _chip(*chip(x, y)), 2 * a + 1,
                                lambda x, y, c, k=k: k, lambda x, y, c, chip=chip: (*chip(x, y), c)))
    return copies


def _rdma_start(name, arrays, copies):
    n, nc = len(arrays), len(copies)

    def body(*refs):
        in_refs = refs[:n]
        send_sems, recv_sems = refs[n], refs[n + 1]
        token = refs[2 * n + 2]
        for k, cp in enumerate(copies):
            _descriptor(in_refs, send_sems, recv_sems, k, cp).start()
        token[...] = jnp.zeros_like(token)

    out = pl.pallas_call(
        body, name=name,
        out_shape=(pltpu.SemaphoreType.DMA((nc,)), pltpu.SemaphoreType.DMA((nc,)),
                   *[pltpu.HBM(a.shape, a.dtype) for a in arrays], _sds((SUBLANES, LANES), F32)),
        in_specs=[HBM_SPEC] * n, out_specs=(SEM_SPEC, SEM_SPEC, *([HBM_SPEC] * n), VMEM_SPEC),
        input_output_aliases={i: i + 2 for i in range(n)},
        compiler_params=pltpu.CompilerParams(has_side_effects=pltpu.SideEffectType.DATAFLOW_SIDE_EFFECTING),
    )(*[pltpu.with_memory_space_constraint(a, pltpu.HBM) for a in arrays])
    return out[0], out[1], list(out[2:2 + n]), out[2 + n]


def _rdma_wait(name, arrays, send_sems, recv_sems, copies, after):
    n = len(arrays)

    def body(*refs):
        in_refs = refs[:n]
        send_ref, recv_ref = refs[n], refs[n + 1]
        for k, cp in enumerate(copies):
            d = _descriptor(in_refs, send_ref, recv_ref, k, cp)
            d.wait_send()
            d.wait_recv()

    out = pl.pallas_call(
        body, name=name, out_shape=tuple(pltpu.HBM(a.shape, a.dtype) for a in arrays),
        in_specs=[HBM_SPEC] * n + [SEM_SPEC, SEM_SPEC, ANY_SPEC], out_specs=tuple([HBM_SPEC] * n),
        input_output_aliases={i: i for i in range(n)},
        compiler_params=pltpu.CompilerParams(has_side_effects=pltpu.SideEffectType.DATAFLOW_SIDE_EFFECTING),
    )(*arrays, send_sems, recv_sems, after)
    return list(out)


def _cast_into_slot(w, pos, name):
    rows, cols = w.shape
    tr = min(rows, 256)

    def body(pos_ref, w_ref, o_ref):
        o_ref[...] = w_ref[...].astype(BF16)

    return pl.pallas_call(
        body, name=name,
        grid_spec=pltpu.PrefetchScalarGridSpec(
            num_scalar_prefetch=1, grid=(rows // tr,),
            in_specs=[pl.BlockSpec((tr, cols), lambda i, p: (i, 0))],
            out_specs=pl.BlockSpec((None, tr, cols), lambda i, p: (_slot(p[0], p[1], p[2]), i, 0))),
        out_shape=_sds((N_DEV, rows, cols), BF16), compiler_params=_params("parallel"),
    )(pos, w)


def _all_gather_slots(arrays, name):
    n = len(arrays)
    first, passed = _gather_first_copies(n), _gather_pass_copies(n)

    def body(*refs):
        in_refs = refs[:n]
        send_sems, recv_sems = refs[2 * n], refs[2 * n + 1]
        nf = len(first)
        for k, cp in enumerate(first):
            _descriptor(in_refs, send_sems, recv_sems, k, cp).start()
        for j, cp in enumerate(passed):
            a, rel = divmod(j, 3)
            _descriptor(in_refs, send_sems, recv_sems, 4 * a + 1 + rel, first[4 * a + 1 + rel]).wait_recv()
            _descriptor(in_refs, send_sems, recv_sems, nf + j, cp).start()
        for a in range(n):
            _descriptor(in_refs, send_sems, recv_sems, 4 * a, first[4 * a]).wait_recv()
        for j, cp in enumerate(passed):
            _descriptor(in_refs, send_sems, recv_sems, nf + j, cp).wait_recv()
        for k, cp in enumerate(first + passed):
            _descriptor(in_refs, send_sems, recv_sems, k, cp).wait_send()

    nsem = len(first) + len(passed)
    out = pl.pallas_call(
        body, name=name, out_shape=tuple(_sds(a.shape, a.dtype) for a in arrays),
        in_specs=[ANY_SPEC] * n, out_specs=tuple([ANY_SPEC] * n),
        input_output_aliases={i: i for i in range(n)},
        scratch_shapes=[pltpu.SemaphoreType.DMA((nsem,)), pltpu.SemaphoreType.DMA((nsem,))],
    )(*arrays)
    return list(out)


def _pair_sum(g, land, pos, name):
    _, rows, cols = land.shape
    tr = min(rows, 128)
    strided = g.shape[0] == N_DEV

    def body(pos_ref, g_ref, l_ref, o_ref):
        o_ref[...] = (g_ref[...].astype(F32) + l_ref[...].astype(F32)).astype(BF16)

    g_map = (lambda q, i, p: (2 * q + p[2], i, 0)) if strided else (lambda q, i, p: (q, i, 0))
    blk = pl.BlockSpec((None, tr, cols), lambda q, i, p: (q, i, 0))
    return pl.pallas_call(
        body, name=name,
        grid_spec=pltpu.PrefetchScalarGridSpec(
            num_scalar_prefetch=1, grid=(N_CHIPS, rows // tr),
            in_specs=[pl.BlockSpec((None, tr, cols), g_map), blk], out_specs=blk),
        out_shape=_sds((N_CHIPS, rows, cols), BF16), compiler_params=_params("parallel", "parallel"),
    )(pos, g, land)


def _reduce_adamw(sums, land, w, m, v, pos, name):
    rows, cols = w.shape
    tr = min(rows, 64)

    def body(pos_ref, s_ref, l_ref, w_ref, m_ref, v_ref, g_ref, d_ref, nm_ref, nv_ref):
        g = s_ref[...].astype(F32)
        for k in range(N_CHIPS - 1):
            g = g + l_ref[k].astype(F32)
        delta, nm, nv = _adamw(w_ref[...], g, m_ref[...], v_ref[...])
        g_ref[...] = g
        d_ref[...] = delta
        nm_ref[...] = nm
        nv_ref[...] = nv

    spec = pl.BlockSpec((tr, cols), lambda i, p: (i, 0))
    return pl.pallas_call(
        body, name=name,
        grid_spec=pltpu.PrefetchScalarGridSpec(
            num_scalar_prefetch=1, grid=(rows // tr,),
            in_specs=[pl.BlockSpec((None, tr, cols), lambda i, p: (_chip(p[0], p[1]), i, 0)),
                      pl.BlockSpec((N_CHIPS - 1, tr, cols), lambda i, p: (0, i, 0)), spec, spec, spec],
            out_specs=(spec, spec, spec, spec)),
        out_shape=tuple([_sds((rows, cols), F32)] * 4), compiler_params=_params("parallel"),
    )(pos, sums, land, w, m, v)


def _small_allreduce_adamw(g, w, m, v):
    rows = g.shape[0]

    def body(g_ref, w_ref, m_ref, v_ref, gs_ref, d_ref, nm_ref, nv_ref, all_ref, send_sems, recv_sems):
        x, y, c = _mesh_pos()
        me = _slot(x, y, c)
        copies = []
        for k in range(1, N_DEV):
            peer = (x ^ (k >> 2), y ^ ((k >> 1) & 1), c ^ (k & 1))
            copies.append(pltpu.make_async_remote_copy(
                src_ref=g_ref, dst_ref=all_ref.at[me], send_sem=send_sems.at[k - 1],
                recv_sem=recv_sems.at[k - 1], device_id=peer, device_id_type=MESH))
        for cp in copies:
            cp.start()
        all_ref[me] = g_ref[...]
        for cp in copies:
            cp.wait_recv()
        total = all_ref[0]
        for s in range(1, N_DEV):
            total = total + all_ref[s]
        delta, nm, nv = _adamw(w_ref[...], total, m_ref[...], v_ref[...])
        gs_ref[...] = total
        d_ref[...] = delta
        nm_ref[...] = nm
        nv_ref[...] = nv
        for cp in copies:
            cp.wait_send()

    return pl.pallas_call(
        body, name="small_allreduce_adamw", out_shape=tuple([_sds((rows, LANES), F32)] * 4),
        in_specs=[VMEM_SPEC] * 4, out_specs=tuple([VMEM_SPEC] * 4),
        scratch_shapes=[pltpu.VMEM((N_DEV, rows, LANES), F32), pltpu.SemaphoreType.DMA((7,)),
                        pltpu.SemaphoreType.DMA((7,))],
    )(g, w, m, v)


_DOT_DIMS = {"nn": ((1,), (0,)), "nt": ((1,), (1,)), "tn": ((0,), (0,))}


def _dot(a, b, mode):
    return lax.dot_general(a, b, (_DOT_DIMS[mode], ((), ())), preferred_element_type=F32)


def _col_chunks(cols):
    return [(c0, min(c0 + DOT_COLS, cols)) for c0 in range(0, cols, DOT_COLS)]


def _matmul(a, b, mode, out_dtype, name, *, res=None, tm=1024, tn=1024):
    if mode == "tn":
        kdim, mdim = a.shape
    else:
        mdim, kdim = a.shape
    ndim = b.shape[0] if mode == "nt" else b.shape[1]
    tm, tn = min(tm, mdim), min(tn, ndim)
    assert mdim % tm == 0 and ndim % tn == 0, (name, mdim, ndim)

    def body(*refs):
        out = _dot(refs[0][...], refs[1][...], mode)
        if res is not None:
            out = out + refs[2][...]
        refs[-1][...] = out.astype(out_dtype)

    a_spec = pl.BlockSpec((kdim, tm), lambda i, j: (0, i)) if mode == "tn" else pl.BlockSpec((tm, kdim), lambda i, j: (i, 0))
    b_spec = pl.BlockSpec((tn, kdim), lambda i, j: (j, 0)) if mode == "nt" else pl.BlockSpec((kdim, tn), lambda i, j: (0, j))
    o_spec = pl.BlockSpec((tm, tn), lambda i, j: (i, j))
    in_specs, args = [a_spec, b_spec], [a, b]
    if res is not None:
        in_specs.append(o_spec)
        args.append(res)
    return pl.pallas_call(
        body, name=name, grid=(mdim // tm, ndim // tn), in_specs=in_specs, out_specs=o_spec,
        out_shape=_sds((mdim, ndim), out_dtype), compiler_params=_params("parallel", "parallel"),
    )(*args)


def _project(h, w_blocks, block_ids, name, *, proj=None, after=None, tm=1024, tk=512):
    s, d = h.shape
    _, _, cw = w_blocks.shape
    n = block_ids.shape[0]
    tm, tk = min(tm, s), min(tk, d)

    n_extra = (proj is not None) + (after is not None)

    def body(ids_ref, h_ref, w_ref, *rest):
        o_ref = rest[n_extra]
        k = pl.program_id(2)

        @pl.when(k == 0)
        def _():
            o_ref[...] = jnp.zeros_like(o_ref)

        for c0, c1 in _col_chunks(cw):
            o_ref[:, c0:c1] += _dot(h_ref[...], w_ref[:, c0:c1], "nn")

    in_specs = [pl.BlockSpec((tm, tk), lambda j, i, k, ids: (i, k)),
                pl.BlockSpec((None, tk, cw), lambda j, i, k, ids: (ids[j], k, 0))]
    args = [block_ids, h, w_blocks]
    aliases = {}
    if proj is not None:
        in_specs.append(ANY_SPEC)
        args.append(proj)
        aliases = {3: 0}
    if after is not None:
        in_specs.append(ANY_SPEC)
        args.append(after)
    return pl.pallas_call(
        body, name=name,
        grid_spec=pltpu.PrefetchScalarGridSpec(
            num_scalar_prefetch=1, grid=(n, s // tm, d // tk), in_specs=in_specs,
            out_specs=pl.BlockSpec((tm, cw), lambda j, i, k, ids: (i, ids[j]))),
        out_shape=_sds((s, N_DEV * cw), F32), input_output_aliases=aliases,
        compiler_params=_params("arbitrary", "arbitrary", "arbitrary", vmem=VMEM_LIMIT_WIDE),
    )(*args)


def _grad_w_in_blocks(h, d_proj, block_ids, cw, name, after=None, *, tm=1024, tk=1024):
    s, d = h.shape
    n = block_ids.shape[0]
    tm, tk = min(tm, d), min(tk, s)
    nk = s // tk

    def body(ids_ref, h_ref, g_ref, *rest):
        o_ref, acc_ref = rest[-2], rest[-1]
        k = pl.program_id(2)

        @pl.when(k == 0)
        def _():
            acc_ref[...] = jnp.zeros_like(acc_ref)

        for c0, c1 in _col_chunks(cw):
            acc_ref[:, c0:c1] += _dot(h_ref[...], g_ref[:, c0:c1], "tn")

        @pl.when(k == nk - 1)
        def _():
            o_ref[...] = acc_ref[...].astype(BF16)

    in_specs = [pl.BlockSpec((tk, tm), lambda q, i, k, ids: (k, i)),
                pl.BlockSpec((tk, cw), lambda q, i, k, ids: (k, ids[q]))]
    args = [block_ids, h, d_proj]
    if after is not None:
        in_specs.append(ANY_SPEC)
        args.append(after)
    return pl.pallas_call(
        body, name=name,
        grid_spec=pltpu.PrefetchScalarGridSpec(
            num_scalar_prefetch=1, grid=(n, d // tm, nk), in_specs=in_specs,
            out_specs=pl.BlockSpec((None, tm, cw), lambda q, i, k, ids: (q, i, 0)),
            scratch_shapes=[pltpu.VMEM((tm, cw), F32)]),
        out_shape=_sds((n, d, cw), BF16),
        compiler_params=_params("parallel", "parallel", "arbitrary", vmem=VMEM_LIMIT_WIDE),
    )(*args)


def _d_hidden(d_proj, w_blocks, after=None, *, tm=1024, tn=1024):
    s = d_proj.shape[0]
    nb, d, cw = w_blocks.shape
    tm, tn = min(tm, s), min(tn, d)

    def body(g_ref, w_ref, *rest):
        o_ref = rest[-1]
        k = pl.program_id(2)

        @pl.when(k == 0)
        def _():
            o_ref[...] = jnp.zeros_like(o_ref)

        o_ref[...] += _dot(g_ref[...], w_ref[...], "nt")

    in_specs = [pl.BlockSpec((tm, cw), lambda i, j, k: (i, k)),
                pl.BlockSpec((None, tn, cw), lambda i, j, k: (k, j, 0))]
    args = [d_proj, w_blocks]
    if after is not None:
        in_specs.append(ANY_SPEC)
        args.append(after)
    return pl.pallas_call(
        body, name="d_h", grid=(s // tm, d // tn, nb), in_specs=in_specs,
        out_specs=pl.BlockSpec((tm, tn), lambda i, j, k: (i, j)), out_shape=_sds((s, d), F32),
        compiler_params=_params("parallel", "parallel", "arbitrary", vmem=VMEM_LIMIT_WIDE),
    )(*args)


def _row_tile(rows):
    return min(rows, 128)


def _segment_specs(rows, d, col0):
    w = d // SEGMENT_SPLIT
    assert col0 % w == 0
    return [pl.BlockSpec((rows, w), lambda i, t=t: (i, col0 // w + t)) for t in range(SEGMENT_SPLIT)]


def _cat(refs):
    return jnp.concatenate([r[...] for r in refs], axis=1)


def _rmsnorm_fwd(x, g, after):
    s, d = x.shape
    tr = _row_tile(s)

    def body(x_ref, g_ref, after_ref, h_ref):
        xv = x_ref[...]
        r = lax.rsqrt(jnp.mean(xv * xv, axis=-1, keepdims=True) + NORM_EPS)
        h_ref[...] = (xv * r * g_ref[...]).astype(BF16)

    row = pl.BlockSpec((tr, d), lambda i: (i, 0))
    vec = pl.BlockSpec((1, d), lambda i: (0, 0))
    return pl.pallas_call(body, name="rmsnorm_fwd", grid=(s // tr,), in_specs=[row, vec, ANY_SPEC], out_specs=row,
                          out_shape=_sds((s, d), BF16), compiler_params=_params("parallel"))(x, g, after)


def _merge_fwd(y_a, y_b, proj, col_m):
    s, d = y_a.shape
    tr = _row_tile(s)
    ns = SEGMENT_SPLIT

    def body(ya_ref, yb_ref, *rest):
        ma, mb, o_ref = _cat(rest[:ns]), _cat(rest[ns:2 * ns]), rest[2 * ns]
        o_ref[...] = (jax.nn.sigmoid(ma) * ya_ref[...] + jax.nn.sigmoid(mb) * yb_ref[...]).astype(BF16)

    row = pl.BlockSpec((tr, d), lambda i: (i, 0))
    return pl.pallas_call(
        body, name="merge_fwd", grid=(s // tr,),
        in_specs=[row, row, *_segment_specs(tr, d, col_m), *_segment_specs(tr, d, col_m + d)],
        out_specs=row, out_shape=_sds((s, d), BF16), compiler_params=_params("parallel"),
    )(y_a, y_b, *([proj] * (2 * ns)))


def _loss_and_final_norm_bwd(x2, target, g):
    s, d = x2.shape
    tr = _row_tile(s)

    def body(x_ref, t_ref, g_ref, loss_ref, dg_ref, dx_ref, dxb_ref):
        @pl.when(pl.program_id(0) == 0)
        def _():
            loss_ref[...] = jnp.zeros_like(loss_ref)
            dg_ref[...] = jnp.zeros_like(dg_ref)

        xv, gv = x_ref[...], g_ref[...]
        r = lax.rsqrt(jnp.mean(xv * xv, axis=-1, keepdims=True) + NORM_EPS)
        xhat = xv * r
        err = xhat * gv - t_ref[...]
        loss_ref[...] += 0.5 * jnp.sum(jnp.mean(err * err, axis=-1, keepdims=True))
        dy = err / d
        dg_ref[...] += jnp.sum(dy * xhat, axis=0, keepdims=True)
        dyg = dy * gv
        dx = r * (dyg - xhat * jnp.mean(dyg * xhat, axis=-1, keepdims=True))
        dx_ref[...] = dx
        dxb_ref[...] = dx.astype(BF16)

    row = pl.BlockSpec((tr, d), lambda i: (i, 0))
    vec = pl.BlockSpec((1, d), lambda i: (0, 0))
    return pl.pallas_call(
        body, name="loss_final_norm_bwd", grid=(s // tr,), in_specs=[row, row, vec],
        out_specs=(pl.BlockSpec((SUBLANES, LANES), lambda i: (0, 0)), vec, row, row),
        out_shape=(_sds((SUBLANES, LANES), F32), _sds((1, d), F32), _sds((s, d), F32), _sds((s, d), BF16)),
        compiler_params=_params("arbitrary"))(x2, target, g)


def _merge_bwd(d_merged, y_a, y_b, proj, col_m):
    s, d = y_a.shape
    tr = _row_tile(s)
    ns = SEGMENT_SPLIT

    def body(dm_ref, ya_ref, yb_ref, *rest):
        ma, mb = _cat(rest[:ns]), _cat(rest[ns:2 * ns])
        dya_ref, dyb_ref, dmg_ref = rest[2 * ns:]
        dm = dm_ref[...]
        sa = jax.nn.sigmoid(ma)
        sb = jax.nn.sigmoid(mb)
        dya_ref[...] = (dm * sa).astype(BF16)
        dyb_ref[...] = (dm * sb).astype(BF16)
        dmg_ref[:, :d] = (dm * ya_ref[...] * (sa * (1.0 - sa))).astype(BF16)
        dmg_ref[:, d:] = (dm * yb_ref[...] * (sb * (1.0 - sb))).astype(BF16)

    row = pl.BlockSpec((tr, d), lambda i: (i, 0))
    wide = pl.BlockSpec((tr, 2 * d), lambda i: (i, 0))
    return pl.pallas_call(
        body, name="merge_bwd", grid=(s // tr,),
        in_specs=[row, row, row, *_segment_specs(tr, d, col_m), *_segment_specs(tr, d, col_m + d)],
        out_specs=(row, row, wide),
        out_shape=(_sds((s, d), BF16), _sds((s, d), BF16), _sds((s, 2 * d), BF16)),
        compiler_params=_params("parallel"))(d_merged, y_a, y_b, *([proj] * (2 * ns)))


def _input_grad(d_h, x, g, dx2):
    s, d = x.shape
    tr = _row_tile(s)

    def body(dh_ref, x_ref, g_ref, dx2_ref, gx_ref, dg_ref):
        @pl.when(pl.program_id(0) == 0)
        def _():
            dg_ref[...] = jnp.zeros_like(dg_ref)

        xv, dh = x_ref[...], dh_ref[...]
        r = lax.rsqrt(jnp.mean(xv * xv, axis=-1, keepdims=True) + NORM_EPS)
        xhat = xv * r
        dg_ref[...] += jnp.sum(dh * xhat, axis=0, keepdims=True)
        dyg = dh * g_ref[...]
        gx_ref[...] = dx2_ref[...] + r * (dyg - xhat * jnp.mean(dyg * xhat, axis=-1, keepdims=True))

    row = pl.BlockSpec((tr, d), lambda i: (i, 0))
    vec = pl.BlockSpec((1, d), lambda i: (0, 0))
    return pl.pallas_call(
        body, name="input_grad", grid=(s // tr,), in_specs=[row, row, vec, row], out_specs=(row, vec),
        out_shape=(_sds((s, d), F32), _sds((1, d), F32)), compiler_params=_params("arbitrary"))(d_h, x, g, dx2)


def _rope_tables(positions):
    inv_freq = ROPE_THETA ** (-jnp.arange(ROPE_HALF, dtype=F32) * 2.0 / ROPE_DIM)
    ang = positions.astype(F32)[:, None] * inv_freq
    cos, sin = jnp.cos(ang), jnp.sin(ang)
    zero = jnp.zeros((positions.shape[0], HEAD_DIM - ROPE_DIM), F32)
    zero_h = jnp.zeros_like(sin)
    c = jnp.concatenate([cos, cos, zero + 1.0], axis=1)
    up = jnp.concatenate([-sin, zero_h, zero], axis=1)
    down = jnp.concatenate([zero_h, sin, zero], axis=1)
    reps = LANES // HEAD_DIM
    return jnp.stack([jnp.tile(c, (1, reps)), jnp.tile(up, (1, reps)), jnp.tile(down, (1, reps))])


def _lane_tiles(x):
    return [x[:, t * LANES:(t + 1) * LANES] for t in range(x.shape[1] // LANES)]


def _rope(x, tab):
    out = [xt * tab[0] + pltpu.roll(xt, LANES - ROPE_HALF, 1) * tab[1] + pltpu.roll(xt, ROPE_HALF, 1) * tab[2]
           for xt in _lane_tiles(x)]
    return out[0] if len(out) == 1 else jnp.concatenate(out, axis=1)


def _rope_bwd(g, tab):
    out = [gt * tab[0] + pltpu.roll(gt * tab[1], ROPE_HALF, 1) + pltpu.roll(gt * tab[2], LANES - ROPE_HALF, 1)
           for gt in _lane_tiles(g)]
    return out[0] if len(out) == 1 else jnp.concatenate(out, axis=1)


def _head(x, h):
    return x[:, h * HEAD_DIM:(h + 1) * HEAD_DIM]


def _stack_heads(x, first, count):
    return jnp.concatenate([_head(x, first + h) for h in range(count)], axis=0)


def _dot_nt(a, b):
    return lax.dot_general(a, b, (((1,), (1,)), ((), ())), preferred_element_type=F32)


def _causal(rows):
    qi = lax.broadcasted_iota(jnp.int32, (rows, BLOCK), 0) % BLOCK
    return lax.broadcasted_iota(jnp.int32, (rows, BLOCK), 1) <= qi


def _band_probs(qs, k_prev, k_cur, sink, causal, blk):
    s_prev = jnp.where(blk > 0, _dot_nt(qs, k_prev), -jnp.inf)
    s = jnp.where(causal, _dot_nt(qs, k_cur), s_prev)
    m = jnp.maximum(jnp.max(s, axis=-1, keepdims=True), sink)
    p = jnp.exp(s - m)
    p_sink = jnp.exp(sink - m)
    inv = 1.0 / (jnp.sum(p, axis=-1, keepdims=True) + p_sink)
    return p * inv, p_sink * inv


def _sink_column(sink_ref, first, count):
    return jnp.concatenate([jnp.full((BLOCK, 1), sink_ref[first + h], F32) for h in range(count)], axis=0)


def _split_band(x, causal):
    return jnp.where(causal, x, 0.0).astype(BF16), jnp.where(causal, 0.0, x).astype(BF16)


def _attn_dims(s, d, kv):
    n_kv = kv // HEAD_DIM
    group = d // kv
    qw = 2 * group * HEAD_DIM
    assert n_kv % 2 == 0 and (d + 2 * kv) % qw == 0 and s % BLOCK == 0
    return group, qw, n_kv // 2, s // BLOCK


def _attention_fwd(proj, tables, sink, kv, after):
    s, d = proj.shape[0], sink.shape[0] * HEAD_DIM
    group, qw, n_pairs, nb = _attn_dims(s, d, kv)

    def body(sink_ref, q_ref, kc_ref, kp_ref, vc_ref, vp_ref, ga_ref, tc_ref, tp_ref, after_ref, attn_ref, ain_ref):
        pair, blk = pl.program_id(0), pl.program_id(1)
        tab_c, tab_p = tc_ref[...], tp_ref[...]
        q = _rope(q_ref[...], tab_c)
        k_cur, k_prev = _rope(kc_ref[...], tab_c), _rope(kp_ref[...], tab_p)
        v_cur, v_prev = vc_ref[...], vp_ref[...]
        causal = _causal(group * BLOCK)
        outs = []
        for a in range(2):
            qs = (_stack_heads(q, a * group, group) * ATTN_SCALE).astype(BF16)
            sink_col = _sink_column(sink_ref, (2 * pair + a) * group, group)
            p, _ = _band_probs(qs, _head(k_prev, a).astype(BF16), _head(k_cur, a).astype(BF16), sink_col, causal, blk)
            p_cur, p_prev = _split_band(p, causal)
            o = (jnp.dot(p_cur, _head(v_cur, a).astype(BF16), preferred_element_type=F32)
                 + jnp.dot(p_prev, _head(v_prev, a).astype(BF16), preferred_element_type=F32))
            outs += [o[h * BLOCK:(h + 1) * BLOCK] for h in range(group)]
        attn = jnp.concatenate(outs, axis=1)
        attn_ref[...] = attn
        silu, _ = _silu_and_grad(ga_ref[...])
        ain_ref[...] = (attn * silu).astype(BF16)

    k0, v0, g0 = d // LANES, (d + kv) // LANES, (d + 2 * kv) // qw
    prev = lambda i: jnp.maximum(i - 1, 0)
    in_specs = [
        SMEM_SPEC,
        pl.BlockSpec((BLOCK, qw), lambda p, i: (i, p)),
        pl.BlockSpec((BLOCK, LANES), lambda p, i: (i, k0 + p)),
        pl.BlockSpec((BLOCK, LANES), lambda p, i: (prev(i), k0 + p)),
        pl.BlockSpec((BLOCK, LANES), lambda p, i: (i, v0 + p)),
        pl.BlockSpec((BLOCK, LANES), lambda p, i: (prev(i), v0 + p)),
        pl.BlockSpec((BLOCK, qw), lambda p, i: (i, g0 + p)),
        pl.BlockSpec((3, BLOCK, LANES), lambda p, i: (0, i, 0)),
        pl.BlockSpec((3, BLOCK, LANES), lambda p, i: (0, prev(i), 0)),
        ANY_SPEC,
    ]
    out = pl.BlockSpec((BLOCK, qw), lambda p, i: (i, p))
    return pl.pallas_call(
        body, name="attention_fwd", grid=(n_pairs, nb), in_specs=in_specs, out_specs=(out, out),
        out_shape=(_sds((s, d), F32), _sds((s, d), BF16)), compiler_params=_params("parallel", "parallel"),
    )(sink, proj, proj, proj, proj, proj, proj, tables, tables, after)


def _attention_bwd(proj, tables, sink, kv, attn, d_ain, after):
    s, d = proj.shape[0], sink.shape[0] * HEAD_DIM
    group, qw, n_pairs, nb = _attn_dims(s, d, kv)

    def body(sink_ref, q_ref, kc_ref, kp_ref, vc_ref, vp_ref, ga_ref, tc_ref, tp_ref, attn_ref, dain_ref, after_ref,
             dq_ref, dk_ref, dv_ref, dga_ref, dsink_ref, carry_k, carry_v):
        pair, blk = pl.program_id(0), pl.program_id(1)

        @pl.when(blk == 0)
        def _():
            carry_k[...] = jnp.zeros_like(carry_k)
            carry_v[...] = jnp.zeros_like(carry_v)
            dsink_ref[...] = jnp.zeros_like(dsink_ref)

        @pl.when(blk < nb)
        def _():
            tab_c, tab_p = tc_ref[...], tp_ref[...]
            q = _rope(q_ref[...], tab_c)
            k_cur, k_prev = _rope(kc_ref[...], tab_c), _rope(kp_ref[...], tab_p)
            v_cur, v_prev = vc_ref[...], vp_ref[...]
            silu, silu_grad = _silu_and_grad(ga_ref[...])
            d_ain_v = dain_ref[...]
            dga_ref[...] = (d_ain_v * attn_ref[...] * silu_grad).astype(BF16)
            d_attn = d_ain_v * silu
            q_t = (q * ATTN_SCALE).T
            d_attn_t = d_attn.T
            causal = _causal(group * BLOCK)
            dq_parts = []
            dk_t = {"cur": [], "prev": []}
            dv_t = {"cur": [], "prev": []}
            lane = lax.broadcasted_iota(jnp.int32, (1, LANES), 1)
            dsink = jnp.zeros((1, LANES), F32)
            for a in range(2):
                first = a * group
                qs = (_stack_heads(q, first, group) * ATTN_SCALE).astype(BF16)
                kc, kp = _head(k_cur, a).astype(BF16), _head(k_prev, a).astype(BF16)
                vc, vp = _head(v_cur, a).astype(BF16), _head(v_prev, a).astype(BF16)
                sink_col = _sink_column(sink_ref, (2 * pair + a) * group, group)
                p, p_sink = _band_probs(qs, kp, kc, sink_col, causal, blk)
                do = _stack_heads(d_attn, first, group).astype(BF16)
                dp = jnp.where(causal, _dot_nt(do, vc), _dot_nt(do, vp))
                delta = jnp.sum(p * dp, axis=-1, keepdims=True)
                ds_cur, ds_prev = _split_band(p * (dp - delta), causal)
                p_cur, p_prev = _split_band(p, causal)
                dqs = (jnp.dot(ds_cur, kc, preferred_element_type=F32)
                       + jnp.dot(ds_prev, kp, preferred_element_type=F32)) * ATTN_SCALE
                dq_parts += [dqs[h * BLOCK:(h + 1) * BLOCK] for h in range(group)]
                rows = lambda t: jnp.concatenate(
                    [t[(first + h) * HEAD_DIM:(first + h + 1) * HEAD_DIM] for h in range(group)], axis=1).astype(BF16)
                qs_t, do_t = rows(q_t), rows(d_attn_t)
                dk_t["cur"].append(jnp.dot(qs_t, ds_cur, preferred_element_type=F32))
                dk_t["prev"].append(jnp.dot(qs_t, ds_prev, preferred_element_type=F32))
                dv_t["cur"].append(jnp.dot(do_t, p_cur, preferred_element_type=F32))
                dv_t["prev"].append(jnp.dot(do_t, p_prev, preferred_element_type=F32))
                ds_sink = -(p_sink * delta)
                for h in range(group):
                    dsink = dsink + jnp.where(lane == first + h, jnp.sum(ds_sink[h * BLOCK:(h + 1) * BLOCK]), 0.0)
            dsink_ref[0] += dsink
            dq_ref[...] = _rope_bwd(jnp.concatenate(dq_parts, axis=1), tab_c).astype(BF16)
            pair_block = lambda parts: jnp.concatenate(parts, axis=0).T
            dk_ref[...] = (carry_k[...] + _rope_bwd(pair_block(dk_t["prev"]), tab_p)).astype(BF16)
            dv_ref[...] = (carry_v[...] + pair_block(dv_t["prev"])).astype(BF16)
            carry_k[...] = _rope_bwd(pair_block(dk_t["cur"]), tab_c)
            carry_v[...] = pair_block(dv_t["cur"])

        @pl.when(blk == nb)
        def _():
            dk_ref[...] = carry_k[...].astype(BF16)
            dv_ref[...] = carry_v[...].astype(BF16)

    k0, v0, g0 = d // LANES, (d + kv) // LANES, (d + 2 * kv) // qw
    cur = lambda i: jnp.minimum(i, nb - 1)
    prev = lambda i: jnp.maximum(cur(i) - 1, 0)
    back = lambda i: jnp.maximum(i - 1, 0)
    q_spec = pl.BlockSpec((BLOCK, qw), lambda p, i: (cur(i), p))
    in_specs = [
        SMEM_SPEC,
        q_spec,
        pl.BlockSpec((BLOCK, LANES), lambda p, i: (cur(i), k0 + p)),
        pl.BlockSpec((BLOCK, LANES), lambda p, i: (prev(i), k0 + p)),
        pl.BlockSpec((BLOCK, LANES), lambda p, i: (cur(i), v0 + p)),
        pl.BlockSpec((BLOCK, LANES), lambda p, i: (prev(i), v0 + p)),
        pl.BlockSpec((BLOCK, qw), lambda p, i: (cur(i), g0 + p)),
        pl.BlockSpec((3, BLOCK, LANES), lambda p, i: (0, cur(i), 0)),
        pl.BlockSpec((3, BLOCK, LANES), lambda p, i: (0, prev(i), 0)),
        q_spec,
        q_spec,
        ANY_SPEC,
    ]
    kv_out = pl.BlockSpec((BLOCK, LANES), lambda p, i: (back(i), p))
    return pl.pallas_call(
        body, name="attention_bwd", grid=(n_pairs, nb + 1), in_specs=in_specs,
        out_specs=(q_spec, kv_out, kv_out, q_spec, pl.BlockSpec((1, 1, LANES), lambda p, i: (p, 0, 0))),
        out_shape=(_sds((s, d), BF16), _sds((s, kv), BF16), _sds((s, kv), BF16), _sds((s, d), BF16),
                   _sds((n_pairs, 1, LANES), F32)),
        scratch_shapes=[pltpu.VMEM((BLOCK, LANES), F32), pltpu.VMEM((BLOCK, LANES), F32)],
        compiler_params=_params("parallel", "arbitrary"),
    )(sink, proj, proj, proj, proj, proj, proj, tables, tables, attn, d_ain, after)


def _gmlp_core(u, vg, ln_g, ln_b, w_ref, bias_t):
    gu = _gelu(u)
    gv = _gelu(vg)
    xc = gv - jnp.mean(gv, axis=-1, keepdims=True)
    rstd = lax.rsqrt(jnp.mean(xc * xc, axis=-1, keepdims=True) + LN_EPS)
    xhat = xc * rstd
    vn = (xhat * ln_g + ln_b).astype(BF16)
    gd = u.shape[1] // GMLP_GROUPS
    tri = (lax.broadcasted_iota(jnp.int32, (BLOCK, BLOCK), 0) >= lax.broadcasted_iota(jnp.int32, (BLOCK, BLOCK), 1))
    w_tri = [jnp.where(tri, w_ref[g], 0.0).astype(BF16) for g in range(GMLP_GROUPS)]
    mixed = jnp.concatenate(
        [jnp.dot(w_tri[g], vn[:, g * gd:(g + 1) * gd], preferred_element_type=F32) + bias_t[:, g:g + 1]
         for g in range(GMLP_GROUPS)], axis=1)
    return gu, xhat, rstd, vn, w_tri, tri, mixed


def _whole(shape):
    return pl.BlockSpec(shape, lambda i: tuple(0 for _ in shape))


def _gmlp_fwd(proj, col_u, d, w_s, bias_t, ln_g, ln_b, after):
    s = proj.shape[0]
    ns = SEGMENT_SPLIT

    def body(*refs):
        u, vg, gb = _cat(refs[:ns]), _cat(refs[ns:2 * ns]), _cat(refs[2 * ns:3 * ns])
        w_ref, bt_ref, lg_ref, lb_ref, after_ref, o_ref = refs[3 * ns:]
        gu, _, _, _, _, _, mixed = _gmlp_core(u, vg, lg_ref[...], lb_ref[...], w_ref, bt_ref[...])
        silu, _ = _silu_and_grad(gb)
        o_ref[...] = ((gu * mixed) * silu).astype(BF16)

    segs = [sp for j in range(3) for sp in _segment_specs(BLOCK, d, col_u + j * d)]
    return pl.pallas_call(
        body, name="gmlp_fwd", grid=(s // BLOCK,),
        in_specs=[*segs, _whole(w_s.shape), _whole(bias_t.shape), _whole((1, d)), _whole((1, d)), ANY_SPEC],
        out_specs=pl.BlockSpec((BLOCK, d), lambda i: (i, 0)), out_shape=_sds((s, d), BF16),
        compiler_params=_params("parallel"),
    )(*([proj] * (3 * ns)), w_s, bias_t, ln_g, ln_b, after)


def _gmlp_bwd(proj, col_u, d, w_s, bias_t, ln_g, ln_b, d_bin, after):
    s = proj.shape[0]
    gd = d // GMLP_GROUPS
    ns = SEGMENT_SPLIT

    def body(*refs):
        u, vg, gb = _cat(refs[:ns]), _cat(refs[ns:2 * ns]), _cat(refs[2 * ns:3 * ns])
        w_ref, bt_ref, lg_ref, lb_ref, dbin_ref, after_ref, dg_ref, dw_ref, dbt_ref, dlg_ref, dlb_ref = refs[3 * ns:]

        @pl.when(pl.program_id(0) == 0)
        def _():
            dw_ref[...] = jnp.zeros_like(dw_ref)
            dbt_ref[...] = jnp.zeros_like(dbt_ref)
            dlg_ref[...] = jnp.zeros_like(dlg_ref)
            dlb_ref[...] = jnp.zeros_like(dlb_ref)

        ln_g = lg_ref[...]
        gu, xhat, rstd, vn, w_tri, tri, mixed = _gmlp_core(u, vg, ln_g, lb_ref[...], w_ref, bt_ref[...])
        silu, silu_grad = _silu_and_grad(gb)
        d_bin_v = dbin_ref[...]
        d_sg = d_bin_v * silu
        dg_ref[:, 2 * d:] = (d_bin_v * (gu * mixed) * silu_grad).astype(BF16)
        dg_ref[:, :d] = (d_sg * mixed * _gelu_grad(u)).astype(BF16)
        d_mixed = d_sg * gu
        d_mixed_b = d_mixed.astype(BF16)
        d_vn, d_bias = [], []
        for g in range(GMLP_GROUPS):
            dm_g = d_mixed_b[:, g * gd:(g + 1) * gd]
            d_bias.append(jnp.sum(d_mixed[:, g * gd:(g + 1) * gd], axis=-1, keepdims=True))
            dw = lax.dot_general(dm_g, vn[:, g * gd:(g + 1) * gd], (((1,), (1,)), ((), ())),
                                 preferred_element_type=F32)
            dw_ref[g] += jnp.where(tri, dw, 0.0)
            d_vn.append(lax.dot_general(w_tri[g], dm_g, (((0,), (0,)), ((), ())), preferred_element_type=F32))
        dbt_ref[...] += jnp.concatenate(d_bias, axis=1)
        d_vn = jnp.concatenate(d_vn, axis=1)
        dlg_ref[...] += jnp.sum(d_vn * xhat, axis=0, keepdims=True)
        dlb_ref[...] += jnp.sum(d_vn, axis=0, keepdims=True)
        d_xhat = d_vn * ln_g
        d_gv = rstd * (d_xhat - jnp.mean(d_xhat, axis=-1, keepdims=True)
                       - xhat * jnp.mean(d_xhat * xhat, axis=-1, keepdims=True))
        dg_ref[:, d:2 * d] = (d_gv * _gelu_grad(vg)).astype(BF16)

    segs = [sp for j in range(3) for sp in _segment_specs(BLOCK, d, col_u + j * d)]
    return pl.pallas_call(
        body, name="gmlp_bwd", grid=(s // BLOCK,),
        in_specs=[*segs, _whole(w_s.shape), _whole(bias_t.shape), _whole((1, d)), _whole((1, d)),
                  pl.BlockSpec((BLOCK, d), lambda i: (i, 0)), ANY_SPEC],
        out_specs=(pl.BlockSpec((BLOCK, 3 * d), lambda i: (i, 0)), _whole(w_s.shape), _whole(bias_t.shape),
                   _whole((1, d)), _whole((1, d))),
        out_shape=(_sds((s, 3 * d), BF16), _sds(w_s.shape, F32), _sds(bias_t.shape, F32), _sds((1, d), F32),
                   _sds((1, d), F32)),
        compiler_params=_params("arbitrary"),
    )(*([proj] * (3 * ns)), w_s, bias_t, ln_g, ln_b, d_bin, after)


def _pack(parts):
    rows = []
    tile = SUBLANES * LANES
    for p in parts:
        flat = p.astype(F32).reshape(-1)
        padded = -(-flat.shape[0] // tile) * tile
        rows.append(jnp.pad(flat, (0, padded - flat.shape[0])).reshape(-1, LANES))
    return jnp.concatenate(rows, axis=0)


def _unpack(packed, shapes):
    out, row = [], 0
    tile = SUBLANES * LANES
    for shape in shapes:
        size = math.prod(shape)
        n_rows = -(-size // tile) * SUBLANES
        out.append(packed[row:row + n_rows].reshape(-1)[:size].reshape(shape))
        row += n_rows
    return out


def kernel(x, positions, norm_g, w_in, attn_sink, gmlp_ln_g, gmlp_ln_b, w_spatial, b_spatial, w_up_attn, w_up_gmlp, w_out, final_norm_g, loss_target, m_norm_g, m_w_in, m_attn_sink, m_gmlp_ln_g, m_gmlp_ln_b, m_w_spatial, m_b_spatial, m_w_up_attn, m_w_up_gmlp, m_w_out, m_final_norm_g, v_norm_g, v_w_in, v_attn_sink, v_gmlp_ln_g, v_gmlp_ln_b, v_w_spatial, v_b_spatial, v_w_up_attn, v_w_up_gmlp, v_w_out, v_final_norm_g):
    x2d, target = x[0], loss_target[0]
    s, d = x2d.shape
    n_q_heads = attn_sink.shape[1]
    cw = w_in.shape[2]
    rw = w_up_attn.shape[1]
    kv = (cw * N_DEV - 7 * d) // 2
    col_u, col_m = 2 * d + 2 * kv, 5 * d + 2 * kv
    final_g = final_norm_g.reshape(1, d)
    sink = attn_sink[0]
    w_s = w_spatial[0]
    bias_t = b_spatial[0].T
    mx, my, mc = _mesh_pos()
    pos = jnp.stack([mx, my, mc]).astype(jnp.int32)
    chips = jnp.arange(N_CHIPS, dtype=jnp.int32)

    def one_block(fn):
        return jnp.reshape(fn(mx, my, mc), (1,)).astype(jnp.int32)

    w_in_b = _cast_into_slot(w_in[0], pos, "cast_w_in")
    squares = [_cast_into_slot(w[0], pos, "cast_" + nm)
               for nm, w in (("w_up_attn", w_up_attn), ("w_up_gmlp", w_up_gmlp), ("w_out", w_out))]
    to_sibling = [_Copy(0, _slot, 0, _slot, _sibling)]
    ici = [[_Copy(0, _slot, 0, _slot, lambda x, y, c, chip=chip: (*chip(x, y, c), c))] for chip in _ICI_STAGES[:2]]
    relayed = lambda x, y, c: _slot(*_ICI_STAGES[1](x, y, c), c)
    ici.append([_Copy(0, relayed, 0, relayed, lambda x, y, c: (*_ICI_STAGES[0](x, y, c), c))])
    passes = []
    for chip in _ICI_STAGES:
        landed = lambda x, y, c, chip=chip: _slot(*chip(x, y, c), c)
        passes.append([_Copy(0, landed, 0, landed, _sibling)])
    sib_sems = _rdma_start("w_in_sibling_start", [w_in_b], to_sibling)
    ici_sems = _rdma_start("w_in_ici0_start", sib_sems[2], ici[0])
    h = _rmsnorm_fwd(x2d, norm_g, ici_sems[3])
    proj = _project(h, ici_sems[2][0], one_block(_slot), "projection_own")
    w_blocks = _rdma_wait("w_in_sibling_wait", ici_sems[2], sib_sems[0], sib_sems[1], to_sibling, proj)
    proj = _project(h, w_blocks[0], one_block(lambda x, y, c: _slot(x, y, 1 - c)), "projection_sibling", proj=proj)
    w_blocks = _rdma_wait("w_in_ici0_wait", w_blocks, ici_sems[0], ici_sems[1], ici[0], proj)
    first = _gather_first_copies(3)
    for k, chip in enumerate(_ICI_STAGES):
        if k + 1 < len(_ICI_STAGES):
            ici_sems = _rdma_start("w_in_ici%d_start" % (k + 1), w_blocks, ici[k + 1])
            w_blocks = ici_sems[2]
        else:
            send1, recv1, thru, _ = _rdma_start("gather_squares_start", squares + w_blocks, first)
            squares, w_blocks = thru[:3], thru[3:]
        pass_sems = _rdma_start("w_in_pass%d_start" % k, w_blocks, passes[k])
        proj = _project(h, pass_sems[2][0], one_block(lambda x, y, c, chip=chip: _slot(*chip(x, y, c), c)),
                        "projection_ici%d" % k, proj=proj, after=pass_sems[3])
        w_blocks = _rdma_wait("w_in_pass%d_wait" % k, pass_sems[2], pass_sems[0], pass_sems[1], passes[k], proj)
        proj = _project(h, w_blocks[0], one_block(lambda x, y, c, chip=chip: _slot(*chip(x, y, 1 - c), 1 - c)),
                        "projection_pass%d" % k, proj=proj)
        if k + 1 < len(_ICI_STAGES):
            w_blocks = _rdma_wait("w_in_ici%d_wait" % (k + 1), w_blocks, ici_sems[0], ici_sems[1], ici[k + 1], proj)
    w_in_b = w_blocks[0]

    tables = _rope_tables(positions[0])
    attn, a_in = _attention_fwd(proj, tables, sink, kv, proj)
    squares = _rdma_wait("gather_squares_wait", squares, send1, recv1, first, attn)
    passed = _gather_pass_copies(3)
    send2, recv2, squares, token = _rdma_start("pass_squares_start", squares, passed)
    b_in = _gmlp_fwd(proj, col_u, d, w_s, bias_t, gmlp_ln_g, gmlp_ln_b, token)
    squares = _rdma_wait("pass_squares_wait", squares, send2, recv2, passed, b_in)
    w_ua, w_ug, w_o = [w.reshape(N_DEV * rw, d) for w in squares]
    y_a = _matmul(a_in, w_ua, "nn", F32, "up_attn")
    y_b = _matmul(b_in, w_ug, "nn", F32, "up_gmlp")
    merged = _merge_fwd(y_a, y_b, proj, col_m)
    x_out = _matmul(merged, w_o, "nn", F32, "out_proj", res=x2d, tn=512)
    loss_p, d_final_g, dx2, dx2_b = _loss_and_final_norm_bwd(x_out, target, final_g)

    d_merged = _matmul(dx2_b, w_o, "nt", F32, "d_merged")
    g_w_out = _matmul(merged, dx2_b, "tn", BF16, "g_w_out")
    d_ya, d_yb, d_mg = _merge_bwd(d_merged, y_a, y_b, proj, col_m)
    d_ain = _matmul(d_ya, w_ua, "nt", F32, "d_a_in")
    g_w_ua = _matmul(a_in, d_ya, "tn", BF16, "g_w_up_attn")
    d_bin = _matmul(d_yb, w_ug, "nt", F32, "d_b_in")
    g_w_ug = _matmul(b_in, d_yb, "tn", BF16, "g_w_up_gmlp")
    sq_grads = [g.reshape(N_DEV, rw, d) for g in (g_w_ua, g_w_ug, g_w_out)]
    sq_land = [lax.empty((N_CHIPS, rw, d), BF16) for _ in sq_grads]
    pairs_sq = _pair_copies_strided(3)
    arrays = [a for gl in zip(sq_grads, sq_land) for a in gl]
    send3, recv3, arrays, token = _rdma_start("pair_squares_start", arrays, pairs_sq)
    d_q, d_k, d_v, d_ga, d_sink = _attention_bwd(proj, tables, sink, kv, attn, d_ain, token)
    arrays = _rdma_wait("pair_squares_wait", arrays, send3, recv3, pairs_sq, d_q)
    sq_sums = [_pair_sum(arrays[2 * a], arrays[2 * a + 1], pos, "pair_sum_%d" % a) for a in range(3)]
    sq_land2 = [lax.empty((N_CHIPS - 1, rw, d), BF16) for _ in sq_sums]
    chip_sq = _chip_sum_copies(3)
    arrays = [a for gl in zip(sq_sums, sq_land2) for a in gl]
    send4, recv4, sq_arrays, token = _rdma_start("chip_squares_start", arrays, chip_sq)
    d_g, d_w_s, d_bias_t, d_ln_g, d_ln_b = _gmlp_bwd(proj, col_u, d, w_s, bias_t, gmlp_ln_g, gmlp_ln_b, d_bin, token)
    d_proj = jnp.concatenate([d_q, d_k, d_v, d_ga, d_g, d_mg], axis=1)

    g_sib = _grad_w_in_blocks(h, d_proj, 2 * chips + 1 - mc, cw, "g_w_in_sibling")
    pairs_in = _pair_copies(1)
    send5, recv5, arrays, token = _rdma_start("pair_w_in_start", [g_sib, lax.empty((N_CHIPS, d, cw), BF16)], pairs_in)
    g_own = _grad_w_in_blocks(h, d_proj, 2 * chips + mc, cw, "g_w_in_own", after=token)
    arrays = _rdma_wait("pair_w_in_wait", arrays, send5, recv5, pairs_in, g_own)
    in_sums = _pair_sum(g_own, arrays[1], pos, "pair_sum_w_in")
    chip_in = _chip_sum_copies(1)
    send6, recv6, in_arrays, token = _rdma_start(
        "chip_w_in_start", [in_sums, lax.empty((N_CHIPS - 1, d, cw), BF16)], chip_in)
    d_h = _d_hidden(d_proj, w_in_b, after=token)
    grad_x, d_norm_g = _input_grad(d_h, x2d, norm_g, dx2)

    sq_arrays = _rdma_wait("chip_squares_wait", sq_arrays, send4, recv4, chip_sq, grad_x)
    big = {}
    for a, (name, w, m, v) in enumerate((("w_up_attn", w_up_attn, m_w_up_attn, v_w_up_attn),
                                         ("w_up_gmlp", w_up_gmlp, m_w_up_gmlp, v_w_up_gmlp),
                                         ("w_out", w_out, m_w_out, v_w_out))):
        big[name] = [r[None] for r in _reduce_adamw(sq_arrays[2 * a], sq_arrays[2 * a + 1], w[0], m[0], v[0], pos,
                                                    "adamw_" + name)]
    in_arrays = _rdma_wait("chip_w_in_wait", in_arrays, send6, recv6, chip_in, big["w_out"][0])
    big["w_in"] = [r[None] for r in _reduce_adamw(in_arrays[0], in_arrays[1], w_in[0], m_w_in[0], v_w_in[0], pos,
                                                  "adamw_w_in")]

    heads_per_pair = 2 * n_q_heads // (kv // HEAD_DIM)
    g_sink = d_sink[:, 0, :heads_per_pair].reshape(1, n_q_heads)
    small_w = [norm_g, attn_sink, gmlp_ln_g, gmlp_ln_b, w_spatial, b_spatial, final_norm_g]
    small_m = [m_norm_g, m_attn_sink, m_gmlp_ln_g, m_gmlp_ln_b, m_w_spatial, m_b_spatial, m_final_norm_g]
    small_v = [v_norm_g, v_attn_sink, v_gmlp_ln_g, v_gmlp_ln_b, v_w_spatial, v_b_spatial, v_final_norm_g]
    small_g = [d_norm_g, g_sink, d_ln_g, d_ln_b, d_w_s[None], d_bias_t.T[None], d_final_g.reshape(d)]
    loss_pad = jnp.zeros((1,), F32)
    shapes = [w.shape for w in small_w] + [(1,)]
    packed = _small_allreduce_adamw(_pack(small_g + [loss_p[0, :1]]), _pack(small_w + [loss_pad]),
                                    _pack(small_m + [loss_pad]), _pack(small_v + [loss_pad]))
    sg, sd, sm, sv = [_unpack(p, shapes) for p in packed]
    loss = sg[-1][0]

    names = ["norm_g", "w_in", "attn_sink", "gmlp_ln_g", "gmlp_ln_b", "w_spatial", "b_spatial", "w_up_attn",
             "w_up_gmlp", "w_out", "final_norm_g"]
    small_names = ["norm_g", "attn_sink", "gmlp_ln_g", "gmlp_ln_b", "w_spatial", "b_spatial", "final_norm_g"]
    outs = [[], [], [], []]
    for nm in names:
        for k in range(4):
            if nm in big:
                outs[k].append(big[nm][k])
            else:
                outs[k].append((sg, sd, sm, sv)[k][small_names.index(nm)])
    return (loss, grad_x[None], *outs[0], *outs[1], *outs[2], *outs[3])
```

```python
import math
from typing import Callable, NamedTuple

import jax
import jax.numpy as jnp
from jax import lax
from jax.experimental import pallas as pl
from jax.experimental.pallas import tpu as pltpu

F32 = jnp.float32
BF16 = jnp.bfloat16
MESH = pl.DeviceIdType.MESH

N_DEV = 8
N_CHIPS = 4
HEAD_DIM = 64
BLOCK = 128
HEADS_PER_TILE = 8
ROPE_DIM = 16
ROPE_HALF = ROPE_DIM // 2
ROPE_THETA = 500000.0
GMLP_GROUPS = 8
NORM_EPS = 1e-5
LN_EPS = 1e-5
ATTN_SCALE = HEAD_DIM ** -0.5
LANES = 128
SUBLANES = 8
VMEM_LIMIT = 48 * 1024 * 1024
VMEM_LIMIT_WIDE = 56 * 1024 * 1024
DOT_COLS = 1024
SEGMENT_SPLIT = 4

ADAM_LR = 0.001
ADAM_B1 = 0.9
ADAM_B2 = 0.999
ADAM_EPS = 1e-08
ADAM_WD = 0.01
ADAM_STEP = 10

GELU_C = math.sqrt(2.0 / math.pi)
GELU_K = 0.044715

HBM_SPEC = pl.BlockSpec(memory_space=pltpu.HBM)
ANY_SPEC = pl.BlockSpec(memory_space=pl.ANY)
SEM_SPEC = pl.BlockSpec(memory_space=pltpu.SEMAPHORE)
VMEM_SPEC = pl.BlockSpec(memory_space=pltpu.VMEM)
SMEM_SPEC = pl.BlockSpec(memory_space=pltpu.SMEM)


def _sds(shape, dtype):
    return jax.ShapeDtypeStruct(shape, dtype)


def _params(*sem, vmem=VMEM_LIMIT):
    return pltpu.CompilerParams(dimension_semantics=sem or None, vmem_limit_bytes=vmem)


def _gelu(x):
    return 0.5 * x * (1.0 + jnp.tanh(GELU_C * (x + GELU_K * x * x * x)))


def _gelu_grad(x):
    t = jnp.tanh(GELU_C * (x + GELU_K * x * x * x))
    return 0.5 * (1.0 + t) + 0.5 * x * (1.0 - t * t) * GELU_C * (1.0 + 3.0 * GELU_K * x * x)


def _silu_and_grad(x):
    s = jax.nn.sigmoid(x)
    return x * s, s * (1.0 + x * (1.0 - s))


def _adamw(w, g, m, v):
    m = ADAM_B1 * m + (1.0 - ADAM_B1) * g
    v = ADAM_B2 * v + (1.0 - ADAM_B2) * (g * g)
    m_hat = m / (1.0 - ADAM_B1 ** ADAM_STEP)
    v_hat = v / (1.0 - ADAM_B2 ** ADAM_STEP)
    delta = -ADAM_LR * (m_hat / (jnp.sqrt(v_hat) + ADAM_EPS) + ADAM_WD * w)
    return delta, m, v


def _mesh_pos():
    return lax.axis_index("x"), lax.axis_index("y"), lax.axis_index("c")


def _slot(x, y, c):
    return 4 * x + 2 * y + c


def _chip(x, y):
    return 2 * x + y


def _sibling(x, y, c):
    return (x, y, 1 - c)


_OTHER_CHIPS = (lambda x, y: (1 - x, y), lambda x, y: (x, 1 - y), lambda x, y: (1 - x, 1 - y))
_ICI_STAGES = (lambda x, y, c: (x ^ c, y ^ (1 - c)), lambda x, y, c: (x ^ (1 - c), y ^ c),
               lambda x, y, c: (1 - x, 1 - y))


class _Copy(NamedTuple):
    src: int
    src_slot: Callable
    dst: int
    dst_slot: Callable
    peer: Callable


def _descriptor(refs, send_sems, recv_sems, k, cp):
    pos = _mesh_pos()
    return pltpu.make_async_remote_copy(
        src_ref=refs[cp.src].at[cp.src_slot(*pos)], dst_ref=refs[cp.dst].at[cp.dst_slot(*pos)],
        send_sem=send_sems.at[k], recv_sem=recv_sems.at[k], device_id=cp.peer(*pos), device_id_type=MESH)


def _gather_first_copies(n_arrays):
    copies = []
    for a in range(n_arrays):
        copies.append(_Copy(a, _slot, a, _slot, _sibling))
        for chip in _OTHER_CHIPS:
            copies.append(_Copy(a, _slot, a, _slot, lambda x, y, c, chip=chip: (*chip(x, y), c)))
    return copies


def _gather_pass_copies(n_arrays):
    copies = []
    for a in range(n_arrays):
        for chip in _OTHER_CHIPS:
            src = lambda x, y, c, chip=chip: _slot(*chip(x, y), c)
            copies.append(_Copy(a, src, a, src, _sibling))
    return copies


def _pair_copies(n_sets):
    copies = []
    for a in range(n_sets):
        for q in range(N_CHIPS):
            copies.append(_Copy(2 * a, lambda x, y, c, q=q: q, 2 * a + 1, lambda x, y, c, q=q: q, _sibling))
    return copies


def _pair_copies_strided(n_sets):
    copies = []
    for a in range(n_sets):
        for q in range(N_CHIPS):
            copies.append(_Copy(2 * a, lambda x, y, c, q=q: 2 * q + 1 - c, 2 * a + 1, lambda x, y, c, q=q: q, _sibling))
    return copies


def _chip_sum_copies(n_sets):
    copies = []
    for a in range(n_sets):
        for k, chip in enumerate(_OTHER_CHIPS):
            copies.append(_Copy(2 * a, lambda x, y, c, chip=chip: _chip(*chip(x, y)), 2 * a + 1,
                                lambda x, y, c, k=k: k, lambda x, y, c, chip=chip: (*chip(x, y), c)))
    return copies


def _rdma_start(name, arrays, copies):
    n, nc = len(arrays), len(copies)

    def body(*refs):
        in_refs = refs[:n]
        send_sems, recv_sems = refs[n], refs[n + 1]
        token = refs[2 * n + 2]
        for k, cp in enumerate(copies):
            _descriptor(in_refs, send_sems, recv_sems, k, cp).start()
        token[...] = jnp.zeros_like(token)

    out = pl.pallas_call(
        body, name=name,
        out_shape=(pltpu.SemaphoreType.DMA((nc,)), pltpu.SemaphoreType.DMA((nc,)),
                   *[pltpu.HBM(a.shape, a.dtype) for a in arrays], _sds((SUBLANES, LANES), F32)),
        in_specs=[HBM_SPEC] * n, out_specs=(SEM_SPEC, SEM_SPEC, *([HBM_SPEC] * n), VMEM_SPEC),
        input_output_aliases={i: i + 2 for i in range(n)},
        compiler_params=pltpu.CompilerParams(has_side_effects=pltpu.SideEffectType.DATAFLOW_SIDE_EFFECTING),
    )(*[pltpu.with_memory_space_constraint(a, pltpu.HBM) for a in arrays])
    return out[0], out[1], list(out[2:2 + n]), out[2 + n]


def _rdma_wait(name, arrays, send_sems, recv_sems, copies, after):
    n = len(arrays)

    def body(*refs):
        in_refs = refs[:n]
        send_ref, recv_ref = refs[n], refs[n + 1]
        for k, cp in enumerate(copies):
            d = _descriptor(in_refs, send_ref, recv_ref, k, cp)
            d.wait_send()
            d.wait_recv()

    out = pl.pallas_call(
        body, name=name, out_shape=tuple(pltpu.HBM(a.shape, a.dtype) for a in arrays),
        in_specs=[HBM_SPEC] * n + [SEM_SPEC, SEM_SPEC, ANY_SPEC], out_specs=tuple([HBM_SPEC] * n),
        input_output_aliases={i: i for i in range(n)},
        compiler_params=pltpu.CompilerParams(has_side_effects=pltpu.SideEffectType.DATAFLOW_SIDE_EFFECTING),
    )(*arrays, send_sems, recv_sems, after)
    return list(out)


def _cast_into_slot(w, pos, name):
    rows, cols = w.shape
    tr = min(rows, 256)

    def body(pos_ref, w_ref, o_ref):
        o_ref[...] = w_ref[...].astype(BF16)

    return pl.pallas_call(
        body, name=name,
        grid_spec=pltpu.PrefetchScalarGridSpec(
            num_scalar_prefetch=1, grid=(rows // tr,),
            in_specs=[pl.BlockSpec((tr, cols), lambda i, p: (i, 0))],
            out_specs=pl.BlockSpec((None, tr, cols), lambda i, p: (_slot(p[0], p[1], p[2]), i, 0))),
        out_shape=_sds((N_DEV, rows, cols), BF16), compiler_params=_params("parallel"),
    )(pos, w)


def _all_gather_slots(arrays, name):
    n = len(arrays)
    first, passed = _gather_first_copies(n), _gather_pass_copies(n)

    def body(*refs):
        in_refs = refs[:n]
        send_sems, recv_sems = refs[2 * n], refs[2 * n + 1]
        nf = len(first)
        for k, cp in enumerate(first):
            _descriptor(in_refs, send_sems, recv_sems, k, cp).start()
        for j, cp in enumerate(passed):
            a, rel = divmod(j, 3)
            _descriptor(in_refs, send_sems, recv_sems, 4 * a + 1 + rel, first[4 * a + 1 + rel]).wait_recv()
            _descriptor(in_refs, send_sems, recv_sems, nf + j, cp).start()
        for a in range(n):
            _descriptor(in_refs, send_sems, recv_sems, 4 * a, first[4 * a]).wait_recv()
        for j, cp in enumerate(passed):
            _descriptor(in_refs, send_sems, recv_sems, nf + j, cp).wait_recv()
        for k, cp in enumerate(first + passed):
            _descriptor(in_refs, send_sems, recv_sems, k, cp).wait_send()

    nsem = len(first) + len(passed)
    out = pl.pallas_call(
        body, name=name, out_shape=tuple(_sds(a.shape, a.dtype) for a in arrays),
        in_specs=[ANY_SPEC] * n, out_specs=tuple([ANY_SPEC] * n),
        input_output_aliases={i: i for i in range(n)},
        scratch_shapes=[pltpu.SemaphoreType.DMA((nsem,)), pltpu.SemaphoreType.DMA((nsem,))],
    )(*arrays)
    return list(out)


def _pair_sum(g, land, pos, name):
    _, rows, cols = land.shape
    tr = min(rows, 128)
    strided = g.shape[0] == N_DEV

    def body(pos_ref, g_ref, l_ref, o_ref):
        o_ref[...] = (g_ref[...].astype(F32) + l_ref[...].astype(F32)).astype(BF16)

    g_map = (lambda q, i, p: (2 * q + p[2], i, 0)) if strided else (lambda q, i, p: (q, i, 0))
    blk = pl.BlockSpec((None, tr, cols), lambda q, i, p: (q, i, 0))
    return pl.pallas_call(
        body, name=name,
        grid_spec=pltpu.PrefetchScalarGridSpec(
            num_scalar_prefetch=1, grid=(N_CHIPS, rows // tr),
            in_specs=[pl.BlockSpec((None, tr, cols), g_map), blk], out_specs=blk),
        out_shape=_sds((N_CHIPS, rows, cols), BF16), compiler_params=_params("parallel", "parallel"),
    )(pos, g, land)


def _reduce_adamw(sums, land, w, m, v, pos, name):
    rows, cols = w.shape
    tr = min(rows, 64)

    def body(pos_ref, s_ref, l_ref, w_ref, m_ref, v_ref, g_ref, d_ref, nm_ref, nv_ref):
        g = s_ref[...].astype(F32)
        for k in range(N_CHIPS - 1):
            g = g + l_ref[k].astype(F32)
        delta, nm, nv = _adamw(w_ref[...], g, m_ref[...], v_ref[...])
        g_ref[...] = g
        d_ref[...] = delta
        nm_ref[...] = nm
        nv_ref[...] = nv

    spec = pl.BlockSpec((tr, cols), lambda i, p: (i, 0))
    return pl.pallas_call(
        body, name=name,
        grid_spec=pltpu.PrefetchScalarGridSpec(
            num_scalar_prefetch=1, grid=(rows // tr,),
            in_specs=[pl.BlockSpec((None, tr, cols), lambda i, p: (_chip(p[0], p[1]), i, 0)),
                      pl.BlockSpec((N_CHIPS - 1, tr, cols), lambda i, p: (0, i, 0)), spec, spec, spec],
            out_specs=(spec, spec, spec, spec)),
        out_shape=tuple([_sds((rows, cols), F32)] * 4), compiler_params=_params("parallel"),
    )(pos, sums, land, w, m, v)


def _small_allreduce_adamw(g, w, m, v):
    rows = g.shape[0]

    def body(g_ref, w_ref, m_ref, v_ref, gs_ref, d_ref, nm_ref, nv_ref, all_ref, send_sems, recv_sems):
        x, y, c = _mesh_pos()
        me = _slot(x, y, c)
        copies = []
        for k in range(1, N_DEV):
            peer = (x ^ (k >> 2), y ^ ((k >> 1) & 1), c ^ (k & 1))
            copies.append(pltpu.make_async_remote_copy(
                src_ref=g_ref, dst_ref=all_ref.at[me], send_sem=send_sems.at[k - 1],
                recv_sem=recv_sems.at[k - 1], device_id=peer, device_id_type=MESH))
        for cp in copies:
            cp.start()
        all_ref[me] = g_ref[...]
        for cp in copies:
            cp.wait_recv()
        total = all_ref[0]
        for s in range(1, N_DEV):
            total = total + all_ref[s]
        delta, nm, nv = _adamw(w_ref[...], total, m_ref[...], v_ref[...])
        gs_ref[...] = total
        d_ref[...] = delta
        nm_ref[...] = nm
        nv_ref[...] = nv
        for cp in copies:
            cp.wait_send()

    return pl.pallas_call(
        body, name="small_allreduce_adamw", out_shape=tuple([_sds((rows, LANES), F32)] * 4),
        in_specs=[VMEM_SPEC] * 4, out_specs=tuple([VMEM_SPEC] * 4),
        scratch_shapes=[pltpu.VMEM((N_DEV, rows, LANES), F32), pltpu.SemaphoreType.DMA((7,)),
                        pltpu.SemaphoreType.DMA((7,))],
    )(g, w, m, v)


_DOT_DIMS = {"nn": ((1,), (0,)), "nt": ((1,), (1,)), "tn": ((0,), (0,))}


def _dot(a, b, mode):
    return lax.dot_general(a, b, (_DOT_DIMS[mode], ((), ())), preferred_element_type=F32)


def _col_chunks(cols):
    return [(c0, min(c0 + DOT_COLS, cols)) for c0 in range(0, cols, DOT_COLS)]


def _matmul(a, b, mode, out_dtype, name, *, res=None, tm=1024, tn=1024):
    if mode == "tn":
        kdim, mdim = a.shape
    else:
        mdim, kdim = a.shape
    ndim = b.shape[0] if mode == "nt" else b.shape[1]
    tm, tn = min(tm, mdim), min(tn, ndim)
    assert mdim % tm == 0 and ndim % tn == 0, (name, mdim, ndim)

    def body(*refs):
        out = _dot(refs[0][...], refs[1][...], mode)
        if res is not None:
            out = out + refs[2][...]
        refs[-1][...] = out.astype(out_dtype)

    a_spec = pl.BlockSpec((kdim, tm), lambda i, j: (0, i)) if mode == "tn" else pl.BlockSpec((tm, kdim), lambda i, j: (i, 0))
    b_spec = pl.BlockSpec((tn, kdim), lambda i, j: (j, 0)) if mode == "nt" else pl.BlockSpec((kdim, tn), lambda i, j: (0, j))
    o_spec = pl.BlockSpec((tm, tn), lambda i, j: (i, j))
    in_specs, args = [a_spec, b_spec], [a, b]
    if res is not None:
        in_specs.append(o_spec)
        args.append(res)
    return pl.pallas_call(
        body, name=name, grid=(mdim // tm, ndim // tn), in_specs=in_specs, out_specs=o_spec,
        out_shape=_sds((mdim, ndim), out_dtype), compiler_params=_params("parallel", "parallel"),
    )(*args)


def _project(h, w_blocks, block_ids, name, *, proj=None, after=None, tm=1024, tk=512):
    s, d = h.shape
    _, _, cw = w_blocks.shape
    n = block_ids.shape[0]
    tm, tk = min(tm, s), min(tk, d)
    nk = d // tk

    def body(ids_ref, h_ref, w_ref, *rest):
        o_ref, acc_ref = rest[-2], rest[-1]
        k = pl.program_id(2)

        @pl.when(k == 0)
        def _():
            acc_ref[...] = jnp.zeros_like(acc_ref)

        for c0, c1 in _col_chunks(cw):
            acc_ref[:, c0:c1] += _dot(h_ref[...], w_ref[:, c0:c1], "nn")

        @pl.when(k == nk - 1)
        def _():
            o_ref[...] = acc_ref[...].astype(BF16)

    in_specs = [pl.BlockSpec((tm, tk), lambda j, i, k, ids: (i, k)),
                pl.BlockSpec((None, tk, cw), lambda j, i, k, ids: (ids[j], k, 0))]
    args = [block_ids, h, w_blocks]
    aliases = {}
    if proj is not None:
        in_specs.append(ANY_SPEC)
        args.append(proj)
        aliases = {3: 0}
    if after is not None:
        in_specs.append(ANY_SPEC)
        args.append(after)
    return pl.pallas_call(
        body, name=name,
        grid_spec=pltpu.PrefetchScalarGridSpec(
            num_scalar_prefetch=1, grid=(n, s // tm, d // tk), in_specs=in_specs,
            out_specs=pl.BlockSpec((tm, cw), lambda j, i, k, ids: (i, ids[j])),
            scratch_shapes=[pltpu.VMEM((tm, cw), F32)]),
        out_shape=_sds((s, N_DEV * cw), BF16), input_output_aliases=aliases,
        compiler_params=_params("arbitrary", "arbitrary", "arbitrary", vmem=VMEM_LIMIT_WIDE),
    )(*args)


def _grad_w_in_blocks(h, d_proj, block_ids, cw, name, *, slots=None, slot0=0, prev=None, init=None, after=None,
                      tm=1024, tk=1024):
    s, d = h.shape
    n = block_ids.shape[0]
    slots = n if slots is None else slots
    tm, tk = min(tm, d), min(tk, s)
    nk = s // tk

    def body(ids_ref, h_ref, g_ref, *rest):
        o_ref, acc_ref = rest[-2], rest[-1]
        k = pl.program_id(2)

        @pl.when(k == 0)
        def _():
            acc_ref[...] = jnp.zeros_like(acc_ref) if init is None else rest[0][...].astype(F32)

        for c0, c1 in _col_chunks(cw):
            acc_ref[:, c0:c1] += _dot(h_ref[...], g_ref[:, c0:c1], "tn")

        @pl.when(k == nk - 1)
        def _():
            o_ref[...] = acc_ref[...].astype(BF16)

    in_specs = [pl.BlockSpec((tk, tm), lambda q, i, k, ids: (k, i)),
                pl.BlockSpec((tk, cw), lambda q, i, k, ids: (k, ids[q]))]
    args = [block_ids, h, d_proj]
    aliases = {}
    if init is not None:
        in_specs.append(pl.BlockSpec((None, tm, cw), lambda q, i, k, ids: (q, i, 0)))
        args.append(init)
    if prev is not None:
        aliases = {len(args): 0}
        in_specs.append(ANY_SPEC)
        args.append(prev)
    if after is not None:
        in_specs.append(ANY_SPEC)
        args.append(after)
    return pl.pallas_call(
        body, name=name,
        grid_spec=pltpu.PrefetchScalarGridSpec(
            num_scalar_prefetch=1, grid=(n, d // tm, nk), in_specs=in_specs,
            out_specs=pl.BlockSpec((None, tm, cw), lambda q, i, k, ids: (slot0 + q, i, 0)),
            scratch_shapes=[pltpu.VMEM((tm, cw), F32)]),
        out_shape=_sds((slots, d, cw), BF16), input_output_aliases=aliases,
        compiler_params=_params("parallel", "parallel", "arbitrary", vmem=VMEM_LIMIT_WIDE),
    )(*args)


def _d_hidden(d_proj, w_blocks, after=None, *, tm=1024, tn=1024):
    s = d_proj.shape[0]
    nb, d, cw = w_blocks.shape
    tm, tn = min(tm, s), min(tn, d)

    def body(g_ref, w_ref, *rest):
        o_ref = rest[-1]
        k = pl.program_id(2)

        @pl.when(k == 0)
        def _():
            o_ref[...] = jnp.zeros_like(o_ref)

        o_ref[...] += _dot(g_ref[...], w_ref[...], "nt")

    in_specs = [pl.BlockSpec((tm, cw), lambda i, j, k: (i, k)),
                pl.BlockSpec((None, tn, cw), lambda i, j, k: (k, j, 0))]
    args = [d_proj, w_blocks]
    if after is not None:
        in_specs.append(ANY_SPEC)
        args.append(after)
    return pl.pallas_call(
        body, name="d_h", grid=(s // tm, d // tn, nb), in_specs=in_specs,
        out_specs=pl.BlockSpec((tm, tn), lambda i, j, k: (i, j)), out_shape=_sds((s, d), F32),
        compiler_params=_params("parallel", "parallel", "arbitrary", vmem=VMEM_LIMIT_WIDE),
    )(*args)


def _row_tile(rows):
    return min(rows, 128)


def _segment_specs(rows, d, col0):
    w = d // SEGMENT_SPLIT
    assert col0 % w == 0
    return [pl.BlockSpec((rows, w), lambda i, t=t: (i, col0 // w + t)) for t in range(SEGMENT_SPLIT)]


def _cat(refs):
    return jnp.concatenate([r[...].astype(F32) for r in refs], axis=1)


def _rmsnorm_fwd(x, g, after):
    s, d = x.shape
    tr = _row_tile(s)

    def body(x_ref, g_ref, after_ref, h_ref):
        xv = x_ref[...]
        r = lax.rsqrt(jnp.mean(xv * xv, axis=-1, keepdims=True) + NORM_EPS)
        h_ref[...] = (xv * r * g_ref[...]).astype(BF16)

    row = pl.BlockSpec((tr, d), lambda i: (i, 0))
    vec = pl.BlockSpec((1, d), lambda i: (0, 0))
    return pl.pallas_call(body, name="rmsnorm_fwd", grid=(s // tr,), in_specs=[row, vec, ANY_SPEC], out_specs=row,
                          out_shape=_sds((s, d), BF16), compiler_params=_params("parallel"))(x, g, after)


def _merge_fwd(y_a, y_b, proj, col_m):
    s, d = y_a.shape
    tr = _row_tile(s)
    ns = SEGMENT_SPLIT

    def body(ya_ref, yb_ref, *rest):
        ma, mb, o_ref = _cat(rest[:ns]), _cat(rest[ns:2 * ns]), rest[2 * ns]
        o_ref[...] = (jax.nn.sigmoid(ma) * ya_ref[...].astype(F32)
                      + jax.nn.sigmoid(mb) * yb_ref[...].astype(F32)).astype(BF16)

    row = pl.BlockSpec((tr, d), lambda i: (i, 0))
    return pl.pallas_call(
        body, name="merge_fwd", grid=(s // tr,),
        in_specs=[row, row, *_segment_specs(tr, d, col_m), *_segment_specs(tr, d, col_m + d)],
        out_specs=row, out_shape=_sds((s, d), BF16), compiler_params=_params("parallel"),
    )(y_a, y_b, *([proj] * (2 * ns)))


def _loss_and_final_norm_bwd(x2, target, g):
    s, d = x2.shape
    tr = _row_tile(s)

    def body(x_ref, t_ref, g_ref, loss_ref, dg_ref, dx_ref, dxb_ref):
        @pl.when(pl.program_id(0) == 0)
        def _():
            loss_ref[...] = jnp.zeros_like(loss_ref)
            dg_ref[...] = jnp.zeros_like(dg_ref)

        xv, gv = x_ref[...], g_ref[...]
        r = lax.rsqrt(jnp.mean(xv * xv, axis=-1, keepdims=True) + NORM_EPS)
        xhat = xv * r
        err = xhat * gv - t_ref[...]
        loss_ref[...] += 0.5 * jnp.sum(jnp.mean(err * err, axis=-1, keepdims=True))
        dy = err / d
        dg_ref[...] += jnp.sum(dy * xhat, axis=0, keepdims=True)
        dyg = dy * gv
        dx = r * (dyg - xhat * jnp.mean(dyg * xhat, axis=-1, keepdims=True))
        dx_ref[...] = dx
        dxb_ref[...] = dx.astype(BF16)

    row = pl.BlockSpec((tr, d), lambda i: (i, 0))
    vec = pl.BlockSpec((1, d), lambda i: (0, 0))
    return pl.pallas_call(
        body, name="loss_final_norm_bwd", grid=(s // tr,), in_specs=[row, row, vec],
        out_specs=(pl.BlockSpec((SUBLANES, LANES), lambda i: (0, 0)), vec, row, row),
        out_shape=(_sds((SUBLANES, LANES), F32), _sds((1, d), F32), _sds((s, d), F32), _sds((s, d), BF16)),
        compiler_params=_params("arbitrary"))(x2, target, g)


def _merge_bwd(d_merged, y_a, y_b, proj, col_m):
    s, d = y_a.shape
    tr = _row_tile(s)
    ns = SEGMENT_SPLIT

    def body(dm_ref, ya_ref, yb_ref, *rest):
        ma, mb = _cat(rest[:ns]), _cat(rest[ns:2 * ns])
        dya_ref, dyb_ref, dmg_ref = rest[2 * ns:]
        dm = dm_ref[...].astype(F32)
        sa = jax.nn.sigmoid(ma)
        sb = jax.nn.sigmoid(mb)
        dya_ref[...] = (dm * sa).astype(BF16)
        dyb_ref[...] = (dm * sb).astype(BF16)
        dmg_ref[:, :d] = (dm * ya_ref[...].astype(F32) * (sa * (1.0 - sa))).astype(BF16)
        dmg_ref[:, d:] = (dm * yb_ref[...].astype(F32) * (sb * (1.0 - sb))).astype(BF16)

    row = pl.BlockSpec((tr, d), lambda i: (i, 0))
    wide = pl.BlockSpec((tr, 2 * d), lambda i: (i, 0))
    return pl.pallas_call(
        body, name="merge_bwd", grid=(s // tr,),
        in_specs=[row, row, row, *_segment_specs(tr, d, col_m), *_segment_specs(tr, d, col_m + d)],
        out_specs=(row, row, wide),
        out_shape=(_sds((s, d), BF16), _sds((s, d), BF16), _sds((s, 2 * d), BF16)),
        compiler_params=_params("parallel"))(d_merged, y_a, y_b, *([proj] * (2 * ns)))


def _input_grad(d_h, x, g, dx2):
    s, d = x.shape
    tr = _row_tile(s)

    def body(dh_ref, x_ref, g_ref, dx2_ref, gx_ref, dg_ref):
        @pl.when(pl.program_id(0) == 0)
        def _():
            dg_ref[...] = jnp.zeros_like(dg_ref)

        xv, dh = x_ref[...], dh_ref[...]
        r = lax.rsqrt(jnp.mean(xv * xv, axis=-1, keepdims=True) + NORM_EPS)
        xhat = xv * r
        dg_ref[...] += jnp.sum(dh * xhat, axis=0, keepdims=True)
        dyg = dh * g_ref[...]
        gx_ref[...] = dx2_ref[...] + r * (dyg - xhat * jnp.mean(dyg * xhat, axis=-1, keepdims=True))

    row = pl.BlockSpec((tr, d), lambda i: (i, 0))
    vec = pl.BlockSpec((1, d), lambda i: (0, 0))
    return pl.pallas_call(
        body, name="input_grad", grid=(s // tr,), in_specs=[row, row, vec, row], out_specs=(row, vec),
        out_shape=(_sds((s, d), F32), _sds((1, d), F32)), compiler_params=_params("arbitrary"))(d_h, x, g, dx2)


def _rope_tables(positions):
    inv_freq = ROPE_THETA ** (-jnp.arange(ROPE_HALF, dtype=F32) * 2.0 / ROPE_DIM)
    ang = positions.astype(F32)[:, None] * inv_freq
    cos, sin = jnp.cos(ang), jnp.sin(ang)
    zero = jnp.zeros((positions.shape[0], HEAD_DIM - ROPE_DIM), F32)
    zero_h = jnp.zeros_like(sin)
    c = jnp.concatenate([cos, cos, zero + 1.0], axis=1)
    up = jnp.concatenate([-sin, zero_h, zero], axis=1)
    down = jnp.concatenate([zero_h, sin, zero], axis=1)
    reps = LANES // HEAD_DIM
    return jnp.stack([jnp.tile(c, (1, reps)), jnp.tile(up, (1, reps)), jnp.tile(down, (1, reps))])


def _lane_tiles(x):
    return [x[:, t * LANES:(t + 1) * LANES] for t in range(x.shape[1] // LANES)]


def _rope(x, tab):
    out = [xt * tab[0] + pltpu.roll(xt, LANES - ROPE_HALF, 1) * tab[1] + pltpu.roll(xt, ROPE_HALF, 1) * tab[2]
           for xt in _lane_tiles(x)]
    return out[0] if len(out) == 1 else jnp.concatenate(out, axis=1)


def _rope_bwd(g, tab):
    out = [gt * tab[0] + pltpu.roll(gt * tab[1], ROPE_HALF, 1) + pltpu.roll(gt * tab[2], LANES - ROPE_HALF, 1)
           for gt in _lane_tiles(g)]
    return out[0] if len(out) == 1 else jnp.concatenate(out, axis=1)


def _head(x, h):
    return x[:, h * HEAD_DIM:(h + 1) * HEAD_DIM]


def _stack_heads(x, first, count):
    return jnp.concatenate([_head(x, first + h) for h in range(count)], axis=0)


def _dot_nt(a, b):
    return lax.dot_general(a, b, (((1,), (1,)), ((), ())), preferred_element_type=F32)


def _causal(rows):
    qi = lax.broadcasted_iota(jnp.int32, (rows, BLOCK), 0) % BLOCK
    return lax.broadcasted_iota(jnp.int32, (rows, BLOCK), 1) <= qi


def _band_probs(qs, k_prev, k_cur, sink, causal, blk):
    s_prev = jnp.where(blk > 0, _dot_nt(qs, k_prev), -jnp.inf)
    s = jnp.where(causal, _dot_nt(qs, k_cur), s_prev)
    m = jnp.maximum(jnp.max(s, axis=-1, keepdims=True), sink)
    p = jnp.exp(s - m)
    p_sink = jnp.exp(sink - m)
    inv = 1.0 / (jnp.sum(p, axis=-1, keepdims=True) + p_sink)
    return p * inv, p_sink * inv


def _sink_column(sink_ref, first, count):
    return jnp.concatenate([jnp.full((BLOCK, 1), sink_ref[first + h], F32) for h in range(count)], axis=0)


def _split_band(x, causal):
    return jnp.where(causal, x, 0.0).astype(BF16), jnp.where(causal, 0.0, x).astype(BF16)


def _attn_dims(s, d, kv):
    n_kv = kv // HEAD_DIM
    group = d // kv
    qw = 2 * group * HEAD_DIM
    assert n_kv % 2 == 0 and (d + 2 * kv) % qw == 0 and s % BLOCK == 0
    return group, qw, n_kv // 2, s // BLOCK


def _attention_fwd(proj, tables, sink, kv, after):
    s, d = proj.shape[0], sink.shape[0] * HEAD_DIM
    group, qw, n_pairs, nb = _attn_dims(s, d, kv)

    def body(sink_ref, q_ref, kc_ref, kp_ref, vc_ref, vp_ref, ga_ref, tc_ref, tp_ref, after_ref, attn_ref, ain_ref):
        pair, blk = pl.program_id(0), pl.program_id(1)
        tab_c, tab_p = tc_ref[...], tp_ref[...]
        q = _rope(q_ref[...].astype(F32), tab_c)
        k_cur, k_prev = _rope(kc_ref[...].astype(F32), tab_c), _rope(kp_ref[...].astype(F32), tab_p)
        v_cur, v_prev = vc_ref[...], vp_ref[...]
        sub = min(group, HEADS_PER_TILE)
        causal = _causal(sub * BLOCK)
        outs = []
        for a in range(2):
            kc, kp = _head(k_cur, a).astype(BF16), _head(k_prev, a).astype(BF16)
            vc, vp = _head(v_cur, a).astype(BF16), _head(v_prev, a).astype(BF16)
            for h0 in range(0, group, sub):
                qs = (_stack_heads(q, a * group + h0, sub) * ATTN_SCALE).astype(BF16)
                sink_col = _sink_column(sink_ref, (2 * pair + a) * group + h0, sub)
                p, _ = _band_probs(qs, kp, kc, sink_col, causal, blk)
                p_cur, p_prev = _split_band(p, causal)
                o = jnp.dot(p_cur, vc, preferred_element_type=F32) + jnp.dot(p_prev, vp, preferred_element_type=F32)
                outs += [o[h * BLOCK:(h + 1) * BLOCK] for h in range(sub)]
        attn = jnp.concatenate(outs, axis=1)
        attn_ref[...] = attn.astype(BF16)
        silu, _ = _silu_and_grad(ga_ref[...].astype(F32))
        ain_ref[...] = (attn * silu).astype(BF16)

    k0, v0, g0 = d // LANES, (d + kv) // LANES, (d + 2 * kv) // qw
    prev = lambda i: jnp.maximum(i - 1, 0)
    in_specs = [
        SMEM_SPEC,
        pl.BlockSpec((BLOCK, qw), lambda p, i: (i, p)),
        pl.BlockSpec((BLOCK, LANES), lambda p, i: (i, k0 + p)),
        pl.BlockSpec((BLOCK, LANES), lambda p, i: (prev(i), k0 + p)),
        pl.BlockSpec((BLOCK, LANES), lambda p, i: (i, v0 + p)),
        pl.BlockSpec((BLOCK, LANES), lambda p, i: (prev(i), v0 + p)),
        pl.BlockSpec((BLOCK, qw), lambda p, i: (i, g0 + p)),
        pl.BlockSpec((3, BLOCK, LANES), lambda p, i: (0, i, 0)),
        pl.BlockSpec((3, BLOCK, LANES), lambda p, i: (0, prev(i), 0)),
        ANY_SPEC,
    ]
    out = pl.BlockSpec((BLOCK, qw), lambda p, i: (i, p))
    return pl.pallas_call(
        body, name="attention_fwd", grid=(n_pairs, nb), in_specs=in_specs, out_specs=(out, out),
        out_shape=(_sds((s, d), BF16), _sds((s, d), BF16)), compiler_params=_params("parallel", "parallel"),
    )(sink, proj, proj, proj, proj, proj, proj, tables, tables, after)


def _attention_bwd(proj, tables, sink, kv, attn, d_ain, after):
    s, d = proj.shape[0], sink.shape[0] * HEAD_DIM
    group, qw, n_pairs, nb = _attn_dims(s, d, kv)

    def body(sink_ref, q_ref, kc_ref, kp_ref, vc_ref, vp_ref, ga_ref, tc_ref, tp_ref, attn_ref, dain_ref, after_ref,
             dq_ref, dk_ref, dv_ref, dga_ref, dsink_ref, carry_k, carry_v):
        pair, blk = pl.program_id(0), pl.program_id(1)

        @pl.when(blk == 0)
        def _():
            carry_k[...] = jnp.zeros_like(carry_k)
            carry_v[...] = jnp.zeros_like(carry_v)
            dsink_ref[...] = jnp.zeros_like(dsink_ref)

        @pl.when(blk < nb)
        def _():
            tab_c, tab_p = tc_ref[...], tp_ref[...]
            q = _rope(q_ref[...].astype(F32), tab_c)
            k_cur, k_prev = _rope(kc_ref[...].astype(F32), tab_c), _rope(kp_ref[...].astype(F32), tab_p)
            v_cur, v_prev = vc_ref[...], vp_ref[...]
            silu, silu_grad = _silu_and_grad(ga_ref[...].astype(F32))
            d_ain_v = dain_ref[...].astype(F32)
            dga_ref[...] = (d_ain_v * attn_ref[...].astype(F32) * silu_grad).astype(BF16)
            d_attn = d_ain_v * silu
            q_t = (q * ATTN_SCALE).T
            d_attn_t = d_attn.T
            causal = _causal(group * BLOCK)
            dq_parts = []
            dk_t = {"cur": [], "prev": []}
            dv_t = {"cur": [], "prev": []}
            lane = lax.broadcasted_iota(jnp.int32, (1, LANES), 1)
            dsink = jnp.zeros((1, LANES), F32)
            for a in range(2):
                first = a * group
                qs = (_stack_heads(q, first, group) * ATTN_SCALE).astype(BF16)
                kc, kp = _head(k_cur, a).astype(BF16), _head(k_prev, a).astype(BF16)
                vc, vp = _head(v_cur, a).astype(BF16), _head(v_prev, a).astype(BF16)
                sink_col = _sink_column(sink_ref, (2 * pair + a) * group, group)
                p, p_sink = _band_probs(qs, kp, kc, sink_col, causal, blk)
                do = _stack_heads(d_attn, first, group).astype(BF16)
                dp = jnp.where(causal, _dot_nt(do, vc), _dot_nt(do, vp))
                delta = jnp.sum(p * dp, axis=-1, keepdims=True)
                ds_cur, ds_prev = _split_band(p * (dp - delta), causal)
                p_cur, p_prev = _split_band(p, causal)
                dqs = (jnp.dot(ds_cur, kc, preferred_element_type=F32)
                       + jnp.dot(ds_prev, kp, preferred_element_type=F32)) * ATTN_SCALE
                dq_parts += [dqs[h * BLOCK:(h + 1) * BLOCK] for h in range(group)]
                rows = lambda t: jnp.concatenate(
                    [t[(first + h) * HEAD_DIM:(first + h + 1) * HEAD_DIM] for h in range(group)], axis=1).astype(BF16)
                qs_t, do_t = rows(q_t), rows(d_attn_t)
                dk_t["cur"].append(jnp.dot(qs_t, ds_cur, preferred_element_type=F32))
                dk_t["prev"].append(jnp.dot(qs_t, ds_prev, preferred_element_type=F32))
                dv_t["cur"].append(jnp.dot(do_t, p_cur, preferred_element_type=F32))
                dv_t["prev"].append(jnp.dot(do_t, p_prev, preferred_element_type=F32))
                ds_sink = -(p_sink * delta)
                for h in range(group):
                    dsink = dsink + jnp.where(lane == first + h, jnp.sum(ds_sink[h * BLOCK:(h + 1) * BLOCK]), 0.0)
            dsink_ref[0] += dsink
            dq_ref[...] = _rope_bwd(jnp.concatenate(dq_parts, axis=1), tab_c).astype(BF16)
            pair_block = lambda parts: jnp.concatenate(parts, axis=0).T
            dk_ref[...] = (carry_k[...] + _rope_bwd(pair_block(dk_t["prev"]), tab_p)).astype(BF16)
            dv_ref[...] = (carry_v[...] + pair_block(dv_t["prev"])).astype(BF16)
            carry_k[...] = _rope_bwd(pair_block(dk_t["cur"]), tab_c)
            carry_v[...] = pair_block(dv_t["cur"])

        @pl.when(blk == nb)
        def _():
            dk_ref[...] = carry_k[...].astype(BF16)
            dv_ref[...] = carry_v[...].astype(BF16)

    k0, v0, g0 = d // LANES, (d + kv) // LANES, (d + 2 * kv) // qw
    cur = lambda i: jnp.minimum(i, nb - 1)
    prev = lambda i: jnp.maximum(cur(i) - 1, 0)
    back = lambda i: jnp.maximum(i - 1, 0)
    q_spec = pl.BlockSpec((BLOCK, qw), lambda p, i: (cur(i), p))
    in_specs = [
        SMEM_SPEC,
        q_spec,
        pl.BlockSpec((BLOCK, LANES), lambda p, i: (cur(i), k0 + p)),
        pl.BlockSpec((BLOCK, LANES), lambda p, i: (prev(i), k0 + p)),
        pl.BlockSpec((BLOCK, LANES), lambda p, i: (cur(i), v0 + p)),
        pl.BlockSpec((BLOCK, LANES), lambda p, i: (prev(i), v0 + p)),
        pl.BlockSpec((BLOCK, qw), lambda p, i: (cur(i), g0 + p)),
        pl.BlockSpec((3, BLOCK, LANES), lambda p, i: (0, cur(i), 0)),
        pl.BlockSpec((3, BLOCK, LANES), lambda p, i: (0, prev(i), 0)),
        q_spec,
        q_spec,
        ANY_SPEC,
    ]
    kv_out = pl.BlockSpec((BLOCK, LANES), lambda p, i: (back(i), p))
    return pl.pallas_call(
        body, name="attention_bwd", grid=(n_pairs, nb + 1), in_specs=in_specs,
        out_specs=(q_spec, kv_out, kv_out, q_spec, pl.BlockSpec((1, 1, LANES), lambda p, i: (p, 0, 0))),
        out_shape=(_sds((s, d), BF16), _sds((s, kv), BF16), _sds((s, kv), BF16), _sds((s, d), BF16),
                   _sds((n_pairs, 1, LANES), F32)),
        scratch_shapes=[pltpu.VMEM((BLOCK, LANES), F32), pltpu.VMEM((BLOCK, LANES), F32)],
        compiler_params=_params("parallel", "arbitrary"),
    )(sink, proj, proj, proj, proj, proj, proj, tables, tables, attn, d_ain, after)


def _gmlp_core(u, vg, ln_g, ln_b, w_ref, bias_t):
    gu = _gelu(u)
    gv = _gelu(vg)
    xc = gv - jnp.mean(gv, axis=-1, keepdims=True)
    rstd = lax.rsqrt(jnp.mean(xc * xc, axis=-1, keepdims=True) + LN_EPS)
    xhat = xc * rstd
    vn = (xhat * ln_g + ln_b).astype(BF16)
    gd = u.shape[1] // GMLP_GROUPS
    tri = (lax.broadcasted_iota(jnp.int32, (BLOCK, BLOCK), 0) >= lax.broadcasted_iota(jnp.int32, (BLOCK, BLOCK), 1))
    w_tri = [jnp.where(tri, w_ref[g], 0.0).astype(BF16) for g in range(GMLP_GROUPS)]
    mixed = jnp.concatenate(
        [jnp.dot(w_tri[g], vn[:, g * gd:(g + 1) * gd], preferred_element_type=F32) + bias_t[:, g:g + 1]
         for g in range(GMLP_GROUPS)], axis=1)
    return gu, xhat, rstd, vn, w_tri, tri, mixed


def _whole(shape):
    return pl.BlockSpec(shape, lambda i: tuple(0 for _ in shape))


def _gmlp_fwd(proj, col_u, d, w_s, bias_t, ln_g, ln_b, after):
    s = proj.shape[0]
    ns = SEGMENT_SPLIT

    def body(*refs):
        u, vg, gb = _cat(refs[:ns]), _cat(refs[ns:2 * ns]), _cat(refs[2 * ns:3 * ns])
        w_ref, bt_ref, lg_ref, lb_ref, after_ref, o_ref = refs[3 * ns:]
        gu, _, _, _, _, _, mixed = _gmlp_core(u, vg, lg_ref[...], lb_ref[...], w_ref, bt_ref[...])
        silu, _ = _silu_and_grad(gb)
        o_ref[...] = ((gu * mixed) * silu).astype(BF16)

    segs = [sp for j in range(3) for sp in _segment_specs(BLOCK, d, col_u + j * d)]
    return pl.pallas_call(
        body, name="gmlp_fwd", grid=(s // BLOCK,),
        in_specs=[*segs, _whole(w_s.shape), _whole(bias_t.shape), _whole((1, d)), _whole((1, d)), ANY_SPEC],
        out_specs=pl.BlockSpec((BLOCK, d), lambda i: (i, 0)), out_shape=_sds((s, d), BF16),
        compiler_params=_params("parallel"),
    )(*([proj] * (3 * ns)), w_s, bias_t, ln_g, ln_b, after)


def _gmlp_bwd(proj, col_u, d, w_s, bias_t, ln_g, ln_b, d_bin, after):
    s = proj.shape[0]
    gd = d // GMLP_GROUPS
    ns = SEGMENT_SPLIT

    def body(*refs):
        u, vg, gb = _cat(refs[:ns]), _cat(refs[ns:2 * ns]), _cat(refs[2 * ns:3 * ns])
        w_ref, bt_ref, lg_ref, lb_ref, dbin_ref, after_ref, dg_ref, dw_ref, dbt_ref, dlg_ref, dlb_ref = refs[3 * ns:]

        @pl.when(pl.program_id(0) == 0)
        def _():
            dw_ref[...] = jnp.zeros_like(dw_ref)
            dbt_ref[...] = jnp.zeros_like(dbt_ref)
            dlg_ref[...] = jnp.zeros_like(dlg_ref)
            dlb_ref[...] = jnp.zeros_like(dlb_ref)

        ln_g = lg_ref[...]
        gu, xhat, rstd, vn, w_tri, tri, mixed = _gmlp_core(u, vg, ln_g, lb_ref[...], w_ref, bt_ref[...])
        silu, silu_grad = _silu_and_grad(gb)
        d_bin_v = dbin_ref[...].astype(F32)
        d_sg = d_bin_v * silu
        dg_ref[:, 2 * d:] = (d_bin_v * (gu * mixed) * silu_grad).astype(BF16)
        dg_ref[:, :d] = (d_sg * mixed * _gelu_grad(u)).astype(BF16)
        d_mixed = d_sg * gu
        d_mixed_b = d_mixed.astype(BF16)
        d_vn, d_bias = [], []
        for g in range(GMLP_GROUPS):
            dm_g = d_mixed_b[:, g * gd:(g + 1) * gd]
            d_bias.append(jnp.sum(d_mixed[:, g * gd:(g + 1) * gd], axis=-1, keepdims=True))
            dw = lax.dot_general(dm_g, vn[:, g * gd:(g + 1) * gd], (((1,), (1,)), ((), ())),
                                 preferred_element_type=F32)
            dw_ref[g] += jnp.where(tri, dw, 0.0)
            d_vn.append(lax.dot_general(w_tri[g], dm_g, (((0,), (0,)), ((), ())), preferred_element_type=F32))
        dbt_ref[...] += jnp.concatenate(d_bias, axis=1)
        d_vn = jnp.concatenate(d_vn, axis=1)
        dlg_ref[...] += jnp.sum(d_vn * xhat, axis=0, keepdims=True)
        dlb_ref[...] += jnp.sum(d_vn, axis=0, keepdims=True)
        d_xhat = d_vn * ln_g
        d_gv = rstd * (d_xhat - jnp.mean(d_xhat, axis=-1, keepdims=True)
                       - xhat * jnp.mean(d_xhat * xhat, axis=-1, keepdims=True))
        dg_ref[:, d:2 * d] = (d_gv * _gelu_grad(vg)).astype(BF16)

    segs = [sp for j in range(3) for sp in _segment_specs(BLOCK, d, col_u + j * d)]
    return pl.pallas_call(
        body, name="gmlp_bwd", grid=(s // BLOCK,),
        in_specs=[*segs, _whole(w_s.shape), _whole(bias_t.shape), _whole((1, d)), _whole((1, d)),
                  pl.BlockSpec((BLOCK, d), lambda i: (i, 0)), ANY_SPEC],
        out_specs=(pl.BlockSpec((BLOCK, 3 * d), lambda i: (i, 0)), _whole(w_s.shape), _whole(bias_t.shape),
                   _whole((1, d)), _whole((1, d))),
        out_shape=(_sds((s, 3 * d), BF16), _sds(w_s.shape, F32), _sds(bias_t.shape, F32), _sds((1, d), F32),
                   _sds((1, d), F32)),
        compiler_params=_params("arbitrary"),
    )(*([proj] * (3 * ns)), w_s, bias_t, ln_g, ln_b, d_bin, after)


def _pack(parts):
    rows = []
    tile = SUBLANES * LANES
    for p in parts:
        flat = p.astype(F32).reshape(-1)
        padded = -(-flat.shape[0] // tile) * tile
        rows.append(jnp.pad(flat, (0, padded - flat.shape[0])).reshape(-1, LANES))
    return jnp.concatenate(rows, axis=0)


def _unpack(packed, shapes):
    out, row = [], 0
    tile = SUBLANES * LANES
    for shape in shapes:
        size = math.prod(shape)
        n_rows = -(-size // tile) * SUBLANES
        out.append(packed[row:row + n_rows].reshape(-1)[:size].reshape(shape))
        row += n_rows
    return out


def kernel(x, positions, norm_g, w_in, attn_sink, gmlp_ln_g, gmlp_ln_b, w_spatial, b_spatial, w_up_attn, w_up_gmlp, w_out, final_norm_g, loss_target, m_norm_g, m_w_in, m_attn_sink, m_gmlp_ln_g, m_gmlp_ln_b, m_w_spatial, m_b_spatial, m_w_up_attn, m_w_up_gmlp, m_w_out, m_final_norm_g, v_norm_g, v_w_in, v_attn_sink, v_gmlp_ln_g, v_gmlp_ln_b, v_w_spatial, v_b_spatial, v_w_up_attn, v_w_up_gmlp, v_w_out, v_final_norm_g):
    x2d, target = x[0], loss_target[0]
    s, d = x2d.shape
    n_q_heads = attn_sink.shape[1]
    cw = w_in.shape[2]
    rw = w_up_attn.shape[1]
    kv = (cw * N_DEV - 7 * d) // 2
    col_u, col_m = 2 * d + 2 * kv, 5 * d + 2 * kv
    final_g = final_norm_g.reshape(1, d)
    sink = attn_sink[0]
    w_s = w_spatial[0]
    bias_t = b_spatial[0].T
    mx, my, mc = _mesh_pos()
    pos = jnp.stack([mx, my, mc]).astype(jnp.int32)
    chips = jnp.arange(N_CHIPS, dtype=jnp.int32)

    def one_block(fn):
        return jnp.reshape(fn(mx, my, mc), (1,)).astype(jnp.int32)

    w_in_b = _cast_into_slot(w_in[0], pos, "cast_w_in")
    squares = [_cast_into_slot(w[0], pos, "cast_" + nm)
               for nm, w in (("w_up_attn", w_up_attn), ("w_up_gmlp", w_up_gmlp), ("w_out", w_out))]
    to_sibling = [_Copy(0, _slot, 0, _slot, _sibling)]
    ici = [[_Copy(0, _slot, 0, _slot, lambda x, y, c, chip=chip: (*chip(x, y, c), c))] for chip in _ICI_STAGES[:2]]
    relayed = lambda x, y, c: _slot(*_ICI_STAGES[1](x, y, c), c)
    ici.append([_Copy(0, relayed, 0, relayed, lambda x, y, c: (*_ICI_STAGES[0](x, y, c), c))])
    passes = []
    for chip in _ICI_STAGES:
        landed = lambda x, y, c, chip=chip: _slot(*chip(x, y, c), c)
        passes.append([_Copy(0, landed, 0, landed, _sibling)])
    sib_sems = _rdma_start("w_in_sibling_start", [w_in_b], to_sibling)
    ici_sems = _rdma_start("w_in_ici0_start", sib_sems[2], ici[0])
    h = _rmsnorm_fwd(x2d, norm_g, ici_sems[3])
    proj = _project(h, ici_sems[2][0], one_block(_slot), "projection_own")
    w_blocks = _rdma_wait("w_in_sibling_wait", ici_sems[2], sib_sems[0], sib_sems[1], to_sibling, proj)
    proj = _project(h, w_blocks[0], one_block(lambda x, y, c: _slot(x, y, 1 - c)), "projection_sibling", proj=proj)
    w_blocks = _rdma_wait("w_in_ici0_wait", w_blocks, ici_sems[0], ici_sems[1], ici[0], proj)
    first = _gather_first_copies(3)
    for k, chip in enumerate(_ICI_STAGES):
        if k + 1 < len(_ICI_STAGES):
            ici_sems = _rdma_start("w_in_ici%d_start" % (k + 1), w_blocks, ici[k + 1])
            w_blocks = ici_sems[2]
        else:
            send1, recv1, thru, _ = _rdma_start("gather_squares_start", squares + w_blocks, first)
            squares, w_blocks = thru[:3], thru[3:]
        pass_sems = _rdma_start("w_in_pass%d_start" % k, w_blocks, passes[k])
        proj = _project(h, pass_sems[2][0], one_block(lambda x, y, c, chip=chip: _slot(*chip(x, y, c), c)),
                        "projection_ici%d" % k, proj=proj, after=pass_sems[3])
        w_blocks = _rdma_wait("w_in_pass%d_wait" % k, pass_sems[2], pass_sems[0], pass_sems[1], passes[k], proj)
        proj = _project(h, w_blocks[0], one_block(lambda x, y, c, chip=chip: _slot(*chip(x, y, 1 - c), 1 - c)),
                        "projection_pass%d" % k, proj=proj)
        if k + 1 < len(_ICI_STAGES):
            w_blocks = _rdma_wait("w_in_ici%d_wait" % (k + 1), w_blocks, ici_sems[0], ici_sems[1], ici[k + 1], proj)
    w_in_b = w_blocks[0]

    tables = _rope_tables(positions[0])
    attn, a_in = _attention_fwd(proj, tables, sink, kv, proj)
    squares = _rdma_wait("gather_squares_wait", squares, send1, recv1, first, attn)
    passed = _gather_pass_copies(3)
    send2, recv2, squares, token = _rdma_start("pass_squares_start", squares, passed)
    b_in = _gmlp_fwd(proj, col_u, d, w_s, bias_t, gmlp_ln_g, gmlp_ln_b, token)
    squares = _rdma_wait("pass_squares_wait", squares, send2, recv2, passed, b_in)
    w_ua, w_ug, w_o = [w.reshape(N_DEV * rw, d) for w in squares]
    y_a = _matmul(a_in, w_ua, "nn", BF16, "up_attn")
    y_b = _matmul(b_in, w_ug, "nn", BF16, "up_gmlp")
    merged = _merge_fwd(y_a, y_b, proj, col_m)
    x_out = _matmul(merged, w_o, "nn", F32, "out_proj", res=x2d, tn=512)
    loss_p, d_final_g, dx2, dx2_b = _loss_and_final_norm_bwd(x_out, target, final_g)

    d_merged = _matmul(dx2_b, w_o, "nt", BF16, "d_merged")
    g_w_out = _matmul(merged, dx2_b, "tn", BF16, "g_w_out")
    d_ya, d_yb, d_mg = _merge_bwd(d_merged, y_a, y_b, proj, col_m)
    d_ain = _matmul(d_ya, w_ua, "nt", BF16, "d_a_in")
    g_w_ua = _matmul(a_in, d_ya, "tn", BF16, "g_w_up_attn")
    d_bin = _matmul(d_yb, w_ug, "nt", BF16, "d_b_in")
    g_w_ug = _matmul(b_in, d_yb, "tn", BF16, "g_w_up_gmlp")
    sq_grads = [g.reshape(N_DEV, rw, d) for g in (g_w_ua, g_w_ug, g_w_out)]
    sq_land = [lax.empty((N_CHIPS, rw, d), BF16) for _ in sq_grads]
    pairs_sq = _pair_copies_strided(3)
    arrays = [a for gl in zip(sq_grads, sq_land) for a in gl]
    send3, recv3, arrays, token = _rdma_start("pair_squares_start", arrays, pairs_sq)
    d_q, d_k, d_v, d_ga, d_sink = _attention_bwd(proj, tables, sink, kv, attn, d_ain, token)
    arrays = _rdma_wait("pair_squares_wait", arrays, send3, recv3, pairs_sq, d_q)
    sq_sums = [_pair_sum(arrays[2 * a], arrays[2 * a + 1], pos, "pair_sum_%d" % a) for a in range(3)]
    sq_land2 = [lax.empty((N_CHIPS - 1, rw, d), BF16) for _ in sq_sums]
    chip_sq = _chip_sum_copies(3)
    arrays = [a for gl in zip(sq_sums, sq_land2) for a in gl]
    send4, recv4, sq_arrays, token = _rdma_start("chip_squares_start", arrays, chip_sq)
    d_g, d_w_s, d_bias_t, d_ln_g, d_ln_b = _gmlp_bwd(proj, col_u, d, w_s, bias_t, gmlp_ln_g, gmlp_ln_b, d_bin, token)
    d_proj = jnp.concatenate([d_q, d_k, d_v, d_ga, d_g, d_mg], axis=1)

    half = N_CHIPS // 2
    pairs_in = [_Copy(0, lambda x, y, c, q=q: q, 1, lambda x, y, c, q=q: q, _sibling) for q in range(half)]
    sent, token = [], None
    for j in range(2):
        g_sib = _grad_w_in_blocks(h, d_proj, 2 * chips[j * half:(j + 1) * half] + 1 - mc, cw,
                                  "g_w_in_sibling%d" % j, after=token)
        sent.append(_rdma_start("pair_w_in%d_start" % j, [g_sib, lax.empty((half, d, cw), BF16)], pairs_in))
        token = sent[-1][3]
    in_sums = None
    for j in range(2):
        send5, recv5, arrays, _ = sent[j]
        arrays = _rdma_wait("pair_w_in%d_wait" % j, arrays, send5, recv5, pairs_in, token if j == 0 else in_sums)
        in_sums = _grad_w_in_blocks(h, d_proj, 2 * chips[j * half:(j + 1) * half] + mc, cw, "g_w_in_own%d" % j,
                                    slots=N_CHIPS, slot0=j * half, prev=in_sums, init=arrays[1], tm=512)
    chip_in = _chip_sum_copies(1)
    send6, recv6, in_arrays, token = _rdma_start(
        "chip_w_in_start", [in_sums, lax.empty((N_CHIPS - 1, d, cw), BF16)], chip_in)
    d_h = _d_hidden(d_proj, w_in_b, after=token)
    grad_x, d_norm_g = _input_grad(d_h, x2d, norm_g, dx2)

    sq_arrays = _rdma_wait("chip_squares_wait", sq_arrays, send4, recv4, chip_sq, grad_x)
    big = {}
    for a, (name, w, m, v) in enumerate((("w_up_attn", w_up_attn, m_w_up_attn, v_w_up_attn),
                                         ("w_up_gmlp", w_up_gmlp, m_w_up_gmlp, v_w_up_gmlp),
                                         ("w_out", w_out, m_w_out, v_w_out))):
        big[name] = [r[None] for r in _reduce_adamw(sq_arrays[2 * a], sq_arrays[2 * a + 1], w[0], m[0], v[0], pos,
                                                    "adamw_" + name)]
    in_arrays = _rdma_wait("chip_w_in_wait", in_arrays, send6, recv6, chip_in, big["w_out"][0])
    big["w_in"] = [r[None] for r in _reduce_adamw(in_arrays[0], in_arrays[1], w_in[0], m_w_in[0], v_w_in[0], pos,
                                                  "adamw_w_in")]

    heads_per_pair = 2 * n_q_heads // (kv // HEAD_DIM)
    g_sink = d_sink[:, 0, :heads_per_pair].reshape(1, n_q_heads)
    small_w = [norm_g, attn_sink, gmlp_ln_g, gmlp_ln_b, w_spatial, b_spatial, final_norm_g]
    small_m = [m_norm_g, m_attn_sink, m_gmlp_ln_g, m_gmlp_ln_b, m_w_spatial, m_b_spatial, m_final_norm_g]
    small_v = [v_norm_g, v_attn_sink, v_gmlp_ln_g, v_gmlp_ln_b, v_w_spatial, v_b_spatial, v_final_norm_g]
    small_g = [d_norm_g, g_sink, d_ln_g, d_ln_b, d_w_s[None], d_bias_t.T[None], d_final_g.reshape(d)]
    loss_pad = jnp.zeros((1,), F32)
    shapes = [w.shape for w in small_w] + [(1,)]
    packed = _small_allreduce_adamw(_pack(small_g + [loss_p[0, :1]]), _pack(small_w + [loss_pad]),
                                    _pack(small_m + [loss_pad]), _pack(small_v + [loss_pad]))
    sg, sd, sm, sv = [_unpack(p, shapes) for p in packed]
    loss = sg[-1][0]

    names = ["norm_g", "w_in", "attn_sink", "gmlp_ln_g", "gmlp_ln_b", "w_spatial", "b_spatial", "w_up_attn",
             "w_up_gmlp", "w_out", "final_norm_g"]
    small_names = ["norm_g", "attn_sink", "gmlp_ln_g", "gmlp_ln_b", "w_spatial", "b_spatial", "final_norm_g"]
    outs = [[], [], [], []]
    for nm in names:
        for k in range(4):
            if nm in big:
                outs[k].append(big[nm][k])
            else:
                outs[k].append((sg, sd, sm, sv)[k][small_names.index(nm)])
    return (loss, grad_x[None], *outs[0], *outs[1], *outs[2], *outs[3])
```

```python
import math
from typing import Callable, NamedTuple

import jax
import jax.numpy as jnp
from jax import lax
from jax.experimental import pallas as pl
from jax.experimental.pallas import tpu as pltpu

F32 = jnp.float32
BF16 = jnp.bfloat16
MESH = pl.DeviceIdType.MESH

N_DEV = 8
N_CHIPS = 4
HEAD_DIM = 64
BLOCK = 128
HEADS_PER_TILE = 8
ROPE_DIM = 16
ROPE_HALF = ROPE_DIM // 2
ROPE_THETA = 500000.0
GMLP_GROUPS = 8
NORM_EPS = 1e-5
LN_EPS = 1e-5
ATTN_SCALE = HEAD_DIM ** -0.5
LANES = 128
SUBLANES = 8
VMEM_LIMIT = 48 * 1024 * 1024
VMEM_LIMIT_WIDE = 56 * 1024 * 1024
DOT_COLS = 1024
SEGMENT_SPLIT = 4

ADAM_LR = 0.001
ADAM_B1 = 0.9
ADAM_B2 = 0.999
ADAM_EPS = 1e-08
ADAM_WD = 0.01
ADAM_STEP = 10

GELU_C = math.sqrt(2.0 / math.pi)
GELU_K = 0.044715

HBM_SPEC = pl.BlockSpec(memory_space=pltpu.HBM)
ANY_SPEC = pl.BlockSpec(memory_space=pl.ANY)
SEM_SPEC = pl.BlockSpec(memory_space=pltpu.SEMAPHORE)
VMEM_SPEC = pl.BlockSpec(memory_space=pltpu.VMEM)
SMEM_SPEC = pl.BlockSpec(memory_space=pltpu.SMEM)


def _sds(shape, dtype):
    return jax.ShapeDtypeStruct(shape, dtype)


def _params(*sem, vmem=VMEM_LIMIT):
    return pltpu.CompilerParams(dimension_semantics=sem or None, vmem_limit_bytes=vmem)


def _gelu(x):
    return 0.5 * x * (1.0 + jnp.tanh(GELU_C * (x + GELU_K * x * x * x)))


def _gelu_and_grad(x):
    t = jnp.tanh(GELU_C * (x + GELU_K * x * x * x))
    half = 0.5 * (1.0 + t)
    return x * half, half + 0.5 * x * (1.0 - t * t) * GELU_C * (1.0 + 3.0 * GELU_K * x * x)


def _silu_and_grad(x):
    s = jax.nn.sigmoid(x)
    return x * s, s * (1.0 + x * (1.0 - s))


def _adamw(w, g, m, v):
    m = ADAM_B1 * m + (1.0 - ADAM_B1) * g
    v = ADAM_B2 * v + (1.0 - ADAM_B2) * (g * g)
    m_hat = m / (1.0 - ADAM_B1 ** ADAM_STEP)
    v_hat = v / (1.0 - ADAM_B2 ** ADAM_STEP)
    delta = -ADAM_LR * (m_hat / (jnp.sqrt(v_hat) + ADAM_EPS) + ADAM_WD * w)
    return delta, m, v


def _mesh_pos():
    return lax.axis_index("x"), lax.axis_index("y"), lax.axis_index("c")


def _slot(x, y, c):
    return 4 * x + 2 * y + c


def _chip(x, y):
    return 2 * x + y


def _sibling(x, y, c):
    return (x, y, 1 - c)


_OTHER_CHIPS = (lambda x, y: (1 - x, y), lambda x, y: (x, 1 - y), lambda x, y: (1 - x, 1 - y))
_ICI_STAGES = (lambda x, y, c: (x ^ c, y ^ (1 - c)), lambda x, y, c: (x ^ (1 - c), y ^ c),
               lambda x, y, c: (1 - x, 1 - y))


class _Copy(NamedTuple):
    src: int
    src_slot: Callable
    dst: int
    dst_slot: Callable
    peer: Callable


def _descriptor(refs, send_sems, recv_sems, k, cp):
    pos = _mesh_pos()
    return pltpu.make_async_remote_copy(
        src_ref=refs[cp.src].at[cp.src_slot(*pos)], dst_ref=refs[cp.dst].at[cp.dst_slot(*pos)],
        send_sem=send_sems.at[k], recv_sem=recv_sems.at[k], device_id=cp.peer(*pos), device_id_type=MESH)


def _gather_first_copies(n_arrays):
    copies = []
    for a in range(n_arrays):
        copies.append(_Copy(a, _slot, a, _slot, _sibling))
        for chip in _OTHER_CHIPS:
            copies.append(_Copy(a, _slot, a, _slot, lambda x, y, c, chip=chip: (*chip(x, y), c)))
    return copies


def _gather_pass_copies(n_arrays):
    copies = []
    for a in range(n_arrays):
        for chip in _OTHER_CHIPS:
            src = lambda x, y, c, chip=chip: _slot(*chip(x, y), c)
            copies.append(_Copy(a, src, a, src, _sibling))
    return copies


def _pair_copies(n_sets):
    copies = []
    for a in range(n_sets):
        for q in range(N_CHIPS):
            copies.append(_Copy(2 * a, lambda x, y, c, q=q: q, 2 * a + 1, lambda x, y, c, q=q: q, _sibling))
    return copies


def _pair_copies_strided(n_sets):
    copies = []
    for a in range(n_sets):
        for q in range(N_CHIPS):
            copies.append(_Copy(2 * a, lambda x, y, c, q=q: 2 * q + 1 - c, 2 * a + 1, lambda x, y, c, q=q: q, _sibling))
    return copies


def _chip_sum_copies(n_sets):
    copies = []
    for a in range(n_sets):
        for k, chip in enumerate(_OTHER_CHIPS):
            copies.append(_Copy(2 * a, lambda x, y, c, chip=chip: _chip(*chip(x, y)), 2 * a + 1,
                                lambda x, y, c, k=k: k, lambda x, y, c, chip=chip: (*chip(x, y), c)))
    return copies


def _rdma_start(name, arrays, copies):
    n, nc = len(arrays), len(copies)

    def body(*refs):
        in_refs = refs[:n]
        send_sems, recv_sems = refs[n], refs[n + 1]
        token = refs[2 * n + 2]
        for k, cp in enumerate(copies):
            _descriptor(in_refs, send_sems, recv_sems, k, cp).start()
        token[...] = jnp.zeros_like(token)

    out = pl.pallas_call(
        body, name=name,
        out_shape=(pltpu.SemaphoreType.DMA((nc,)), pltpu.SemaphoreType.DMA((nc,)),
                   *[pltpu.HBM(a.shape, a.dtype) for a in arrays], _sds((SUBLANES, LANES), F32)),
        in_specs=[HBM_SPEC] * n, out_specs=(SEM_SPEC, SEM_SPEC, *([HBM_SPEC] * n), VMEM_SPEC),
        input_output_aliases={i: i + 2 for i in range(n)},
        compiler_params=pltpu.CompilerParams(has_side_effects=pltpu.SideEffectType.DATAFLOW_SIDE_EFFECTING),
    )(*[pltpu.with_memory_space_constraint(a, pltpu.HBM) for a in arrays])
    return out[0], out[1], list(out[2:2 + n]), out[2 + n]


def _rdma_wait(name, arrays, send_sems, recv_sems, copies, after):
    n = len(arrays)

    def body(*refs):
        in_refs = refs[:n]
        send_ref, recv_ref = refs[n], refs[n + 1]
        for k, cp in enumerate(copies):
            d = _descriptor(in_refs, send_ref, recv_ref, k, cp)
            d.wait_send()
            d.wait_recv()

    out = pl.pallas_call(
        body, name=name, out_shape=tuple(pltpu.HBM(a.shape, a.dtype) for a in arrays),
        in_specs=[HBM_SPEC] * n + [SEM_SPEC, SEM_SPEC, ANY_SPEC], out_specs=tuple([HBM_SPEC] * n),
        input_output_aliases={i: i for i in range(n)},
        compiler_params=pltpu.CompilerParams(has_side_effects=pltpu.SideEffectType.DATAFLOW_SIDE_EFFECTING),
    )(*arrays, send_sems, recv_sems, after)
    return list(out)


def _cast_into_slot(w, pos, name):
    rows, cols = w.shape
    tr = min(rows, 256)

    def body(pos_ref, w_ref, o_ref):
        o_ref[...] = w_ref[...].astype(BF16)

    return pl.pallas_call(
        body, name=name,
        grid_spec=pltpu.PrefetchScalarGridSpec(
            num_scalar_prefetch=1, grid=(rows // tr,),
            in_specs=[pl.BlockSpec((tr, cols), lambda i, p: (i, 0))],
            out_specs=pl.BlockSpec((None, tr, cols), lambda i, p: (_slot(p[0], p[1], p[2]), i, 0))),
        out_shape=_sds((N_DEV, rows, cols), BF16), compiler_params=_params("parallel"),
    )(pos, w)


def _all_gather_slots(arrays, name):
    n = len(arrays)
    first, passed = _gather_first_copies(n), _gather_pass_copies(n)

    def body(*refs):
        in_refs = refs[:n]
        send_sems, recv_sems = refs[2 * n], refs[2 * n + 1]
        nf = len(first)
        for k, cp in enumerate(first):
            _descriptor(in_refs, send_sems, recv_sems, k, cp).start()
        for j, cp in enumerate(passed):
            a, rel = divmod(j, 3)
            _descriptor(in_refs, send_sems, recv_sems, 4 * a + 1 + rel, first[4 * a + 1 + rel]).wait_recv()
            _descriptor(in_refs, send_sems, recv_sems, nf + j, cp).start()
        for a in range(n):
            _descriptor(in_refs, send_sems, recv_sems, 4 * a, first[4 * a]).wait_recv()
        for j, cp in enumerate(passed):
            _descriptor(in_refs, send_sems, recv_sems, nf + j, cp).wait_recv()
        for k, cp in enumerate(first + passed):
            _descriptor(in_refs, send_sems, recv_sems, k, cp).wait_send()

    nsem = len(first) + len(passed)
    out = pl.pallas_call(
        body, name=name, out_shape=tuple(_sds(a.shape, a.dtype) for a in arrays),
        in_specs=[ANY_SPEC] * n, out_specs=tuple([ANY_SPEC] * n),
        input_output_aliases={i: i for i in range(n)},
        scratch_shapes=[pltpu.SemaphoreType.DMA((nsem,)), pltpu.SemaphoreType.DMA((nsem,))],
    )(*arrays)
    return list(out)


def _pair_sum(g, land, pos, name):
    _, rows, cols = land.shape
    tr = min(rows, 128)
    strided = g.shape[0] == N_DEV

    def body(pos_ref, g_ref, l_ref, o_ref):
        o_ref[...] = (g_ref[...].astype(F32) + l_ref[...].astype(F32)).astype(BF16)

    g_map = (lambda q, i, p: (2 * q + p[2], i, 0)) if strided else (lambda q, i, p: (q, i, 0))
    blk = pl.BlockSpec((None, tr, cols), lambda q, i, p: (q, i, 0))
    return pl.pallas_call(
        body, name=name,
        grid_spec=pltpu.PrefetchScalarGridSpec(
            num_scalar_prefetch=1, grid=(N_CHIPS, rows // tr),
            in_specs=[pl.BlockSpec((None, tr, cols), g_map), blk], out_specs=blk),
        out_shape=_sds((N_CHIPS, rows, cols), BF16), compiler_params=_params("parallel", "parallel"),
    )(pos, g, land)


def _reduce_adamw(sums, land, w, m, v, pos, name):
    rows, cols = w.shape
    tr = min(rows, 64)

    def body(pos_ref, s_ref, l_ref, w_ref, m_ref, v_ref, g_ref, d_ref, nm_ref, nv_ref):
        g = s_ref[...].astype(F32)
        for k in range(N_CHIPS - 1):
            g = g + l_ref[k].astype(F32)
        delta, nm, nv = _adamw(w_ref[...], g, m_ref[...], v_ref[...])
        g_ref[...] = g
        d_ref[...] = delta
        nm_ref[...] = nm
        nv_ref[...] = nv

    spec = pl.BlockSpec((tr, cols), lambda i, p: (i, 0))
    return pl.pallas_call(
        body, name=name,
        grid_spec=pltpu.PrefetchScalarGridSpec(
            num_scalar_prefetch=1, grid=(rows // tr,),
            in_specs=[pl.BlockSpec((None, tr, cols), lambda i, p: (_chip(p[0], p[1]), i, 0)),
                      pl.BlockSpec((N_CHIPS - 1, tr, cols), lambda i, p: (0, i, 0)), spec, spec, spec],
            out_specs=(spec, spec, spec, spec)),
        out_shape=tuple([_sds((rows, cols), F32)] * 4), compiler_params=_params("parallel"),
    )(pos, sums, land, w, m, v)


def _small_allreduce_adamw(g, w, m, v):
    rows = g.shape[0]

    def body(g_ref, w_ref, m_ref, v_ref, gs_ref, d_ref, nm_ref, nv_ref, all_ref, send_sems, recv_sems):
        x, y, c = _mesh_pos()
        me = _slot(x, y, c)
        copies = []
        for k in range(1, N_DEV):
            peer = (x ^ (k >> 2), y ^ ((k >> 1) & 1), c ^ (k & 1))
            copies.append(pltpu.make_async_remote_copy(
                src_ref=g_ref, dst_ref=all_ref.at[me], send_sem=send_sems.at[k - 1],
                recv_sem=recv_sems.at[k - 1], device_id=peer, device_id_type=MESH))
        for cp in copies:
            cp.start()
        all_ref[me] = g_ref[...]
        for cp in copies:
            cp.wait_recv()
        total = all_ref[0]
        for s in range(1, N_DEV):
            total = total + all_ref[s]
        delta, nm, nv = _adamw(w_ref[...], total, m_ref[...], v_ref[...])
        gs_ref[...] = total
        d_ref[...] = delta
        nm_ref[...] = nm
        nv_ref[...] = nv
        for cp in copies:
            cp.wait_send()

    return pl.pallas_call(
        body, name="small_allreduce_adamw", out_shape=tuple([_sds((rows, LANES), F32)] * 4),
        in_specs=[VMEM_SPEC] * 4, out_specs=tuple([VMEM_SPEC] * 4),
        scratch_shapes=[pltpu.VMEM((N_DEV, rows, LANES), F32), pltpu.SemaphoreType.DMA((7,)),
                        pltpu.SemaphoreType.DMA((7,))],
    )(g, w, m, v)


_DOT_DIMS = {"nn": ((1,), (0,)), "nt": ((1,), (1,)), "tn": ((0,), (0,))}


def _dot(a, b, mode):
    return lax.dot_general(a, b, (_DOT_DIMS[mode], ((), ())), preferred_element_type=F32)


def _col_chunks(cols):
    return [(c0, min(c0 + DOT_COLS, cols)) for c0 in range(0, cols, DOT_COLS)]


def _matmul(a, b, mode, out_dtype, name, *, res=None, tm=1024, tn=1024):
    if mode == "tn":
        kdim, mdim = a.shape
    else:
        mdim, kdim = a.shape
    ndim = b.shape[0] if mode == "nt" else b.shape[1]
    tm, tn = min(tm, mdim), min(tn, ndim)
    assert mdim % tm == 0 and ndim % tn == 0, (name, mdim, ndim)

    def body(*refs):
        out = _dot(refs[0][...], refs[1][...], mode)
        if res is not None:
            out = out + refs[2][...]
        refs[-1][...] = out.astype(out_dtype)

    a_spec = pl.BlockSpec((kdim, tm), lambda i, j: (0, i)) if mode == "tn" else pl.BlockSpec((tm, kdim), lambda i, j: (i, 0))
    b_spec = pl.BlockSpec((tn, kdim), lambda i, j: (j, 0)) if mode == "nt" else pl.BlockSpec((kdim, tn), lambda i, j: (0, j))
    o_spec = pl.BlockSpec((tm, tn), lambda i, j: (i, j))
    in_specs, args = [a_spec, b_spec], [a, b]
    if res is not None:
        in_specs.append(o_spec)
        args.append(res)
    return pl.pallas_call(
        body, name=name, grid=(mdim // tm, ndim // tn), in_specs=in_specs, out_specs=o_spec,
        out_shape=_sds((mdim, ndim), out_dtype), compiler_params=_params("parallel", "parallel"),
    )(*args)


def _project(h, w_blocks, block_ids, name, *, proj=None, after=None, tm=1024, tk=512):
    s, d = h.shape
    _, _, cw = w_blocks.shape
    n = block_ids.shape[0]
    tm, tk = min(tm, s), min(tk, d)
    nk = d // tk

    def body(ids_ref, h_ref, w_ref, *rest):
        o_ref, acc_ref = rest[-2], rest[-1]
        k = pl.program_id(2)

        @pl.when(k == 0)
        def _():
            acc_ref[...] = jnp.zeros_like(acc_ref)

        for c0, c1 in _col_chunks(cw):
            acc_ref[:, c0:c1] += _dot(h_ref[...], w_ref[:, c0:c1], "nn")

        @pl.when(k == nk - 1)
        def _():
            o_ref[...] = acc_ref[...].astype(BF16)

    in_specs = [pl.BlockSpec((tm, tk), lambda j, i, k, ids: (i, k)),
                pl.BlockSpec((None, tk, cw), lambda j, i, k, ids: (ids[j], k, 0))]
    args = [block_ids, h, w_blocks]
    aliases = {}
    if proj is not None:
        in_specs.append(ANY_SPEC)
        args.append(proj)
        aliases = {3: 0}
    if after is not None:
        in_specs.append(ANY_SPEC)
        args.append(after)
    return pl.pallas_call(
        body, name=name,
        grid_spec=pltpu.PrefetchScalarGridSpec(
            num_scalar_prefetch=1, grid=(n, s // tm, d // tk), in_specs=in_specs,
            out_specs=pl.BlockSpec((tm, cw), lambda j, i, k, ids: (i, ids[j])),
            scratch_shapes=[pltpu.VMEM((tm, cw), F32)]),
        out_shape=_sds((s, N_DEV * cw), BF16), input_output_aliases=aliases,
        compiler_params=_params("arbitrary", "arbitrary", "arbitrary", vmem=VMEM_LIMIT_WIDE),
    )(*args)


def _grad_w_in_blocks(h, d_proj, block_ids, cw, name, *, slots=None, slot0=0, prev=None, init=None, after=None,
                      tm=1024, tk=1024):
    s, d = h.shape
    n = block_ids.shape[0]
    slots = n if slots is None else slots
    tm, tk = min(tm, d), min(tk, s)
    nk = s // tk

    def body(ids_ref, h_ref, g_ref, *rest):
        o_ref, acc_ref = rest[-2], rest[-1]
        k = pl.program_id(2)

        @pl.when(k == 0)
        def _():
            acc_ref[...] = jnp.zeros_like(acc_ref) if init is None else rest[0][...].astype(F32)

        for c0, c1 in _col_chunks(cw):
            acc_ref[:, c0:c1] += _dot(h_ref[...], g_ref[:, c0:c1], "tn")

        @pl.when(k == nk - 1)
        def _():
            o_ref[...] = acc_ref[...].astype(BF16)

    in_specs = [pl.BlockSpec((tk, tm), lambda q, i, k, ids: (k, i)),
                pl.BlockSpec((tk, cw), lambda q, i, k, ids: (k, ids[q]))]
    args = [block_ids, h, d_proj]
    aliases = {}
    if init is not None:
        in_specs.append(pl.BlockSpec((None, tm, cw), lambda q, i, k, ids: (q, i, 0)))
        args.append(init)
    if prev is not None:
        aliases = {len(args): 0}
        in_specs.append(ANY_SPEC)
        args.append(prev)
    if after is not None:
        in_specs.append(ANY_SPEC)
        args.append(after)
    return pl.pallas_call(
        body, name=name,
        grid_spec=pltpu.PrefetchScalarGridSpec(
            num_scalar_prefetch=1, grid=(n, d // tm, nk), in_specs=in_specs,
            out_specs=pl.BlockSpec((None, tm, cw), lambda q, i, k, ids: (slot0 + q, i, 0)),
            scratch_shapes=[pltpu.VMEM((tm, cw), F32)]),
        out_shape=_sds((slots, d, cw), BF16), input_output_aliases=aliases,
        compiler_params=_params("parallel", "parallel", "arbitrary", vmem=VMEM_LIMIT_WIDE),
    )(*args)


def _d_hidden(d_proj, w_blocks, after=None, *, tm=1024, tn=1024):
    s = d_proj.shape[0]
    nb, d, cw = w_blocks.shape
    tm, tn = min(tm, s), min(tn, d)

    def body(g_ref, w_ref, *rest):
        o_ref = rest[-1]
        k = pl.program_id(2)

        @pl.when(k == 0)
        def _():
            o_ref[...] = jnp.zeros_like(o_ref)

        o_ref[...] += _dot(g_ref[...], w_ref[...], "nt")

    in_specs = [pl.BlockSpec((tm, cw), lambda i, j, k: (i, k)),
                pl.BlockSpec((None, tn, cw), lambda i, j, k: (k, j, 0))]
    args = [d_proj, w_blocks]
    if after is not None:
        in_specs.append(ANY_SPEC)
        args.append(after)
    return pl.pallas_call(
        body, name="d_h", grid=(s // tm, d // tn, nb), in_specs=in_specs,
        out_specs=pl.BlockSpec((tm, tn), lambda i, j, k: (i, j)), out_shape=_sds((s, d), F32),
        compiler_params=_params("parallel", "parallel", "arbitrary", vmem=VMEM_LIMIT_WIDE),
    )(*args)


def _row_tile(rows):
    return min(rows, 128)


def _segment_specs(rows, d, col0):
    w = d // SEGMENT_SPLIT
    assert col0 % w == 0
    return [pl.BlockSpec((rows, w), lambda i, t=t: (i, col0 // w + t)) for t in range(SEGMENT_SPLIT)]


def _cat(refs):
    return jnp.concatenate([r[...].astype(F32) for r in refs], axis=1)


def _rmsnorm_fwd(x, g, after):
    s, d = x.shape
    tr = _row_tile(s)

    def body(x_ref, g_ref, after_ref, h_ref):
        xv = x_ref[...]
        r = lax.rsqrt(jnp.mean(xv * xv, axis=-1, keepdims=True) + NORM_EPS)
        h_ref[...] = (xv * r * g_ref[...]).astype(BF16)

    row = pl.BlockSpec((tr, d), lambda i: (i, 0))
    vec = pl.BlockSpec((1, d), lambda i: (0, 0))
    return pl.pallas_call(body, name="rmsnorm_fwd", grid=(s // tr,), in_specs=[row, vec, ANY_SPEC], out_specs=row,
                          out_shape=_sds((s, d), BF16), compiler_params=_params("parallel"))(x, g, after)


def _merge_fwd(y_a, y_b, proj, col_m):
    s, d = y_a.shape
    tr = _row_tile(s)
    ns = SEGMENT_SPLIT

    def body(ya_ref, yb_ref, *rest):
        ma, mb, o_ref = _cat(rest[:ns]), _cat(rest[ns:2 * ns]), rest[2 * ns]
        o_ref[...] = (jax.nn.sigmoid(ma) * ya_ref[...].astype(F32)
                      + jax.nn.sigmoid(mb) * yb_ref[...].astype(F32)).astype(BF16)

    row = pl.BlockSpec((tr, d), lambda i: (i, 0))
    return pl.pallas_call(
        body, name="merge_fwd", grid=(s // tr,),
        in_specs=[row, row, *_segment_specs(tr, d, col_m), *_segment_specs(tr, d, col_m + d)],
        out_specs=row, out_shape=_sds((s, d), BF16), compiler_params=_params("parallel"),
    )(y_a, y_b, *([proj] * (2 * ns)))


def _loss_and_final_norm_bwd(x2, target, g):
    s, d = x2.shape
    tr = _row_tile(s)

    def body(x_ref, t_ref, g_ref, loss_ref, dg_ref, dx_ref, dxb_ref):
        @pl.when(pl.program_id(0) == 0)
        def _():
            loss_ref[...] = jnp.zeros_like(loss_ref)
            dg_ref[...] = jnp.zeros_like(dg_ref)

        xv, gv = x_ref[...], g_ref[...]
        r = lax.rsqrt(jnp.mean(xv * xv, axis=-1, keepdims=True) + NORM_EPS)
        xhat = xv * r
        err = xhat * gv - t_ref[...]
        loss_ref[...] += 0.5 * jnp.sum(jnp.mean(err * err, axis=-1, keepdims=True))
        dy = err / d
        dg_ref[...] += jnp.sum(dy * xhat, axis=0, keepdims=True)
        dyg = dy * gv
        dx = r * (dyg - xhat * jnp.mean(dyg * xhat, axis=-1, keepdims=True))
        dx_ref[...] = dx
        dxb_ref[...] = dx.astype(BF16)

    row = pl.BlockSpec((tr, d), lambda i: (i, 0))
    vec = pl.BlockSpec((1, d), lambda i: (0, 0))
    return pl.pallas_call(
        body, name="loss_final_norm_bwd", grid=(s // tr,), in_specs=[row, row, vec],
        out_specs=(pl.BlockSpec((SUBLANES, LANES), lambda i: (0, 0)), vec, row, row),
        out_shape=(_sds((SUBLANES, LANES), F32), _sds((1, d), F32), _sds((s, d), F32), _sds((s, d), BF16)),
        compiler_params=_params("arbitrary"))(x2, target, g)


def _merge_bwd(d_merged, y_a, y_b, proj, col_m):
    s, d = y_a.shape
    tr = _row_tile(s)
    ns = SEGMENT_SPLIT

    def body(dm_ref, ya_ref, yb_ref, *rest):
        ma, mb = _cat(rest[:ns]), _cat(rest[ns:2 * ns])
        dya_ref, dyb_ref, dmg_ref = rest[2 * ns:]
        dm = dm_ref[...].astype(F32)
        sa = jax.nn.sigmoid(ma)
        sb = jax.nn.sigmoid(mb)
        dya_ref[...] = (dm * sa).astype(BF16)
        dyb_ref[...] = (dm * sb).astype(BF16)
        dmg_ref[:, :d] = (dm * ya_ref[...].astype(F32) * (sa * (1.0 - sa))).astype(BF16)
        dmg_ref[:, d:] = (dm * yb_ref[...].astype(F32) * (sb * (1.0 - sb))).astype(BF16)

    row = pl.BlockSpec((tr, d), lambda i: (i, 0))
    wide = pl.BlockSpec((tr, 2 * d), lambda i: (i, 0))
    return pl.pallas_call(
        body, name="merge_bwd", grid=(s // tr,),
        in_specs=[row, row, row, *_segment_specs(tr, d, col_m), *_segment_specs(tr, d, col_m + d)],
        out_specs=(row, row, wide),
        out_shape=(_sds((s, d), BF16), _sds((s, d), BF16), _sds((s, 2 * d), BF16)),
        compiler_params=_params("parallel"))(d_merged, y_a, y_b, *([proj] * (2 * ns)))


def _input_grad(d_h, x, g, dx2):
    s, d = x.shape
    tr = _row_tile(s)

    def body(dh_ref, x_ref, g_ref, dx2_ref, gx_ref, dg_ref):
        @pl.when(pl.program_id(0) == 0)
        def _():
            dg_ref[...] = jnp.zeros_like(dg_ref)

        xv, dh = x_ref[...], dh_ref[...]
        r = lax.rsqrt(jnp.mean(xv * xv, axis=-1, keepdims=True) + NORM_EPS)
        xhat = xv * r
        dg_ref[...] += jnp.sum(dh * xhat, axis=0, keepdims=True)
        dyg = dh * g_ref[...]
        gx_ref[...] = dx2_ref[...] + r * (dyg - xhat * jnp.mean(dyg * xhat, axis=-1, keepdims=True))

    row = pl.BlockSpec((tr, d), lambda i: (i, 0))
    vec = pl.BlockSpec((1, d), lambda i: (0, 0))
    return pl.pallas_call(
        body, name="input_grad", grid=(s // tr,), in_specs=[row, row, vec, row], out_specs=(row, vec),
        out_shape=(_sds((s, d), F32), _sds((1, d), F32)), compiler_params=_params("arbitrary"))(d_h, x, g, dx2)


def _rope_tables(positions):
    inv_freq = ROPE_THETA ** (-jnp.arange(ROPE_HALF, dtype=F32) * 2.0 / ROPE_DIM)
    ang = positions.astype(F32)[:, None] * inv_freq
    cos, sin = jnp.cos(ang), jnp.sin(ang)
    zero = jnp.zeros((positions.shape[0], HEAD_DIM - ROPE_DIM), F32)
    zero_h = jnp.zeros_like(sin)
    c = jnp.concatenate([cos, cos, zero + 1.0], axis=1)
    up = jnp.concatenate([-sin, zero_h, zero], axis=1)
    down = jnp.concatenate([zero_h, sin, zero], axis=1)
    reps = LANES // HEAD_DIM
    return jnp.stack([jnp.tile(c, (1, reps)), jnp.tile(up, (1, reps)), jnp.tile(down, (1, reps))])


def _lane_tiles(x):
    return [x[:, t * LANES:(t + 1) * LANES] for t in range(x.shape[1] // LANES)]


def _rope(x, tab):
    out = [xt * tab[0] + pltpu.roll(xt, LANES - ROPE_HALF, 1) * tab[1] + pltpu.roll(xt, ROPE_HALF, 1) * tab[2]
           for xt in _lane_tiles(x)]
    return out[0] if len(out) == 1 else jnp.concatenate(out, axis=1)


def _rope_bwd(g, tab):
    out = [gt * tab[0] + pltpu.roll(gt * tab[1], ROPE_HALF, 1) + pltpu.roll(gt * tab[2], LANES - ROPE_HALF, 1)
           for gt in _lane_tiles(g)]
    return out[0] if len(out) == 1 else jnp.concatenate(out, axis=1)


def _head(x, h):
    return x[:, h * HEAD_DIM:(h + 1) * HEAD_DIM]


def _stack_heads(x, first, count):
    return jnp.concatenate([_head(x, first + h) for h in range(count)], axis=0)


def _dot_nt(a, b):
    return lax.dot_general(a, b, (((1,), (1,)), ((), ())), preferred_element_type=F32)


def _causal(rows):
    qi = lax.broadcasted_iota(jnp.int32, (rows, BLOCK), 0) % BLOCK
    return lax.broadcasted_iota(jnp.int32, (rows, BLOCK), 1) <= qi


def _band_probs(qs, k_prev, k_cur, sink, causal, blk):
    s_prev = jnp.where(blk > 0, _dot_nt(qs, k_prev), -jnp.inf)
    s = jnp.where(causal, _dot_nt(qs, k_cur), s_prev)
    m = jnp.maximum(jnp.max(s, axis=-1, keepdims=True), sink)
    p = jnp.exp(s - m)
    p_sink = jnp.exp(sink - m)
    inv = 1.0 / (jnp.sum(p, axis=-1, keepdims=True) + p_sink)
    return p * inv, p_sink * inv


def _band_probs_by_head(qs, k_band, sink_ref, first, count, causal, blk):
    s_band = _dot_nt(qs, k_band)
    p_all, p_band, p_sink = [], [], []
    for h in range(count):
        rows = slice(h * BLOCK, (h + 1) * BLOCK)
        sink = sink_ref[first + h]
        s = jnp.where(causal, s_band[rows, :BLOCK], jnp.where(blk > 0, s_band[rows, BLOCK:], -jnp.inf))
        m = jnp.maximum(jnp.max(s, axis=-1, keepdims=True), sink)
        p = jnp.exp(s - m)
        ps = jnp.exp(sink - m)
        inv = 1.0 / (jnp.sum(p, axis=-1, keepdims=True) + ps)
        p = p * inv
        p_all.append(p)
        p_sink.append(ps * inv)
        p_band.append(_split_band(p, causal))
    cat = lambda parts: jnp.concatenate(parts, axis=0)
    return cat([c for c, _ in p_band]), cat([v for _, v in p_band]), cat(p_all), cat(p_sink)


def _sink_column(sink_ref, first, count):
    return jnp.concatenate([jnp.full((BLOCK, 1), sink_ref[first + h], F32) for h in range(count)], axis=0)


def _split_band(x, causal):
    return jnp.where(causal, x, 0.0).astype(BF16), jnp.where(causal, 0.0, x).astype(BF16)


def _attn_dims(s, d, kv):
    n_kv = kv // HEAD_DIM
    group = d // kv
    qw = 2 * group * HEAD_DIM
    assert n_kv % 2 == 0 and (d + 2 * kv) % qw == 0 and s % BLOCK == 0
    return group, qw, n_kv // 2, s // BLOCK


def _attention_fwd(proj, tables, sink, kv, after):
    s, d = proj.shape[0], sink.shape[0] * HEAD_DIM
    group, qw, n_pairs, nb = _attn_dims(s, d, kv)

    def body(sink_ref, q_ref, kc_ref, kp_ref, vc_ref, vp_ref, ga_ref, tc_ref, tp_ref, after_ref,
             attn_ref, ain_ref, qrot_ref, krot_ref):
        pair, blk = pl.program_id(0), pl.program_id(1)
        tab_c, tab_p = tc_ref[...], tp_ref[...]
        q = _rope(q_ref[...].astype(F32), tab_c) * ATTN_SCALE
        k_cur, k_prev = _rope(kc_ref[...].astype(F32), tab_c), _rope(kp_ref[...].astype(F32), tab_p)
        qrot_ref[...] = q.astype(BF16)
        krot_ref[...] = k_cur.astype(BF16)
        v_cur, v_prev = vc_ref[...], vp_ref[...]
        causal = _causal(BLOCK)
        outs = []
        for a in range(2):
            k_band = jnp.concatenate([_head(k_cur, a), _head(k_prev, a)], axis=0).astype(BF16)
            vc, vp = _head(v_cur, a).astype(BF16), _head(v_prev, a).astype(BF16)
            qs = _stack_heads(q, a * group, group).astype(BF16)
            p_cur, p_prev, _, _ = _band_probs_by_head(qs, k_band, sink_ref, (2 * pair + a) * group, group, causal, blk)
            o = jnp.dot(p_cur, vc, preferred_element_type=F32) + jnp.dot(p_prev, vp, preferred_element_type=F32)
            outs += [o[h * BLOCK:(h + 1) * BLOCK] for h in range(group)]
        attn = jnp.concatenate(outs, axis=1)
        attn_ref[...] = attn.astype(BF16)
        silu, _ = _silu_and_grad(ga_ref[...].astype(F32))
        ain_ref[...] = (attn * silu).astype(BF16)

    k0, v0, g0 = d // LANES, (d + kv) // LANES, (d + 2 * kv) // qw
    prev = lambda i: jnp.maximum(i - 1, 0)
    in_specs = [
        SMEM_SPEC,
        pl.BlockSpec((BLOCK, qw), lambda p, i: (i, p)),
        pl.BlockSpec((BLOCK, LANES), lambda p, i: (i, k0 + p)),
        pl.BlockSpec((BLOCK, LANES), lambda p, i: (prev(i), k0 + p)),
        pl.BlockSpec((BLOCK, LANES), lambda p, i: (i, v0 + p)),
        pl.BlockSpec((BLOCK, LANES), lambda p, i: (prev(i), v0 + p)),
        pl.BlockSpec((BLOCK, qw), lambda p, i: (i, g0 + p)),
        pl.BlockSpec((3, BLOCK, LANES), lambda p, i: (0, i, 0)),
        pl.BlockSpec((3, BLOCK, LANES), lambda p, i: (0, prev(i), 0)),
        ANY_SPEC,
    ]
    out = pl.BlockSpec((BLOCK, qw), lambda p, i: (i, p))
    k_out = pl.BlockSpec((BLOCK, LANES), lambda p, i: (i, p))
    return pl.pallas_call(
        body, name="attention_fwd", grid=(n_pairs, nb), in_specs=in_specs, out_specs=(out, out, out, k_out),
        out_shape=(_sds((s, d), BF16), _sds((s, d), BF16), _sds((s, d), BF16), _sds((s, kv), BF16)),
        compiler_params=_params("parallel", "parallel"),
    )(sink, proj, proj, proj, proj, proj, proj, tables, tables, after)


def _attention_bwd(proj, q_rot, k_rot, tables, sink, kv, attn, d_ain, after):
    s, d = proj.shape[0], sink.shape[0] * HEAD_DIM
    group, qw, n_pairs, nb = _attn_dims(s, d, kv)

    def body(sink_ref, q_ref, kc_ref, kp_ref, vc_ref, vp_ref, ga_ref, tc_ref, tp_ref, attn_ref, dain_ref, after_ref,
             dq_ref, dk_ref, dv_ref, dga_ref, dsink_ref, carry_k, carry_v, sink_acc):
        pair, blk = pl.program_id(0), pl.program_id(1)

        @pl.when(blk == 0)
        def _():
            carry_k[...] = jnp.zeros_like(carry_k)
            carry_v[...] = jnp.zeros_like(carry_v)
            sink_acc[...] = jnp.zeros_like(sink_acc)

        @pl.when(blk < nb)
        def _():
            tab_c, tab_p = tc_ref[...], tp_ref[...]
            q = q_ref[...].astype(F32)
            k_cur, k_prev = kc_ref[...].astype(F32), kp_ref[...].astype(F32)
            v_cur, v_prev = vc_ref[...], vp_ref[...]
            silu, silu_grad = _silu_and_grad(ga_ref[...].astype(F32))
            d_ain_v = dain_ref[...].astype(F32)
            dga_ref[...] = (d_ain_v * attn_ref[...].astype(F32) * silu_grad).astype(BF16)
            d_attn = d_ain_v * silu
            q_t = q.T
            d_attn_t = d_attn.T
            causal = _causal(BLOCK)
            dq_parts = []
            dk_t = {"cur": [], "prev": []}
            dv_t = {"cur": [], "prev": []}
            lane = lax.broadcasted_iota(jnp.int32, (BLOCK, LANES), 1)
            dsink = jnp.zeros((BLOCK, LANES), F32)
            for a in range(2):
                first = a * group
                qs = _stack_heads(q, first, group).astype(BF16)
                kc, kp = _head(k_cur, a).astype(BF16), _head(k_prev, a).astype(BF16)
                k_band = jnp.concatenate([_head(k_cur, a), _head(k_prev, a)], axis=0).astype(BF16)
                v_band = jnp.concatenate([_head(v_cur, a), _head(v_prev, a)], axis=0).astype(BF16)
                p_cur, p_prev, p, p_sink = _band_probs_by_head(qs, k_band, sink_ref, (2 * pair + a) * group, group,
                                                               causal, blk)
                do = _stack_heads(d_attn, first, group).astype(BF16)
                dp_band = _dot_nt(do, v_band)
                ds_parts, delta = [], []
                for h in range(group):
                    rows = slice(h * BLOCK, (h + 1) * BLOCK)
                    dp = jnp.where(causal, dp_band[rows, :BLOCK], dp_band[rows, BLOCK:])
                    delta.append(jnp.sum(p[rows] * dp, axis=-1, keepdims=True))
                    ds_parts.append(_split_band(p[rows] * (dp - delta[-1]), causal))
                ds_cur = jnp.concatenate([c for c, _ in ds_parts], axis=0)
                ds_prev = jnp.concatenate([v for _, v in ds_parts], axis=0)
                delta = jnp.concatenate(delta, axis=0)
                dqs = (jnp.dot(ds_cur, kc, preferred_element_type=F32)
                       + jnp.dot(ds_prev, kp, preferred_element_type=F32)) * ATTN_SCALE
                dq_parts += [dqs[h * BLOCK:(h + 1) * BLOCK] for h in range(group)]
                rows = lambda t: jnp.concatenate(
                    [t[(first + h) * HEAD_DIM:(first + h + 1) * HEAD_DIM] for h in range(group)], axis=1).astype(BF16)
                qs_t, do_t = rows(q_t), rows(d_attn_t)
                dk_t["cur"].append(jnp.dot(qs_t, ds_cur, preferred_element_type=F32))
                dk_t["prev"].append(jnp.dot(qs_t, ds_prev, preferred_element_type=F32))
                dv_t["cur"].append(jnp.dot(do_t, p_cur, preferred_element_type=F32))
                dv_t["prev"].append(jnp.dot(do_t, p_prev, preferred_element_type=F32))
                ds_sink = -(p_sink * delta)
                for h in range(group):
                    dsink = dsink + jnp.where(lane == first + h, ds_sink[h * BLOCK:(h + 1) * BLOCK], 0.0)
            sink_acc[...] += dsink
            dq_ref[...] = _rope_bwd(jnp.concatenate(dq_parts, axis=1), tab_c).astype(BF16)
            pair_block = lambda parts: jnp.concatenate(parts, axis=0).T
            dk_ref[...] = (carry_k[...] + _rope_bwd(pair_block(dk_t["prev"]), tab_p)).astype(BF16)
            dv_ref[...] = (carry_v[...] + pair_block(dv_t["prev"])).astype(BF16)
            carry_k[...] = _rope_bwd(pair_block(dk_t["cur"]), tab_c)
            carry_v[...] = pair_block(dv_t["cur"])

        @pl.when(blk == nb)
        def _():
            dk_ref[...] = carry_k[...].astype(BF16)
            dv_ref[...] = carry_v[...].astype(BF16)
            dsink_ref[0] = jnp.sum(sink_acc[...], axis=0, keepdims=True)

    v0, g0 = (d + kv) // LANES, (d + 2 * kv) // qw
    cur = lambda i: jnp.minimum(i, nb - 1)
    prev = lambda i: jnp.maximum(cur(i) - 1, 0)
    back = lambda i: jnp.maximum(i - 1, 0)
    q_spec = pl.BlockSpec((BLOCK, qw), lambda p, i: (cur(i), p))
    in_specs = [
        SMEM_SPEC,
        q_spec,
        pl.BlockSpec((BLOCK, LANES), lambda p, i: (cur(i), p)),
        pl.BlockSpec((BLOCK, LANES), lambda p, i: (prev(i), p)),
        pl.BlockSpec((BLOCK, LANES), lambda p, i: (cur(i), v0 + p)),
        pl.BlockSpec((BLOCK, LANES), lambda p, i: (prev(i), v0 + p)),
        pl.BlockSpec((BLOCK, qw), lambda p, i: (cur(i), g0 + p)),
        pl.BlockSpec((3, BLOCK, LANES), lambda p, i: (0, cur(i), 0)),
        pl.BlockSpec((3, BLOCK, LANES), lambda p, i: (0, prev(i), 0)),
        q_spec,
        q_spec,
        ANY_SPEC,
    ]
    kv_out = pl.BlockSpec((BLOCK, LANES), lambda p, i: (back(i), p))
    return pl.pallas_call(
        body, name="attention_bwd", grid=(n_pairs, nb + 1), in_specs=in_specs,
        out_specs=(q_spec, kv_out, kv_out, q_spec, pl.BlockSpec((1, 1, LANES), lambda p, i: (p, 0, 0))),
        out_shape=(_sds((s, d), BF16), _sds((s, kv), BF16), _sds((s, kv), BF16), _sds((s, d), BF16),
                   _sds((n_pairs, 1, LANES), F32)),
        scratch_shapes=[pltpu.VMEM((BLOCK, LANES), F32), pltpu.VMEM((BLOCK, LANES), F32),
                        pltpu.VMEM((BLOCK, LANES), F32)],
        compiler_params=_params("parallel", "arbitrary"),
    )(sink, q_rot, k_rot, k_rot, proj, proj, proj, tables, tables, attn, d_ain, after)


def _gmlp_core(gu, gv, ln_g, ln_b, w_ref, bias_t):
    xc = gv - jnp.mean(gv, axis=-1, keepdims=True)
    rstd = lax.rsqrt(jnp.mean(xc * xc, axis=-1, keepdims=True) + LN_EPS)
    xhat = xc * rstd
    vn = (xhat * ln_g + ln_b).astype(BF16)
    gd = gu.shape[1] // GMLP_GROUPS
    tri = (lax.broadcasted_iota(jnp.int32, (BLOCK, BLOCK), 0) >= lax.broadcasted_iota(jnp.int32, (BLOCK, BLOCK), 1))
    w_tri = [jnp.where(tri, w_ref[g], 0.0).astype(BF16) for g in range(GMLP_GROUPS)]
    mixed = jnp.concatenate(
        [jnp.dot(w_tri[g], vn[:, g * gd:(g + 1) * gd], preferred_element_type=F32) + bias_t[:, g:g + 1]
         for g in range(GMLP_GROUPS)], axis=1)
    return gu, xhat, rstd, vn, w_tri, tri, mixed


def _whole(shape):
    return pl.BlockSpec(shape, lambda i: tuple(0 for _ in shape))


def _gmlp_fwd(proj, col_u, d, w_s, bias_t, ln_g, ln_b, after):
    s = proj.shape[0]
    ns = SEGMENT_SPLIT

    def body(*refs):
        u, vg, gb = _cat(refs[:ns]), _cat(refs[ns:2 * ns]), _cat(refs[2 * ns:3 * ns])
        w_ref, bt_ref, lg_ref, lb_ref, after_ref, o_ref = refs[3 * ns:]
        gu, _, _, _, _, _, mixed = _gmlp_core(_gelu(u), _gelu(vg), lg_ref[...], lb_ref[...], w_ref, bt_ref[...])
        silu, _ = _silu_and_grad(gb)
        o_ref[...] = ((gu * mixed) * silu).astype(BF16)

    segs = [sp for j in range(3) for sp in _segment_specs(BLOCK, d, col_u + j * d)]
    return pl.pallas_call(
        body, name="gmlp_fwd", grid=(s // BLOCK,),
        in_specs=[*segs, _whole(w_s.shape), _whole(bias_t.shape), _whole((1, d)), _whole((1, d)), ANY_SPEC],
        out_specs=pl.BlockSpec((BLOCK, d), lambda i: (i, 0)), out_shape=_sds((s, d), BF16),
        compiler_params=_params("parallel"),
    )(*([proj] * (3 * ns)), w_s, bias_t, ln_g, ln_b, after)


def _gmlp_bwd(proj, col_u, d, w_s, bias_t, ln_g, ln_b, d_bin, after):
    s = proj.shape[0]
    gd = d // GMLP_GROUPS
    ns = SEGMENT_SPLIT

    def body(*refs):
        u, vg, gb = _cat(refs[:ns]), _cat(refs[ns:2 * ns]), _cat(refs[2 * ns:3 * ns])
        w_ref, bt_ref, lg_ref, lb_ref, dbin_ref, after_ref, dg_ref, dw_ref, dbt_ref, dlg_ref, dlb_ref = refs[3 * ns:]

        @pl.when(pl.program_id(0) == 0)
        def _():
            dw_ref[...] = jnp.zeros_like(dw_ref)
            dbt_ref[...] = jnp.zeros_like(dbt_ref)
            dlg_ref[...] = jnp.zeros_like(dlg_ref)
            dlb_ref[...] = jnp.zeros_like(dlb_ref)

        ln_g = lg_ref[...]
        (gu, gu_grad), (gv, gv_grad) = _gelu_and_grad(u), _gelu_and_grad(vg)
        gu, xhat, rstd, vn, w_tri, tri, mixed = _gmlp_core(gu, gv, ln_g, lb_ref[...], w_ref, bt_ref[...])
        silu, silu_grad = _silu_and_grad(gb)
        d_bin_v = dbin_ref[...].astype(F32)
        d_sg = d_bin_v * silu
        dg_ref[:, 2 * d:] = (d_bin_v * (gu * mixed) * silu_grad).astype(BF16)
        dg_ref[:, :d] = (d_sg * mixed * gu_grad).astype(BF16)
        d_mixed = d_sg * gu
        d_mixed_b = d_mixed.astype(BF16)
        d_vn, d_bias = [], []
        for g in range(GMLP_GROUPS):
            dm_g = d_mixed_b[:, g * gd:(g + 1) * gd]
            d_bias.append(jnp.sum(d_mixed[:, g * gd:(g + 1) * gd], axis=-1, keepdims=True))
            dw = lax.dot_general(dm_g, vn[:, g * gd:(g + 1) * gd], (((1,), (1,)), ((), ())),
                                 preferred_element_type=F32)
            dw_ref[g] += jnp.where(tri, dw, 0.0)
            d_vn.append(lax.dot_general(w_tri[g], dm_g, (((0,), (0,)), ((), ())), preferred_element_type=F32))
        dbt_ref[...] += jnp.concatenate(d_bias, axis=1)
        d_vn = jnp.concatenate(d_vn, axis=1)
        dlg_ref[...] += jnp.sum(d_vn * xhat, axis=0, keepdims=True)
        dlb_ref[...] += jnp.sum(d_vn, axis=0, keepdims=True)
        d_xhat = d_vn * ln_g
        d_gv = rstd * (d_xhat - jnp.mean(d_xhat, axis=-1, keepdims=True)
                       - xhat * jnp.mean(d_xhat * xhat, axis=-1, keepdims=True))
        dg_ref[:, d:2 * d] = (d_gv * gv_grad).astype(BF16)

    segs = [sp for j in range(3) for sp in _segment_specs(BLOCK, d, col_u + j * d)]
    return pl.pallas_call(
        body, name="gmlp_bwd", grid=(s // BLOCK,),
        in_specs=[*segs, _whole(w_s.shape), _whole(bias_t.shape), _whole((1, d)), _whole((1, d)),
                  pl.BlockSpec((BLOCK, d), lambda i: (i, 0)), ANY_SPEC],
        out_specs=(pl.BlockSpec((BLOCK, 3 * d), lambda i: (i, 0)), _whole(w_s.shape), _whole(bias_t.shape),
                   _whole((1, d)), _whole((1, d))),
        out_shape=(_sds((s, 3 * d), BF16), _sds(w_s.shape, F32), _sds(bias_t.shape, F32), _sds((1, d), F32),
                   _sds((1, d), F32)),
        compiler_params=_params("arbitrary"),
    )(*([proj] * (3 * ns)), w_s, bias_t, ln_g, ln_b, d_bin, after)


def _pack(parts):
    rows = []
    tile = SUBLANES * LANES
    for p in parts:
        flat = p.astype(F32).reshape(-1)
        padded = -(-flat.shape[0] // tile) * tile
        rows.append(jnp.pad(flat, (0, padded - flat.shape[0])).reshape(-1, LANES))
    return jnp.concatenate(rows, axis=0)


def _unpack(packed, shapes):
    out, row = [], 0
    tile = SUBLANES * LANES
    for shape in shapes:
        size = math.prod(shape)
        n_rows = -(-size // tile) * SUBLANES
        out.append(packed[row:row + n_rows].reshape(-1)[:size].reshape(shape))
        row += n_rows
    return out


def kernel(x, positions, norm_g, w_in, attn_sink, gmlp_ln_g, gmlp_ln_b, w_spatial, b_spatial, w_up_attn, w_up_gmlp, w_out, final_norm_g, loss_target, m_norm_g, m_w_in, m_attn_sink, m_gmlp_ln_g, m_gmlp_ln_b, m_w_spatial, m_b_spatial, m_w_up_attn, m_w_up_gmlp, m_w_out, m_final_norm_g, v_norm_g, v_w_in, v_attn_sink, v_gmlp_ln_g, v_gmlp_ln_b, v_w_spatial, v_b_spatial, v_w_up_attn, v_w_up_gmlp, v_w_out, v_final_norm_g):
    x2d, target = x[0], loss_target[0]
    s, d = x2d.shape
    n_q_heads = attn_sink.shape[1]
    cw = w_in.shape[2]
    rw = w_up_attn.shape[1]
    kv = (cw * N_DEV - 7 * d) // 2
    col_u, col_m = 2 * d + 2 * kv, 5 * d + 2 * kv
    final_g = final_norm_g.reshape(1, d)
    sink = attn_sink[0]
    w_s = w_spatial[0]
    bias_t = b_spatial[0].T
    mx, my, mc = _mesh_pos()
    pos = jnp.stack([mx, my, mc]).astype(jnp.int32)
    chips = jnp.arange(N_CHIPS, dtype=jnp.int32)

    def one_block(fn):
        return jnp.reshape(fn(mx, my, mc), (1,)).astype(jnp.int32)

    w_in_b = _cast_into_slot(w_in[0], pos, "cast_w_in")
    squares = [_cast_into_slot(w[0], pos, "cast_" + nm)
               for nm, w in (("w_up_attn", w_up_attn), ("w_up_gmlp", w_up_gmlp), ("w_out", w_out))]
    to_sibling = [_Copy(0, _slot, 0, _slot, _sibling)]
    ici = [[_Copy(0, _slot, 0, _slot, lambda x, y, c, chip=chip: (*chip(x, y, c), c))] for chip in _ICI_STAGES[:2]]
    relayed = lambda x, y, c: _slot(*_ICI_STAGES[1](x, y, c), c)
    ici.append([_Copy(0, relayed, 0, relayed, lambda x, y, c: (*_ICI_STAGES[0](x, y, c), c))])
    passes = []
    for chip in _ICI_STAGES:
        landed = lambda x, y, c, chip=chip: _slot(*chip(x, y, c), c)
        passes.append([_Copy(0, landed, 0, landed, _sibling)])
    sib_sems = _rdma_start("w_in_sibling_start", [w_in_b], to_sibling)
    ici_sems = _rdma_start("w_in_ici0_start", sib_sems[2], ici[0])
    h = _rmsnorm_fwd(x2d, norm_g, ici_sems[3])
    proj = _project(h, ici_sems[2][0], one_block(_slot), "projection_own")
    w_blocks = _rdma_wait("w_in_sibling_wait", ici_sems[2], sib_sems[0], sib_sems[1], to_sibling, proj)
    proj = _project(h, w_blocks[0], one_block(lambda x, y, c: _slot(x, y, 1 - c)), "projection_sibling", proj=proj)
    w_blocks = _rdma_wait("w_in_ici0_wait", w_blocks, ici_sems[0], ici_sems[1], ici[0], proj)
    first = _gather_first_copies(3)
    for k, chip in enumerate(_ICI_STAGES):
        if k + 1 < len(_ICI_STAGES):
            ici_sems = _rdma_start("w_in_ici%d_start" % (k + 1), w_blocks, ici[k + 1])
            w_blocks = ici_sems[2]
        else:
            send1, recv1, thru, _ = _rdma_start("gather_squares_start", squares + w_blocks, first)
            squares, w_blocks = thru[:3], thru[3:]
        pass_sems = _rdma_start("w_in_pass%d_start" % k, w_blocks, passes[k])
        proj = _project(h, pass_sems[2][0], one_block(lambda x, y, c, chip=chip: _slot(*chip(x, y, c), c)),
                        "projection_ici%d" % k, proj=proj, after=pass_sems[3])
        w_blocks = _rdma_wait("w_in_pass%d_wait" % k, pass_sems[2], pass_sems[0], pass_sems[1], passes[k], proj)
        proj = _project(h, w_blocks[0], one_block(lambda x, y, c, chip=chip: _slot(*chip(x, y, 1 - c), 1 - c)),
                        "projection_pass%d" % k, proj=proj)
        if k + 1 < len(_ICI_STAGES):
            w_blocks = _rdma_wait("w_in_ici%d_wait" % (k + 1), w_blocks, ici_sems[0], ici_sems[1], ici[k + 1], proj)
    w_in_b = w_blocks[0]

    tables = _rope_tables(positions[0])
    attn, a_in, q_rot, k_rot = _attention_fwd(proj, tables, sink, kv, proj)
    squares = _rdma_wait("gather_squares_wait", squares, send1, recv1, first, attn)
    passed = _gather_pass_copies(3)
    send2, recv2, squares, token = _rdma_start("pass_squares_start", squares, passed)
    b_in = _gmlp_fwd(proj, col_u, d, w_s, bias_t, gmlp_ln_g, gmlp_ln_b, token)
    squares = _rdma_wait("pass_squares_wait", squares, send2, recv2, passed, b_in)
    w_ua, w_ug, w_o = [w.reshape(N_DEV * rw, d) for w in squares]
    y_a = _matmul(a_in, w_ua, "nn", BF16, "up_attn")
    y_b = _matmul(b_in, w_ug, "nn", BF16, "up_gmlp")
    merged = _merge_fwd(y_a, y_b, proj, col_m)
    x_out = _matmul(merged, w_o, "nn", F32, "out_proj", res=x2d, tn=512)
    loss_p, d_final_g, dx2, dx2_b = _loss_and_final_norm_bwd(x_out, target, final_g)

    d_merged = _matmul(dx2_b, w_o, "nt", BF16, "d_merged")
    g_w_out = _matmul(merged, dx2_b, "tn", BF16, "g_w_out")
    d_ya, d_yb, d_mg = _merge_bwd(d_merged, y_a, y_b, proj, col_m)
    d_ain = _matmul(d_ya, w_ua, "nt", BF16, "d_a_in")
    g_w_ua = _matmul(a_in, d_ya, "tn", BF16, "g_w_up_attn")
    d_bin = _matmul(d_yb, w_ug, "nt", BF16, "d_b_in")
    g_w_ug = _matmul(b_in, d_yb, "tn", BF16, "g_w_up_gmlp")
    sq_grads = [g.reshape(N_DEV, rw, d) for g in (g_w_ua, g_w_ug, g_w_out)]
    sq_land = [lax.empty((N_CHIPS, rw, d), BF16) for _ in sq_grads]
    pairs_sq = _pair_copies_strided(3)
    arrays = [a for gl in zip(sq_grads, sq_land) for a in gl]
    send3, recv3, arrays, token = _rdma_start("pair_squares_start", arrays, pairs_sq)
    d_q, d_k, d_v, d_ga, d_sink = _attention_bwd(proj, q_rot, k_rot, tables, sink, kv, attn, d_ain, token)
    arrays = _rdma_wait("pair_squares_wait", arrays, send3, recv3, pairs_sq, d_q)
    sq_sums = [_pair_sum(arrays[2 * a], arrays[2 * a + 1], pos, "pair_sum_%d" % a) for a in range(3)]
    sq_land2 = [lax.empty((N_CHIPS - 1, rw, d), BF16) for _ in sq_sums]
    chip_sq = _chip_sum_copies(3)
    arrays = [a for gl in zip(sq_sums, sq_land2) for a in gl]
    send4, recv4, sq_arrays, token = _rdma_start("chip_squares_start", arrays, chip_sq)
    d_g, d_w_s, d_bias_t, d_ln_g, d_ln_b = _gmlp_bwd(proj, col_u, d, w_s, bias_t, gmlp_ln_g, gmlp_ln_b, d_bin, token)
    d_proj = jnp.concatenate([d_q, d_k, d_v, d_ga, d_g, d_mg], axis=1)

    half = N_CHIPS // 2
    pairs_in = [_Copy(0, lambda x, y, c, q=q: q, 1, lambda x, y, c, q=q: q, _sibling) for q in range(half)]
    sent, token = [], None
    for j in range(2):
        g_sib = _grad_w_in_blocks(h, d_proj, 2 * chips[j * half:(j + 1) * half] + 1 - mc, cw,
                                  "g_w_in_sibling%d" % j, after=token)
        sent.append(_rdma_start("pair_w_in%d_start" % j, [g_sib, lax.empty((half, d, cw), BF16)], pairs_in))
        token = sent[-1][3]
    in_sums = None
    for j in range(2):
        send5, recv5, arrays, _ = sent[j]
        arrays = _rdma_wait("pair_w_in%d_wait" % j, arrays, send5, recv5, pairs_in, token if j == 0 else in_sums)
        in_sums = _grad_w_in_blocks(h, d_proj, 2 * chips[j * half:(j + 1) * half] + mc, cw, "g_w_in_own%d" % j,
                                    slots=N_CHIPS, slot0=j * half, prev=in_sums, init=arrays[1], tm=512)
    chip_in = _chip_sum_copies(1)
    send6, recv6, in_arrays, token = _rdma_start(
        "chip_w_in_start", [in_sums, lax.empty((N_CHIPS - 1, d, cw), BF16)], chip_in)
    d_h = _d_hidden(d_proj, w_in_b, after=token)
    grad_x, d_norm_g = _input_grad(d_h, x2d, norm_g, dx2)

    sq_arrays = _rdma_wait("chip_squares_wait", sq_arrays, send4, recv4, chip_sq, grad_x)
    big = {}
    for a, (name, w, m, v) in enumerate((("w_up_attn", w_up_attn, m_w_up_attn, v_w_up_attn),
                                         ("w_up_gmlp", w_up_gmlp, m_w_up_gmlp, v_w_up_gmlp),
                                         ("w_out", w_out, m_w_out, v_w_out))):
        big[name] = [r[None] for r in _reduce_adamw(sq_arrays[2 * a], sq_arrays[2 * a + 1], w[0], m[0], v[0], pos,
                                                    "adamw_" + name)]
    in_arrays = _rdma_wait("chip_w_in_wait", in_arrays, send6, recv6, chip_in, big["w_out"][0])
    big["w_in"] = [r[None] for r in _reduce_adamw(in_arrays[0], in_arrays[1], w_in[0], m_w_in[0], v_w_in[0], pos,
                                                  "adamw_w_in")]

    heads_per_pair = 2 * n_q_heads // (kv // HEAD_DIM)
    g_sink = d_sink[:, 0, :heads_per_pair].reshape(1, n_q_heads)
    small_w = [norm_g, attn_sink, gmlp_ln_g, gmlp_ln_b, w_spatial, b_spatial, final_norm_g]
    small_m = [m_norm_g, m_attn_sink, m_gmlp_ln_g, m_gmlp_ln_b, m_w_spatial, m_b_spatial, m_final_norm_g]
    small_v = [v_norm_g, v_attn_sink, v_gmlp_ln_g, v_gmlp_ln_b, v_w_spatial, v_b_spatial, v_final_norm_g]
    small_g = [d_norm_g, g_sink, d_ln_g, d_ln_b, d_w_s[None], d_bias_t.T[None], d_final_g.reshape(d)]
    loss_pad = jnp.zeros((1,), F32)
    shapes = [w.shape for w in small_w] + [(1,)]
    packed = _small_allreduce_adamw(_pack(small_g + [loss_p[0, :1]]), _pack(small_w + [loss_pad]),
                                    _pack(small_m + [loss_pad]), _pack(small_v + [loss_pad]))
    sg, sd, sm, sv = [_unpack(p, shapes) for p in packed]
    loss = sg[-1][0]

    names = ["norm_g", "w_in", "attn_sink", "gmlp_ln_g", "gmlp_ln_b", "w_spatial", "b_spatial", "w_up_attn",
             "w_up_gmlp", "w_out", "final_norm_g"]
    small_names = ["norm_g", "attn_sink", "gmlp_ln_g", "gmlp_ln_b", "w_spatial", "b_spatial", "final_norm_g"]
    outs = [[], [], [], []]
    for nm in names:
        for k in range(4):
            if nm in big:
                outs[k].append(big[nm][k])
            else:
                outs[k].append((sg, sd, sm, sv)[k][small_names.index(nm)])
    return (loss, grad_x[None], *outs[0], *outs[1], *outs[2], *outs[3])
```

```python
import math
from typing import Callable, NamedTuple

import jax
import jax.numpy as jnp
from jax import lax
from jax.experimental import pallas as pl
from jax.experimental.pallas import tpu as pltpu

F32 = jnp.float32
BF16 = jnp.bfloat16
MESH = pl.DeviceIdType.MESH

N_DEV = 8
N_CHIPS = 4
HEAD_DIM = 64
BLOCK = 128
HEADS_PER_TILE = 8
ROPE_DIM = 16
ROPE_HALF = ROPE_DIM // 2
ROPE_THETA = 500000.0
GMLP_GROUPS = 8
NORM_EPS = 1e-5
LN_EPS = 1e-5
ATTN_SCALE = HEAD_DIM ** -0.5
LANES = 128
SUBLANES = 8
VMEM_LIMIT = 48 * 1024 * 1024
VMEM_LIMIT_WIDE = 56 * 1024 * 1024
DOT_COLS = 1024
SEGMENT_SPLIT = 4

ADAM_LR = 0.001
ADAM_B1 = 0.9
ADAM_B2 = 0.999
ADAM_EPS = 1e-08
ADAM_WD = 0.01
ADAM_STEP = 10

GELU_C = math.sqrt(2.0 / math.pi)
GELU_K = 0.044715

HBM_SPEC = pl.BlockSpec(memory_space=pltpu.HBM)
ANY_SPEC = pl.BlockSpec(memory_space=pl.ANY)
SEM_SPEC = pl.BlockSpec(memory_space=pltpu.SEMAPHORE)
VMEM_SPEC = pl.BlockSpec(memory_space=pltpu.VMEM)
SMEM_SPEC = pl.BlockSpec(memory_space=pltpu.SMEM)


def _sds(shape, dtype):
    return jax.ShapeDtypeStruct(shape, dtype)


def _params(*sem, vmem=VMEM_LIMIT):
    return pltpu.CompilerParams(dimension_semantics=sem or None, vmem_limit_bytes=vmem)


def _gelu(x):
    return 0.5 * x * (1.0 + jnp.tanh(GELU_C * (x + GELU_K * x * x * x)))


def _gelu_and_grad(x):
    t = jnp.tanh(GELU_C * (x + GELU_K * x * x * x))
    half = 0.5 * (1.0 + t)
    return x * half, half + 0.5 * x * (1.0 - t * t) * GELU_C * (1.0 + 3.0 * GELU_K * x * x)


def _silu_and_grad(x):
    s = jax.nn.sigmoid(x)
    return x * s, s * (1.0 + x * (1.0 - s))


def _adamw(w, g, m, v):
    m = ADAM_B1 * m + (1.0 - ADAM_B1) * g
    v = ADAM_B2 * v + (1.0 - ADAM_B2) * (g * g)
    m_hat = m / (1.0 - ADAM_B1 ** ADAM_STEP)
    v_hat = v / (1.0 - ADAM_B2 ** ADAM_STEP)
    delta = -ADAM_LR * (m_hat / (jnp.sqrt(v_hat) + ADAM_EPS) + ADAM_WD * w)
    return delta, m, v


def _mesh_pos():
    return lax.axis_index("x"), lax.axis_index("y"), lax.axis_index("c")


def _slot(x, y, c):
    return 4 * x + 2 * y + c


def _chip(x, y):
    return 2 * x + y


def _sibling(x, y, c):
    return (x, y, 1 - c)


_OTHER_CHIPS = (lambda x, y: (1 - x, y), lambda x, y: (x, 1 - y), lambda x, y: (1 - x, 1 - y))
_ICI_STAGES = (lambda x, y, c: (x ^ c, y ^ (1 - c)), lambda x, y, c: (x ^ (1 - c), y ^ c),
               lambda x, y, c: (1 - x, 1 - y))


class _Copy(NamedTuple):
    src: int
    src_slot: Callable
    dst: int
    dst_slot: Callable
    peer: Callable


def _descriptor(refs, send_sems, recv_sems, k, cp):
    pos = _mesh_pos()
    return pltpu.make_async_remote_copy(
        src_ref=refs[cp.src].at[cp.src_slot(*pos)], dst_ref=refs[cp.dst].at[cp.dst_slot(*pos)],
        send_sem=send_sems.at[k], recv_sem=recv_sems.at[k], device_id=cp.peer(*pos), device_id_type=MESH)


def _gather_first_copies(n_arrays):
    copies = []
    for a in range(n_arrays):
        copies.append(_Copy(a, _slot, a, _slot, _sibling))
        for chip in _OTHER_CHIPS:
            copies.append(_Copy(a, _slot, a, _slot, lambda x, y, c, chip=chip: (*chip(x, y), c)))
    return copies


def _gather_pass_copies(n_arrays):
    copies = []
    for a in range(n_arrays):
        for chip in _OTHER_CHIPS:
            src = lambda x, y, c, chip=chip: _slot(*chip(x, y), c)
            copies.append(_Copy(a, src, a, src, _sibling))
    return copies


def _pair_copies(n_sets):
    copies = []
    for a in range(n_sets):
        for q in range(N_CHIPS):
            copies.append(_Copy(2 * a, lambda x, y, c, q=q: q, 2 * a + 1, lambda x, y, c, q=q: q, _sibling))
    return copies


def _pair_copies_strided(n_sets):
    copies = []
    for a in range(n_sets):
        for q in range(N_CHIPS):
            copies.append(_Copy(2 * a, lambda x, y, c, q=q: 2 * q + 1 - c, 2 * a + 1, lambda x, y, c, q=q: q, _sibling))
    return copies


def _chip_sum_copies(n_sets):
    copies = []
    for a in range(n_sets):
        for k, chip in enumerate(_OTHER_CHIPS):
            copies.append(_Copy(2 * a, lambda x, y, c, chip=chip: _chip(*chip(x, y)), 2 * a + 1,
                                lambda x, y, c, k=k: k, lambda x, y, c, chip=chip: (*chip(x, y), c)))
    return copies


def _rdma_start(name, arrays, copies):
    n, nc = len(arrays), len(copies)

    def body(*refs):
        in_refs = refs[:n]
        send_sems, recv_sems = refs[n], refs[n + 1]
        token = refs[2 * n + 2]
        for k, cp in enumerate(copies):
            _descriptor(in_refs, send_sems, recv_sems, k, cp).start()
        token[...] = jnp.zeros_like(token)

    out = pl.pallas_call(
        body, name=name,
        out_shape=(pltpu.SemaphoreType.DMA((nc,)), pltpu.SemaphoreType.DMA((nc,)),
                   *[pltpu.HBM(a.shape, a.dtype) for a in arrays], _sds((SUBLANES, LANES), F32)),
        in_specs=[HBM_SPEC] * n, out_specs=(SEM_SPEC, SEM_SPEC, *([HBM_SPEC] * n), VMEM_SPEC),
        input_output_aliases={i: i + 2 for i in range(n)},
        compiler_params=pltpu.CompilerParams(has_side_effects=pltpu.SideEffectType.DATAFLOW_SIDE_EFFECTING),
    )(*[pltpu.with_memory_space_constraint(a, pltpu.HBM) for a in arrays])
    return out[0], out[1], list(out[2:2 + n]), out[2 + n]


def _rdma_wait(name, arrays, send_sems, recv_sems, copies, after):
    n = len(arrays)

    def body(*refs):
        in_refs = refs[:n]
        send_ref, recv_ref = refs[n], refs[n + 1]
        for k, cp in enumerate(copies):
            d = _descriptor(in_refs, send_ref, recv_ref, k, cp)
            d.wait_send()
            d.wait_recv()

    out = pl.pallas_call(
        body, name=name, out_shape=tuple(pltpu.HBM(a.shape, a.dtype) for a in arrays),
        in_specs=[HBM_SPEC] * n + [SEM_SPEC, SEM_SPEC, ANY_SPEC], out_specs=tuple([HBM_SPEC] * n),
        input_output_aliases={i: i for i in range(n)},
        compiler_params=pltpu.CompilerParams(has_side_effects=pltpu.SideEffectType.DATAFLOW_SIDE_EFFECTING),
    )(*arrays, send_sems, recv_sems, after)
    return list(out)


def _cast_into_slot(w, pos, name):
    rows, cols = w.shape
    tr = min(rows, 256)

    def body(pos_ref, w_ref, o_ref):
        o_ref[...] = w_ref[...].astype(BF16)

    return pl.pallas_call(
        body, name=name,
        grid_spec=pltpu.PrefetchScalarGridSpec(
            num_scalar_prefetch=1, grid=(rows // tr,),
            in_specs=[pl.BlockSpec((tr, cols), lambda i, p: (i, 0))],
            out_specs=pl.BlockSpec((None, tr, cols), lambda i, p: (_slot(p[0], p[1], p[2]), i, 0))),
        out_shape=_sds((N_DEV, rows, cols), BF16), compiler_params=_params("parallel"),
    )(pos, w)


def _all_gather_slots(arrays, name):
    n = len(arrays)
    first, passed = _gather_first_copies(n), _gather_pass_copies(n)

    def body(*refs):
        in_refs = refs[:n]
        send_sems, recv_sems = refs[2 * n], refs[2 * n + 1]
        nf = len(first)
        for k, cp in enumerate(first):
            _descriptor(in_refs, send_sems, recv_sems, k, cp).start()
        for j, cp in enumerate(passed):
            a, rel = divmod(j, 3)
            _descriptor(in_refs, send_sems, recv_sems, 4 * a + 1 + rel, first[4 * a + 1 + rel]).wait_recv()
            _descriptor(in_refs, send_sems, recv_sems, nf + j, cp).start()
        for a in range(n):
            _descriptor(in_refs, send_sems, recv_sems, 4 * a, first[4 * a]).wait_recv()
        for j, cp in enumerate(passed):
            _descriptor(in_refs, send_sems, recv_sems, nf + j, cp).wait_recv()
        for k, cp in enumerate(first + passed):
            _descriptor(in_refs, send_sems, recv_sems, k, cp).wait_send()

    nsem = len(first) + len(passed)
    out = pl.pallas_call(
        body, name=name, out_shape=tuple(_sds(a.shape, a.dtype) for a in arrays),
        in_specs=[ANY_SPEC] * n, out_specs=tuple([ANY_SPEC] * n),
        input_output_aliases={i: i for i in range(n)},
        scratch_shapes=[pltpu.SemaphoreType.DMA((nsem,)), pltpu.SemaphoreType.DMA((nsem,))],
    )(*arrays)
    return list(out)


def _pair_sum(g, land, pos, name):
    _, rows, cols = land.shape
    tr = min(rows, 128)
    strided = g.shape[0] == N_DEV

    def body(pos_ref, g_ref, l_ref, o_ref):
        o_ref[...] = (g_ref[...].astype(F32) + l_ref[...].astype(F32)).astype(BF16)

    g_map = (lambda q, i, p: (2 * q + p[2], i, 0)) if strided else (lambda q, i, p: (q, i, 0))
    blk = pl.BlockSpec((None, tr, cols), lambda q, i, p: (q, i, 0))
    return pl.pallas_call(
        body, name=name,
        grid_spec=pltpu.PrefetchScalarGridSpec(
            num_scalar_prefetch=1, grid=(N_CHIPS, rows // tr),
            in_specs=[pl.BlockSpec((None, tr, cols), g_map), blk], out_specs=blk),
        out_shape=_sds((N_CHIPS, rows, cols), BF16), compiler_params=_params("parallel", "parallel"),
    )(pos, g, land)


def _reduce_adamw(sums, land, w, m, v, pos, name):
    rows, cols = w.shape
    tr = min(rows, 64)

    def body(pos_ref, s_ref, l_ref, w_ref, m_ref, v_ref, g_ref, d_ref, nm_ref, nv_ref):
        g = s_ref[...].astype(F32)
        for k in range(N_CHIPS - 1):
            g = g + l_ref[k].astype(F32)
        delta, nm, nv = _adamw(w_ref[...], g, m_ref[...], v_ref[...])
        g_ref[...] = g
        d_ref[...] = delta
        nm_ref[...] = nm
        nv_ref[...] = nv

    spec = pl.BlockSpec((tr, cols), lambda i, p: (i, 0))
    return pl.pallas_call(
        body, name=name,
        grid_spec=pltpu.PrefetchScalarGridSpec(
            num_scalar_prefetch=1, grid=(rows // tr,),
            in_specs=[pl.BlockSpec((None, tr, cols), lambda i, p: (_chip(p[0], p[1]), i, 0)),
                      pl.BlockSpec((N_CHIPS - 1, tr, cols), lambda i, p: (0, i, 0)), spec, spec, spec],
            out_specs=(spec, spec, spec, spec)),
        out_shape=tuple([_sds((rows, cols), F32)] * 4), compiler_params=_params("parallel"),
    )(pos, sums, land, w, m, v)


def _small_allreduce_adamw(g, w, m, v):
    rows = g.shape[0]

    def body(g_ref, w_ref, m_ref, v_ref, gs_ref, d_ref, nm_ref, nv_ref, all_ref, send_sems, recv_sems):
        x, y, c = _mesh_pos()
        me = _slot(x, y, c)
        copies = []
        for k in range(1, N_DEV):
            peer = (x ^ (k >> 2), y ^ ((k >> 1) & 1), c ^ (k & 1))
            copies.append(pltpu.make_async_remote_copy(
                src_ref=g_ref, dst_ref=all_ref.at[me], send_sem=send_sems.at[k - 1],
                recv_sem=recv_sems.at[k - 1], device_id=peer, device_id_type=MESH))
        for cp in copies:
            cp.start()
        all_ref[me] = g_ref[...]
        for cp in copies:
            cp.wait_recv()
        total = all_ref[0]
        for s in range(1, N_DEV):
            total = total + all_ref[s]
        delta, nm, nv = _adamw(w_ref[...], total, m_ref[...], v_ref[...])
        gs_ref[...] = total
        d_ref[...] = delta
        nm_ref[...] = nm
        nv_ref[...] = nv
        for cp in copies:
            cp.wait_send()

    return pl.pallas_call(
        body, name="small_allreduce_adamw", out_shape=tuple([_sds((rows, LANES), F32)] * 4),
        in_specs=[VMEM_SPEC] * 4, out_specs=tuple([VMEM_SPEC] * 4),
        scratch_shapes=[pltpu.VMEM((N_DEV, rows, LANES), F32), pltpu.SemaphoreType.DMA((7,)),
                        pltpu.SemaphoreType.DMA((7,))],
    )(g, w, m, v)


_DOT_DIMS = {"nn": ((1,), (0,)), "nt": ((1,), (1,)), "tn": ((0,), (0,))}


def _dot(a, b, mode):
    return lax.dot_general(a, b, (_DOT_DIMS[mode], ((), ())), preferred_element_type=F32)


def _col_chunks(cols):
    return [(c0, min(c0 + DOT_COLS, cols)) for c0 in range(0, cols, DOT_COLS)]


def _matmul(a, b, mode, out_dtype, name, *, res=None, tm=1024, tn=1024):
    if mode == "tn":
        kdim, mdim = a.shape
    else:
        mdim, kdim = a.shape
    ndim = b.shape[0] if mode == "nt" else b.shape[1]
    tm, tn = min(tm, mdim), min(tn, ndim)
    assert mdim % tm == 0 and ndim % tn == 0, (name, mdim, ndim)

    def body(*refs):
        out = _dot(refs[0][...], refs[1][...], mode)
        if res is not None:
            out = out + refs[2][...]
        refs[-1][...] = out.astype(out_dtype)

    a_spec = pl.BlockSpec((kdim, tm), lambda i, j: (0, i)) if mode == "tn" else pl.BlockSpec((tm, kdim), lambda i, j: (i, 0))
    b_spec = pl.BlockSpec((tn, kdim), lambda i, j: (j, 0)) if mode == "nt" else pl.BlockSpec((kdim, tn), lambda i, j: (0, j))
    o_spec = pl.BlockSpec((tm, tn), lambda i, j: (i, j))
    in_specs, args = [a_spec, b_spec], [a, b]
    if res is not None:
        in_specs.append(o_spec)
        args.append(res)
    return pl.pallas_call(
        body, name=name, grid=(mdim // tm, ndim // tn), in_specs=in_specs, out_specs=o_spec,
        out_shape=_sds((mdim, ndim), out_dtype), compiler_params=_params("parallel", "parallel"),
    )(*args)


def _project(h, w_blocks, block_ids, name, *, proj=None, after=None, tm=1024, tk=1024):
    s, d = h.shape
    _, _, cw = w_blocks.shape
    n = block_ids.shape[0]
    tm, tk = min(tm, s), min(tk, d)
    nk = d // tk

    def body(ids_ref, h_ref, w_ref, *rest):
        o_ref, acc_ref = rest[-2], rest[-1]
        k = pl.program_id(2)

        @pl.when(k == 0)
        def _():
            acc_ref[...] = jnp.zeros_like(acc_ref)

        for c0, c1 in _col_chunks(cw):
            acc_ref[:, c0:c1] += _dot(h_ref[...], w_ref[:, c0:c1], "nn")

        @pl.when(k == nk - 1)
        def _():
            o_ref[...] = acc_ref[...].astype(BF16)

    in_specs = [pl.BlockSpec((tm, tk), lambda j, i, k, ids: (i, k)),
                pl.BlockSpec((None, tk, cw), lambda j, i, k, ids: (ids[j], k, 0))]
    args = [block_ids, h, w_blocks]
    aliases = {}
    if proj is not None:
        in_specs.append(ANY_SPEC)
        args.append(proj)
        aliases = {3: 0}
    if after is not None:
        in_specs.append(ANY_SPEC)
        args.append(after)
    return pl.pallas_call(
        body, name=name,
        grid_spec=pltpu.PrefetchScalarGridSpec(
            num_scalar_prefetch=1, grid=(n, s // tm, d // tk), in_specs=in_specs,
            out_specs=pl.BlockSpec((tm, cw), lambda j, i, k, ids: (i, ids[j])),
            scratch_shapes=[pltpu.VMEM((tm, cw), F32)]),
        out_shape=_sds((s, N_DEV * cw), BF16), input_output_aliases=aliases,
        compiler_params=_params("arbitrary", "arbitrary", "arbitrary", vmem=VMEM_LIMIT_WIDE),
    )(*args)


def _grad_w_in_blocks(h, d_proj, block_ids, cw, name, *, slots=None, slot0=0, prev=None, init=None, after=None,
                      tm=1024, tk=1024):
    s, d = h.shape
    n = block_ids.shape[0]
    slots = n if slots is None else slots
    tm, tk = min(tm, d), min(tk, s)
    nk = s // tk

    def body(ids_ref, h_ref, g_ref, *rest):
        o_ref, acc_ref = rest[-2], rest[-1]
        k = pl.program_id(2)

        @pl.when(k == 0)
        def _():
            acc_ref[...] = jnp.zeros_like(acc_ref) if init is None else rest[0][...].astype(F32)

        for c0, c1 in _col_chunks(cw):
            acc_ref[:, c0:c1] += _dot(h_ref[...], g_ref[:, c0:c1], "tn")

        @pl.when(k == nk - 1)
        def _():
            o_ref[...] = acc_ref[...].astype(BF16)

    in_specs = [pl.BlockSpec((tk, tm), lambda q, i, k, ids: (k, i)),
                pl.BlockSpec((tk, cw), lambda q, i, k, ids: (k, ids[q]))]
    args = [block_ids, h, d_proj]
    aliases = {}
    if init is not None:
        in_specs.append(pl.BlockSpec((None, tm, cw), lambda q, i, k, ids: (q, i, 0)))
        args.append(init)
    if prev is not None:
        aliases = {len(args): 0}
        in_specs.append(ANY_SPEC)
        args.append(prev)
    if after is not None:
        in_specs.append(ANY_SPEC)
        args.append(after)
    return pl.pallas_call(
        body, name=name,
        grid_spec=pltpu.PrefetchScalarGridSpec(
            num_scalar_prefetch=1, grid=(n, d // tm, nk), in_specs=in_specs,
            out_specs=pl.BlockSpec((None, tm, cw), lambda q, i, k, ids: (slot0 + q, i, 0)),
            scratch_shapes=[pltpu.VMEM((tm, cw), F32)]),
        out_shape=_sds((slots, d, cw), BF16), input_output_aliases=aliases,
        compiler_params=_params("parallel", "parallel", "arbitrary", vmem=VMEM_LIMIT_WIDE),
    )(*args)


def _d_hidden(d_proj, w_blocks, after=None, *, tm=1024, tn=1024):
    s = d_proj.shape[0]
    nb, d, cw = w_blocks.shape
    tm, tn = min(tm, s), min(tn, d)

    def body(g_ref, w_ref, *rest):
        o_ref = rest[-1]
        k = pl.program_id(2)

        @pl.when(k == 0)
        def _():
            o_ref[...] = jnp.zeros_like(o_ref)

        o_ref[...] += _dot(g_ref[...], w_ref[...], "nt")

    in_specs = [pl.BlockSpec((tm, cw), lambda i, j, k: (i, k)),
                pl.BlockSpec((None, tn, cw), lambda i, j, k: (k, j, 0))]
    args = [d_proj, w_blocks]
    if after is not None:
        in_specs.append(ANY_SPEC)
        args.append(after)
    return pl.pallas_call(
        body, name="d_h", grid=(s // tm, d // tn, nb), in_specs=in_specs,
        out_specs=pl.BlockSpec((tm, tn), lambda i, j, k: (i, j)), out_shape=_sds((s, d), F32),
        compiler_params=_params("parallel", "parallel", "arbitrary", vmem=VMEM_LIMIT_WIDE),
    )(*args)


def _row_tile(rows):
    return min(rows, 128)


def _segment_specs(rows, d, col0):
    w = d // SEGMENT_SPLIT
    assert col0 % w == 0
    return [pl.BlockSpec((rows, w), lambda i, t=t: (i, col0 // w + t)) for t in range(SEGMENT_SPLIT)]


def _cat(refs):
    return jnp.concatenate([r[...].astype(F32) for r in refs], axis=1)


def _rmsnorm_fwd(x, g, after):
    s, d = x.shape
    tr = _row_tile(s)

    def body(x_ref, g_ref, after_ref, h_ref):
        xv = x_ref[...]
        r = lax.rsqrt(jnp.mean(xv * xv, axis=-1, keepdims=True) + NORM_EPS)
        h_ref[...] = (xv * r * g_ref[...]).astype(BF16)

    row = pl.BlockSpec((tr, d), lambda i: (i, 0))
    vec = pl.BlockSpec((1, d), lambda i: (0, 0))
    return pl.pallas_call(body, name="rmsnorm_fwd", grid=(s // tr,), in_specs=[row, vec, ANY_SPEC], out_specs=row,
                          out_shape=_sds((s, d), BF16), compiler_params=_params("parallel"))(x, g, after)


def _merge_fwd(y_a, y_b, proj, col_m):
    s, d = y_a.shape
    tr = _row_tile(s)
    ns = SEGMENT_SPLIT

    def body(ya_ref, yb_ref, *rest):
        ma, mb, o_ref = _cat(rest[:ns]), _cat(rest[ns:2 * ns]), rest[2 * ns]
        o_ref[...] = (jax.nn.sigmoid(ma) * ya_ref[...].astype(F32)
                      + jax.nn.sigmoid(mb) * yb_ref[...].astype(F32)).astype(BF16)

    row = pl.BlockSpec((tr, d), lambda i: (i, 0))
    return pl.pallas_call(
        body, name="merge_fwd", grid=(s // tr,),
        in_specs=[row, row, *_segment_specs(tr, d, col_m), *_segment_specs(tr, d, col_m + d)],
        out_specs=row, out_shape=_sds((s, d), BF16), compiler_params=_params("parallel"),
    )(y_a, y_b, *([proj] * (2 * ns)))


def _loss_and_final_norm_bwd(x2, target, g):
    s, d = x2.shape
    tr = _row_tile(s)

    def body(x_ref, t_ref, g_ref, loss_ref, dg_ref, dx_ref, dxb_ref):
        @pl.when(pl.program_id(0) == 0)
        def _():
            loss_ref[...] = jnp.zeros_like(loss_ref)
            dg_ref[...] = jnp.zeros_like(dg_ref)

        xv, gv = x_ref[...], g_ref[...]
        r = lax.rsqrt(jnp.mean(xv * xv, axis=-1, keepdims=True) + NORM_EPS)
        xhat = xv * r
        err = xhat * gv - t_ref[...]
        loss_ref[...] += 0.5 * jnp.sum(jnp.mean(err * err, axis=-1, keepdims=True))
        dy = err / d
        dg_ref[...] += jnp.sum(dy * xhat, axis=0, keepdims=True)
        dyg = dy * gv
        dx = r * (dyg - xhat * jnp.mean(dyg * xhat, axis=-1, keepdims=True))
        dx_ref[...] = dx
        dxb_ref[...] = dx.astype(BF16)

    row = pl.BlockSpec((tr, d), lambda i: (i, 0))
    vec = pl.BlockSpec((1, d), lambda i: (0, 0))
    return pl.pallas_call(
        body, name="loss_final_norm_bwd", grid=(s // tr,), in_specs=[row, row, vec],
        out_specs=(pl.BlockSpec((SUBLANES, LANES), lambda i: (0, 0)), vec, row, row),
        out_shape=(_sds((SUBLANES, LANES), F32), _sds((1, d), F32), _sds((s, d), F32), _sds((s, d), BF16)),
        compiler_params=_params("arbitrary"))(x2, target, g)


def _write_behind(step, n_steps, buf, sems, wide_ref, rows, col0, fill):
    cols = buf.shape[2]
    slot = step % 2

    def copy(at_step, at_slot):
        dst = wide_ref.at[pl.ds(pl.multiple_of(at_step * rows, rows), rows), pl.ds(col0, cols)]
        return pltpu.make_async_copy(buf.at[at_slot], dst, sems.at[at_slot])

    @pl.when(step >= 2)
    def _():
        copy(step - 2, slot).wait()

    fill(buf.at[slot])
    copy(step, slot).start()

    @pl.when(step == n_steps - 1)
    def _():
        copy(step, slot).wait()
        if n_steps > 1:
            copy(step - 1, 1 - slot).wait()


def _merge_bwd(d_merged, y_a, y_b, proj, col_m, d_proj):
    s, d = y_a.shape
    tr = _row_tile(s)
    ns = SEGMENT_SPLIT
    n_steps = s // tr

    def body(dm_ref, ya_ref, yb_ref, *rest):
        ma, mb = _cat(rest[:ns]), _cat(rest[ns:2 * ns])
        dya_ref, dyb_ref, wide_ref, buf, sems = rest[2 * ns + 1:]
        dm = dm_ref[...].astype(F32)
        sa = jax.nn.sigmoid(ma)
        sb = jax.nn.sigmoid(mb)
        dya_ref[...] = (dm * sa).astype(BF16)
        dyb_ref[...] = (dm * sb).astype(BF16)

        def fill(out):
            out[:, :d] = (dm * ya_ref[...].astype(F32) * (sa * (1.0 - sa))).astype(BF16)
            out[:, d:] = (dm * yb_ref[...].astype(F32) * (sb * (1.0 - sb))).astype(BF16)

        _write_behind(pl.program_id(0), n_steps, buf, sems, wide_ref, tr, col_m, fill)

    row = pl.BlockSpec((tr, d), lambda i: (i, 0))
    n_in = 3 + 2 * ns
    return pl.pallas_call(
        body, name="merge_bwd", grid=(n_steps,),
        in_specs=[row, row, row, *_segment_specs(tr, d, col_m), *_segment_specs(tr, d, col_m + d), ANY_SPEC],
        out_specs=(row, row, ANY_SPEC),
        out_shape=(_sds((s, d), BF16), _sds((s, d), BF16), _sds(d_proj.shape, BF16)),
        input_output_aliases={n_in: 2},
        scratch_shapes=[pltpu.VMEM((2, tr, 2 * d), BF16), pltpu.SemaphoreType.DMA((2,))],
        compiler_params=_params("arbitrary"))(d_merged, y_a, y_b, *([proj] * (2 * ns)), d_proj)


def _place(d_proj, piece, col0, name):
    s, w = piece.shape
    bw = math.gcd(w, col0) if col0 else w
    tr = _row_tile(s)

    def body(p_ref, wide_in, o_ref):
        o_ref[...] = p_ref[...]

    return pl.pallas_call(
        body, name=name, grid=(s // tr, w // bw),
        in_specs=[pl.BlockSpec((tr, bw), lambda i, j: (i, j)), ANY_SPEC],
        out_specs=pl.BlockSpec((tr, bw), lambda i, j: (i, col0 // bw + j)),
        out_shape=_sds(d_proj.shape, d_proj.dtype), input_output_aliases={1: 0},
        compiler_params=_params("parallel", "parallel"))(piece, d_proj)


def _input_grad(d_h, x, g, dx2):
    s, d = x.shape
    tr = _row_tile(s)

    def body(dh_ref, x_ref, g_ref, dx2_ref, gx_ref, dg_ref):
        @pl.when(pl.program_id(0) == 0)
        def _():
            dg_ref[...] = jnp.zeros_like(dg_ref)

        xv, dh = x_ref[...], dh_ref[...]
        r = lax.rsqrt(jnp.mean(xv * xv, axis=-1, keepdims=True) + NORM_EPS)
        xhat = xv * r
        dg_ref[...] += jnp.sum(dh * xhat, axis=0, keepdims=True)
        dyg = dh * g_ref[...]
        gx_ref[...] = dx2_ref[...] + r * (dyg - xhat * jnp.mean(dyg * xhat, axis=-1, keepdims=True))

    row = pl.BlockSpec((tr, d), lambda i: (i, 0))
    vec = pl.BlockSpec((1, d), lambda i: (0, 0))
    return pl.pallas_call(
        body, name="input_grad", grid=(s // tr,), in_specs=[row, row, vec, row], out_specs=(row, vec),
        out_shape=(_sds((s, d), F32), _sds((1, d), F32)), compiler_params=_params("arbitrary"))(d_h, x, g, dx2)


def _rope_tables(positions):
    inv_freq = ROPE_THETA ** (-jnp.arange(ROPE_HALF, dtype=F32) * 2.0 / ROPE_DIM)
    ang = positions.astype(F32)[:, None] * inv_freq
    cos, sin = jnp.cos(ang), jnp.sin(ang)
    zero = jnp.zeros((positions.shape[0], HEAD_DIM - ROPE_DIM), F32)
    zero_h = jnp.zeros_like(sin)
    c = jnp.concatenate([cos, cos, zero + 1.0], axis=1)
    up = jnp.concatenate([-sin, zero_h, zero], axis=1)
    down = jnp.concatenate([zero_h, sin, zero], axis=1)
    reps = LANES // HEAD_DIM
    return jnp.stack([jnp.tile(c, (1, reps)), jnp.tile(up, (1, reps)), jnp.tile(down, (1, reps))])


def _lane_tiles(x):
    return [x[:, t * LANES:(t + 1) * LANES] for t in range(x.shape[1] // LANES)]


def _rope(x, tab):
    out = [xt * tab[0] + pltpu.roll(xt, LANES - ROPE_HALF, 1) * tab[1] + pltpu.roll(xt, ROPE_HALF, 1) * tab[2]
           for xt in _lane_tiles(x)]
    return out[0] if len(out) == 1 else jnp.concatenate(out, axis=1)


def _rope_bwd(g, tab):
    out = [gt * tab[0] + pltpu.roll(gt * tab[1], ROPE_HALF, 1) + pltpu.roll(gt * tab[2], LANES - ROPE_HALF, 1)
           for gt in _lane_tiles(g)]
    return out[0] if len(out) == 1 else jnp.concatenate(out, axis=1)


def _head(x, h):
    return x[:, h * HEAD_DIM:(h + 1) * HEAD_DIM]


def _stack_heads(x, first, count):
    return jnp.concatenate([_head(x, first + h) for h in range(count)], axis=0)


def _dot_nt(a, b):
    return lax.dot_general(a, b, (((1,), (1,)), ((), ())), preferred_element_type=F32)


def _causal(rows):
    qi = lax.broadcasted_iota(jnp.int32, (rows, BLOCK), 0) % BLOCK
    return lax.broadcasted_iota(jnp.int32, (rows, BLOCK), 1) <= qi


def _band_probs(qs, k_prev, k_cur, sink, causal, blk):
    s_prev = jnp.where(blk > 0, _dot_nt(qs, k_prev), -jnp.inf)
    s = jnp.where(causal, _dot_nt(qs, k_cur), s_prev)
    m = jnp.maximum(jnp.max(s, axis=-1, keepdims=True), sink)
    p = jnp.exp(s - m)
    p_sink = jnp.exp(sink - m)
    inv = 1.0 / (jnp.sum(p, axis=-1, keepdims=True) + p_sink)
    return p * inv, p_sink * inv


def _band_probs_by_head(qs, k_band, sink_ref, first, count, causal, blk):
    s_band = _dot_nt(qs, k_band)
    p_all, p_band, p_sink = [], [], []
    for h in range(count):
        rows = slice(h * BLOCK, (h + 1) * BLOCK)
        sink = sink_ref[first + h]
        s = jnp.where(causal, s_band[rows, :BLOCK], jnp.where(blk > 0, s_band[rows, BLOCK:], -jnp.inf))
        m = jnp.maximum(jnp.max(s, axis=-1, keepdims=True), sink)
        p = jnp.exp(s - m)
        ps = jnp.exp(sink - m)
        inv = 1.0 / (jnp.sum(p, axis=-1, keepdims=True) + ps)
        p = p * inv
        p_all.append(p)
        p_sink.append(ps * inv)
        p_band.append(_split_band(p, causal))
    cat = lambda parts: jnp.concatenate(parts, axis=0)
    return cat([c for c, _ in p_band]), cat([v for _, v in p_band]), cat(p_all), cat(p_sink)


def _sink_column(sink_ref, first, count):
    return jnp.concatenate([jnp.full((BLOCK, 1), sink_ref[first + h], F32) for h in range(count)], axis=0)


def _split_band(x, causal):
    return jnp.where(causal, x, 0.0).astype(BF16), jnp.where(causal, 0.0, x).astype(BF16)


def _attn_dims(s, d, kv):
    n_kv = kv // HEAD_DIM
    group = d // kv
    qw = 2 * group * HEAD_DIM
    assert n_kv % 2 == 0 and (d + 2 * kv) % qw == 0 and s % BLOCK == 0
    return group, qw, n_kv // 2, s // BLOCK


def _attention_fwd(proj, tables, sink, kv, after):
    s, d = proj.shape[0], sink.shape[0] * HEAD_DIM
    group, qw, n_pairs, nb = _attn_dims(s, d, kv)

    def body(sink_ref, q_ref, kc_ref, kp_ref, vc_ref, vp_ref, ga_ref, tc_ref, tp_ref, after_ref,
             attn_ref, ain_ref, qrot_ref, krot_ref):
        pair, blk = pl.program_id(0), pl.program_id(1)
        tab_c, tab_p = tc_ref[...], tp_ref[...]
        q = _rope(q_ref[...].astype(F32), tab_c) * ATTN_SCALE
        k_cur, k_prev = _rope(kc_ref[...].astype(F32), tab_c), _rope(kp_ref[...].astype(F32), tab_p)
        qrot_ref[...] = q.astype(BF16)
        krot_ref[...] = k_cur.astype(BF16)
        v_cur, v_prev = vc_ref[...], vp_ref[...]
        causal = _causal(BLOCK)
        outs = []
        for a in range(2):
            k_band = jnp.concatenate([_head(k_cur, a), _head(k_prev, a)], axis=0).astype(BF16)
            vc, vp = _head(v_cur, a).astype(BF16), _head(v_prev, a).astype(BF16)
            qs = _stack_heads(q, a * group, group).astype(BF16)
            p_cur, p_prev, _, _ = _band_probs_by_head(qs, k_band, sink_ref, (2 * pair + a) * group, group, causal, blk)
            o = jnp.dot(p_cur, vc, preferred_element_type=F32) + jnp.dot(p_prev, vp, preferred_element_type=F32)
            outs += [o[h * BLOCK:(h + 1) * BLOCK] for h in range(group)]
        attn = jnp.concatenate(outs, axis=1)
        attn_ref[...] = attn.astype(BF16)
        silu, _ = _silu_and_grad(ga_ref[...].astype(F32))
        ain_ref[...] = (attn * silu).astype(BF16)

    k0, v0, g0 = d // LANES, (d + kv) // LANES, (d + 2 * kv) // qw
    prev = lambda i: jnp.maximum(i - 1, 0)
    in_specs = [
        SMEM_SPEC,
        pl.BlockSpec((BLOCK, qw), lambda p, i: (i, p)),
        pl.BlockSpec((BLOCK, LANES), lambda p, i: (i, k0 + p)),
        pl.BlockSpec((BLOCK, LANES), lambda p, i: (prev(i), k0 + p)),
        pl.BlockSpec((BLOCK, LANES), lambda p, i: (i, v0 + p)),
        pl.BlockSpec((BLOCK, LANES), lambda p, i: (prev(i), v0 + p)),
        pl.BlockSpec((BLOCK, qw), lambda p, i: (i, g0 + p)),
        pl.BlockSpec((3, BLOCK, LANES), lambda p, i: (0, i, 0)),
        pl.BlockSpec((3, BLOCK, LANES), lambda p, i: (0, prev(i), 0)),
        ANY_SPEC,
    ]
    out = pl.BlockSpec((BLOCK, qw), lambda p, i: (i, p))
    k_out = pl.BlockSpec((BLOCK, LANES), lambda p, i: (i, p))
    return pl.pallas_call(
        body, name="attention_fwd", grid=(n_pairs, nb), in_specs=in_specs, out_specs=(out, out, out, k_out),
        out_shape=(_sds((s, d), BF16), _sds((s, d), BF16), _sds((s, d), BF16), _sds((s, kv), BF16)),
        compiler_params=_params("parallel", "parallel"),
    )(sink, proj, proj, proj, proj, proj, proj, tables, tables, after)


def _attention_bwd(proj, q_rot, k_rot, tables, sink, kv, attn, d_ain, after):
    s, d = proj.shape[0], sink.shape[0] * HEAD_DIM
    group, qw, n_pairs, nb = _attn_dims(s, d, kv)

    def body(sink_ref, q_ref, kc_ref, kp_ref, vc_ref, vp_ref, ga_ref, tc_ref, tp_ref, attn_ref, dain_ref, after_ref,
             dq_ref, dk_ref, dv_ref, dga_ref, dsink_ref, carry_k, carry_v, sink_acc):
        pair, blk = pl.program_id(0), pl.program_id(1)

        @pl.when(blk == 0)
        def _():
            carry_k[...] = jnp.zeros_like(carry_k)
            carry_v[...] = jnp.zeros_like(carry_v)
            sink_acc[...] = jnp.zeros_like(sink_acc)

        @pl.when(blk < nb)
        def _():
            tab_c, tab_p = tc_ref[...], tp_ref[...]
            q = q_ref[...].astype(F32)
            k_cur, k_prev = kc_ref[...].astype(F32), kp_ref[...].astype(F32)
            v_cur, v_prev = vc_ref[...], vp_ref[...]
            silu, silu_grad = _silu_and_grad(ga_ref[...].astype(F32))
            d_ain_v = dain_ref[...].astype(F32)
            dga_ref[...] = (d_ain_v * attn_ref[...].astype(F32) * silu_grad).astype(BF16)
            d_attn = d_ain_v * silu
            q_t = q.T
            d_attn_t = d_attn.T
            causal = _causal(BLOCK)
            dq_parts = []
            dk_t = {"cur": [], "prev": []}
            dv_t = {"cur": [], "prev": []}
            lane = lax.broadcasted_iota(jnp.int32, (BLOCK, LANES), 1)
            dsink = jnp.zeros((BLOCK, LANES), F32)
            for a in range(2):
                first = a * group
                qs = _stack_heads(q, first, group).astype(BF16)
                kc, kp = _head(k_cur, a).astype(BF16), _head(k_prev, a).astype(BF16)
                k_band = jnp.concatenate([_head(k_cur, a), _head(k_prev, a)], axis=0).astype(BF16)
                v_band = jnp.concatenate([_head(v_cur, a), _head(v_prev, a)], axis=0).astype(BF16)
                p_cur, p_prev, p, p_sink = _band_probs_by_head(qs, k_band, sink_ref, (2 * pair + a) * group, group,
                                                               causal, blk)
                do = _stack_heads(d_attn, first, group).astype(BF16)
                dp_band = _dot_nt(do, v_band)
                ds_parts, delta = [], []
                for h in range(group):
                    rows = slice(h * BLOCK, (h + 1) * BLOCK)
                    dp = jnp.where(causal, dp_band[rows, :BLOCK], dp_band[rows, BLOCK:])
                    delta.append(jnp.sum(p[rows] * dp, axis=-1, keepdims=True))
                    ds_parts.append(_split_band(p[rows] * (dp - delta[-1]), causal))
                ds_cur = jnp.concatenate([c for c, _ in ds_parts], axis=0)
                ds_prev = jnp.concatenate([v for _, v in ds_parts], axis=0)
                delta = jnp.concatenate(delta, axis=0)
                dqs = (jnp.dot(ds_cur, kc, preferred_element_type=F32)
                       + jnp.dot(ds_prev, kp, preferred_element_type=F32)) * ATTN_SCALE
                dq_parts += [dqs[h * BLOCK:(h + 1) * BLOCK] for h in range(group)]
                rows = lambda t: jnp.concatenate(
                    [t[(first + h) * HEAD_DIM:(first + h + 1) * HEAD_DIM] for h in range(group)], axis=1).astype(BF16)
                qs_t, do_t = rows(q_t), rows(d_attn_t)
                dk_t["cur"].append(jnp.dot(qs_t, ds_cur, preferred_element_type=F32))
                dk_t["prev"].append(jnp.dot(qs_t, ds_prev, preferred_element_type=F32))
                dv_t["cur"].append(jnp.dot(do_t, p_cur, preferred_element_type=F32))
                dv_t["prev"].append(jnp.dot(do_t, p_prev, preferred_element_type=F32))
                ds_sink = -(p_sink * delta)
                for h in range(group):
                    dsink = dsink + jnp.where(lane == first + h, ds_sink[h * BLOCK:(h + 1) * BLOCK], 0.0)
            sink_acc[...] += dsink
            dq_ref[...] = _rope_bwd(jnp.concatenate(dq_parts, axis=1), tab_c).astype(BF16)
            pair_block = lambda parts: jnp.concatenate(parts, axis=0).T
            dk_ref[...] = (carry_k[...] + _rope_bwd(pair_block(dk_t["prev"]), tab_p)).astype(BF16)
            dv_ref[...] = (carry_v[...] + pair_block(dv_t["prev"])).astype(BF16)
            carry_k[...] = _rope_bwd(pair_block(dk_t["cur"]), tab_c)
            carry_v[...] = pair_block(dv_t["cur"])

        @pl.when(blk == nb)
        def _():
            dk_ref[...] = carry_k[...].astype(BF16)
            dv_ref[...] = carry_v[...].astype(BF16)
            dsink_ref[0] = jnp.sum(sink_acc[...], axis=0, keepdims=True)

    v0, g0 = (d + kv) // LANES, (d + 2 * kv) // qw
    cur = lambda i: jnp.minimum(i, nb - 1)
    prev = lambda i: jnp.maximum(cur(i) - 1, 0)
    back = lambda i: jnp.maximum(i - 1, 0)
    q_spec = pl.BlockSpec((BLOCK, qw), lambda p, i: (cur(i), p))
    in_specs = [
        SMEM_SPEC,
        q_spec,
        pl.BlockSpec((BLOCK, LANES), lambda p, i: (cur(i), p)),
        pl.BlockSpec((BLOCK, LANES), lambda p, i: (prev(i), p)),
        pl.BlockSpec((BLOCK, LANES), lambda p, i: (cur(i), v0 + p)),
        pl.BlockSpec((BLOCK, LANES), lambda p, i: (prev(i), v0 + p)),
        pl.BlockSpec((BLOCK, qw), lambda p, i: (cur(i), g0 + p)),
        pl.BlockSpec((3, BLOCK, LANES), lambda p, i: (0, cur(i), 0)),
        pl.BlockSpec((3, BLOCK, LANES), lambda p, i: (0, prev(i), 0)),
        q_spec,
        q_spec,
        ANY_SPEC,
    ]
    kv_out = pl.BlockSpec((BLOCK, LANES), lambda p, i: (back(i), p))
    return pl.pallas_call(
        body, name="attention_bwd", grid=(n_pairs, nb + 1), in_specs=in_specs,
        out_specs=(q_spec, kv_out, kv_out, q_spec, pl.BlockSpec((1, 1, LANES), lambda p, i: (p, 0, 0))),
        out_shape=(_sds((s, d), BF16), _sds((s, kv), BF16), _sds((s, kv), BF16), _sds((s, d), BF16),
                   _sds((n_pairs, 1, LANES), F32)),
        scratch_shapes=[pltpu.VMEM((BLOCK, LANES), F32), pltpu.VMEM((BLOCK, LANES), F32),
                        pltpu.VMEM((BLOCK, LANES), F32)],
        compiler_params=_params("parallel", "arbitrary"),
    )(sink, q_rot, k_rot, k_rot, proj, proj, proj, tables, tables, attn, d_ain, after)


def _gmlp_core(gu, gv, ln_g, ln_b, w_ref, bias_t):
    xc = gv - jnp.mean(gv, axis=-1, keepdims=True)
    rstd = lax.rsqrt(jnp.mean(xc * xc, axis=-1, keepdims=True) + LN_EPS)
    xhat = xc * rstd
    vn = (xhat * ln_g + ln_b).astype(BF16)
    gd = gu.shape[1] // GMLP_GROUPS
    tri = (lax.broadcasted_iota(jnp.int32, (BLOCK, BLOCK), 0) >= lax.broadcasted_iota(jnp.int32, (BLOCK, BLOCK), 1))
    w_tri = [jnp.where(tri, w_ref[g], 0.0).astype(BF16) for g in range(GMLP_GROUPS)]
    mixed = jnp.concatenate(
        [jnp.dot(w_tri[g], vn[:, g * gd:(g + 1) * gd], preferred_element_type=F32) + bias_t[:, g:g + 1]
         for g in range(GMLP_GROUPS)], axis=1)
    return gu, xhat, rstd, vn, w_tri, tri, mixed


def _whole(shape):
    return pl.BlockSpec(shape, lambda i: tuple(0 for _ in shape))


def _gmlp_fwd(proj, col_u, d, w_s, bias_t, ln_g, ln_b, after):
    s = proj.shape[0]
    ns = SEGMENT_SPLIT

    def body(*refs):
        u, vg, gb = _cat(refs[:ns]), _cat(refs[ns:2 * ns]), _cat(refs[2 * ns:3 * ns])
        w_ref, bt_ref, lg_ref, lb_ref, after_ref, o_ref = refs[3 * ns:]
        gu, _, _, _, _, _, mixed = _gmlp_core(_gelu(u), _gelu(vg), lg_ref[...], lb_ref[...], w_ref, bt_ref[...])
        silu, _ = _silu_and_grad(gb)
        o_ref[...] = ((gu * mixed) * silu).astype(BF16)

    segs = [sp for j in range(3) for sp in _segment_specs(BLOCK, d, col_u + j * d)]
    return pl.pallas_call(
        body, name="gmlp_fwd", grid=(s // BLOCK,),
        in_specs=[*segs, _whole(w_s.shape), _whole(bias_t.shape), _whole((1, d)), _whole((1, d)), ANY_SPEC],
        out_specs=pl.BlockSpec((BLOCK, d), lambda i: (i, 0)), out_shape=_sds((s, d), BF16),
        compiler_params=_params("parallel"),
    )(*([proj] * (3 * ns)), w_s, bias_t, ln_g, ln_b, after)


def _gmlp_bwd(proj, col_u, d, w_s, bias_t, ln_g, ln_b, d_bin, after, d_proj):
    s = proj.shape[0]
    gd = d // GMLP_GROUPS
    ns = SEGMENT_SPLIT
    n_steps = s // BLOCK

    def body(*refs):
        u, vg, gb = _cat(refs[:ns]), _cat(refs[ns:2 * ns]), _cat(refs[2 * ns:3 * ns])
        (w_ref, bt_ref, lg_ref, lb_ref, dbin_ref, after_ref, wide_in, wide_ref, dw_ref, dbt_ref, dlg_ref, dlb_ref,
         buf, sems) = refs[3 * ns:]

        @pl.when(pl.program_id(0) == 0)
        def _():
            dw_ref[...] = jnp.zeros_like(dw_ref)
            dbt_ref[...] = jnp.zeros_like(dbt_ref)
            dlg_ref[...] = jnp.zeros_like(dlg_ref)
            dlb_ref[...] = jnp.zeros_like(dlb_ref)

        ln_g = lg_ref[...]
        (gu, gu_grad), (gv, gv_grad) = _gelu_and_grad(u), _gelu_and_grad(vg)
        gu, xhat, rstd, vn, w_tri, tri, mixed = _gmlp_core(gu, gv, ln_g, lb_ref[...], w_ref, bt_ref[...])
        silu, silu_grad = _silu_and_grad(gb)
        d_bin_v = dbin_ref[...].astype(F32)
        d_sg = d_bin_v * silu
        d_gate = (d_bin_v * (gu * mixed) * silu_grad).astype(BF16)
        d_u = (d_sg * mixed * gu_grad).astype(BF16)
        d_mixed = d_sg * gu
        d_mixed_b = d_mixed.astype(BF16)
        d_vn, d_bias = [], []
        for g in range(GMLP_GROUPS):
            dm_g = d_mixed_b[:, g * gd:(g + 1) * gd]
            d_bias.append(jnp.sum(d_mixed[:, g * gd:(g + 1) * gd], axis=-1, keepdims=True))
            dw = lax.dot_general(dm_g, vn[:, g * gd:(g + 1) * gd], (((1,), (1,)), ((), ())),
                                 preferred_element_type=F32)
            dw_ref[g] += jnp.where(tri, dw, 0.0)
            d_vn.append(lax.dot_general(w_tri[g], dm_g, (((0,), (0,)), ((), ())), preferred_element_type=F32))
        dbt_ref[...] += jnp.concatenate(d_bias, axis=1)
        d_vn = jnp.concatenate(d_vn, axis=1)
        dlg_ref[...] += jnp.sum(d_vn * xhat, axis=0, keepdims=True)
        dlb_ref[...] += jnp.sum(d_vn, axis=0, keepdims=True)
        d_xhat = d_vn * ln_g
        d_gv = rstd * (d_xhat - jnp.mean(d_xhat, axis=-1, keepdims=True)
                       - xhat * jnp.mean(d_xhat * xhat, axis=-1, keepdims=True))
        d_v = (d_gv * gv_grad).astype(BF16)

        def fill(out):
            out[:, :d] = d_u
            out[:, d:2 * d] = d_v
            out[:, 2 * d:] = d_gate

        _write_behind(pl.program_id(0), n_steps, buf, sems, wide_ref, BLOCK, col_u, fill)

    segs = [sp for j in range(3) for sp in _segment_specs(BLOCK, d, col_u + j * d)]
    return pl.pallas_call(
        body, name="gmlp_bwd", grid=(n_steps,),
        in_specs=[*segs, _whole(w_s.shape), _whole(bias_t.shape), _whole((1, d)), _whole((1, d)),
                  pl.BlockSpec((BLOCK, d), lambda i: (i, 0)), ANY_SPEC, ANY_SPEC],
        out_specs=(ANY_SPEC, _whole(w_s.shape), _whole(bias_t.shape), _whole((1, d)), _whole((1, d))),
        out_shape=(_sds(d_proj.shape, BF16), _sds(w_s.shape, F32), _sds(bias_t.shape, F32), _sds((1, d), F32),
                   _sds((1, d), F32)),
        input_output_aliases={3 * ns + 6: 0},
        scratch_shapes=[pltpu.VMEM((2, BLOCK, 3 * d), BF16), pltpu.SemaphoreType.DMA((2,))],
        compiler_params=_params("arbitrary"),
    )(*([proj] * (3 * ns)), w_s, bias_t, ln_g, ln_b, d_bin, after, d_proj)


def _pack(parts):
    rows = []
    tile = SUBLANES * LANES
    for p in parts:
        flat = p.astype(F32).reshape(-1)
        padded = -(-flat.shape[0] // tile) * tile
        rows.append(jnp.pad(flat, (0, padded - flat.shape[0])).reshape(-1, LANES))
    return jnp.concatenate(rows, axis=0)


def _unpack(packed, shapes):
    out, row = [], 0
    tile = SUBLANES * LANES
    for shape in shapes:
        size = math.prod(shape)
        n_rows = -(-size // tile) * SUBLANES
        out.append(packed[row:row + n_rows].reshape(-1)[:size].reshape(shape))
        row += n_rows
    return out


def kernel(x, positions, norm_g, w_in, attn_sink, gmlp_ln_g, gmlp_ln_b, w_spatial, b_spatial, w_up_attn, w_up_gmlp, w_out, final_norm_g, loss_target, m_norm_g, m_w_in, m_attn_sink, m_gmlp_ln_g, m_gmlp_ln_b, m_w_spatial, m_b_spatial, m_w_up_attn, m_w_up_gmlp, m_w_out, m_final_norm_g, v_norm_g, v_w_in, v_attn_sink, v_gmlp_ln_g, v_gmlp_ln_b, v_w_spatial, v_b_spatial, v_w_up_attn, v_w_up_gmlp, v_w_out, v_final_norm_g):
    x2d, target = x[0], loss_target[0]
    s, d = x2d.shape
    n_q_heads = attn_sink.shape[1]
    cw = w_in.shape[2]
    rw = w_up_attn.shape[1]
    kv = (cw * N_DEV - 7 * d) // 2
    col_u, col_m = 2 * d + 2 * kv, 5 * d + 2 * kv
    final_g = final_norm_g.reshape(1, d)
    sink = attn_sink[0]
    w_s = w_spatial[0]
    bias_t = b_spatial[0].T
    mx, my, mc = _mesh_pos()
    pos = jnp.stack([mx, my, mc]).astype(jnp.int32)
    chips = jnp.arange(N_CHIPS, dtype=jnp.int32)

    def one_block(fn):
        return jnp.reshape(fn(mx, my, mc), (1,)).astype(jnp.int32)

    w_in_b = _cast_into_slot(w_in[0], pos, "cast_w_in")
    squares = [_cast_into_slot(w[0], pos, "cast_" + nm)
               for nm, w in (("w_up_attn", w_up_attn), ("w_up_gmlp", w_up_gmlp), ("w_out", w_out))]
    to_sibling = [_Copy(0, _slot, 0, _slot, _sibling)]
    ici = [[_Copy(0, _slot, 0, _slot, lambda x, y, c, chip=chip: (*chip(x, y, c), c))] for chip in _ICI_STAGES[:2]]
    relayed = lambda x, y, c: _slot(*_ICI_STAGES[1](x, y, c), c)
    ici.append([_Copy(0, relayed, 0, relayed, lambda x, y, c: (*_ICI_STAGES[0](x, y, c), c))])
    passes = []
    for chip in _ICI_STAGES:
        landed = lambda x, y, c, chip=chip: _slot(*chip(x, y, c), c)
        passes.append([_Copy(0, landed, 0, landed, _sibling)])
    sib_sems = _rdma_start("w_in_sibling_start", [w_in_b], to_sibling)
    ici_sems = _rdma_start("w_in_ici0_start", sib_sems[2], ici[0])
    h = _rmsnorm_fwd(x2d, norm_g, ici_sems[3])
    proj = _project(h, ici_sems[2][0], one_block(_slot), "projection_own")
    w_blocks = _rdma_wait("w_in_sibling_wait", ici_sems[2], sib_sems[0], sib_sems[1], to_sibling, proj)
    proj = _project(h, w_blocks[0], one_block(lambda x, y, c: _slot(x, y, 1 - c)), "projection_sibling", proj=proj)
    w_blocks = _rdma_wait("w_in_ici0_wait", w_blocks, ici_sems[0], ici_sems[1], ici[0], proj)
    first = _gather_first_copies(3)
    for k, chip in enumerate(_ICI_STAGES):
        if k + 1 < len(_ICI_STAGES):
            ici_sems = _rdma_start("w_in_ici%d_start" % (k + 1), w_blocks, ici[k + 1])
            w_blocks = ici_sems[2]
        else:
            send1, recv1, thru, _ = _rdma_start("gather_squares_start", squares + w_blocks, first)
            squares, w_blocks = thru[:3], thru[3:]
        pass_sems = _rdma_start("w_in_pass%d_start" % k, w_blocks, passes[k])
        proj = _project(h, pass_sems[2][0], one_block(lambda x, y, c, chip=chip: _slot(*chip(x, y, c), c)),
                        "projection_ici%d" % k, proj=proj, after=pass_sems[3])
        w_blocks = _rdma_wait("w_in_pass%d_wait" % k, pass_sems[2], pass_sems[0], pass_sems[1], passes[k], proj)
        proj = _project(h, w_blocks[0], one_block(lambda x, y, c, chip=chip: _slot(*chip(x, y, 1 - c), 1 - c)),
                        "projection_pass%d" % k, proj=proj)
        if k + 1 < len(_ICI_STAGES):
            w_blocks = _rdma_wait("w_in_ici%d_wait" % (k + 1), w_blocks, ici_sems[0], ici_sems[1], ici[k + 1], proj)
    w_in_b = w_blocks[0]

    tables = _rope_tables(positions[0])
    attn, a_in, q_rot, k_rot = _attention_fwd(proj, tables, sink, kv, proj)
    squares = _rdma_wait("gather_squares_wait", squares, send1, recv1, first, attn)
    passed = _gather_pass_copies(3)
    send2, recv2, squares, token = _rdma_start("pass_squares_start", squares, passed)
    b_in = _gmlp_fwd(proj, col_u, d, w_s, bias_t, gmlp_ln_g, gmlp_ln_b, token)
    squares = _rdma_wait("pass_squares_wait", squares, send2, recv2, passed, b_in)
    w_ua, w_ug, w_o = [w.reshape(N_DEV * rw, d) for w in squares]
    y_a = _matmul(a_in, w_ua, "nn", BF16, "up_attn")
    y_b = _matmul(b_in, w_ug, "nn", BF16, "up_gmlp")
    merged = _merge_fwd(y_a, y_b, proj, col_m)
    x_out = _matmul(merged, w_o, "nn", F32, "out_proj", res=x2d, tn=512)
    loss_p, d_final_g, dx2, dx2_b = _loss_and_final_norm_bwd(x_out, target, final_g)

    d_merged = _matmul(dx2_b, w_o, "nt", BF16, "d_merged")
    g_w_out = _matmul(merged, dx2_b, "tn", BF16, "g_w_out")
    d_ya, d_yb, d_proj = _merge_bwd(d_merged, y_a, y_b, proj, col_m, lax.empty(proj.shape, BF16))
    d_ain = _matmul(d_ya, w_ua, "nt", BF16, "d_a_in")
    g_w_ua = _matmul(a_in, d_ya, "tn", BF16, "g_w_up_attn")
    d_bin = _matmul(d_yb, w_ug, "nt", BF16, "d_b_in")
    g_w_ug = _matmul(b_in, d_yb, "tn", BF16, "g_w_up_gmlp")
    sq_grads = [g.reshape(N_DEV, rw, d) for g in (g_w_ua, g_w_ug, g_w_out)]
    sq_land = [lax.empty((N_CHIPS, rw, d), BF16) for _ in sq_grads]
    pairs_sq = _pair_copies_strided(3)
    arrays = [a for gl in zip(sq_grads, sq_land) for a in gl]
    send3, recv3, arrays, token = _rdma_start("pair_squares_start", arrays, pairs_sq)
    d_q, d_k, d_v, d_ga, d_sink = _attention_bwd(proj, q_rot, k_rot, tables, sink, kv, attn, d_ain, token)
    arrays = _rdma_wait("pair_squares_wait", arrays, send3, recv3, pairs_sq, d_q)
    sq_sums = [_pair_sum(arrays[2 * a], arrays[2 * a + 1], pos, "pair_sum_%d" % a) for a in range(3)]
    sq_land2 = [lax.empty((N_CHIPS - 1, rw, d), BF16) for _ in sq_sums]
    chip_sq = _chip_sum_copies(3)
    arrays = [a for gl in zip(sq_sums, sq_land2) for a in gl]
    send4, recv4, sq_arrays, token = _rdma_start("chip_squares_start", arrays, chip_sq)
    d_proj, d_w_s, d_bias_t, d_ln_g, d_ln_b = _gmlp_bwd(proj, col_u, d, w_s, bias_t, gmlp_ln_g, gmlp_ln_b, d_bin, token,
                                                        d_proj)
    for piece, col0, nm in ((d_q, 0, "d_q"), (d_k, d, "d_k"), (d_v, d + kv, "d_v"), (d_ga, d + 2 * kv, "d_gate")):
        d_proj = _place(d_proj, piece, col0, "place_" + nm)

    half = N_CHIPS // 2
    pairs_in = [_Copy(0, lambda x, y, c, q=q: q, 1, lambda x, y, c, q=q: q, _sibling) for q in range(half)]
    sent, token = [], None
    for j in range(2):
        g_sib = _grad_w_in_blocks(h, d_proj, 2 * chips[j * half:(j + 1) * half] + 1 - mc, cw,
                                  "g_w_in_sibling%d" % j, after=token)
        sent.append(_rdma_start("pair_w_in%d_start" % j, [g_sib, lax.empty((half, d, cw), BF16)], pairs_in))
        token = sent[-1][3]
    in_sums = None
    for j in range(2):
        send5, recv5, arrays, _ = sent[j]
        arrays = _rdma_wait("pair_w_in%d_wait" % j, arrays, send5, recv5, pairs_in, token if j == 0 else in_sums)
        in_sums = _grad_w_in_blocks(h, d_proj, 2 * chips[j * half:(j + 1) * half] + mc, cw, "g_w_in_own%d" % j,
                                    slots=N_CHIPS, slot0=j * half, prev=in_sums, init=arrays[1], tm=512)
    chip_in = _chip_sum_copies(1)
    send6, recv6, in_arrays, token = _rdma_start(
        "chip_w_in_start", [in_sums, lax.empty((N_CHIPS - 1, d, cw), BF16)], chip_in)
    d_h = _d_hidden(d_proj, w_in_b, after=token)
    grad_x, d_norm_g = _input_grad(d_h, x2d, norm_g, dx2)

    sq_arrays = _rdma_wait("chip_squares_wait", sq_arrays, send4, recv4, chip_sq, grad_x)
    big = {}
    for a, (name, w, m, v) in enumerate((("w_up_attn", w_up_attn, m_w_up_attn, v_w_up_attn),
                                         ("w_up_gmlp", w_up_gmlp, m_w_up_gmlp, v_w_up_gmlp),
                                         ("w_out", w_out, m_w_out, v_w_out))):
        big[name] = [r[None] for r in _reduce_adamw(sq_arrays[2 * a], sq_arrays[2 * a + 1], w[0], m[0], v[0], pos,
                                                    "adamw_" + name)]

    heads_per_pair = 2 * n_q_heads // (kv // HEAD_DIM)
    g_sink = d_sink[:, 0, :heads_per_pair].reshape(1, n_q_heads)
    small_w = [norm_g, attn_sink, gmlp_ln_g, gmlp_ln_b, w_spatial, b_spatial, final_norm_g]
    small_m = [m_norm_g, m_attn_sink, m_gmlp_ln_g, m_gmlp_ln_b, m_w_spatial, m_b_spatial, m_final_norm_g]
    small_v = [v_norm_g, v_attn_sink, v_gmlp_ln_g, v_gmlp_ln_b, v_w_spatial, v_b_spatial, v_final_norm_g]
    small_g = [d_norm_g, g_sink, d_ln_g, d_ln_b, d_w_s[None], d_bias_t.T[None], d_final_g.reshape(d)]
    loss_pad = jnp.zeros((1,), F32)
    shapes = [w.shape for w in small_w] + [(1,)]
    packed = _small_allreduce_adamw(_pack(small_g + [loss_p[0, :1]]), _pack(small_w + [loss_pad]),
                                    _pack(small_m + [loss_pad]), _pack(small_v + [loss_pad]))
    sg, sd, sm, sv = [_unpack(p, shapes) for p in packed]
    loss = sg[-1][0]
    in_arrays = _rdma_wait("chip_w_in_wait", in_arrays, send6, recv6, chip_in, packed[0])
    big["w_in"] = [r[None] for r in _reduce_adamw(in_arrays[0], in_arrays[1], w_in[0], m_w_in[0], v_w_in[0], pos,
                                                  "adamw_w_in")]

    names = ["norm_g", "w_in", "attn_sink", "gmlp_ln_g", "gmlp_ln_b", "w_spatial", "b_spatial", "w_up_attn",
             "w_up_gmlp", "w_out", "final_norm_g"]
    small_names = ["norm_g", "attn_sink", "gmlp_ln_g", "gmlp_ln_b", "w_spatial", "b_spatial", "final_norm_g"]
    outs = [[], [], [], []]
    for nm in names:
        for k in range(4):
            if nm in big:
                outs[k].append(big[nm][k])
            else:
                outs[k].append((sg, sd, sm, sv)[k][small_names.index(nm)])
    return (loss, grad_x[None], *outs[0], *outs[1], *outs[2], *outs[3])
```

```python
import math
from typing import Callable, NamedTuple

import jax
import jax.numpy as jnp
from jax import lax
from jax.experimental import pallas as pl
from jax.experimental.pallas import tpu as pltpu

F32 = jnp.float32
BF16 = jnp.bfloat16
MESH = pl.DeviceIdType.MESH

N_DEV = 8
N_CHIPS = 4
HEAD_DIM = 64
BLOCK = 128
ROPE_DIM = 16
ROPE_HALF = ROPE_DIM // 2
ROPE_THETA = 500000.0
GMLP_GROUPS = 8
NORM_EPS = 1e-5
LN_EPS = 1e-5
ATTN_SCALE = HEAD_DIM ** -0.5
LANES = 128
SUBLANES = 8
VMEM_LIMIT = 48 * 1024 * 1024
VMEM_LIMIT_WIDE = 56 * 1024 * 1024
DOT_COLS = 1024
SEGMENT_SPLIT = 4

ADAM_LR = 0.001
ADAM_B1 = 0.9
ADAM_B2 = 0.999
ADAM_EPS = 1e-08
ADAM_WD = 0.01
ADAM_STEP = 10

GELU_C = math.sqrt(2.0 / math.pi)
GELU_K = 0.044715

HBM_SPEC = pl.BlockSpec(memory_space=pltpu.HBM)
ANY_SPEC = pl.BlockSpec(memory_space=pl.ANY)
SEM_SPEC = pl.BlockSpec(memory_space=pltpu.SEMAPHORE)
VMEM_SPEC = pl.BlockSpec(memory_space=pltpu.VMEM)
SMEM_SPEC = pl.BlockSpec(memory_space=pltpu.SMEM)


def _sds(shape, dtype):
    return jax.ShapeDtypeStruct(shape, dtype)


def _params(*sem, vmem=VMEM_LIMIT):
    return pltpu.CompilerParams(dimension_semantics=sem or None, vmem_limit_bytes=vmem)


def _gelu(x):
    return 0.5 * x * (1.0 + jnp.tanh(GELU_C * (x + GELU_K * x * x * x)))


def _gelu_and_grad(x):
    x2 = x * x
    t = jnp.tanh(x * (GELU_C + (GELU_C * GELU_K) * x2))
    half = 0.5 + 0.5 * t
    return x * half, half + (x * (1.0 - t * t)) * (0.5 * GELU_C + (1.5 * GELU_C * GELU_K) * x2)


def _silu_and_grad(x):
    s = jax.nn.sigmoid(x)
    return x * s, s * (1.0 + x * (1.0 - s))


def _adamw(w, g, m, v):
    m = ADAM_B1 * m + (1.0 - ADAM_B1) * g
    v = ADAM_B2 * v + (1.0 - ADAM_B2) * (g * g)
    m_hat = m / (1.0 - ADAM_B1 ** ADAM_STEP)
    v_hat = v / (1.0 - ADAM_B2 ** ADAM_STEP)
    delta = -ADAM_LR * (m_hat / (jnp.sqrt(v_hat) + ADAM_EPS) + ADAM_WD * w)
    return delta, m, v


def _mesh_pos():
    return lax.axis_index("x"), lax.axis_index("y"), lax.axis_index("c")


def _slot(x, y, c):
    return 4 * x + 2 * y + c


def _chip(x, y):
    return 2 * x + y


def _sibling(x, y, c):
    return (x, y, 1 - c)


_OTHER_CHIPS = (lambda x, y: (1 - x, y), lambda x, y: (x, 1 - y), lambda x, y: (1 - x, 1 - y))
_ICI_STAGES = (lambda x, y, c: (x ^ c, y ^ (1 - c)), lambda x, y, c: (x ^ (1 - c), y ^ c),
               lambda x, y, c: (1 - x, 1 - y))


class _Copy(NamedTuple):
    src: int
    src_slot: Callable
    dst: int
    dst_slot: Callable
    peer: Callable


def _descriptor(refs, send_sems, recv_sems, k, cp):
    pos = _mesh_pos()
    return pltpu.make_async_remote_copy(
        src_ref=refs[cp.src].at[cp.src_slot(*pos)], dst_ref=refs[cp.dst].at[cp.dst_slot(*pos)],
        send_sem=send_sems.at[k], recv_sem=recv_sems.at[k], device_id=cp.peer(*pos), device_id_type=MESH)


def _gather_first_copies(n_arrays):
    copies = []
    for a in range(n_arrays):
        copies.append(_Copy(a, _slot, a, _slot, _sibling))
        for chip in _OTHER_CHIPS:
            copies.append(_Copy(a, _slot, a, _slot, lambda x, y, c, chip=chip: (*chip(x, y), c)))
    return copies


def _gather_pass_copies(n_arrays):
    copies = []
    for a in range(n_arrays):
        for chip in _OTHER_CHIPS:
            src = lambda x, y, c, chip=chip: _slot(*chip(x, y), c)
            copies.append(_Copy(a, src, a, src, _sibling))
    return copies


def _pair_copies_strided(n_sets):
    copies = []
    for a in range(n_sets):
        for q in range(N_CHIPS):
            copies.append(_Copy(2 * a, lambda x, y, c, q=q: 2 * q + 1 - c, 2 * a + 1, lambda x, y, c, q=q: q, _sibling))
    return copies


def _chip_sum_copies(n_sets):
    copies = []
    for a in range(n_sets):
        for k, chip in enumerate(_OTHER_CHIPS):
            copies.append(_Copy(2 * a, lambda x, y, c, chip=chip: _chip(*chip(x, y)), 2 * a + 1,
                                lambda x, y, c, k=k: k, lambda x, y, c, chip=chip: (*chip(x, y), c)))
    return copies


def _rdma_start(name, arrays, copies):
    n, nc = len(arrays), len(copies)

    def body(*refs):
        in_refs = refs[:n]
        send_sems, recv_sems = refs[n], refs[n + 1]
        token = refs[2 * n + 2]
        for k, cp in enumerate(copies):
            _descriptor(in_refs, send_sems, recv_sems, k, cp).start()
        token[...] = jnp.zeros_like(token)

    out = pl.pallas_call(
        body, name=name,
        out_shape=(pltpu.SemaphoreType.DMA((nc,)), pltpu.SemaphoreType.DMA((nc,)),
                   *[pltpu.HBM(a.shape, a.dtype) for a in arrays], _sds((SUBLANES, LANES), F32)),
        in_specs=[HBM_SPEC] * n, out_specs=(SEM_SPEC, SEM_SPEC, *([HBM_SPEC] * n), VMEM_SPEC),
        input_output_aliases={i: i + 2 for i in range(n)},
        compiler_params=pltpu.CompilerParams(has_side_effects=pltpu.SideEffectType.DATAFLOW_SIDE_EFFECTING),
    )(*[pltpu.with_memory_space_constraint(a, pltpu.HBM) for a in arrays])
    return out[0], out[1], list(out[2:2 + n]), out[2 + n]


def _rdma_wait(name, arrays, send_sems, recv_sems, copies, after):
    n = len(arrays)

    def body(*refs):
        in_refs = refs[:n]
        send_ref, recv_ref = refs[n], refs[n + 1]
        for k, cp in enumerate(copies):
            d = _descriptor(in_refs, send_ref, recv_ref, k, cp)
            d.wait_send()
            d.wait_recv()

    out = pl.pallas_call(
        body, name=name, out_shape=tuple(pltpu.HBM(a.shape, a.dtype) for a in arrays),
        in_specs=[HBM_SPEC] * n + [SEM_SPEC, SEM_SPEC, ANY_SPEC], out_specs=tuple([HBM_SPEC] * n),
        input_output_aliases={i: i for i in range(n)},
        compiler_params=pltpu.CompilerParams(has_side_effects=pltpu.SideEffectType.DATAFLOW_SIDE_EFFECTING),
    )(*arrays, send_sems, recv_sems, after)
    return list(out)


def _cast_into_slot(w, pos, name):
    rows, cols = w.shape
    tr = min(rows, 256)

    def body(pos_ref, w_ref, o_ref):
        o_ref[...] = w_ref[...].astype(BF16)

    return pl.pallas_call(
        body, name=name,
        grid_spec=pltpu.PrefetchScalarGridSpec(
            num_scalar_prefetch=1, grid=(rows // tr,),
            in_specs=[pl.BlockSpec((tr, cols), lambda i, p: (i, 0))],
            out_specs=pl.BlockSpec((None, tr, cols), lambda i, p: (_slot(p[0], p[1], p[2]), i, 0))),
        out_shape=_sds((N_DEV, rows, cols), BF16), compiler_params=_params("parallel"),
    )(pos, w)


def _pair_sum(g, land, pos, name):
    _, rows, cols = land.shape
    tr = min(rows, 128)
    strided = g.shape[0] == N_DEV

    def body(pos_ref, g_ref, l_ref, o_ref):
        o_ref[...] = (g_ref[...].astype(F32) + l_ref[...].astype(F32)).astype(BF16)

    g_map = (lambda q, i, p: (2 * q + p[2], i, 0)) if strided else (lambda q, i, p: (q, i, 0))
    blk = pl.BlockSpec((None, tr, cols), lambda q, i, p: (q, i, 0))
    return pl.pallas_call(
        body, name=name,
        grid_spec=pltpu.PrefetchScalarGridSpec(
            num_scalar_prefetch=1, grid=(N_CHIPS, rows // tr),
            in_specs=[pl.BlockSpec((None, tr, cols), g_map), blk], out_specs=blk),
        out_shape=_sds((N_CHIPS, rows, cols), BF16), compiler_params=_params("parallel", "parallel"),
    )(pos, g, land)


def _reduce_adamw(sums, land, w, m, v, pos, name):
    rows, cols = w.shape
    tr = min(rows, 64)

    def body(pos_ref, s_ref, l_ref, w_ref, m_ref, v_ref, g_ref, d_ref, nm_ref, nv_ref):
        g = s_ref[...].astype(F32)
        for k in range(N_CHIPS - 1):
            g = g + l_ref[k].astype(F32)
        delta, nm, nv = _adamw(w_ref[...], g, m_ref[...], v_ref[...])
        g_ref[...] = g
        d_ref[...] = delta
        nm_ref[...] = nm
        nv_ref[...] = nv

    spec = pl.BlockSpec((tr, cols), lambda i, p: (i, 0))
    return pl.pallas_call(
        body, name=name,
        grid_spec=pltpu.PrefetchScalarGridSpec(
            num_scalar_prefetch=1, grid=(rows // tr,),
            in_specs=[pl.BlockSpec((None, tr, cols), lambda i, p: (_chip(p[0], p[1]), i, 0)),
                      pl.BlockSpec((N_CHIPS - 1, tr, cols), lambda i, p: (0, i, 0)), spec, spec, spec],
            out_specs=(spec, spec, spec, spec)),
        out_shape=tuple([_sds((rows, cols), F32)] * 4), compiler_params=_params("parallel"),
    )(pos, sums, land, w, m, v)


def _small_allreduce_adamw(g, w, m, v):
    rows = g.shape[0]

    def body(g_ref, w_ref, m_ref, v_ref, gs_ref, d_ref, nm_ref, nv_ref, all_ref, send_sems, recv_sems):
        x, y, c = _mesh_pos()
        me = _slot(x, y, c)
        copies = []
        for k in range(1, N_DEV):
            peer = (x ^ (k >> 2), y ^ ((k >> 1) & 1), c ^ (k & 1))
            copies.append(pltpu.make_async_remote_copy(
                src_ref=g_ref, dst_ref=all_ref.at[me], send_sem=send_sems.at[k - 1],
                recv_sem=recv_sems.at[k - 1], device_id=peer, device_id_type=MESH))
        for cp in copies:
            cp.start()
        all_ref[me] = g_ref[...]
        for cp in copies:
            cp.wait_recv()
        total = all_ref[0]
        for s in range(1, N_DEV):
            total = total + all_ref[s]
        delta, nm, nv = _adamw(w_ref[...], total, m_ref[...], v_ref[...])
        gs_ref[...] = total
        d_ref[...] = delta
        nm_ref[...] = nm
        nv_ref[...] = nv
        for cp in copies:
            cp.wait_send()

    return pl.pallas_call(
        body, name="small_allreduce_adamw", out_shape=tuple([_sds((rows, LANES), F32)] * 4),
        in_specs=[VMEM_SPEC] * 4, out_specs=tuple([VMEM_SPEC] * 4),
        scratch_shapes=[pltpu.VMEM((N_DEV, rows, LANES), F32), pltpu.SemaphoreType.DMA((7,)),
                        pltpu.SemaphoreType.DMA((7,))],
    )(g, w, m, v)


_DOT_DIMS = {"nn": ((1,), (0,)), "nt": ((1,), (1,)), "tn": ((0,), (0,))}


def _dot(a, b, mode):
    return lax.dot_general(a, b, (_DOT_DIMS[mode], ((), ())), preferred_element_type=F32)


def _col_chunks(cols):
    return [(c0, min(c0 + DOT_COLS, cols)) for c0 in range(0, cols, DOT_COLS)]


def _matmul(a, b, mode, out_dtype, name, *, res=None, tm=1024, tn=1024):
    if mode == "tn":
        kdim, mdim = a.shape
    else:
        mdim, kdim = a.shape
    ndim = b.shape[0] if mode == "nt" else b.shape[1]
    tm, tn = min(tm, mdim), min(tn, ndim)
    assert mdim % tm == 0 and ndim % tn == 0, (name, mdim, ndim)

    def body(*refs):
        out = _dot(refs[0][...], refs[1][...], mode)
        if res is not None:
            out = out + refs[2][...]
        refs[-1][...] = out.astype(out_dtype)

    a_spec = pl.BlockSpec((kdim, tm), lambda i, j: (0, i)) if mode == "tn" else pl.BlockSpec((tm, kdim), lambda i, j: (i, 0))
    b_spec = pl.BlockSpec((tn, kdim), lambda i, j: (j, 0)) if mode == "nt" else pl.BlockSpec((kdim, tn), lambda i, j: (0, j))
    o_spec = pl.BlockSpec((tm, tn), lambda i, j: (i, j))
    in_specs, args = [a_spec, b_spec], [a, b]
    if res is not None:
        in_specs.append(o_spec)
        args.append(res)
    return pl.pallas_call(
        body, name=name, grid=(mdim // tm, ndim // tn), in_specs=in_specs, out_specs=o_spec,
        out_shape=_sds((mdim, ndim), out_dtype), compiler_params=_params("parallel", "parallel"),
    )(*args)


def _project(h, w_blocks, block_ids, name, *, proj=None, after=None, tm=1024, tk=1024):
    s, d = h.shape
    _, _, cw = w_blocks.shape
    n = block_ids.shape[0]
    tm, tk = min(tm, s), min(tk, d)
    nk = d // tk

    def body(ids_ref, h_ref, w_ref, *rest):
        o_ref, acc_ref = rest[-2], rest[-1]
        k = pl.program_id(2)

        @pl.when(k == 0)
        def _():
            acc_ref[...] = jnp.zeros_like(acc_ref)

        for c0, c1 in _col_chunks(cw):
            acc_ref[:, c0:c1] += _dot(h_ref[...], w_ref[:, c0:c1], "nn")

        @pl.when(k == nk - 1)
        def _():
            o_ref[...] = acc_ref[...].astype(BF16)

    in_specs = [pl.BlockSpec((tm, tk), lambda j, i, k, ids: (i, k)),
                pl.BlockSpec((None, tk, cw), lambda j, i, k, ids: (ids[j], k, 0))]
    args = [block_ids, h, w_blocks]
    aliases = {}
    if proj is not None:
        in_specs.append(ANY_SPEC)
        args.append(proj)
        aliases = {3: 0}
    if after is not None:
        in_specs.append(ANY_SPEC)
        args.append(after)
    return pl.pallas_call(
        body, name=name,
        grid_spec=pltpu.PrefetchScalarGridSpec(
            num_scalar_prefetch=1, grid=(n, s // tm, d // tk), in_specs=in_specs,
            out_specs=pl.BlockSpec((tm, cw), lambda j, i, k, ids: (i, ids[j])),
            scratch_shapes=[pltpu.VMEM((tm, cw), F32)]),
        out_shape=_sds((s, N_DEV * cw), BF16), input_output_aliases=aliases,
        compiler_params=_params("arbitrary", "arbitrary", "arbitrary", vmem=VMEM_LIMIT_WIDE),
    )(*args)


def _grad_w_in_blocks(h, d_proj, block_ids, cw, name, *, slots=None, slot0=0, prev=None, init=None, after=None,
                      tm=1024, tk=1024):
    s, d = h.shape
    n = block_ids.shape[0]
    slots = n if slots is None else slots
    tm, tk = min(tm, d), min(tk, s)
    nk = s // tk

    def body(ids_ref, h_ref, g_ref, *rest):
        o_ref, acc_ref = rest[-2], rest[-1]
        k = pl.program_id(2)

        @pl.when(k == 0)
        def _():
            acc_ref[...] = jnp.zeros_like(acc_ref) if init is None else rest[0][...].astype(F32)

        for c0, c1 in _col_chunks(cw):
            acc_ref[:, c0:c1] += _dot(h_ref[...], g_ref[:, c0:c1], "tn")

        @pl.when(k == nk - 1)
        def _():
            o_ref[...] = acc_ref[...].astype(BF16)

    in_specs = [pl.BlockSpec((tk, tm), lambda q, i, k, ids: (k, i)),
                pl.BlockSpec((tk, cw), lambda q, i, k, ids: (k, ids[q]))]
    args = [block_ids, h, d_proj]
    aliases = {}
    if init is not None:
        in_specs.append(pl.BlockSpec((None, tm, cw), lambda q, i, k, ids: (q, i, 0)))
        args.append(init)
    if prev is not None:
        aliases = {len(args): 0}
        in_specs.append(ANY_SPEC)
        args.append(prev)
    if after is not None:
        in_specs.append(ANY_SPEC)
        args.append(after)
    return pl.pallas_call(
        body, name=name,
        grid_spec=pltpu.PrefetchScalarGridSpec(
            num_scalar_prefetch=1, grid=(n, d // tm, nk), in_specs=in_specs,
            out_specs=pl.BlockSpec((None, tm, cw), lambda q, i, k, ids: (slot0 + q, i, 0)),
            scratch_shapes=[pltpu.VMEM((tm, cw), F32)]),
        out_shape=_sds((slots, d, cw), BF16), input_output_aliases=aliases,
        compiler_params=_params("parallel", "parallel", "arbitrary", vmem=VMEM_LIMIT_WIDE),
    )(*args)


def _d_hidden(d_proj, w_blocks, after=None, *, tm=1024, tn=1024):
    s = d_proj.shape[0]
    nb, d, cw = w_blocks.shape
    tm, tn = min(tm, s), min(tn, d)

    def body(g_ref, w_ref, *rest):
        o_ref = rest[-1]
        k = pl.program_id(2)

        @pl.when(k == 0)
        def _():
            o_ref[...] = jnp.zeros_like(o_ref)

        o_ref[...] += _dot(g_ref[...], w_ref[...], "nt")

    in_specs = [pl.BlockSpec((tm, cw), lambda i, j, k: (i, k)),
                pl.BlockSpec((None, tn, cw), lambda i, j, k: (k, j, 0))]
    args = [d_proj, w_blocks]
    if after is not None:
        in_specs.append(ANY_SPEC)
        args.append(after)
    return pl.pallas_call(
        body, name="d_h", grid=(s // tm, d // tn, nb), in_specs=in_specs,
        out_specs=pl.BlockSpec((tm, tn), lambda i, j, k: (i, j)), out_shape=_sds((s, d), F32),
        compiler_params=_params("parallel", "parallel", "arbitrary", vmem=VMEM_LIMIT_WIDE),
    )(*args)


def _row_tile(rows):
    return min(rows, 128)


def _segment_specs(rows, d, col0):
    w = d // SEGMENT_SPLIT
    assert col0 % w == 0
    return [pl.BlockSpec((rows, w), lambda i, t=t: (i, col0 // w + t)) for t in range(SEGMENT_SPLIT)]


def _cat(refs):
    return jnp.concatenate([r[...].astype(F32) for r in refs], axis=1)


def _rmsnorm_fwd(x, g, after):
    s, d = x.shape
    tr = _row_tile(s)

    def body(x_ref, g_ref, after_ref, h_ref):
        xv = x_ref[...]
        r = lax.rsqrt(jnp.mean(xv * xv, axis=-1, keepdims=True) + NORM_EPS)
        h_ref[...] = (xv * r * g_ref[...]).astype(BF16)

    row = pl.BlockSpec((tr, d), lambda i: (i, 0))
    vec = pl.BlockSpec((1, d), lambda i: (0, 0))
    return pl.pallas_call(body, name="rmsnorm_fwd", grid=(s // tr,), in_specs=[row, vec, ANY_SPEC], out_specs=row,
                          out_shape=_sds((s, d), BF16), compiler_params=_params("parallel"))(x, g, after)


def _merge_fwd(y_a, y_b, proj, col_m):
    s, d = y_a.shape
    tr = _row_tile(s)
    ns = SEGMENT_SPLIT

    def body(ya_ref, yb_ref, *rest):
        ma, mb, o_ref = _cat(rest[:ns]), _cat(rest[ns:2 * ns]), rest[2 * ns]
        o_ref[...] = (jax.nn.sigmoid(ma) * ya_ref[...].astype(F32)
                      + jax.nn.sigmoid(mb) * yb_ref[...].astype(F32)).astype(BF16)

    row = pl.BlockSpec((tr, d), lambda i: (i, 0))
    return pl.pallas_call(
        body, name="merge_fwd", grid=(s // tr,),
        in_specs=[row, row, *_segment_specs(tr, d, col_m), *_segment_specs(tr, d, col_m + d)],
        out_specs=row, out_shape=_sds((s, d), BF16), compiler_params=_params("parallel"),
    )(y_a, y_b, *([proj] * (2 * ns)))


def _loss_and_final_norm_bwd(x2, target, g):
    s, d = x2.shape
    tr = _row_tile(s)

    def body(x_ref, t_ref, g_ref, loss_ref, dg_ref, dx_ref, dxb_ref):
        @pl.when(pl.program_id(0) == 0)
        def _():
            loss_ref[...] = jnp.zeros_like(loss_ref)
            dg_ref[...] = jnp.zeros_like(dg_ref)

        xv, gv = x_ref[...], g_ref[...]
        r = lax.rsqrt(jnp.mean(xv * xv, axis=-1, keepdims=True) + NORM_EPS)
        xhat = xv * r
        err = xhat * gv - t_ref[...]
        loss_ref[...] += 0.5 * jnp.sum(jnp.mean(err * err, axis=-1, keepdims=True))
        dy = err / d
        dg_ref[...] += jnp.sum(dy * xhat, axis=0, keepdims=True)
        dyg = dy * gv
        dx = r * (dyg - xhat * jnp.mean(dyg * xhat, axis=-1, keepdims=True))
        dx_ref[...] = dx
        dxb_ref[...] = dx.astype(BF16)

    row = pl.BlockSpec((tr, d), lambda i: (i, 0))
    vec = pl.BlockSpec((1, d), lambda i: (0, 0))
    return pl.pallas_call(
        body, name="loss_final_norm_bwd", grid=(s // tr,), in_specs=[row, row, vec],
        out_specs=(pl.BlockSpec((SUBLANES, LANES), lambda i: (0, 0)), vec, row, row),
        out_shape=(_sds((SUBLANES, LANES), F32), _sds((1, d), F32), _sds((s, d), F32), _sds((s, d), BF16)),
        compiler_params=_params("arbitrary"))(x2, target, g)


def _write_behind(step, n_steps, buf, sems, wide_ref, rows, col0, fill):
    cols = buf.shape[2]
    slot = step % 2

    def copy(at_step, at_slot):
        dst = wide_ref.at[pl.ds(pl.multiple_of(at_step * rows, rows), rows), pl.ds(col0, cols)]
        return pltpu.make_async_copy(buf.at[at_slot], dst, sems.at[at_slot])

    @pl.when(step >= 2)
    def _():
        copy(step - 2, slot).wait()

    fill(buf.at[slot])
    copy(step, slot).start()

    @pl.when(step == n_steps - 1)
    def _():
        copy(step, slot).wait()
        if n_steps > 1:
            copy(step - 1, 1 - slot).wait()


def _merge_bwd(d_merged, y_a, y_b, proj, col_m, d_proj):
    s, d = y_a.shape
    tr = _row_tile(s)
    ns = SEGMENT_SPLIT
    n_steps = s // tr

    def body(dm_ref, ya_ref, yb_ref, *rest):
        ma, mb = _cat(rest[:ns]), _cat(rest[ns:2 * ns])
        dya_ref, dyb_ref, wide_ref, buf, sems = rest[2 * ns + 1:]
        dm = dm_ref[...].astype(F32)
        sa = jax.nn.sigmoid(ma)
        sb = jax.nn.sigmoid(mb)
        dya_ref[...] = (dm * sa).astype(BF16)
        dyb_ref[...] = (dm * sb).astype(BF16)

        def fill(out):
            out[:, :d] = (dm * ya_ref[...].astype(F32) * (sa * (1.0 - sa))).astype(BF16)
            out[:, d:] = (dm * yb_ref[...].astype(F32) * (sb * (1.0 - sb))).astype(BF16)

        _write_behind(pl.program_id(0), n_steps, buf, sems, wide_ref, tr, col_m, fill)

    row = pl.BlockSpec((tr, d), lambda i: (i, 0))
    n_in = 3 + 2 * ns
    return pl.pallas_call(
        body, name="merge_bwd", grid=(n_steps,),
        in_specs=[row, row, row, *_segment_specs(tr, d, col_m), *_segment_specs(tr, d, col_m + d), ANY_SPEC],
        out_specs=(row, row, ANY_SPEC),
        out_shape=(_sds((s, d), BF16), _sds((s, d), BF16), _sds(d_proj.shape, BF16)),
        input_output_aliases={n_in: 2},
        scratch_shapes=[pltpu.VMEM((2, tr, 2 * d), BF16), pltpu.SemaphoreType.DMA((2,))],
        compiler_params=_params("arbitrary"))(d_merged, y_a, y_b, *([proj] * (2 * ns)), d_proj)


def _place(d_proj, piece, col0, name):
    s, w = piece.shape
    bw = math.gcd(w, col0) if col0 else w
    tr = min(s, 512)

    def body(p_ref, wide_in, o_ref):
        o_ref[...] = p_ref[...]

    return pl.pallas_call(
        body, name=name, grid=(s // tr, w // bw),
        in_specs=[pl.BlockSpec((tr, bw), lambda i, j: (i, j)), ANY_SPEC],
        out_specs=pl.BlockSpec((tr, bw), lambda i, j: (i, col0 // bw + j)),
        out_shape=_sds(d_proj.shape, d_proj.dtype), input_output_aliases={1: 0},
        compiler_params=_params("parallel", "parallel"))(piece, d_proj)


def _input_grad(d_h, x, g, dx2):
    s, d = x.shape
    tr = _row_tile(s)

    def body(dh_ref, x_ref, g_ref, dx2_ref, gx_ref, dg_ref):
        @pl.when(pl.program_id(0) == 0)
        def _():
            dg_ref[...] = jnp.zeros_like(dg_ref)

        xv, dh = x_ref[...], dh_ref[...]
        r = lax.rsqrt(jnp.mean(xv * xv, axis=-1, keepdims=True) + NORM_EPS)
        xhat = xv * r
        dg_ref[...] += jnp.sum(dh * xhat, axis=0, keepdims=True)
        dyg = dh * g_ref[...]
        gx_ref[...] = dx2_ref[...] + r * (dyg - xhat * jnp.mean(dyg * xhat, axis=-1, keepdims=True))

    row = pl.BlockSpec((tr, d), lambda i: (i, 0))
    vec = pl.BlockSpec((1, d), lambda i: (0, 0))
    return pl.pallas_call(
        body, name="input_grad", grid=(s // tr,), in_specs=[row, row, vec, row], out_specs=(row, vec),
        out_shape=(_sds((s, d), F32), _sds((1, d), F32)), compiler_params=_params("arbitrary"))(d_h, x, g, dx2)


def _rope_tables(positions):
    inv_freq = ROPE_THETA ** (-jnp.arange(ROPE_HALF, dtype=F32) * 2.0 / ROPE_DIM)
    ang = positions.astype(F32)[:, None] * inv_freq
    cos, sin = jnp.cos(ang), jnp.sin(ang)
    zero = jnp.zeros((positions.shape[0], HEAD_DIM - ROPE_DIM), F32)
    zero_h = jnp.zeros_like(sin)
    c = jnp.concatenate([cos, cos, zero + 1.0], axis=1)
    up = jnp.concatenate([-sin, zero_h, zero], axis=1)
    down = jnp.concatenate([zero_h, sin, zero], axis=1)
    reps = LANES // HEAD_DIM
    return jnp.stack([jnp.tile(c, (1, reps)), jnp.tile(up, (1, reps)), jnp.tile(down, (1, reps))])


def _lane_tiles(x):
    return [x[:, t * LANES:(t + 1) * LANES] for t in range(x.shape[1] // LANES)]


def _rope(x, tab):
    out = [xt * tab[0] + pltpu.roll(xt, LANES - ROPE_HALF, 1) * tab[1] + pltpu.roll(xt, ROPE_HALF, 1) * tab[2]
           for xt in _lane_tiles(x)]
    return out[0] if len(out) == 1 else jnp.concatenate(out, axis=1)


def _rope_bwd(g, tab):
    out = [gt * tab[0] + pltpu.roll(gt * tab[1], ROPE_HALF, 1) + pltpu.roll(gt * tab[2], LANES - ROPE_HALF, 1)
           for gt in _lane_tiles(g)]
    return out[0] if len(out) == 1 else jnp.concatenate(out, axis=1)


def _head(x, h):
    return x[:, h * HEAD_DIM:(h + 1) * HEAD_DIM]


def _stack_heads(x, first, count):
    return jnp.concatenate([_head(x, first + h) for h in range(count)], axis=0)


def _dot_nt(a, b):
    return lax.dot_general(a, b, (((1,), (1,)), ((), ())), preferred_element_type=F32)


def _causal(rows):
    qi = lax.broadcasted_iota(jnp.int32, (rows, BLOCK), 0) % BLOCK
    return lax.broadcasted_iota(jnp.int32, (rows, BLOCK), 1) <= qi


def _band_probs_by_head(qs, k_band, sink_ref, first, count, causal, blk):
    s_band = _dot_nt(qs, k_band)
    p_all, p_band, p_sink = [], [], []
    for h in range(count):
        rows = slice(h * BLOCK, (h + 1) * BLOCK)
        sink = sink_ref[first + h]
        s = jnp.where(causal, s_band[rows, :BLOCK], jnp.where(blk > 0, s_band[rows, BLOCK:], -jnp.inf))
        m = jnp.maximum(jnp.max(s, axis=-1, keepdims=True), sink)
        p = jnp.exp(s - m)
        ps = jnp.exp(sink - m)
        inv = 1.0 / (jnp.sum(p, axis=-1, keepdims=True) + ps)
        p = p * inv
        p_all.append(p)
        p_sink.append(ps * inv)
        p_band.append(_split_band(p, causal))
    cat = lambda parts: jnp.concatenate(parts, axis=0)
    return cat([c for c, _ in p_band]), cat([v for _, v in p_band]), cat(p_all), cat(p_sink)


def _split_band(x, causal):
    return jnp.where(causal, x, 0.0).astype(BF16), jnp.where(causal, 0.0, x).astype(BF16)


def _attn_dims(s, d, kv):
    n_kv = kv // HEAD_DIM
    group = d // kv
    qw = 2 * group * HEAD_DIM
    assert n_kv % 2 == 0 and (d + 2 * kv) % qw == 0 and s % BLOCK == 0
    return group, qw, n_kv // 2, s // BLOCK


def _attention_fwd(proj, tables, sink, kv, after):
    s, d = proj.shape[0], sink.shape[0] * HEAD_DIM
    group, qw, n_pairs, nb = _attn_dims(s, d, kv)

    def body(sink_ref, q_ref, kc_ref, kp_ref, vc_ref, vp_ref, ga_ref, tc_ref, tp_ref, after_ref,
             attn_ref, ain_ref, qrot_ref, krot_ref):
        pair, blk = pl.program_id(0), pl.program_id(1)
        tab_c, tab_p = tc_ref[...], tp_ref[...]
        q = _rope(q_ref[...].astype(F32), tab_c) * ATTN_SCALE
        k_cur, k_prev = _rope(kc_ref[...].astype(F32), tab_c), _rope(kp_ref[...].astype(F32), tab_p)
        qrot_ref[...] = q.astype(BF16)
        krot_ref[...] = k_cur.astype(BF16)
        v_cur, v_prev = vc_ref[...], vp_ref[...]
        causal = _causal(BLOCK)
        outs = []
        for a in range(2):
            k_band = jnp.concatenate([_head(k_cur, a), _head(k_prev, a)], axis=0).astype(BF16)
            vc, vp = _head(v_cur, a).astype(BF16), _head(v_prev, a).astype(BF16)
            qs = _stack_heads(q, a * group, group).astype(BF16)
            p_cur, p_prev, _, _ = _band_probs_by_head(qs, k_band, sink_ref, (2 * pair + a) * group, group, causal, blk)
            o = jnp.dot(p_cur, vc, preferred_element_type=F32) + jnp.dot(p_prev, vp, preferred_element_type=F32)
            outs += [o[h * BLOCK:(h + 1) * BLOCK] for h in range(group)]
        attn = jnp.concatenate(outs, axis=1)
        attn_ref[...] = attn.astype(BF16)
        silu, _ = _silu_and_grad(ga_ref[...].astype(F32))
        ain_ref[...] = (attn * silu).astype(BF16)

    k0, v0, g0 = d // LANES, (d + kv) // LANES, (d + 2 * kv) // qw
    prev = lambda i: jnp.maximum(i - 1, 0)
    in_specs = [
        SMEM_SPEC,
        pl.BlockSpec((BLOCK, qw), lambda p, i: (i, p)),
        pl.BlockSpec((BLOCK, LANES), lambda p, i: (i, k0 + p)),
        pl.BlockSpec((BLOCK, LANES), lambda p, i: (prev(i), k0 + p)),
        pl.BlockSpec((BLOCK, LANES), lambda p, i: (i, v0 + p)),
        pl.BlockSpec((BLOCK, LANES), lambda p, i: (prev(i), v0 + p)),
        pl.BlockSpec((BLOCK, qw), lambda p, i: (i, g0 + p)),
        pl.BlockSpec((3, BLOCK, LANES), lambda p, i: (0, i, 0)),
        pl.BlockSpec((3, BLOCK, LANES), lambda p, i: (0, prev(i), 0)),
        ANY_SPEC,
    ]
    out = pl.BlockSpec((BLOCK, qw), lambda p, i: (i, p))
    k_out = pl.BlockSpec((BLOCK, LANES), lambda p, i: (i, p))
    return pl.pallas_call(
        body, name="attention_fwd", grid=(n_pairs, nb), in_specs=in_specs, out_specs=(out, out, out, k_out),
        out_shape=(_sds((s, d), BF16), _sds((s, d), BF16), _sds((s, d), BF16), _sds((s, kv), BF16)),
        compiler_params=_params("parallel", "parallel"),
    )(sink, proj, proj, proj, proj, proj, proj, tables, tables, after)


def _attention_bwd(proj, q_rot, k_rot, tables, sink, kv, attn, d_ain, after):
    s, d = proj.shape[0], sink.shape[0] * HEAD_DIM
    group, qw, n_pairs, nb = _attn_dims(s, d, kv)

    def body(sink_ref, q_ref, kc_ref, kp_ref, vc_ref, vp_ref, ga_ref, tc_ref, tp_ref, attn_ref, dain_ref, after_ref,
             dq_ref, dk_ref, dv_ref, dga_ref, dsink_ref, carry_k, carry_v, sink_acc):
        pair, blk = pl.program_id(0), pl.program_id(1)

        @pl.when(blk == 0)
        def _():
            carry_k[...] = jnp.zeros_like(carry_k)
            carry_v[...] = jnp.zeros_like(carry_v)
            sink_acc[...] = jnp.zeros_like(sink_acc)

        @pl.when(blk < nb)
        def _():
            tab_c, tab_p = tc_ref[...], tp_ref[...]
            q = q_ref[...].astype(F32)
            k_cur, k_prev = kc_ref[...].astype(F32), kp_ref[...].astype(F32)
            v_cur, v_prev = vc_ref[...], vp_ref[...]
            silu, silu_grad = _silu_and_grad(ga_ref[...].astype(F32))
            d_ain_v = dain_ref[...].astype(F32)
            dga_ref[...] = (d_ain_v * attn_ref[...].astype(F32) * silu_grad).astype(BF16)
            d_attn = d_ain_v * silu
            q_t = q.T
            d_attn_t = d_attn.T
            causal = _causal(BLOCK)
            dq_parts = []
            dk_t = {"cur": [], "prev": []}
            dv_t = {"cur": [], "prev": []}
            lane = lax.broadcasted_iota(jnp.int32, (BLOCK, LANES), 1)
            dsink = jnp.zeros((BLOCK, LANES), F32)
            for a in range(2):
                first = a * group
                qs = _stack_heads(q, first, group).astype(BF16)
                kc, kp = _head(k_cur, a).astype(BF16), _head(k_prev, a).astype(BF16)
                k_band = jnp.concatenate([_head(k_cur, a), _head(k_prev, a)], axis=0).astype(BF16)
                v_band = jnp.concatenate([_head(v_cur, a), _head(v_prev, a)], axis=0).astype(BF16)
                p_cur, p_prev, p, p_sink = _band_probs_by_head(qs, k_band, sink_ref, (2 * pair + a) * group, group,
                                                               causal, blk)
                do = _stack_heads(d_attn, first, group).astype(BF16)
                dp_band = _dot_nt(do, v_band)
                ds_parts, delta = [], []
                for h in range(group):
                    rows = slice(h * BLOCK, (h + 1) * BLOCK)
                    dp = jnp.where(causal, dp_band[rows, :BLOCK], dp_band[rows, BLOCK:])
                    delta.append(jnp.sum(p[rows] * dp, axis=-1, keepdims=True))
                    ds_parts.append(_split_band(p[rows] * (dp - delta[-1]), causal))
                ds_cur = jnp.concatenate([c for c, _ in ds_parts], axis=0)
                ds_prev = jnp.concatenate([v for _, v in ds_parts], axis=0)
                delta = jnp.concatenate(delta, axis=0)
                dqs = (jnp.dot(ds_cur, kc, preferred_element_type=F32)
                       + jnp.dot(ds_prev, kp, preferred_element_type=F32)) * ATTN_SCALE
                dq_parts += [dqs[h * BLOCK:(h + 1) * BLOCK] for h in range(group)]
                rows = lambda t: jnp.concatenate(
                    [t[(first + h) * HEAD_DIM:(first + h + 1) * HEAD_DIM] for h in range(group)], axis=1).astype(BF16)
                qs_t, do_t = rows(q_t), rows(d_attn_t)
                dk_t["cur"].append(jnp.dot(qs_t, ds_cur, preferred_element_type=F32))
                dk_t["prev"].append(jnp.dot(qs_t, ds_prev, preferred_element_type=F32))
                dv_t["cur"].append(jnp.dot(do_t, p_cur, preferred_element_type=F32))
                dv_t["prev"].append(jnp.dot(do_t, p_prev, preferred_element_type=F32))
                ds_sink = -(p_sink * delta)
                for h in range(group):
                    dsink = dsink + jnp.where(lane == first + h, ds_sink[h * BLOCK:(h + 1) * BLOCK], 0.0)
            sink_acc[...] += dsink
            dq_ref[...] = _rope_bwd(jnp.concatenate(dq_parts, axis=1), tab_c).astype(BF16)
            pair_block = lambda parts: jnp.concatenate(parts, axis=0).T
            dk_ref[...] = (carry_k[...] + _rope_bwd(pair_block(dk_t["prev"]), tab_p)).astype(BF16)
            dv_ref[...] = (carry_v[...] + pair_block(dv_t["prev"])).astype(BF16)
            carry_k[...] = _rope_bwd(pair_block(dk_t["cur"]), tab_c)
            carry_v[...] = pair_block(dv_t["cur"])

        @pl.when(blk == nb)
        def _():
            dk_ref[...] = carry_k[...].astype(BF16)
            dv_ref[...] = carry_v[...].astype(BF16)
            dsink_ref[0] = jnp.sum(sink_acc[...], axis=0, keepdims=True)

    v0, g0 = (d + kv) // LANES, (d + 2 * kv) // qw
    cur = lambda i: jnp.minimum(i, nb - 1)
    prev = lambda i: jnp.maximum(cur(i) - 1, 0)
    back = lambda i: jnp.maximum(i - 1, 0)
    q_spec = pl.BlockSpec((BLOCK, qw), lambda p, i: (cur(i), p))
    in_specs = [
        SMEM_SPEC,
        q_spec,
        pl.BlockSpec((BLOCK, LANES), lambda p, i: (cur(i), p)),
        pl.BlockSpec((BLOCK, LANES), lambda p, i: (prev(i), p)),
        pl.BlockSpec((BLOCK, LANES), lambda p, i: (cur(i), v0 + p)),
        pl.BlockSpec((BLOCK, LANES), lambda p, i: (prev(i), v0 + p)),
        pl.BlockSpec((BLOCK, qw), lambda p, i: (cur(i), g0 + p)),
        pl.BlockSpec((3, BLOCK, LANES), lambda p, i: (0, cur(i), 0)),
        pl.BlockSpec((3, BLOCK, LANES), lambda p, i: (0, prev(i), 0)),
        q_spec,
        q_spec,
        ANY_SPEC,
    ]
    kv_out = pl.BlockSpec((BLOCK, LANES), lambda p, i: (back(i), p))
    return pl.pallas_call(
        body, name="attention_bwd", grid=(n_pairs, nb + 1), in_specs=in_specs,
        out_specs=(q_spec, kv_out, kv_out, q_spec, pl.BlockSpec((1, 1, LANES), lambda p, i: (p, 0, 0))),
        out_shape=(_sds((s, d), BF16), _sds((s, kv), BF16), _sds((s, kv), BF16), _sds((s, d), BF16),
                   _sds((n_pairs, 1, LANES), F32)),
        scratch_shapes=[pltpu.VMEM((BLOCK, LANES), F32), pltpu.VMEM((BLOCK, LANES), F32),
                        pltpu.VMEM((BLOCK, LANES), F32)],
        compiler_params=_params("parallel", "arbitrary"),
    )(sink, q_rot, k_rot, k_rot, proj, proj, proj, tables, tables, attn, d_ain, after)


def _gmlp_core(gu, gv, ln_g, ln_b, w_ref, bias_t):
    xc = gv - jnp.mean(gv, axis=-1, keepdims=True)
    rstd = lax.rsqrt(jnp.mean(xc * xc, axis=-1, keepdims=True) + LN_EPS)
    xhat = xc * rstd
    vn = (xhat * ln_g + ln_b).astype(BF16)
    gd = gu.shape[1] // GMLP_GROUPS
    tri = (lax.broadcasted_iota(jnp.int32, (BLOCK, BLOCK), 0) >= lax.broadcasted_iota(jnp.int32, (BLOCK, BLOCK), 1))
    w_tri = [jnp.where(tri, w_ref[g], 0.0).astype(BF16) for g in range(GMLP_GROUPS)]
    mixed = jnp.concatenate(
        [jnp.dot(w_tri[g], vn[:, g * gd:(g + 1) * gd], preferred_element_type=F32) + bias_t[:, g:g + 1]
         for g in range(GMLP_GROUPS)], axis=1)
    return gu, xhat, rstd, vn, w_tri, tri, mixed


def _whole(shape):
    return pl.BlockSpec(shape, lambda i: tuple(0 for _ in shape))


def _gmlp_fwd(proj, col_u, d, w_s, bias_t, ln_g, ln_b, after):
    s = proj.shape[0]
    ns = SEGMENT_SPLIT

    def body(*refs):
        u, vg, gb = _cat(refs[:ns]), _cat(refs[ns:2 * ns]), _cat(refs[2 * ns:3 * ns])
        w_ref, bt_ref, lg_ref, lb_ref, after_ref, o_ref = refs[3 * ns:]
        gu, _, _, _, _, _, mixed = _gmlp_core(_gelu(u), _gelu(vg), lg_ref[...], lb_ref[...], w_ref, bt_ref[...])
        silu, _ = _silu_and_grad(gb)
        o_ref[...] = ((gu * mixed) * silu).astype(BF16)

    segs = [sp for j in range(3) for sp in _segment_specs(BLOCK, d, col_u + j * d)]
    return pl.pallas_call(
        body, name="gmlp_fwd", grid=(s // BLOCK,),
        in_specs=[*segs, _whole(w_s.shape), _whole(bias_t.shape), _whole((1, d)), _whole((1, d)), ANY_SPEC],
        out_specs=pl.BlockSpec((BLOCK, d), lambda i: (i, 0)), out_shape=_sds((s, d), BF16),
        compiler_params=_params("parallel"),
    )(*([proj] * (3 * ns)), w_s, bias_t, ln_g, ln_b, after)


def _gmlp_bwd(proj, col_u, d, w_s, bias_t, ln_g, ln_b, d_bin, after, d_proj):
    s = proj.shape[0]
    gd = d // GMLP_GROUPS
    ns = SEGMENT_SPLIT
    n_steps = s // BLOCK

    def body(*refs):
        u, vg, gb = _cat(refs[:ns]), _cat(refs[ns:2 * ns]), _cat(refs[2 * ns:3 * ns])
        (w_ref, bt_ref, lg_ref, lb_ref, dbin_ref, after_ref, wide_in, wide_ref, dw_ref, dbt_ref, dlg_ref, dlb_ref,
         buf, sems) = refs[3 * ns:]

        @pl.when(pl.program_id(0) == 0)
        def _():
            dw_ref[...] = jnp.zeros_like(dw_ref)
            dbt_ref[...] = jnp.zeros_like(dbt_ref)
            dlg_ref[...] = jnp.zeros_like(dlg_ref)
            dlb_ref[...] = jnp.zeros_like(dlb_ref)

        ln_g = lg_ref[...]
        (gu, gu_grad), (gv, gv_grad) = _gelu_and_grad(u), _gelu_and_grad(vg)
        gu, xhat, rstd, vn, w_tri, tri, mixed = _gmlp_core(gu, gv, ln_g, lb_ref[...], w_ref, bt_ref[...])
        silu, silu_grad = _silu_and_grad(gb)
        d_bin_v = dbin_ref[...].astype(F32)
        d_sg = d_bin_v * silu
        d_gate = (d_bin_v * (gu * mixed) * silu_grad).astype(BF16)
        d_u = (d_sg * mixed * gu_grad).astype(BF16)
        d_mixed = d_sg * gu
        d_mixed_b = d_mixed.astype(BF16)
        d_vn, d_bias = [], []
        for g in range(GMLP_GROUPS):
            dm_g = d_mixed_b[:, g * gd:(g + 1) * gd]
            d_bias.append(jnp.sum(d_mixed[:, g * gd:(g + 1) * gd], axis=-1, keepdims=True))
            dw = lax.dot_general(dm_g, vn[:, g * gd:(g + 1) * gd], (((1,), (1,)), ((), ())),
                                 preferred_element_type=F32)
            dw_ref[g] += jnp.where(tri, dw, 0.0)
            d_vn.append(lax.dot_general(w_tri[g], dm_g, (((0,), (0,)), ((), ())), preferred_element_type=F32))
        dbt_ref[...] += jnp.concatenate(d_bias, axis=1)
        d_vn = jnp.concatenate(d_vn, axis=1)
        dlg_ref[...] += jnp.sum(d_vn * xhat, axis=0, keepdims=True)
        dlb_ref[...] += jnp.sum(d_vn, axis=0, keepdims=True)
        d_xhat = d_vn * ln_g
        d_gv = rstd * (d_xhat - jnp.mean(d_xhat, axis=-1, keepdims=True)
                       - xhat * jnp.mean(d_xhat * xhat, axis=-1, keepdims=True))
        d_v = (d_gv * gv_grad).astype(BF16)

        def fill(out):
            out[:, :d] = d_u
            out[:, d:2 * d] = d_v
            out[:, 2 * d:] = d_gate

        _write_behind(pl.program_id(0), n_steps, buf, sems, wide_ref, BLOCK, col_u, fill)

    segs = [sp for j in range(3) for sp in _segment_specs(BLOCK, d, col_u + j * d)]
    return pl.pallas_call(
        body, name="gmlp_bwd", grid=(n_steps,),
        in_specs=[*segs, _whole(w_s.shape), _whole(bias_t.shape), _whole((1, d)), _whole((1, d)),
                  pl.BlockSpec((BLOCK, d), lambda i: (i, 0)), ANY_SPEC, ANY_SPEC],
        out_specs=(ANY_SPEC, _whole(w_s.shape), _whole(bias_t.shape), _whole((1, d)), _whole((1, d))),
        out_shape=(_sds(d_proj.shape, BF16), _sds(w_s.shape, F32), _sds(bias_t.shape, F32), _sds((1, d), F32),
                   _sds((1, d), F32)),
        input_output_aliases={3 * ns + 6: 0},
        scratch_shapes=[pltpu.VMEM((2, BLOCK, 3 * d), BF16), pltpu.SemaphoreType.DMA((2,))],
        compiler_params=_params("arbitrary"),
    )(*([proj] * (3 * ns)), w_s, bias_t, ln_g, ln_b, d_bin, after, d_proj)


def _pack(parts):
    rows = []
    tile = SUBLANES * LANES
    for p in parts:
        flat = p.astype(F32).reshape(-1)
        padded = -(-flat.shape[0] // tile) * tile
        rows.append(jnp.pad(flat, (0, padded - flat.shape[0])).reshape(-1, LANES))
    return jnp.concatenate(rows, axis=0)


def _unpack(packed, shapes):
    out, row = [], 0
    tile = SUBLANES * LANES
    for shape in shapes:
        size = math.prod(shape)
        n_rows = -(-size // tile) * SUBLANES
        out.append(packed[row:row + n_rows].reshape(-1)[:size].reshape(shape))
        row += n_rows
    return out


def kernel(x, positions, norm_g, w_in, attn_sink, gmlp_ln_g, gmlp_ln_b, w_spatial, b_spatial, w_up_attn, w_up_gmlp, w_out, final_norm_g, loss_target, m_norm_g, m_w_in, m_attn_sink, m_gmlp_ln_g, m_gmlp_ln_b, m_w_spatial, m_b_spatial, m_w_up_attn, m_w_up_gmlp, m_w_out, m_final_norm_g, v_norm_g, v_w_in, v_attn_sink, v_gmlp_ln_g, v_gmlp_ln_b, v_w_spatial, v_b_spatial, v_w_up_attn, v_w_up_gmlp, v_w_out, v_final_norm_g):
    x2d, target = x[0], loss_target[0]
    s, d = x2d.shape
    n_q_heads = attn_sink.shape[1]
    cw = w_in.shape[2]
    rw = w_up_attn.shape[1]
    kv = (cw * N_DEV - 7 * d) // 2
    col_u, col_m = 2 * d + 2 * kv, 5 * d + 2 * kv
    final_g = final_norm_g.reshape(1, d)
    sink = attn_sink[0]
    w_s = w_spatial[0]
    bias_t = b_spatial[0].T
    mx, my, mc = _mesh_pos()
    pos = jnp.stack([mx, my, mc]).astype(jnp.int32)
    chips = jnp.arange(N_CHIPS, dtype=jnp.int32)

    def one_block(fn):
        return jnp.reshape(fn(mx, my, mc), (1,)).astype(jnp.int32)

    w_in_b = _cast_into_slot(w_in[0], pos, "cast_w_in")
    squares = [_cast_into_slot(w[0], pos, "cast_" + nm)
               for nm, w in (("w_up_attn", w_up_attn), ("w_up_gmlp", w_up_gmlp), ("w_out", w_out))]
    to_sibling = [_Copy(0, _slot, 0, _slot, _sibling)]
    ici = [[_Copy(0, _slot, 0, _slot, lambda x, y, c, chip=chip: (*chip(x, y, c), c))] for chip in _ICI_STAGES[:2]]
    relayed = lambda x, y, c: _slot(*_ICI_STAGES[1](x, y, c), c)
    ici.append([_Copy(0, relayed, 0, relayed, lambda x, y, c: (*_ICI_STAGES[0](x, y, c), c))])
    passes = []
    for chip in _ICI_STAGES:
        landed = lambda x, y, c, chip=chip: _slot(*chip(x, y, c), c)
        passes.append([_Copy(0, landed, 0, landed, _sibling)])
    sib_sems = _rdma_start("w_in_sibling_start", [w_in_b], to_sibling)
    ici_sems = _rdma_start("w_in_ici0_start", sib_sems[2], ici[0])
    h = _rmsnorm_fwd(x2d, norm_g, ici_sems[3])
    proj = _project(h, ici_sems[2][0], one_block(_slot), "projection_own")
    w_blocks = _rdma_wait("w_in_sibling_wait", ici_sems[2], sib_sems[0], sib_sems[1], to_sibling, proj)
    proj = _project(h, w_blocks[0], one_block(lambda x, y, c: _slot(x, y, 1 - c)), "projection_sibling", proj=proj)
    w_blocks = _rdma_wait("w_in_ici0_wait", w_blocks, ici_sems[0], ici_sems[1], ici[0], proj)
    first = _gather_first_copies(3)
    for k, chip in enumerate(_ICI_STAGES):
        if k + 1 < len(_ICI_STAGES):
            ici_sems = _rdma_start("w_in_ici%d_start" % (k + 1), w_blocks, ici[k + 1])
            w_blocks = ici_sems[2]
        else:
            send1, recv1, thru, _ = _rdma_start("gather_squares_start", squares + w_blocks, first)
            squares, w_blocks = thru[:3], thru[3:]
        pass_sems = _rdma_start("w_in_pass%d_start" % k, w_blocks, passes[k])
        proj = _project(h, pass_sems[2][0], one_block(lambda x, y, c, chip=chip: _slot(*chip(x, y, c), c)),
                        "projection_ici%d" % k, proj=proj, after=pass_sems[3])
        w_blocks = _rdma_wait("w_in_pass%d_wait" % k, pass_sems[2], pass_sems[0], pass_sems[1], passes[k], proj)
        proj = _project(h, w_blocks[0], one_block(lambda x, y, c, chip=chip: _slot(*chip(x, y, 1 - c), 1 - c)),
                        "projection_pass%d" % k, proj=proj)
        if k + 1 < len(_ICI_STAGES):
            w_blocks = _rdma_wait("w_in_ici%d_wait" % (k + 1), w_blocks, ici_sems[0], ici_sems[1], ici[k + 1], proj)
    w_in_b = w_blocks[0]

    tables = _rope_tables(positions[0])
    attn, a_in, q_rot, k_rot = _attention_fwd(proj, tables, sink, kv, proj)
    squares = _rdma_wait("gather_squares_wait", squares, send1, recv1, first, attn)
    passed = _gather_pass_copies(3)
    send2, recv2, squares, token = _rdma_start("pass_squares_start", squares, passed)
    b_in = _gmlp_fwd(proj, col_u, d, w_s, bias_t, gmlp_ln_g, gmlp_ln_b, token)
    squares = _rdma_wait("pass_squares_wait", squares, send2, recv2, passed, b_in)
    w_ua, w_ug, w_o = [w.reshape(N_DEV * rw, d) for w in squares]
    y_a = _matmul(a_in, w_ua, "nn", BF16, "up_attn")
    y_b = _matmul(b_in, w_ug, "nn", BF16, "up_gmlp")
    merged = _merge_fwd(y_a, y_b, proj, col_m)
    x_out = _matmul(merged, w_o, "nn", F32, "out_proj", res=x2d, tn=512)
    loss_p, d_final_g, dx2, dx2_b = _loss_and_final_norm_bwd(x_out, target, final_g)

    d_merged = _matmul(dx2_b, w_o, "nt", BF16, "d_merged")
    g_w_out = _matmul(merged, dx2_b, "tn", BF16, "g_w_out")
    d_ya, d_yb, d_proj = _merge_bwd(d_merged, y_a, y_b, proj, col_m, lax.empty(proj.shape, BF16))
    d_ain = _matmul(d_ya, w_ua, "nt", BF16, "d_a_in")
    g_w_ua = _matmul(a_in, d_ya, "tn", BF16, "g_w_up_attn")
    d_bin = _matmul(d_yb, w_ug, "nt", BF16, "d_b_in")
    g_w_ug = _matmul(b_in, d_yb, "tn", BF16, "g_w_up_gmlp")
    sq_grads = [g.reshape(N_DEV, rw, d) for g in (g_w_ua, g_w_ug, g_w_out)]
    sq_land = [lax.empty((N_CHIPS, rw, d), BF16) for _ in sq_grads]
    pairs_sq = _pair_copies_strided(3)
    arrays = [a for gl in zip(sq_grads, sq_land) for a in gl]
    send3, recv3, arrays, token = _rdma_start("pair_squares_start", arrays, pairs_sq)
    d_q, d_k, d_v, d_ga, d_sink = _attention_bwd(proj, q_rot, k_rot, tables, sink, kv, attn, d_ain, token)
    arrays = _rdma_wait("pair_squares_wait", arrays, send3, recv3, pairs_sq, d_q)
    sq_sums = [_pair_sum(arrays[2 * a], arrays[2 * a + 1], pos, "pair_sum_%d" % a) for a in range(3)]
    sq_land2 = [lax.empty((N_CHIPS - 1, rw, d), BF16) for _ in sq_sums]
    chip_sq = _chip_sum_copies(3)
    arrays = [a for gl in zip(sq_sums, sq_land2) for a in gl]
    send4, recv4, sq_arrays, token = _rdma_start("chip_squares_start", arrays, chip_sq)
    d_proj, d_w_s, d_bias_t, d_ln_g, d_ln_b = _gmlp_bwd(proj, col_u, d, w_s, bias_t, gmlp_ln_g, gmlp_ln_b, d_bin, token,
                                                        d_proj)
    for piece, col0, nm in ((d_q, 0, "d_q"), (d_k, d, "d_k"), (d_v, d + kv, "d_v"), (d_ga, d + 2 * kv, "d_gate")):
        d_proj = _place(d_proj, piece, col0, "place_" + nm)

    half = N_CHIPS // 2
    pairs_in = [_Copy(0, lambda x, y, c, q=q: q, 1, lambda x, y, c, q=q: q, _sibling) for q in range(half)]
    sent, token = [], None
    for j in range(2):
        g_sib = _grad_w_in_blocks(h, d_proj, 2 * chips[j * half:(j + 1) * half] + 1 - mc, cw,
                                  "g_w_in_sibling%d" % j, after=token)
        sent.append(_rdma_start("pair_w_in%d_start" % j, [g_sib, lax.empty((half, d, cw), BF16)], pairs_in))
        token = sent[-1][3]
    in_sums = None
    for j in range(2):
        send5, recv5, arrays, _ = sent[j]
        arrays = _rdma_wait("pair_w_in%d_wait" % j, arrays, send5, recv5, pairs_in, token if j == 0 else in_sums)
        in_sums = _grad_w_in_blocks(h, d_proj, 2 * chips[j * half:(j + 1) * half] + mc, cw, "g_w_in_own%d" % j,
                                    slots=N_CHIPS, slot0=j * half, prev=in_sums, init=arrays[1], tm=512)
    chip_in = _chip_sum_copies(1)
    send6, recv6, in_arrays, token = _rdma_start(
        "chip_w_in_start", [in_sums, lax.empty((N_CHIPS - 1, d, cw), BF16)], chip_in)
    d_h = _d_hidden(d_proj, w_in_b, after=token)
    grad_x, d_norm_g = _input_grad(d_h, x2d, norm_g, dx2)

    sq_arrays = _rdma_wait("chip_squares_wait", sq_arrays, send4, recv4, chip_sq, grad_x)
    big = {}
    for a, (name, w, m, v) in enumerate((("w_up_attn", w_up_attn, m_w_up_attn, v_w_up_attn),
                                         ("w_up_gmlp", w_up_gmlp, m_w_up_gmlp, v_w_up_gmlp),
                                         ("w_out", w_out, m_w_out, v_w_out))):
        big[name] = [r[None] for r in _reduce_adamw(sq_arrays[2 * a], sq_arrays[2 * a + 1], w[0], m[0], v[0], pos,
                                                    "adamw_" + name)]

    heads_per_pair = 2 * n_q_heads // (kv // HEAD_DIM)
    g_sink = d_sink[:, 0, :heads_per_pair].reshape(1, n_q_heads)
    small_w = [norm_g, attn_sink, gmlp_ln_g, gmlp_ln_b, w_spatial, b_spatial, final_norm_g]
    small_m = [m_norm_g, m_attn_sink, m_gmlp_ln_g, m_gmlp_ln_b, m_w_spatial, m_b_spatial, m_final_norm_g]
    small_v = [v_norm_g, v_attn_sink, v_gmlp_ln_g, v_gmlp_ln_b, v_w_spatial, v_b_spatial, v_final_norm_g]
    small_g = [d_norm_g, g_sink, d_ln_g, d_ln_b, d_w_s[None], d_bias_t.T[None], d_final_g.reshape(d)]
    loss_pad = jnp.zeros((1,), F32)
    shapes = [w.shape for w in small_w] + [(1,)]
    packed = _small_allreduce_adamw(_pack(small_g + [loss_p[0, :1]]), _pack(small_w + [loss_pad]),
                                    _pack(small_m + [loss_pad]), _pack(small_v + [loss_pad]))
    sg, sd, sm, sv = [_unpack(p, shapes) for p in packed]
    loss = sg[-1][0]
    in_arrays = _rdma_wait("chip_w_in_wait", in_arrays, send6, recv6, chip_in, packed[0])
    big["w_in"] = [r[None] for r in _reduce_adamw(in_arrays[0], in_arrays[1], w_in[0], m_w_in[0], v_w_in[0], pos,
                                                  "adamw_w_in")]

    names = ["norm_g", "w_in", "attn_sink", "gmlp_ln_g", "gmlp_ln_b", "w_spatial", "b_spatial", "w_up_attn",
             "w_up_gmlp", "w_out", "final_norm_g"]
    small_names = ["norm_g", "attn_sink", "gmlp_ln_g", "gmlp_ln_b", "w_spatial", "b_spatial", "final_norm_g"]
    outs = [[], [], [], []]
    for nm in names:
        for k in range(4):
            if nm in big:
                outs[k].append(big[nm][k])
            else:
                outs[k].append((sg, sd, sm, sv)[k][small_names.index(nm)])
    return (loss, grad_x[None], *outs[0], *outs[1], *outs[2], *outs[3])
```

```python
import math
from typing import Callable, NamedTuple

import jax
import jax.numpy as jnp
from jax import lax
from jax.experimental import pallas as pl
from jax.experimental.pallas import tpu as pltpu

F32 = jnp.float32
BF16 = jnp.bfloat16
MESH = pl.DeviceIdType.MESH

N_DEV = 8
N_CHIPS = 4
HEAD_DIM = 64
BLOCK = 128
ROPE_DIM = 16
ROPE_HALF = ROPE_DIM // 2
ROPE_THETA = 500000.0
GMLP_GROUPS = 8
NORM_EPS = 1e-5
LN_EPS = 1e-5
ATTN_SCALE = HEAD_DIM ** -0.5
LANES = 128
SUBLANES = 8
VMEM_LIMIT = 48 * 1024 * 1024
VMEM_LIMIT_WIDE = 56 * 1024 * 1024
DOT_COLS = 1024
SEGMENT_SPLIT = 4

ADAM_LR = 0.001
ADAM_B1 = 0.9
ADAM_B2 = 0.999
ADAM_EPS = 1e-08
ADAM_WD = 0.01
ADAM_STEP = 10

GELU_C = math.sqrt(2.0 / math.pi)
GELU_K = 0.044715

HBM_SPEC = pl.BlockSpec(memory_space=pltpu.HBM)
ANY_SPEC = pl.BlockSpec(memory_space=pl.ANY)
SEM_SPEC = pl.BlockSpec(memory_space=pltpu.SEMAPHORE)
VMEM_SPEC = pl.BlockSpec(memory_space=pltpu.VMEM)
SMEM_SPEC = pl.BlockSpec(memory_space=pltpu.SMEM)


def _sds(shape, dtype):
    return jax.ShapeDtypeStruct(shape, dtype)


def _params(*sem, vmem=VMEM_LIMIT):
    return pltpu.CompilerParams(dimension_semantics=sem or None, vmem_limit_bytes=vmem)


def _gelu(x):
    return 0.5 * x * (1.0 + jnp.tanh(GELU_C * (x + GELU_K * x * x * x)))


def _gelu_and_grad(x):
    x2 = x * x
    t = jnp.tanh(x * (GELU_C + (GELU_C * GELU_K) * x2))
    half = 0.5 + 0.5 * t
    return x * half, half + (x * (1.0 - t * t)) * (0.5 * GELU_C + (1.5 * GELU_C * GELU_K) * x2)


def _silu_and_grad(x):
    s = jax.nn.sigmoid(x)
    return x * s, s * (1.0 + x * (1.0 - s))


def _adamw(w, g, m, v):
    m = ADAM_B1 * m + (1.0 - ADAM_B1) * g
    v = ADAM_B2 * v + (1.0 - ADAM_B2) * (g * g)
    m_hat = m / (1.0 - ADAM_B1 ** ADAM_STEP)
    v_hat = v / (1.0 - ADAM_B2 ** ADAM_STEP)
    delta = -ADAM_LR * (m_hat / (jnp.sqrt(v_hat) + ADAM_EPS) + ADAM_WD * w)
    return delta, m, v


def _mesh_pos():
    return lax.axis_index("x"), lax.axis_index("y"), lax.axis_index("c")


def _slot(x, y, c):
    return 4 * x + 2 * y + c


def _chip(x, y):
    return 2 * x + y


def _sibling(x, y, c):
    return (x, y, 1 - c)


_OTHER_CHIPS = (lambda x, y: (1 - x, y), lambda x, y: (x, 1 - y), lambda x, y: (1 - x, 1 - y))
_ICI_STAGES = (lambda x, y, c: (x ^ c, y ^ (1 - c)), lambda x, y, c: (x ^ (1 - c), y ^ c),
               lambda x, y, c: (1 - x, 1 - y))


class _Copy(NamedTuple):
    src: int
    src_slot: Callable
    dst: int
    dst_slot: Callable
    peer: Callable


def _descriptor(refs, send_sems, recv_sems, k, cp):
    pos = _mesh_pos()
    return pltpu.make_async_remote_copy(
        src_ref=refs[cp.src].at[cp.src_slot(*pos)], dst_ref=refs[cp.dst].at[cp.dst_slot(*pos)],
        send_sem=send_sems.at[k], recv_sem=recv_sems.at[k], device_id=cp.peer(*pos), device_id_type=MESH)


def _gather_first_copies(n_arrays):
    copies = []
    for a in range(n_arrays):
        copies.append(_Copy(a, _slot, a, _slot, _sibling))
        for chip in _OTHER_CHIPS:
            copies.append(_Copy(a, _slot, a, _slot, lambda x, y, c, chip=chip: (*chip(x, y), c)))
    return copies


def _gather_pass_copies(n_arrays):
    copies = []
    for a in range(n_arrays):
        for chip in _OTHER_CHIPS:
            src = lambda x, y, c, chip=chip: _slot(*chip(x, y), c)
            copies.append(_Copy(a, src, a, src, _sibling))
    return copies


def _pair_copies_strided(n_sets):
    copies = []
    for a in range(n_sets):
        for q in range(N_CHIPS):
            copies.append(_Copy(2 * a, lambda x, y, c, q=q: 2 * q + 1 - c, 2 * a + 1, lambda x, y, c, q=q: q, _sibling))
    return copies


def _chip_sum_copies(n_sets):
    copies = []
    for a in range(n_sets):
        for k, chip in enumerate(_OTHER_CHIPS):
            copies.append(_Copy(2 * a, lambda x, y, c, chip=chip: _chip(*chip(x, y)), 2 * a + 1,
                                lambda x, y, c, k=k: k, lambda x, y, c, chip=chip: (*chip(x, y), c)))
    return copies


def _rdma_start(name, arrays, copies):
    n, nc = len(arrays), len(copies)

    def body(*refs):
        in_refs = refs[:n]
        send_sems, recv_sems = refs[n], refs[n + 1]
        token = refs[2 * n + 2]
        for k, cp in enumerate(copies):
            _descriptor(in_refs, send_sems, recv_sems, k, cp).start()
        token[...] = jnp.zeros_like(token)

    out = pl.pallas_call(
        body, name=name,
        out_shape=(pltpu.SemaphoreType.DMA((nc,)), pltpu.SemaphoreType.DMA((nc,)),
                   *[pltpu.HBM(a.shape, a.dtype) for a in arrays], _sds((SUBLANES, LANES), F32)),
        in_specs=[HBM_SPEC] * n, out_specs=(SEM_SPEC, SEM_SPEC, *([HBM_SPEC] * n), VMEM_SPEC),
        input_output_aliases={i: i + 2 for i in range(n)},
        compiler_params=pltpu.CompilerParams(has_side_effects=pltpu.SideEffectType.DATAFLOW_SIDE_EFFECTING),
    )(*[pltpu.with_memory_space_constraint(a, pltpu.HBM) for a in arrays])
    return out[0], out[1], list(out[2:2 + n]), out[2 + n]


def _rdma_wait(name, arrays, send_sems, recv_sems, copies, after):
    n = len(arrays)

    def body(*refs):
        in_refs = refs[:n]
        send_ref, recv_ref = refs[n], refs[n + 1]
        for k, cp in enumerate(copies):
            d = _descriptor(in_refs, send_ref, recv_ref, k, cp)
            d.wait_send()
            d.wait_recv()

    out = pl.pallas_call(
        body, name=name, out_shape=tuple(pltpu.HBM(a.shape, a.dtype) for a in arrays),
        in_specs=[HBM_SPEC] * n + [SEM_SPEC, SEM_SPEC, ANY_SPEC], out_specs=tuple([HBM_SPEC] * n),
        input_output_aliases={i: i for i in range(n)},
        compiler_params=pltpu.CompilerParams(has_side_effects=pltpu.SideEffectType.DATAFLOW_SIDE_EFFECTING),
    )(*arrays, send_sems, recv_sems, after)
    return list(out)


def _cast_into_slot(w, pos, name, after=None):
    rows, cols = w.shape
    tr = min(rows, 256)

    def body(pos_ref, w_ref, *rest):
        rest[-1][...] = w_ref[...].astype(BF16)

    in_specs, args = [pl.BlockSpec((tr, cols), lambda i, p: (i, 0))], [pos, w]
    if after is not None:
        in_specs.append(ANY_SPEC)
        args.append(after)
    return pl.pallas_call(
        body, name=name,
        grid_spec=pltpu.PrefetchScalarGridSpec(
            num_scalar_prefetch=1, grid=(rows // tr,), in_specs=in_specs,
            out_specs=pl.BlockSpec((None, tr, cols), lambda i, p: (_slot(p[0], p[1], p[2]), i, 0))),
        out_shape=_sds((N_DEV, rows, cols), BF16), compiler_params=_params("parallel"),
    )(*args)


def _pair_sum(g, land, pos, name):
    _, rows, cols = land.shape
    tr = min(rows, 128)
    strided = g.shape[0] == N_DEV

    def body(pos_ref, g_ref, l_ref, o_ref):
        o_ref[...] = (g_ref[...].astype(F32) + l_ref[...].astype(F32)).astype(BF16)

    g_map = (lambda q, i, p: (2 * q + p[2], i, 0)) if strided else (lambda q, i, p: (q, i, 0))
    blk = pl.BlockSpec((None, tr, cols), lambda q, i, p: (q, i, 0))
    return pl.pallas_call(
        body, name=name,
        grid_spec=pltpu.PrefetchScalarGridSpec(
            num_scalar_prefetch=1, grid=(N_CHIPS, rows // tr),
            in_specs=[pl.BlockSpec((None, tr, cols), g_map), blk], out_specs=blk),
        out_shape=_sds((N_CHIPS, rows, cols), BF16), compiler_params=_params("parallel", "parallel"),
    )(pos, g, land)


def _reduce_adamw(sums, land, w, m, v, pos, name):
    rows, cols = w.shape
    tr = min(rows, 64)

    def body(pos_ref, s_ref, l_ref, w_ref, m_ref, v_ref, g_ref, d_ref, nm_ref, nv_ref):
        g = s_ref[...].astype(F32)
        for k in range(N_CHIPS - 1):
            g = g + l_ref[k].astype(F32)
        delta, nm, nv = _adamw(w_ref[...], g, m_ref[...], v_ref[...])
        g_ref[...] = g
        d_ref[...] = delta
        nm_ref[...] = nm
        nv_ref[...] = nv

    spec = pl.BlockSpec((tr, cols), lambda i, p: (i, 0))
    return pl.pallas_call(
        body, name=name,
        grid_spec=pltpu.PrefetchScalarGridSpec(
            num_scalar_prefetch=1, grid=(rows // tr,),
            in_specs=[pl.BlockSpec((None, tr, cols), lambda i, p: (_chip(p[0], p[1]), i, 0)),
                      pl.BlockSpec((N_CHIPS - 1, tr, cols), lambda i, p: (0, i, 0)), spec, spec, spec],
            out_specs=(spec, spec, spec, spec)),
        out_shape=tuple([_sds((rows, cols), F32)] * 4), compiler_params=_params("parallel"),
    )(pos, sums, land, w, m, v)


def _small_allreduce_adamw(g, w, m, v):
    rows = g.shape[0]

    def body(g_ref, w_ref, m_ref, v_ref, gs_ref, d_ref, nm_ref, nv_ref, all_ref, send_sems, recv_sems):
        x, y, c = _mesh_pos()
        me = _slot(x, y, c)
        copies = []
        for k in range(1, N_DEV):
            peer = (x ^ (k >> 2), y ^ ((k >> 1) & 1), c ^ (k & 1))
            copies.append(pltpu.make_async_remote_copy(
                src_ref=g_ref, dst_ref=all_ref.at[me], send_sem=send_sems.at[k - 1],
                recv_sem=recv_sems.at[k - 1], device_id=peer, device_id_type=MESH))
        for cp in copies:
            cp.start()
        all_ref[me] = g_ref[...]
        for cp in copies:
            cp.wait_recv()
        total = all_ref[0]
        for s in range(1, N_DEV):
            total = total + all_ref[s]
        delta, nm, nv = _adamw(w_ref[...], total, m_ref[...], v_ref[...])
        gs_ref[...] = total
        d_ref[...] = delta
        nm_ref[...] = nm
        nv_ref[...] = nv
        for cp in copies:
            cp.wait_send()

    return pl.pallas_call(
        body, name="small_allreduce_adamw", out_shape=tuple([_sds((rows, LANES), F32)] * 4),
        in_specs=[VMEM_SPEC] * 4, out_specs=tuple([VMEM_SPEC] * 4),
        scratch_shapes=[pltpu.VMEM((N_DEV, rows, LANES), F32), pltpu.SemaphoreType.DMA((7,)),
                        pltpu.SemaphoreType.DMA((7,))],
    )(g, w, m, v)


_DOT_DIMS = {"nn": ((1,), (0,)), "nt": ((1,), (1,)), "tn": ((0,), (0,))}


def _dot(a, b, mode):
    return lax.dot_general(a, b, (_DOT_DIMS[mode], ((), ())), preferred_element_type=F32)


def _col_chunks(cols):
    return [(c0, min(c0 + DOT_COLS, cols)) for c0 in range(0, cols, DOT_COLS)]


def _matmul(a, b, mode, out_dtype, name, *, res=None, tm=1024, tn=1024):
    if mode == "tn":
        kdim, mdim = a.shape
    else:
        mdim, kdim = a.shape
    ndim = b.shape[0] if mode == "nt" else b.shape[1]
    tm, tn = min(tm, mdim), min(tn, ndim)
    assert mdim % tm == 0 and ndim % tn == 0, (name, mdim, ndim)

    def body(*refs):
        out = _dot(refs[0][...], refs[1][...], mode)
        if res is not None:
            out = out + refs[2][...]
        refs[-1][...] = out.astype(out_dtype)

    a_spec = pl.BlockSpec((kdim, tm), lambda i, j: (0, i)) if mode == "tn" else pl.BlockSpec((tm, kdim), lambda i, j: (i, 0))
    b_spec = pl.BlockSpec((tn, kdim), lambda i, j: (j, 0)) if mode == "nt" else pl.BlockSpec((kdim, tn), lambda i, j: (0, j))
    o_spec = pl.BlockSpec((tm, tn), lambda i, j: (i, j))
    in_specs, args = [a_spec, b_spec], [a, b]
    if res is not None:
        in_specs.append(o_spec)
        args.append(res)
    return pl.pallas_call(
        body, name=name, grid=(mdim // tm, ndim // tn), in_specs=in_specs, out_specs=o_spec,
        out_shape=_sds((mdim, ndim), out_dtype), compiler_params=_params("parallel", "parallel"),
    )(*args)


def _project(h, w_blocks, block_ids, name, *, proj=None, after=None, tm=1024, tk=1024):
    s, d = h.shape
    _, _, cw = w_blocks.shape
    n = block_ids.shape[0]
    tm, tk = min(tm, s), min(tk, d)
    nk = d // tk

    def body(ids_ref, h_ref, w_ref, *rest):
        o_ref, acc_ref = rest[-2], rest[-1]
        k = pl.program_id(2)

        @pl.when(k == 0)
        def _():
            acc_ref[...] = jnp.zeros_like(acc_ref)

        for c0, c1 in _col_chunks(cw):
            acc_ref[:, c0:c1] += _dot(h_ref[...], w_ref[:, c0:c1], "nn")

        @pl.when(k == nk - 1)
        def _():
            o_ref[...] = acc_ref[...].astype(BF16)

    in_specs = [pl.BlockSpec((tm, tk), lambda j, i, k, ids: (i, k)),
                pl.BlockSpec((None, tk, cw), lambda j, i, k, ids: (ids[j], k, 0))]
    args = [block_ids, h, w_blocks]
    aliases = {}
    if proj is not None:
        in_specs.append(ANY_SPEC)
        args.append(proj)
        aliases = {3: 0}
    if after is not None:
        in_specs.append(ANY_SPEC)
        args.append(after)
    return pl.pallas_call(
        body, name=name,
        grid_spec=pltpu.PrefetchScalarGridSpec(
            num_scalar_prefetch=1, grid=(n, s // tm, d // tk), in_specs=in_specs,
            out_specs=pl.BlockSpec((tm, cw), lambda j, i, k, ids: (i, ids[j])),
            scratch_shapes=[pltpu.VMEM((tm, cw), F32)]),
        out_shape=_sds((s, N_DEV * cw), BF16), input_output_aliases=aliases,
        compiler_params=_params("arbitrary", "arbitrary", "arbitrary", vmem=VMEM_LIMIT_WIDE),
    )(*args)


def _grad_w_in_blocks(h, d_proj, block_ids, cw, name, *, slots=None, slot0=0, prev=None, init=None, after=None,
                      tm=1024, tk=1024):
    s, d = h.shape
    n = block_ids.shape[0]
    slots = n if slots is None else slots
    tm, tk = min(tm, d), min(tk, s)
    nk = s // tk

    def body(ids_ref, h_ref, g_ref, *rest):
        o_ref, acc_ref = rest[-2], rest[-1]
        k = pl.program_id(2)

        @pl.when(k == 0)
        def _():
            acc_ref[...] = jnp.zeros_like(acc_ref) if init is None else rest[0][...].astype(F32)

        for c0, c1 in _col_chunks(cw):
            acc_ref[:, c0:c1] += _dot(h_ref[...], g_ref[:, c0:c1], "tn")

        @pl.when(k == nk - 1)
        def _():
            o_ref[...] = acc_ref[...].astype(BF16)

    in_specs = [pl.BlockSpec((tk, tm), lambda q, i, k, ids: (k, i)),
                pl.BlockSpec((tk, cw), lambda q, i, k, ids: (k, ids[q]))]
    args = [block_ids, h, d_proj]
    aliases = {}
    if init is not None:
        in_specs.append(pl.BlockSpec((None, tm, cw), lambda q, i, k, ids: (q, i, 0)))
        args.append(init)
    if prev is not None:
        aliases = {len(args): 0}
        in_specs.append(ANY_SPEC)
        args.append(prev)
    if after is not None:
        in_specs.append(ANY_SPEC)
        args.append(after)
    return pl.pallas_call(
        body, name=name,
        grid_spec=pltpu.PrefetchScalarGridSpec(
            num_scalar_prefetch=1, grid=(n, d // tm, nk), in_specs=in_specs,
            out_specs=pl.BlockSpec((None, tm, cw), lambda q, i, k, ids: (slot0 + q, i, 0)),
            scratch_shapes=[pltpu.VMEM((tm, cw), F32)]),
        out_shape=_sds((slots, d, cw), BF16), input_output_aliases=aliases,
        compiler_params=_params("parallel", "parallel", "arbitrary", vmem=VMEM_LIMIT_WIDE),
    )(*args)


def _d_hidden(d_proj, w_blocks, after=None, *, tm=1024, tn=1024):
    s = d_proj.shape[0]
    nb, d, cw = w_blocks.shape
    tm, tn = min(tm, s), min(tn, d)

    def body(g_ref, w_ref, *rest):
        o_ref = rest[-1]
        k = pl.program_id(2)

        @pl.when(k == 0)
        def _():
            o_ref[...] = jnp.zeros_like(o_ref)

        o_ref[...] += _dot(g_ref[...], w_ref[...], "nt")

    in_specs = [pl.BlockSpec((tm, cw), lambda i, j, k: (i, k)),
                pl.BlockSpec((None, tn, cw), lambda i, j, k: (k, j, 0))]
    args = [d_proj, w_blocks]
    if after is not None:
        in_specs.append(ANY_SPEC)
        args.append(after)
    return pl.pallas_call(
        body, name="d_h", grid=(s // tm, d // tn, nb), in_specs=in_specs,
        out_specs=pl.BlockSpec((tm, tn), lambda i, j, k: (i, j)), out_shape=_sds((s, d), F32),
        compiler_params=_params("parallel", "parallel", "arbitrary", vmem=VMEM_LIMIT_WIDE),
    )(*args)


def _row_tile(rows):
    return min(rows, 128)


def _segment_specs(rows, d, col0):
    w = d // SEGMENT_SPLIT
    assert col0 % w == 0
    return [pl.BlockSpec((rows, w), lambda i, t=t: (i, col0 // w + t)) for t in range(SEGMENT_SPLIT)]


def _cat(refs):
    return jnp.concatenate([r[...].astype(F32) for r in refs], axis=1)


def _rmsnorm_fwd(x, g, after):
    s, d = x.shape
    tr = min(s, 2 * _row_tile(s))

    def body(x_ref, g_ref, after_ref, h_ref):
        xv = x_ref[...]
        r = lax.rsqrt(jnp.mean(xv * xv, axis=-1, keepdims=True) + NORM_EPS)
        h_ref[...] = (xv * r * g_ref[...]).astype(BF16)

    row = pl.BlockSpec((tr, d), lambda i: (i, 0))
    vec = pl.BlockSpec((1, d), lambda i: (0, 0))
    return pl.pallas_call(body, name="rmsnorm_fwd", grid=(s // tr,), in_specs=[row, vec, ANY_SPEC], out_specs=row,
                          out_shape=_sds((s, d), BF16), compiler_params=_params("parallel"))(x, g, after)


def _merge_fwd(y_a, y_b, proj, col_m):
    s, d = y_a.shape
    tr = _row_tile(s)
    ns = SEGMENT_SPLIT

    def body(ya_ref, yb_ref, *rest):
        ma, mb, o_ref = _cat(rest[:ns]), _cat(rest[ns:2 * ns]), rest[2 * ns]
        o_ref[...] = (jax.nn.sigmoid(ma) * ya_ref[...].astype(F32)
                      + jax.nn.sigmoid(mb) * yb_ref[...].astype(F32)).astype(BF16)

    row = pl.BlockSpec((tr, d), lambda i: (i, 0))
    return pl.pallas_call(
        body, name="merge_fwd", grid=(s // tr,),
        in_specs=[row, row, *_segment_specs(tr, d, col_m), *_segment_specs(tr, d, col_m + d)],
        out_specs=row, out_shape=_sds((s, d), BF16), compiler_params=_params("parallel"),
    )(y_a, y_b, *([proj] * (2 * ns)))


def _loss_and_final_norm_bwd(x2, target, g):
    s, d = x2.shape
    tr = min(s, 2 * _row_tile(s))

    def body(x_ref, t_ref, g_ref, loss_ref, dg_ref, dxb_ref):
        @pl.when(pl.program_id(0) == 0)
        def _():
            loss_ref[...] = jnp.zeros_like(loss_ref)
            dg_ref[...] = jnp.zeros_like(dg_ref)

        xv, gv = x_ref[...], g_ref[...]
        r = lax.rsqrt(jnp.mean(xv * xv, axis=-1, keepdims=True) + NORM_EPS)
        xhat = xv * r
        err = xhat * gv - t_ref[...]
        loss_ref[...] += 0.5 * jnp.sum(jnp.mean(err * err, axis=-1, keepdims=True))
        dy = err / d
        dg_ref[...] += jnp.sum(dy * xhat, axis=0, keepdims=True)
        dyg = dy * gv
        dxb_ref[...] = (r * (dyg - xhat * jnp.mean(dyg * xhat, axis=-1, keepdims=True))).astype(BF16)

    row = pl.BlockSpec((tr, d), lambda i: (i, 0))
    vec = pl.BlockSpec((1, d), lambda i: (0, 0))
    return pl.pallas_call(
        body, name="loss_final_norm_bwd", grid=(s // tr,), in_specs=[row, row, vec],
        out_specs=(pl.BlockSpec((SUBLANES, LANES), lambda i: (0, 0)), vec, row),
        out_shape=(_sds((SUBLANES, LANES), F32), _sds((1, d), F32), _sds((s, d), BF16)),
        compiler_params=_params("arbitrary"))(x2, target, g)


def _write_behind(step, n_steps, buf, sems, wide_ref, rows, col0, fill):
    cols = buf.shape[2]
    slot = step % 2

    def copy(at_step, at_slot):
        dst = wide_ref.at[pl.ds(pl.multiple_of(at_step * rows, rows), rows), pl.ds(col0, cols)]
        return pltpu.make_async_copy(buf.at[at_slot], dst, sems.at[at_slot])

    @pl.when(step >= 2)
    def _():
        copy(step - 2, slot).wait()

    fill(buf.at[slot])
    copy(step, slot).start()

    @pl.when(step == n_steps - 1)
    def _():
        copy(step, slot).wait()
        if n_steps > 1:
            copy(step - 1, 1 - slot).wait()


def _merge_bwd(d_merged, y_a, y_b, proj, col_m, d_proj):
    s, d = y_a.shape
    tr = _row_tile(s)
    ns = SEGMENT_SPLIT
    n_steps = s // tr

    def body(dm_ref, ya_ref, yb_ref, *rest):
        ma, mb = _cat(rest[:ns]), _cat(rest[ns:2 * ns])
        dya_ref, dyb_ref, wide_ref, buf, sems = rest[2 * ns + 1:]
        dm = dm_ref[...].astype(F32)
        sa = jax.nn.sigmoid(ma)
        sb = jax.nn.sigmoid(mb)
        dya_ref[...] = (dm * sa).astype(BF16)
        dyb_ref[...] = (dm * sb).astype(BF16)

        def fill(out):
            out[:, :d] = (dm * ya_ref[...].astype(F32) * (sa * (1.0 - sa))).astype(BF16)
            out[:, d:] = (dm * yb_ref[...].astype(F32) * (sb * (1.0 - sb))).astype(BF16)

        _write_behind(pl.program_id(0), n_steps, buf, sems, wide_ref, tr, col_m, fill)

    row = pl.BlockSpec((tr, d), lambda i: (i, 0))
    n_in = 3 + 2 * ns
    return pl.pallas_call(
        body, name="merge_bwd", grid=(n_steps,),
        in_specs=[row, row, row, *_segment_specs(tr, d, col_m), *_segment_specs(tr, d, col_m + d), ANY_SPEC],
        out_specs=(row, row, ANY_SPEC),
        out_shape=(_sds((s, d), BF16), _sds((s, d), BF16), _sds(d_proj.shape, BF16)),
        input_output_aliases={n_in: 2},
        scratch_shapes=[pltpu.VMEM((2, tr, 2 * d), BF16), pltpu.SemaphoreType.DMA((2,))],
        compiler_params=_params("arbitrary"))(d_merged, y_a, y_b, *([proj] * (2 * ns)), d_proj)


def _place(d_proj, piece, col0, name):
    s, w = piece.shape
    bw = math.gcd(w, col0) if col0 else w
    tr = min(s, 512)

    def body(p_ref, wide_in, o_ref):
        o_ref[...] = p_ref[...]

    return pl.pallas_call(
        body, name=name, grid=(s // tr, w // bw),
        in_specs=[pl.BlockSpec((tr, bw), lambda i, j: (i, j)), ANY_SPEC],
        out_specs=pl.BlockSpec((tr, bw), lambda i, j: (i, col0 // bw + j)),
        out_shape=_sds(d_proj.shape, d_proj.dtype), input_output_aliases={1: 0},
        compiler_params=_params("parallel", "parallel"))(piece, d_proj)


def _input_grad(d_h, x, g, dx2):
    s, d = x.shape
    tr = _row_tile(s)

    def body(dh_ref, x_ref, g_ref, dx2_ref, gx_ref, dg_ref):
        @pl.when(pl.program_id(0) == 0)
        def _():
            dg_ref[...] = jnp.zeros_like(dg_ref)

        xv, dh = x_ref[...], dh_ref[...]
        r = lax.rsqrt(jnp.mean(xv * xv, axis=-1, keepdims=True) + NORM_EPS)
        xhat = xv * r
        dg_ref[...] += jnp.sum(dh * xhat, axis=0, keepdims=True)
        dyg = dh * g_ref[...]
        gx_ref[...] = dx2_ref[...].astype(F32) + r * (dyg - xhat * jnp.mean(dyg * xhat, axis=-1, keepdims=True))

    row = pl.BlockSpec((tr, d), lambda i: (i, 0))
    vec = pl.BlockSpec((1, d), lambda i: (0, 0))
    return pl.pallas_call(
        body, name="input_grad", grid=(s // tr,), in_specs=[row, row, vec, row], out_specs=(row, vec),
        out_shape=(_sds((s, d), F32), _sds((1, d), F32)), compiler_params=_params("arbitrary"))(d_h, x, g, dx2)


def _rope_tables(positions):
    inv_freq = ROPE_THETA ** (-jnp.arange(ROPE_HALF, dtype=F32) * 2.0 / ROPE_DIM)
    ang = positions.astype(F32)[:, None] * inv_freq
    cos, sin = jnp.cos(ang), jnp.sin(ang)
    zero = jnp.zeros((positions.shape[0], HEAD_DIM - ROPE_DIM), F32)
    zero_h = jnp.zeros_like(sin)
    c = jnp.concatenate([cos, cos, zero + 1.0], axis=1)
    up = jnp.concatenate([-sin, zero_h, zero], axis=1)
    down = jnp.concatenate([zero_h, sin, zero], axis=1)
    reps = LANES // HEAD_DIM
    return jnp.stack([jnp.tile(c, (1, reps)), jnp.tile(up, (1, reps)), jnp.tile(down, (1, reps))])


def _lane_tiles(x):
    return [x[:, t * LANES:(t + 1) * LANES] for t in range(x.shape[1] // LANES)]


def _rope(x, tab):
    out = [xt * tab[0] + pltpu.roll(xt, LANES - ROPE_HALF, 1) * tab[1] + pltpu.roll(xt, ROPE_HALF, 1) * tab[2]
           for xt in _lane_tiles(x)]
    return out[0] if len(out) == 1 else jnp.concatenate(out, axis=1)


def _rope_bwd(g, tab):
    out = [gt * tab[0] + pltpu.roll(gt * tab[1], ROPE_HALF, 1) + pltpu.roll(gt * tab[2], LANES - ROPE_HALF, 1)
           for gt in _lane_tiles(g)]
    return out[0] if len(out) == 1 else jnp.concatenate(out, axis=1)


def _head(x, h):
    return x[:, h * HEAD_DIM:(h + 1) * HEAD_DIM]


def _stack_heads(x, first, count):
    return jnp.concatenate([_head(x, first + h) for h in range(count)], axis=0)


def _dot_nt(a, b):
    return lax.dot_general(a, b, (((1,), (1,)), ((), ())), preferred_element_type=F32)


def _causal(rows):
    qi = lax.broadcasted_iota(jnp.int32, (rows, BLOCK), 0) % BLOCK
    return lax.broadcasted_iota(jnp.int32, (rows, BLOCK), 1) <= qi


def _band_probs_by_head(qs, k_band, sink_ref, first, count, causal, blk):
    s_band = _dot_nt(qs, k_band)
    p_all, p_band, p_sink = [], [], []
    for h in range(count):
        rows = slice(h * BLOCK, (h + 1) * BLOCK)
        sink = sink_ref[first + h]
        s = jnp.where(causal, s_band[rows, :BLOCK], jnp.where(blk > 0, s_band[rows, BLOCK:], -jnp.inf))
        m = jnp.maximum(jnp.max(s, axis=-1, keepdims=True), sink)
        p = jnp.exp(s - m)
        ps = jnp.exp(sink - m)
        inv = 1.0 / (jnp.sum(p, axis=-1, keepdims=True) + ps)
        p = p * inv
        p_all.append(p)
        p_sink.append(ps * inv)
        p_band.append(_split_band(p, causal))
    cat = lambda parts: jnp.concatenate(parts, axis=0)
    return cat([c for c, _ in p_band]), cat([v for _, v in p_band]), cat(p_all), cat(p_sink)


def _split_band(x, causal):
    return jnp.where(causal, x, 0.0).astype(BF16), jnp.where(causal, 0.0, x).astype(BF16)


def _attn_dims(s, d, kv):
    n_kv = kv // HEAD_DIM
    group = d // kv
    qw = 2 * group * HEAD_DIM
    assert n_kv % 2 == 0 and (d + 2 * kv) % qw == 0 and s % BLOCK == 0
    return group, qw, n_kv // 2, s // BLOCK


def _attention_fwd(proj, tables, sink, kv, after):
    s, d = proj.shape[0], sink.shape[0] * HEAD_DIM
    group, qw, n_pairs, nb = _attn_dims(s, d, kv)

    def body(sink_ref, q_ref, kc_ref, kp_ref, vc_ref, vp_ref, ga_ref, tc_ref, tp_ref, after_ref,
             attn_ref, ain_ref, qrot_ref, krot_ref):
        pair, blk = pl.program_id(0), pl.program_id(1)
        tab_c, tab_p = tc_ref[...], tp_ref[...]
        q = _rope(q_ref[...].astype(F32), tab_c) * ATTN_SCALE
        k_cur, k_prev = _rope(kc_ref[...].astype(F32), tab_c), _rope(kp_ref[...].astype(F32), tab_p)
        qrot_ref[...] = q.astype(BF16)
        krot_ref[...] = k_cur.astype(BF16)
        v_cur, v_prev = vc_ref[...], vp_ref[...]
        causal = _causal(BLOCK)
        outs = []
        for a in range(2):
            k_band = jnp.concatenate([_head(k_cur, a), _head(k_prev, a)], axis=0).astype(BF16)
            vc, vp = _head(v_cur, a).astype(BF16), _head(v_prev, a).astype(BF16)
            qs = _stack_heads(q, a * group, group).astype(BF16)
            p_cur, p_prev, _, _ = _band_probs_by_head(qs, k_band, sink_ref, (2 * pair + a) * group, group, causal, blk)
            o = jnp.dot(p_cur, vc, preferred_element_type=F32) + jnp.dot(p_prev, vp, preferred_element_type=F32)
            outs += [o[h * BLOCK:(h + 1) * BLOCK] for h in range(group)]
        attn = jnp.concatenate(outs, axis=1)
        attn_ref[...] = attn.astype(BF16)
        silu, _ = _silu_and_grad(ga_ref[...].astype(F32))
        ain_ref[...] = (attn * silu).astype(BF16)

    k0, v0, g0 = d // LANES, (d + kv) // LANES, (d + 2 * kv) // qw
    prev = lambda i: jnp.maximum(i - 1, 0)
    in_specs = [
        SMEM_SPEC,
        pl.BlockSpec((BLOCK, qw), lambda p, i: (i, p)),
        pl.BlockSpec((BLOCK, LANES), lambda p, i: (i, k0 + p)),
        pl.BlockSpec((BLOCK, LANES), lambda p, i: (prev(i), k0 + p)),
        pl.BlockSpec((BLOCK, LANES), lambda p, i: (i, v0 + p)),
        pl.BlockSpec((BLOCK, LANES), lambda p, i: (prev(i), v0 + p)),
        pl.BlockSpec((BLOCK, qw), lambda p, i: (i, g0 + p)),
        pl.BlockSpec((3, BLOCK, LANES), lambda p, i: (0, i, 0)),
        pl.BlockSpec((3, BLOCK, LANES), lambda p, i: (0, prev(i), 0)),
        ANY_SPEC,
    ]
    out = pl.BlockSpec((BLOCK, qw), lambda p, i: (i, p))
    k_out = pl.BlockSpec((BLOCK, LANES), lambda p, i: (i, p))
    return pl.pallas_call(
        body, name="attention_fwd", grid=(n_pairs, nb), in_specs=in_specs, out_specs=(out, out, out, k_out),
        out_shape=(_sds((s, d), BF16), _sds((s, d), BF16), _sds((s, d), BF16), _sds((s, kv), BF16)),
        compiler_params=_params("parallel", "parallel"),
    )(sink, proj, proj, proj, proj, proj, proj, tables, tables, after)


def _attention_bwd(proj, q_rot, k_rot, tables, sink, kv, attn, d_ain, after):
    s, d = proj.shape[0], sink.shape[0] * HEAD_DIM
    group, qw, n_pairs, nb = _attn_dims(s, d, kv)

    def body(sink_ref, q_ref, kc_ref, kp_ref, vc_ref, vp_ref, ga_ref, tc_ref, tp_ref, attn_ref, dain_ref, after_ref,
             dq_ref, dk_ref, dv_ref, dga_ref, dsink_ref, carry_k, carry_v, sink_acc):
        pair, blk = pl.program_id(0), pl.program_id(1)

        @pl.when(blk == 0)
        def _():
            carry_k[...] = jnp.zeros_like(carry_k)
            carry_v[...] = jnp.zeros_like(carry_v)
            sink_acc[...] = jnp.zeros_like(sink_acc)

        @pl.when(blk < nb)
        def _():
            tab_c, tab_p = tc_ref[...], tp_ref[...]
            q = q_ref[...].astype(F32)
            k_cur, k_prev = kc_ref[...].astype(F32), kp_ref[...].astype(F32)
            v_cur, v_prev = vc_ref[...], vp_ref[...]
            silu, silu_grad = _silu_and_grad(ga_ref[...].astype(F32))
            d_ain_v = dain_ref[...].astype(F32)
            dga_ref[...] = (d_ain_v * attn_ref[...].astype(F32) * silu_grad).astype(BF16)
            d_attn = d_ain_v * silu
            q_t = q.T
            d_attn_t = d_attn.T
            causal = _causal(BLOCK)
            dq_parts = []
            dk_t = {"cur": [], "prev": []}
            dv_t = {"cur": [], "prev": []}
            lane = lax.broadcasted_iota(jnp.int32, (BLOCK, LANES), 1)
            dsink = jnp.zeros((BLOCK, LANES), F32)
            for a in range(2):
                first = a * group
                qs = _stack_heads(q, first, group).astype(BF16)
                kc, kp = _head(k_cur, a).astype(BF16), _head(k_prev, a).astype(BF16)
                k_band = jnp.concatenate([_head(k_cur, a), _head(k_prev, a)], axis=0).astype(BF16)
                v_band = jnp.concatenate([_head(v_cur, a), _head(v_prev, a)], axis=0).astype(BF16)
                p_cur, p_prev, p, p_sink = _band_probs_by_head(qs, k_band, sink_ref, (2 * pair + a) * group, group,
                                                               causal, blk)
                do = _stack_heads(d_attn, first, group).astype(BF16)
                dp_band = _dot_nt(do, v_band)
                ds_parts, delta = [], []
                for h in range(group):
                    rows = slice(h * BLOCK, (h + 1) * BLOCK)
                    dp = jnp.where(causal, dp_band[rows, :BLOCK], dp_band[rows, BLOCK:])
                    delta.append(jnp.sum(p[rows] * dp, axis=-1, keepdims=True))
                    ds_parts.append(_split_band(p[rows] * (dp - delta[-1]), causal))
                ds_cur = jnp.concatenate([c for c, _ in ds_parts], axis=0)
                ds_prev = jnp.concatenate([v for _, v in ds_parts], axis=0)
                delta = jnp.concatenate(delta, axis=0)
                dqs = (jnp.dot(ds_cur, kc, preferred_element_type=F32)
                       + jnp.dot(ds_prev, kp, preferred_element_type=F32)) * ATTN_SCALE
                dq_parts += [dqs[h * BLOCK:(h + 1) * BLOCK] for h in range(group)]
                rows = lambda t: jnp.concatenate(
                    [t[(first + h) * HEAD_DIM:(first + h + 1) * HEAD_DIM] for h in range(group)], axis=1).astype(BF16)
                qs_t, do_t = rows(q_t), rows(d_attn_t)
                dk_t["cur"].append(jnp.dot(qs_t, ds_cur, preferred_element_type=F32))
                dk_t["prev"].append(jnp.dot(qs_t, ds_prev, preferred_element_type=F32))
                dv_t["cur"].append(jnp.dot(do_t, p_cur, preferred_element_type=F32))
                dv_t["prev"].append(jnp.dot(do_t, p_prev, preferred_element_type=F32))
                ds_sink = -(p_sink * delta)
                for h in range(group):
                    dsink = dsink + jnp.where(lane == first + h, ds_sink[h * BLOCK:(h + 1) * BLOCK], 0.0)
            sink_acc[...] += dsink
            dq_ref[...] = _rope_bwd(jnp.concatenate(dq_parts, axis=1), tab_c).astype(BF16)
            pair_block = lambda parts: jnp.concatenate(parts, axis=0).T
            dk_ref[...] = (carry_k[...] + _rope_bwd(pair_block(dk_t["prev"]), tab_p)).astype(BF16)
            dv_ref[...] = (carry_v[...] + pair_block(dv_t["prev"])).astype(BF16)
            carry_k[...] = _rope_bwd(pair_block(dk_t["cur"]), tab_c)
            carry_v[...] = pair_block(dv_t["cur"])

        @pl.when(blk == nb)
        def _():
            dk_ref[...] = carry_k[...].astype(BF16)
            dv_ref[...] = carry_v[...].astype(BF16)
            dsink_ref[0] = jnp.sum(sink_acc[...], axis=0, keepdims=True)

    v0, g0 = (d + kv) // LANES, (d + 2 * kv) // qw
    cur = lambda i: jnp.minimum(i, nb - 1)
    prev = lambda i: jnp.maximum(cur(i) - 1, 0)
    back = lambda i: jnp.maximum(i - 1, 0)
    q_spec = pl.BlockSpec((BLOCK, qw), lambda p, i: (cur(i), p))
    in_specs = [
        SMEM_SPEC,
        q_spec,
        pl.BlockSpec((BLOCK, LANES), lambda p, i: (cur(i), p)),
        pl.BlockSpec((BLOCK, LANES), lambda p, i: (prev(i), p)),
        pl.BlockSpec((BLOCK, LANES), lambda p, i: (cur(i), v0 + p)),
        pl.BlockSpec((BLOCK, LANES), lambda p, i: (prev(i), v0 + p)),
        pl.BlockSpec((BLOCK, qw), lambda p, i: (cur(i), g0 + p)),
        pl.BlockSpec((3, BLOCK, LANES), lambda p, i: (0, cur(i), 0)),
        pl.BlockSpec((3, BLOCK, LANES), lambda p, i: (0, prev(i), 0)),
        q_spec,
        q_spec,
        ANY_SPEC,
    ]
    kv_out = pl.BlockSpec((BLOCK, LANES), lambda p, i: (back(i), p))
    return pl.pallas_call(
        body, name="attention_bwd", grid=(n_pairs, nb + 1), in_specs=in_specs,
        out_specs=(q_spec, kv_out, kv_out, q_spec, pl.BlockSpec((1, 1, LANES), lambda p, i: (p, 0, 0))),
        out_shape=(_sds((s, d), BF16), _sds((s, kv), BF16), _sds((s, kv), BF16), _sds((s, d), BF16),
                   _sds((n_pairs, 1, LANES), F32)),
        scratch_shapes=[pltpu.VMEM((BLOCK, LANES), F32), pltpu.VMEM((BLOCK, LANES), F32),
                        pltpu.VMEM((BLOCK, LANES), F32)],
        compiler_params=_params("parallel", "arbitrary"),
    )(sink, q_rot, k_rot, k_rot, proj, proj, proj, tables, tables, attn, d_ain, after)


def _gmlp_core(gu, gv, ln_g, ln_b, w_ref, bias_t):
    xc = gv - jnp.mean(gv, axis=-1, keepdims=True)
    rstd = lax.rsqrt(jnp.mean(xc * xc, axis=-1, keepdims=True) + LN_EPS)
    xhat = xc * rstd
    vn = (xhat * ln_g + ln_b).astype(BF16)
    gd = gu.shape[1] // GMLP_GROUPS
    tri = (lax.broadcasted_iota(jnp.int32, (BLOCK, BLOCK), 0) >= lax.broadcasted_iota(jnp.int32, (BLOCK, BLOCK), 1))
    w_tri = [jnp.where(tri, w_ref[g], 0.0).astype(BF16) for g in range(GMLP_GROUPS)]
    mixed = jnp.concatenate(
        [jnp.dot(w_tri[g], vn[:, g * gd:(g + 1) * gd], preferred_element_type=F32) + bias_t[:, g:g + 1]
         for g in range(GMLP_GROUPS)], axis=1)
    return gu, xhat, rstd, vn, w_tri, tri, mixed


def _whole(shape):
    return pl.BlockSpec(shape, lambda i: tuple(0 for _ in shape))


def _gmlp_fwd(proj, col_u, d, w_s, bias_t, ln_g, ln_b, after):
    s = proj.shape[0]
    ns = SEGMENT_SPLIT

    def body(*refs):
        u, vg, gb = _cat(refs[:ns]), _cat(refs[ns:2 * ns]), _cat(refs[2 * ns:3 * ns])
        w_ref, bt_ref, lg_ref, lb_ref, after_ref, o_ref = refs[3 * ns:]
        gu, _, _, _, _, _, mixed = _gmlp_core(_gelu(u), _gelu(vg), lg_ref[...], lb_ref[...], w_ref, bt_ref[...])
        silu, _ = _silu_and_grad(gb)
        o_ref[...] = ((gu * mixed) * silu).astype(BF16)

    segs = [sp for j in range(3) for sp in _segment_specs(BLOCK, d, col_u + j * d)]
    return pl.pallas_call(
        body, name="gmlp_fwd", grid=(s // BLOCK,),
        in_specs=[*segs, _whole(w_s.shape), _whole(bias_t.shape), _whole((1, d)), _whole((1, d)), ANY_SPEC],
        out_specs=pl.BlockSpec((BLOCK, d), lambda i: (i, 0)), out_shape=_sds((s, d), BF16),
        compiler_params=_params("parallel"),
    )(*([proj] * (3 * ns)), w_s, bias_t, ln_g, ln_b, after)


def _gmlp_bwd(proj, col_u, d, w_s, bias_t, ln_g, ln_b, d_bin, after, d_proj):
    s = proj.shape[0]
    gd = d // GMLP_GROUPS
    ns = SEGMENT_SPLIT
    n_steps = s // BLOCK

    def body(*refs):
        u, vg, gb = _cat(refs[:ns]), _cat(refs[ns:2 * ns]), _cat(refs[2 * ns:3 * ns])
        (w_ref, bt_ref, lg_ref, lb_ref, dbin_ref, after_ref, wide_in, wide_ref, dw_ref, dbt_ref, dlg_ref, dlb_ref,
         buf, sems) = refs[3 * ns:]

        @pl.when(pl.program_id(0) == 0)
        def _():
            dw_ref[...] = jnp.zeros_like(dw_ref)
            dbt_ref[...] = jnp.zeros_like(dbt_ref)
            dlg_ref[...] = jnp.zeros_like(dlg_ref)
            dlb_ref[...] = jnp.zeros_like(dlb_ref)

        ln_g = lg_ref[...]
        (gu, gu_grad), (gv, gv_grad) = _gelu_and_grad(u), _gelu_and_grad(vg)
        gu, xhat, rstd, vn, w_tri, tri, mixed = _gmlp_core(gu, gv, ln_g, lb_ref[...], w_ref, bt_ref[...])
        silu, silu_grad = _silu_and_grad(gb)
        d_bin_v = dbin_ref[...].astype(F32)
        d_sg = d_bin_v * silu
        d_gate = (d_bin_v * (gu * mixed) * silu_grad).astype(BF16)
        d_u = (d_sg * mixed * gu_grad).astype(BF16)
        d_mixed = d_sg * gu
        d_mixed_b = d_mixed.astype(BF16)
        d_vn, d_bias = [], []
        for g in range(GMLP_GROUPS):
            dm_g = d_mixed_b[:, g * gd:(g + 1) * gd]
            d_bias.append(jnp.sum(d_mixed[:, g * gd:(g + 1) * gd], axis=-1, keepdims=True))
            dw = lax.dot_general(dm_g, vn[:, g * gd:(g + 1) * gd], (((1,), (1,)), ((), ())),
                                 preferred_element_type=F32)
            dw_ref[g] += jnp.where(tri, dw, 0.0)
            d_vn.append(lax.dot_general(w_tri[g], dm_g, (((0,), (0,)), ((), ())), preferred_element_type=F32))
        dbt_ref[...] += jnp.concatenate(d_bias, axis=1)
        d_vn = jnp.concatenate(d_vn, axis=1)
        dlg_ref[...] += jnp.sum(d_vn * xhat, axis=0, keepdims=True)
        dlb_ref[...] += jnp.sum(d_vn, axis=0, keepdims=True)
        d_xhat = d_vn * ln_g
        d_gv = rstd * (d_xhat - jnp.mean(d_xhat, axis=-1, keepdims=True)
                       - xhat * jnp.mean(d_xhat * xhat, axis=-1, keepdims=True))
        d_v = (d_gv * gv_grad).astype(BF16)

        def fill(out):
            out[:, :d] = d_u
            out[:, d:2 * d] = d_v
            out[:, 2 * d:] = d_gate

        _write_behind(pl.program_id(0), n_steps, buf, sems, wide_ref, BLOCK, col_u, fill)

    segs = [sp for j in range(3) for sp in _segment_specs(BLOCK, d, col_u + j * d)]
    return pl.pallas_call(
        body, name="gmlp_bwd", grid=(n_steps,),
        in_specs=[*segs, _whole(w_s.shape), _whole(bias_t.shape), _whole((1, d)), _whole((1, d)),
                  pl.BlockSpec((BLOCK, d), lambda i: (i, 0)), ANY_SPEC, ANY_SPEC],
        out_specs=(ANY_SPEC, _whole(w_s.shape), _whole(bias_t.shape), _whole((1, d)), _whole((1, d))),
        out_shape=(_sds(d_proj.shape, BF16), _sds(w_s.shape, F32), _sds(bias_t.shape, F32), _sds((1, d), F32),
                   _sds((1, d), F32)),
        input_output_aliases={3 * ns + 6: 0},
        scratch_shapes=[pltpu.VMEM((2, BLOCK, 3 * d), BF16), pltpu.SemaphoreType.DMA((2,))],
        compiler_params=_params("arbitrary"),
    )(*([proj] * (3 * ns)), w_s, bias_t, ln_g, ln_b, d_bin, after, d_proj)


def _pack(parts):
    rows = []
    tile = SUBLANES * LANES
    for p in parts:
        flat = p.astype(F32).reshape(-1)
        padded = -(-flat.shape[0] // tile) * tile
        rows.append(jnp.pad(flat, (0, padded - flat.shape[0])).reshape(-1, LANES))
    return jnp.concatenate(rows, axis=0)


def _unpack(packed, shapes):
    out, row = [], 0
    tile = SUBLANES * LANES
    for shape in shapes:
        size = math.prod(shape)
        n_rows = -(-size // tile) * SUBLANES
        out.append(packed[row:row + n_rows].reshape(-1)[:size].reshape(shape))
        row += n_rows
    return out


def kernel(x, positions, norm_g, w_in, attn_sink, gmlp_ln_g, gmlp_ln_b, w_spatial, b_spatial, w_up_attn, w_up_gmlp, w_out, final_norm_g, loss_target, m_norm_g, m_w_in, m_attn_sink, m_gmlp_ln_g, m_gmlp_ln_b, m_w_spatial, m_b_spatial, m_w_up_attn, m_w_up_gmlp, m_w_out, m_final_norm_g, v_norm_g, v_w_in, v_attn_sink, v_gmlp_ln_g, v_gmlp_ln_b, v_w_spatial, v_b_spatial, v_w_up_attn, v_w_up_gmlp, v_w_out, v_final_norm_g):
    x2d, target = x[0], loss_target[0]
    s, d = x2d.shape
    n_q_heads = attn_sink.shape[1]
    cw = w_in.shape[2]
    rw = w_up_attn.shape[1]
    kv = (cw * N_DEV - 7 * d) // 2
    col_u, col_m = 2 * d + 2 * kv, 5 * d + 2 * kv
    final_g = final_norm_g.reshape(1, d)
    sink = attn_sink[0]
    w_s = w_spatial[0]
    bias_t = b_spatial[0].T
    mx, my, mc = _mesh_pos()
    pos = jnp.stack([mx, my, mc]).astype(jnp.int32)
    chips = jnp.arange(N_CHIPS, dtype=jnp.int32)

    def one_block(fn):
        return jnp.reshape(fn(mx, my, mc), (1,)).astype(jnp.int32)

    w_in_b = _cast_into_slot(w_in[0], pos, "cast_w_in")
    to_sibling = [_Copy(0, _slot, 0, _slot, _sibling)]
    ici = [[_Copy(0, _slot, 0, _slot, lambda x, y, c, chip=chip: (*chip(x, y, c), c))] for chip in _ICI_STAGES[:2]]
    relayed = lambda x, y, c: _slot(*_ICI_STAGES[1](x, y, c), c)
    ici.append([_Copy(0, relayed, 0, relayed, lambda x, y, c: (*_ICI_STAGES[0](x, y, c), c))])
    passes = []
    for chip in _ICI_STAGES:
        landed = lambda x, y, c, chip=chip: _slot(*chip(x, y, c), c)
        passes.append([_Copy(0, landed, 0, landed, _sibling)])
    sib_sems = _rdma_start("w_in_sibling_start", [w_in_b], to_sibling)
    ici_sems = _rdma_start("w_in_ici0_start", sib_sems[2], ici[0])
    h = _rmsnorm_fwd(x2d, norm_g, ici_sems[3])
    proj = _project(h, ici_sems[2][0], one_block(_slot), "projection_own")
    w_blocks = _rdma_wait("w_in_sibling_wait", ici_sems[2], sib_sems[0], sib_sems[1], to_sibling, proj)
    proj = _project(h, w_blocks[0], one_block(lambda x, y, c: _slot(x, y, 1 - c)), "projection_sibling", proj=proj)
    w_blocks = _rdma_wait("w_in_ici0_wait", w_blocks, ici_sems[0], ici_sems[1], ici[0], proj)
    first = _gather_first_copies(3)
    for k, chip in enumerate(_ICI_STAGES):
        if k + 1 < len(_ICI_STAGES):
            ici_sems = _rdma_start("w_in_ici%d_start" % (k + 1), w_blocks, ici[k + 1])
            w_blocks = ici_sems[2]
        else:
            send1, recv1, thru, _ = _rdma_start("gather_squares_start", squares + w_blocks, first)
            squares, w_blocks = thru[:3], thru[3:]
        pass_sems = _rdma_start("w_in_pass%d_start" % k, w_blocks, passes[k])
        if k == 0:
            squares = [_cast_into_slot(w[0], pos, "cast_" + nm, after=pass_sems[3])
                       for nm, w in (("w_up_attn", w_up_attn), ("w_up_gmlp", w_up_gmlp), ("w_out", w_out))]
        proj = _project(h, pass_sems[2][0], one_block(lambda x, y, c, chip=chip: _slot(*chip(x, y, c), c)),
                        "projection_ici%d" % k, proj=proj, after=pass_sems[3])
        w_blocks = _rdma_wait("w_in_pass%d_wait" % k, pass_sems[2], pass_sems[0], pass_sems[1], passes[k], proj)
        proj = _project(h, w_blocks[0], one_block(lambda x, y, c, chip=chip: _slot(*chip(x, y, 1 - c), 1 - c)),
                        "projection_pass%d" % k, proj=proj)
        if k + 1 < len(_ICI_STAGES):
            w_blocks = _rdma_wait("w_in_ici%d_wait" % (k + 1), w_blocks, ici_sems[0], ici_sems[1], ici[k + 1], proj)
    w_in_b = w_blocks[0]

    tables = _rope_tables(positions[0])
    attn, a_in, q_rot, k_rot = _attention_fwd(proj, tables, sink, kv, proj)
    squares = _rdma_wait("gather_squares_wait", squares, send1, recv1, first, attn)
    passed = _gather_pass_copies(3)
    send2, recv2, squares, token = _rdma_start("pass_squares_start", squares, passed)
    b_in = _gmlp_fwd(proj, col_u, d, w_s, bias_t, gmlp_ln_g, gmlp_ln_b, token)
    squares = _rdma_wait("pass_squares_wait", squares, send2, recv2, passed, b_in)
    w_ua, w_ug, w_o = [w.reshape(N_DEV * rw, d) for w in squares]
    y_a = _matmul(a_in, w_ua, "nn", BF16, "up_attn")
    y_b = _matmul(b_in, w_ug, "nn", BF16, "up_gmlp")
    merged = _merge_fwd(y_a, y_b, proj, col_m)
    x_out = _matmul(merged, w_o, "nn", F32, "out_proj", res=x2d, tn=512)
    loss_p, d_final_g, dx2_b = _loss_and_final_norm_bwd(x_out, target, final_g)

    d_merged = _matmul(dx2_b, w_o, "nt", BF16, "d_merged")
    g_w_out = _matmul(merged, dx2_b, "tn", BF16, "g_w_out")
    d_ya, d_yb, d_proj = _merge_bwd(d_merged, y_a, y_b, proj, col_m, lax.empty(proj.shape, BF16))
    d_ain = _matmul(d_ya, w_ua, "nt", BF16, "d_a_in")
    g_w_ua = _matmul(a_in, d_ya, "tn", BF16, "g_w_up_attn")
    d_bin = _matmul(d_yb, w_ug, "nt", BF16, "d_b_in")
    g_w_ug = _matmul(b_in, d_yb, "tn", BF16, "g_w_up_gmlp")
    sq_grads = [g.reshape(N_DEV, rw, d) for g in (g_w_ua, g_w_ug, g_w_out)]
    sq_land = [lax.empty((N_CHIPS, rw, d), BF16) for _ in sq_grads]
    pairs_sq = _pair_copies_strided(3)
    arrays = [a for gl in zip(sq_grads, sq_land) for a in gl]
    send3, recv3, arrays, token = _rdma_start("pair_squares_start", arrays, pairs_sq)
    d_q, d_k, d_v, d_ga, d_sink = _attention_bwd(proj, q_rot, k_rot, tables, sink, kv, attn, d_ain, token)
    arrays = _rdma_wait("pair_squares_wait", arrays, send3, recv3, pairs_sq, d_q)
    sq_sums = [_pair_sum(arrays[2 * a], arrays[2 * a + 1], pos, "pair_sum_%d" % a) for a in range(3)]
    sq_land2 = [lax.empty((N_CHIPS - 1, rw, d), BF16) for _ in sq_sums]
    chip_sq = _chip_sum_copies(3)
    arrays = [a for gl in zip(sq_sums, sq_land2) for a in gl]
    send4, recv4, sq_arrays, token = _rdma_start("chip_squares_start", arrays, chip_sq)
    d_proj, d_w_s, d_bias_t, d_ln_g, d_ln_b = _gmlp_bwd(proj, col_u, d, w_s, bias_t, gmlp_ln_g, gmlp_ln_b, d_bin, token,
                                                        d_proj)
    for piece, col0, nm in ((d_q, 0, "d_q"), (d_k, d, "d_k"), (d_v, d + kv, "d_v"), (d_ga, d + 2 * kv, "d_gate")):
        d_proj = _place(d_proj, piece, col0, "place_" + nm)

    half = N_CHIPS // 2
    pairs_in = [_Copy(0, lambda x, y, c, q=q: q, 1, lambda x, y, c, q=q: q, _sibling) for q in range(half)]
    sent, token = [], None
    for j in range(2):
        g_sib = _grad_w_in_blocks(h, d_proj, 2 * chips[j * half:(j + 1) * half] + 1 - mc, cw,
                                  "g_w_in_sibling%d" % j, after=token)
        sent.append(_rdma_start("pair_w_in%d_start" % j, [g_sib, lax.empty((half, d, cw), BF16)], pairs_in))
        token = sent[-1][3]
    in_sums = None
    for j in range(2):
        send5, recv5, arrays, _ = sent[j]
        arrays = _rdma_wait("pair_w_in%d_wait" % j, arrays, send5, recv5, pairs_in, token if j == 0 else in_sums)
        in_sums = _grad_w_in_blocks(h, d_proj, 2 * chips[j * half:(j + 1) * half] + mc, cw, "g_w_in_own%d" % j,
                                    slots=N_CHIPS, slot0=j * half, prev=in_sums, init=arrays[1], tm=512)
    chip_in = _chip_sum_copies(1)
    send6, recv6, in_arrays, token = _rdma_start(
        "chip_w_in_start", [in_sums, lax.empty((N_CHIPS - 1, d, cw), BF16)], chip_in)
    d_h = _d_hidden(d_proj, w_in_b, after=token)
    grad_x, d_norm_g = _input_grad(d_h, x2d, norm_g, dx2_b)

    sq_arrays = _rdma_wait("chip_squares_wait", sq_arrays, send4, recv4, chip_sq, grad_x)
    big = {}
    for a, (name, w, m, v) in enumerate((("w_up_attn", w_up_attn, m_w_up_attn, v_w_up_attn),
                                         ("w_up_gmlp", w_up_gmlp, m_w_up_gmlp, v_w_up_gmlp),
                                         ("w_out", w_out, m_w_out, v_w_out))):
        big[name] = [r[None] for r in _reduce_adamw(sq_arrays[2 * a], sq_arrays[2 * a + 1], w[0], m[0], v[0], pos,
                                                    "adamw_" + name)]

    heads_per_pair = 2 * n_q_heads // (kv // HEAD_DIM)
    g_sink = d_sink[:, 0, :heads_per_pair].reshape(1, n_q_heads)
    small_w = [norm_g, attn_sink, gmlp_ln_g, gmlp_ln_b, w_spatial, b_spatial, final_norm_g]
    small_m = [m_norm_g, m_attn_sink, m_gmlp_ln_g, m_gmlp_ln_b, m_w_spatial, m_b_spatial, m_final_norm_g]
    small_v = [v_norm_g, v_attn_sink, v_gmlp_ln_g, v_gmlp_ln_b, v_w_spatial, v_b_spatial, v_final_norm_g]
    small_g = [d_norm_g, g_sink, d_ln_g, d_ln_b, d_w_s[None], d_bias_t.T[None], d_final_g.reshape(d)]
    loss_pad = jnp.zeros((1,), F32)
    shapes = [w.shape for w in small_w] + [(1,)]
    packed = _small_allreduce_adamw(_pack(small_g + [loss_p[0, :1]]), _pack(small_w + [loss_pad]),
                                    _pack(small_m + [loss_pad]), _pack(small_v + [loss_pad]))
    sg, sd, sm, sv = [_unpack(p, shapes) for p in packed]
    loss = sg[-1][0]
    in_arrays = _rdma_wait("chip_w_in_wait", in_arrays, send6, recv6, chip_in, packed[0])
    big["w_in"] = [r[None] for r in _reduce_adamw(in_arrays[0], in_arrays[1], w_in[0], m_w_in[0], v_w_in[0], pos,
                                                  "adamw_w_in")]

    names = ["norm_g", "w_in", "attn_sink", "gmlp_ln_g", "gmlp_ln_b", "w_spatial", "b_spatial", "w_up_attn",
             "w_up_gmlp", "w_out", "final_norm_g"]
    small_names = ["norm_g", "attn_sink", "gmlp_ln_g", "gmlp_ln_b", "w_spatial", "b_spatial", "final_norm_g"]
    outs = [[], [], [], []]
    for nm in names:
        for k in range(4):
            if nm in big:
                outs[k].append(big[nm][k])
            else:
                outs[k].append((sg, sd, sm, sv)[k][small_names.index(nm)])
    return (loss, grad_x[None], *outs[0], *outs[1], *outs[2], *outs[3])
```

```python
import math
from typing import Callable, NamedTuple

import jax
import jax.numpy as jnp
from jax import lax
from jax.experimental import pallas as pl
from jax.experimental.pallas import tpu as pltpu

F32 = jnp.float32
BF16 = jnp.bfloat16
MESH = pl.DeviceIdType.MESH

N_DEV = 8
N_CHIPS = 4
HEAD_DIM = 64
BLOCK = 128
ROPE_DIM = 16
ROPE_HALF = ROPE_DIM // 2
ROPE_THETA = 500000.0
GMLP_GROUPS = 8
NORM_EPS = 1e-5
LN_EPS = 1e-5
ATTN_SCALE = HEAD_DIM ** -0.5
LANES = 128
SUBLANES = 8
VMEM_LIMIT = 48 * 1024 * 1024
VMEM_LIMIT_WIDE = 56 * 1024 * 1024
DOT_COLS = 1024
SEGMENT_SPLIT = 4

ADAM_LR = 0.001
ADAM_B1 = 0.9
ADAM_B2 = 0.999
ADAM_EPS = 1e-08
ADAM_WD = 0.01
ADAM_STEP = 10

GELU_C = math.sqrt(2.0 / math.pi)
GELU_K = 0.044715

HBM_SPEC = pl.BlockSpec(memory_space=pltpu.HBM)
ANY_SPEC = pl.BlockSpec(memory_space=pl.ANY)
SEM_SPEC = pl.BlockSpec(memory_space=pltpu.SEMAPHORE)
VMEM_SPEC = pl.BlockSpec(memory_space=pltpu.VMEM)
SMEM_SPEC = pl.BlockSpec(memory_space=pltpu.SMEM)


def _sds(shape, dtype):
    return jax.ShapeDtypeStruct(shape, dtype)


def _params(*sem, vmem=VMEM_LIMIT):
    return pltpu.CompilerParams(dimension_semantics=sem or None, vmem_limit_bytes=vmem)


def _gelu(x):
    return 0.5 * x * (1.0 + jnp.tanh(GELU_C * (x + GELU_K * x * x * x)))


def _gelu_and_grad(x):
    x2 = x * x
    t = jnp.tanh(x * (GELU_C + (GELU_C * GELU_K) * x2))
    half = 0.5 + 0.5 * t
    return x * half, half + (x * (1.0 - t * t)) * (0.5 * GELU_C + (1.5 * GELU_C * GELU_K) * x2)


def _silu_and_grad(x):
    s = jax.nn.sigmoid(x)
    return x * s, s * (1.0 + x * (1.0 - s))


def _adamw(w, g, m, v):
    m = ADAM_B1 * m + (1.0 - ADAM_B1) * g
    v = ADAM_B2 * v + (1.0 - ADAM_B2) * (g * g)
    m_hat = m / (1.0 - ADAM_B1 ** ADAM_STEP)
    v_hat = v / (1.0 - ADAM_B2 ** ADAM_STEP)
    delta = -ADAM_LR * (m_hat / (jnp.sqrt(v_hat) + ADAM_EPS) + ADAM_WD * w)
    return delta, m, v


def _mesh_pos():
    return lax.axis_index("x"), lax.axis_index("y"), lax.axis_index("c")


def _slot(x, y, c):
    return 4 * x + 2 * y + c


def _chip(x, y):
    return 2 * x + y


def _sibling(x, y, c):
    return (x, y, 1 - c)


_OTHER_CHIPS = (lambda x, y: (1 - x, y), lambda x, y: (x, 1 - y), lambda x, y: (1 - x, 1 - y))
_ICI_STAGES = (lambda x, y, c: (x ^ c, y ^ (1 - c)), lambda x, y, c: (x ^ (1 - c), y ^ c),
               lambda x, y, c: (1 - x, 1 - y))


class _Copy(NamedTuple):
    src: int
    src_slot: Callable
    dst: int
    dst_slot: Callable
    peer: Callable


def _descriptor(refs, send_sems, recv_sems, k, cp):
    pos = _mesh_pos()
    return pltpu.make_async_remote_copy(
        src_ref=refs[cp.src].at[cp.src_slot(*pos)], dst_ref=refs[cp.dst].at[cp.dst_slot(*pos)],
        send_sem=send_sems.at[k], recv_sem=recv_sems.at[k], device_id=cp.peer(*pos), device_id_type=MESH)


def _gather_first_copies(n_arrays):
    copies = []
    for a in range(n_arrays):
        copies.append(_Copy(a, _slot, a, _slot, _sibling))
        for chip in _OTHER_CHIPS:
            copies.append(_Copy(a, _slot, a, _slot, lambda x, y, c, chip=chip: (*chip(x, y), c)))
    return copies


def _gather_pass_copies(n_arrays):
    copies = []
    for a in range(n_arrays):
        for chip in _OTHER_CHIPS:
            src = lambda x, y, c, chip=chip: _slot(*chip(x, y), c)
            copies.append(_Copy(a, src, a, src, _sibling))
    return copies


def _pair_copies_strided(n_sets):
    copies = []
    for a in range(n_sets):
        for q in range(N_CHIPS):
            copies.append(_Copy(2 * a, lambda x, y, c, q=q: 2 * q + 1 - c, 2 * a + 1, lambda x, y, c, q=q: q, _sibling))
    return copies


def _chip_sum_copies(n_sets):
    copies = []
    for a in range(n_sets):
        for k, chip in enumerate(_OTHER_CHIPS):
            copies.append(_Copy(2 * a, lambda x, y, c, chip=chip: _chip(*chip(x, y)), 2 * a + 1,
                                lambda x, y, c, k=k: k, lambda x, y, c, chip=chip: (*chip(x, y), c)))
    return copies


def _rdma_start(name, arrays, copies):
    n, nc = len(arrays), len(copies)

    def body(*refs):
        in_refs = refs[:n]
        send_sems, recv_sems = refs[n], refs[n + 1]
        token = refs[2 * n + 2]
        for k, cp in enumerate(copies):
            _descriptor(in_refs, send_sems, recv_sems, k, cp).start()
        token[...] = jnp.zeros_like(token)

    out = pl.pallas_call(
        body, name=name,
        out_shape=(pltpu.SemaphoreType.DMA((nc,)), pltpu.SemaphoreType.DMA((nc,)),
                   *[pltpu.HBM(a.shape, a.dtype) for a in arrays], _sds((SUBLANES, LANES), F32)),
        in_specs=[HBM_SPEC] * n, out_specs=(SEM_SPEC, SEM_SPEC, *([HBM_SPEC] * n), VMEM_SPEC),
        input_output_aliases={i: i + 2 for i in range(n)},
        compiler_params=pltpu.CompilerParams(has_side_effects=pltpu.SideEffectType.DATAFLOW_SIDE_EFFECTING),
    )(*[pltpu.with_memory_space_constraint(a, pltpu.HBM) for a in arrays])
    return out[0], out[1], list(out[2:2 + n]), out[2 + n]


def _rdma_wait(name, arrays, send_sems, recv_sems, copies, after):
    n = len(arrays)

    def body(*refs):
        in_refs = refs[:n]
        send_ref, recv_ref = refs[n], refs[n + 1]
        for k, cp in enumerate(copies):
            d = _descriptor(in_refs, send_ref, recv_ref, k, cp)
            d.wait_send()
            d.wait_recv()

    out = pl.pallas_call(
        body, name=name, out_shape=tuple(pltpu.HBM(a.shape, a.dtype) for a in arrays),
        in_specs=[HBM_SPEC] * n + [SEM_SPEC, SEM_SPEC, ANY_SPEC], out_specs=tuple([HBM_SPEC] * n),
        input_output_aliases={i: i for i in range(n)},
        compiler_params=pltpu.CompilerParams(has_side_effects=pltpu.SideEffectType.DATAFLOW_SIDE_EFFECTING),
    )(*arrays, send_sems, recv_sems, after)
    return list(out)


def _cast_into_slot(w, pos, name, after=None):
    rows, cols = w.shape
    tr = min(rows, 256)

    def body(pos_ref, w_ref, *rest):
        rest[-1][...] = w_ref[...].astype(BF16)

    in_specs, args = [pl.BlockSpec((tr, cols), lambda i, p: (i, 0))], [pos, w]
    if after is not None:
        in_specs.append(ANY_SPEC)
        args.append(after)
    return pl.pallas_call(
        body, name=name,
        grid_spec=pltpu.PrefetchScalarGridSpec(
            num_scalar_prefetch=1, grid=(rows // tr,), in_specs=in_specs,
            out_specs=pl.BlockSpec((None, tr, cols), lambda i, p: (_slot(p[0], p[1], p[2]), i, 0))),
        out_shape=_sds((N_DEV, rows, cols), BF16), compiler_params=_params("parallel"),
    )(*args)


def _pair_sum(g, land, pos, name):
    _, rows, cols = land.shape
    tr = min(rows, 128)
    strided = g.shape[0] == N_DEV

    def body(pos_ref, g_ref, l_ref, o_ref):
        o_ref[...] = (g_ref[...].astype(F32) + l_ref[...].astype(F32)).astype(BF16)

    g_map = (lambda q, i, p: (2 * q + p[2], i, 0)) if strided else (lambda q, i, p: (q, i, 0))
    blk = pl.BlockSpec((None, tr, cols), lambda q, i, p: (q, i, 0))
    return pl.pallas_call(
        body, name=name,
        grid_spec=pltpu.PrefetchScalarGridSpec(
            num_scalar_prefetch=1, grid=(N_CHIPS, rows // tr),
            in_specs=[pl.BlockSpec((None, tr, cols), g_map), blk], out_specs=blk),
        out_shape=_sds((N_CHIPS, rows, cols), BF16), compiler_params=_params("parallel", "parallel"),
    )(pos, g, land)


def _reduce_adamw(sums, land, w, m, v, pos, name, own_slot=None):
    rows, cols = w.shape
    tr = min(rows, 64)
    own = (lambda p: _chip(p[0], p[1])) if own_slot is None else (lambda p: own_slot)

    def body(pos_ref, s_ref, l_ref, w_ref, m_ref, v_ref, g_ref, d_ref, nm_ref, nv_ref):
        g = s_ref[...].astype(F32)
        for k in range(N_CHIPS - 1):
            g = g + l_ref[k].astype(F32)
        delta, nm, nv = _adamw(w_ref[...], g, m_ref[...], v_ref[...])
        g_ref[...] = g
        d_ref[...] = delta
        nm_ref[...] = nm
        nv_ref[...] = nv

    spec = pl.BlockSpec((tr, cols), lambda i, p: (i, 0))
    return pl.pallas_call(
        body, name=name,
        grid_spec=pltpu.PrefetchScalarGridSpec(
            num_scalar_prefetch=1, grid=(rows // tr,),
            in_specs=[pl.BlockSpec((None, tr, cols), lambda i, p: (own(p), i, 0)),
                      pl.BlockSpec((N_CHIPS - 1, tr, cols), lambda i, p: (0, i, 0)), spec, spec, spec],
            out_specs=(spec, spec, spec, spec)),
        out_shape=tuple([_sds((rows, cols), F32)] * 4), compiler_params=_params("parallel"),
    )(pos, sums, land, w, m, v)


def _small_allreduce_adamw(g, w, m, v):
    rows = g.shape[0]

    def body(g_ref, w_ref, m_ref, v_ref, gs_ref, d_ref, nm_ref, nv_ref, all_ref, send_sems, recv_sems):
        x, y, c = _mesh_pos()
        me = _slot(x, y, c)
        copies = []
        for k in range(1, N_DEV):
            peer = (x ^ (k >> 2), y ^ ((k >> 1) & 1), c ^ (k & 1))
            copies.append(pltpu.make_async_remote_copy(
                src_ref=g_ref, dst_ref=all_ref.at[me], send_sem=send_sems.at[k - 1],
                recv_sem=recv_sems.at[k - 1], device_id=peer, device_id_type=MESH))
        for cp in copies:
            cp.start()
        all_ref[me] = g_ref[...]
        for cp in copies:
            cp.wait_recv()
        total = all_ref[0]
        for s in range(1, N_DEV):
            total = total + all_ref[s]
        delta, nm, nv = _adamw(w_ref[...], total, m_ref[...], v_ref[...])
        gs_ref[...] = total
        d_ref[...] = delta
        nm_ref[...] = nm
        nv_ref[...] = nv
        for cp in copies:
            cp.wait_send()

    return pl.pallas_call(
        body, name="small_allreduce_adamw", out_shape=tuple([_sds((rows, LANES), F32)] * 4),
        in_specs=[VMEM_SPEC] * 4, out_specs=tuple([VMEM_SPEC] * 4),
        scratch_shapes=[pltpu.VMEM((N_DEV, rows, LANES), F32), pltpu.SemaphoreType.DMA((7,)),
                        pltpu.SemaphoreType.DMA((7,))],
    )(g, w, m, v)


_DOT_DIMS = {"nn": ((1,), (0,)), "nt": ((1,), (1,)), "tn": ((0,), (0,))}


def _dot(a, b, mode):
    return lax.dot_general(a, b, (_DOT_DIMS[mode], ((), ())), preferred_element_type=F32)


def _col_chunks(cols):
    return [(c0, min(c0 + DOT_COLS, cols)) for c0 in range(0, cols, DOT_COLS)]


def _matmul(a, b, mode, out_dtype, name, *, res=None, tm=1024, tn=1024):
    if mode == "tn":
        kdim, mdim = a.shape
    else:
        mdim, kdim = a.shape
    ndim = b.shape[0] if mode == "nt" else b.shape[1]
    tm, tn = min(tm, mdim), min(tn, ndim)
    assert mdim % tm == 0 and ndim % tn == 0, (name, mdim, ndim)

    def body(*refs):
        out = _dot(refs[0][...], refs[1][...], mode)
        if res is not None:
            out = out + refs[2][...]
        refs[-1][...] = out.astype(out_dtype)

    a_spec = pl.BlockSpec((kdim, tm), lambda i, j: (0, i)) if mode == "tn" else pl.BlockSpec((tm, kdim), lambda i, j: (i, 0))
    b_spec = pl.BlockSpec((tn, kdim), lambda i, j: (j, 0)) if mode == "nt" else pl.BlockSpec((kdim, tn), lambda i, j: (0, j))
    o_spec = pl.BlockSpec((tm, tn), lambda i, j: (i, j))
    in_specs, args = [a_spec, b_spec], [a, b]
    if res is not None:
        in_specs.append(o_spec)
        args.append(res)
    return pl.pallas_call(
        body, name=name, grid=(mdim // tm, ndim // tn), in_specs=in_specs, out_specs=o_spec,
        out_shape=_sds((mdim, ndim), out_dtype), compiler_params=_params("parallel", "parallel"),
    )(*args)


def _project(h, w_blocks, block_ids, name, *, proj=None, after=None, tm=1024, tk=1024):
    s, d = h.shape
    _, _, cw = w_blocks.shape
    n = block_ids.shape[0]
    tm, tk = min(tm, s), min(tk, d)
    nk = d // tk

    def body(ids_ref, h_ref, w_ref, *rest):
        o_ref, acc_ref = rest[-2], rest[-1]
        k = pl.program_id(2)

        @pl.when(k == 0)
        def _():
            acc_ref[...] = jnp.zeros_like(acc_ref)

        for c0, c1 in _col_chunks(cw):
            acc_ref[:, c0:c1] += _dot(h_ref[...], w_ref[:, c0:c1], "nn")

        @pl.when(k == nk - 1)
        def _():
            o_ref[...] = acc_ref[...].astype(BF16)

    in_specs = [pl.BlockSpec((tm, tk), lambda j, i, k, ids: (i, k)),
                pl.BlockSpec((None, tk, cw), lambda j, i, k, ids: (ids[j], k, 0))]
    args = [block_ids, h, w_blocks]
    aliases = {}
    if proj is not None:
        in_specs.append(ANY_SPEC)
        args.append(proj)
        aliases = {3: 0}
    if after is not None:
        in_specs.append(ANY_SPEC)
        args.append(after)
    return pl.pallas_call(
        body, name=name,
        grid_spec=pltpu.PrefetchScalarGridSpec(
            num_scalar_prefetch=1, grid=(n, s // tm, d // tk), in_specs=in_specs,
            out_specs=pl.BlockSpec((tm, cw), lambda j, i, k, ids: (i, ids[j])),
            scratch_shapes=[pltpu.VMEM((tm, cw), F32)]),
        out_shape=_sds((s, N_DEV * cw), BF16), input_output_aliases=aliases,
        compiler_params=_params("arbitrary", "arbitrary", "arbitrary", vmem=VMEM_LIMIT_WIDE),
    )(*args)


def _grad_w_in_blocks(h, d_proj, block_ids, cw, name, *, slots=None, slot0=0, prev=None, init=None, after=None,
                      tm=1024, tk=1024):
    s, d = h.shape
    n = block_ids.shape[0]
    slots = n if slots is None else slots
    tm, tk = min(tm, d), min(tk, s)
    nk = s // tk

    def body(ids_ref, h_ref, g_ref, *rest):
        o_ref, acc_ref = rest[-2], rest[-1]
        k = pl.program_id(2)

        @pl.when(k == 0)
        def _():
            acc_ref[...] = jnp.zeros_like(acc_ref) if init is None else rest[0][...].astype(F32)

        for c0, c1 in _col_chunks(cw):
            acc_ref[:, c0:c1] += _dot(h_ref[...], g_ref[:, c0:c1], "tn")

        @pl.when(k == nk - 1)
        def _():
            o_ref[...] = acc_ref[...].astype(BF16)

    in_specs = [pl.BlockSpec((tk, tm), lambda q, i, k, ids: (k, i)),
                pl.BlockSpec((tk, cw), lambda q, i, k, ids: (k, ids[q]))]
    args = [block_ids, h, d_proj]
    aliases = {}
    if init is not None:
        in_specs.append(pl.BlockSpec((None, tm, cw), lambda q, i, k, ids: (q, i, 0)))
        args.append(init)
    if prev is not None:
        aliases = {len(args): 0}
        in_specs.append(ANY_SPEC)
        args.append(prev)
    if after is not None:
        in_specs.append(ANY_SPEC)
        args.append(after)
    return pl.pallas_call(
        body, name=name,
        grid_spec=pltpu.PrefetchScalarGridSpec(
            num_scalar_prefetch=1, grid=(n, d // tm, nk), in_specs=in_specs,
            out_specs=pl.BlockSpec((None, tm, cw), lambda q, i, k, ids: (slot0 + q, i, 0)),
            scratch_shapes=[pltpu.VMEM((tm, cw), F32)]),
        out_shape=_sds((slots, d, cw), BF16), input_output_aliases=aliases,
        compiler_params=_params("parallel", "parallel", "arbitrary", vmem=VMEM_LIMIT_WIDE),
    )(*args)


def _d_hidden(d_proj, w_blocks, after=None, *, tm=1024, tn=1024):
    s = d_proj.shape[0]
    nb, d, cw = w_blocks.shape
    tm, tn = min(tm, s), min(tn, d)

    def body(g_ref, w_ref, *rest):
        o_ref = rest[-1]
        k = pl.program_id(2)

        @pl.when(k == 0)
        def _():
            o_ref[...] = jnp.zeros_like(o_ref)

        o_ref[...] += _dot(g_ref[...], w_ref[...], "nt")

    in_specs = [pl.BlockSpec((tm, cw), lambda i, j, k: (i, k)),
                pl.BlockSpec((None, tn, cw), lambda i, j, k: (k, j, 0))]
    args = [d_proj, w_blocks]
    if after is not None:
        in_specs.append(ANY_SPEC)
        args.append(after)
    return pl.pallas_call(
        body, name="d_h", grid=(s // tm, d // tn, nb), in_specs=in_specs,
        out_specs=pl.BlockSpec((tm, tn), lambda i, j, k: (i, j)), out_shape=_sds((s, d), F32),
        compiler_params=_params("parallel", "parallel", "arbitrary", vmem=VMEM_LIMIT_WIDE),
    )(*args)


def _row_tile(rows):
    return min(rows, 128)


def _segment_specs(rows, d, col0):
    w = d // SEGMENT_SPLIT
    assert col0 % w == 0
    return [pl.BlockSpec((rows, w), lambda i, t=t: (i, col0 // w + t)) for t in range(SEGMENT_SPLIT)]


def _cat(refs):
    return jnp.concatenate([r[...].astype(F32) for r in refs], axis=1)


def _rmsnorm_fwd(x, g, after):
    s, d = x.shape
    tr = min(s, 2 * _row_tile(s))

    def body(x_ref, g_ref, after_ref, h_ref):
        xv = x_ref[...]
        r = lax.rsqrt(jnp.mean(xv * xv, axis=-1, keepdims=True) + NORM_EPS)
        h_ref[...] = (xv * r * g_ref[...]).astype(BF16)

    row = pl.BlockSpec((tr, d), lambda i: (i, 0))
    vec = pl.BlockSpec((1, d), lambda i: (0, 0))
    return pl.pallas_call(body, name="rmsnorm_fwd", grid=(s // tr,), in_specs=[row, vec, ANY_SPEC], out_specs=row,
                          out_shape=_sds((s, d), BF16), compiler_params=_params("parallel"))(x, g, after)


def _merge_fwd(y_a, y_b, proj, col_m):
    s, d = y_a.shape
    tr = _row_tile(s)
    ns = SEGMENT_SPLIT

    def body(ya_ref, yb_ref, *rest):
        ma, mb, o_ref = _cat(rest[:ns]), _cat(rest[ns:2 * ns]), rest[2 * ns]
        o_ref[...] = (jax.nn.sigmoid(ma) * ya_ref[...].astype(F32)
                      + jax.nn.sigmoid(mb) * yb_ref[...].astype(F32)).astype(BF16)

    row = pl.BlockSpec((tr, d), lambda i: (i, 0))
    return pl.pallas_call(
        body, name="merge_fwd", grid=(s // tr,),
        in_specs=[row, row, *_segment_specs(tr, d, col_m), *_segment_specs(tr, d, col_m + d)],
        out_specs=row, out_shape=_sds((s, d), BF16), compiler_params=_params("parallel"),
    )(y_a, y_b, *([proj] * (2 * ns)))


def _loss_and_final_norm_bwd(x2, target, g):
    s, d = x2.shape
    tr = min(s, 2 * _row_tile(s))

    def body(x_ref, t_ref, g_ref, loss_ref, dg_ref, dxb_ref):
        @pl.when(pl.program_id(0) == 0)
        def _():
            loss_ref[...] = jnp.zeros_like(loss_ref)
            dg_ref[...] = jnp.zeros_like(dg_ref)

        xv, gv = x_ref[...], g_ref[...]
        r = lax.rsqrt(jnp.mean(xv * xv, axis=-1, keepdims=True) + NORM_EPS)
        xhat = xv * r
        err = xhat * gv - t_ref[...]
        loss_ref[...] += 0.5 * jnp.sum(jnp.mean(err * err, axis=-1, keepdims=True))
        dy = err / d
        dg_ref[...] += jnp.sum(dy * xhat, axis=0, keepdims=True)
        dyg = dy * gv
        dxb_ref[...] = (r * (dyg - xhat * jnp.mean(dyg * xhat, axis=-1, keepdims=True))).astype(BF16)

    row = pl.BlockSpec((tr, d), lambda i: (i, 0))
    vec = pl.BlockSpec((1, d), lambda i: (0, 0))
    return pl.pallas_call(
        body, name="loss_final_norm_bwd", grid=(s // tr,), in_specs=[row, row, vec],
        out_specs=(pl.BlockSpec((SUBLANES, LANES), lambda i: (0, 0)), vec, row),
        out_shape=(_sds((SUBLANES, LANES), F32), _sds((1, d), F32), _sds((s, d), BF16)),
        compiler_params=_params("arbitrary"))(x2, target, g)


def _write_behind(step, n_steps, buf, sems, wide_ref, rows, col0, fill):
    cols = buf.shape[2]
    slot = step % 2

    def copy(at_step, at_slot):
        dst = wide_ref.at[pl.ds(pl.multiple_of(at_step * rows, rows), rows), pl.ds(col0, cols)]
        return pltpu.make_async_copy(buf.at[at_slot], dst, sems.at[at_slot])

    @pl.when(step >= 2)
    def _():
        copy(step - 2, slot).wait()

    fill(buf.at[slot])
    copy(step, slot).start()

    @pl.when(step == n_steps - 1)
    def _():
        copy(step, slot).wait()
        if n_steps > 1:
            copy(step - 1, 1 - slot).wait()


def _merge_bwd(d_merged, y_a, y_b, proj, col_m, d_proj):
    s, d = y_a.shape
    tr = _row_tile(s)
    ns = SEGMENT_SPLIT
    n_steps = s // tr

    def body(dm_ref, ya_ref, yb_ref, *rest):
        ma, mb = _cat(rest[:ns]), _cat(rest[ns:2 * ns])
        dya_ref, dyb_ref, wide_ref, buf, sems = rest[2 * ns + 1:]
        dm = dm_ref[...].astype(F32)
        sa = jax.nn.sigmoid(ma)
        sb = jax.nn.sigmoid(mb)
        dya_ref[...] = (dm * sa).astype(BF16)
        dyb_ref[...] = (dm * sb).astype(BF16)

        def fill(out):
            out[:, :d] = (dm * ya_ref[...].astype(F32) * (sa * (1.0 - sa))).astype(BF16)
            out[:, d:] = (dm * yb_ref[...].astype(F32) * (sb * (1.0 - sb))).astype(BF16)

        _write_behind(pl.program_id(0), n_steps, buf, sems, wide_ref, tr, col_m, fill)

    row = pl.BlockSpec((tr, d), lambda i: (i, 0))
    n_in = 3 + 2 * ns
    return pl.pallas_call(
        body, name="merge_bwd", grid=(n_steps,),
        in_specs=[row, row, row, *_segment_specs(tr, d, col_m), *_segment_specs(tr, d, col_m + d), ANY_SPEC],
        out_specs=(row, row, ANY_SPEC),
        out_shape=(_sds((s, d), BF16), _sds((s, d), BF16), _sds(d_proj.shape, BF16)),
        input_output_aliases={n_in: 2},
        scratch_shapes=[pltpu.VMEM((2, tr, 2 * d), BF16), pltpu.SemaphoreType.DMA((2,))],
        compiler_params=_params("arbitrary"))(d_merged, y_a, y_b, *([proj] * (2 * ns)), d_proj)


def _place(d_proj, piece, col0, name):
    s, w = piece.shape
    bw = math.gcd(w, col0) if col0 else w
    tr = min(s, 512)

    def body(p_ref, wide_in, o_ref):
        o_ref[...] = p_ref[...]

    return pl.pallas_call(
        body, name=name, grid=(s // tr, w // bw),
        in_specs=[pl.BlockSpec((tr, bw), lambda i, j: (i, j)), ANY_SPEC],
        out_specs=pl.BlockSpec((tr, bw), lambda i, j: (i, col0 // bw + j)),
        out_shape=_sds(d_proj.shape, d_proj.dtype), input_output_aliases={1: 0},
        compiler_params=_params("parallel", "parallel"))(piece, d_proj)


def _input_grad(d_h, x, g, dx2):
    s, d = x.shape
    tr = _row_tile(s)

    def body(dh_ref, x_ref, g_ref, dx2_ref, gx_ref, dg_ref):
        @pl.when(pl.program_id(0) == 0)
        def _():
            dg_ref[...] = jnp.zeros_like(dg_ref)

        xv, dh = x_ref[...], dh_ref[...]
        r = lax.rsqrt(jnp.mean(xv * xv, axis=-1, keepdims=True) + NORM_EPS)
        xhat = xv * r
        dg_ref[...] += jnp.sum(dh * xhat, axis=0, keepdims=True)
        dyg = dh * g_ref[...]
        gx_ref[...] = dx2_ref[...].astype(F32) + r * (dyg - xhat * jnp.mean(dyg * xhat, axis=-1, keepdims=True))

    row = pl.BlockSpec((tr, d), lambda i: (i, 0))
    vec = pl.BlockSpec((1, d), lambda i: (0, 0))
    return pl.pallas_call(
        body, name="input_grad", grid=(s // tr,), in_specs=[row, row, vec, row], out_specs=(row, vec),
        out_shape=(_sds((s, d), F32), _sds((1, d), F32)), compiler_params=_params("arbitrary"))(d_h, x, g, dx2)


def _rope_tables(positions):
    inv_freq = ROPE_THETA ** (-jnp.arange(ROPE_HALF, dtype=F32) * 2.0 / ROPE_DIM)
    ang = positions.astype(F32)[:, None] * inv_freq
    cos, sin = jnp.cos(ang), jnp.sin(ang)
    zero = jnp.zeros((positions.shape[0], HEAD_DIM - ROPE_DIM), F32)
    zero_h = jnp.zeros_like(sin)
    c = jnp.concatenate([cos, cos, zero + 1.0], axis=1)
    up = jnp.concatenate([-sin, zero_h, zero], axis=1)
    down = jnp.concatenate([zero_h, sin, zero], axis=1)
    reps = LANES // HEAD_DIM
    return jnp.stack([jnp.tile(c, (1, reps)), jnp.tile(up, (1, reps)), jnp.tile(down, (1, reps))])


def _lane_tiles(x):
    return [x[:, t * LANES:(t + 1) * LANES] for t in range(x.shape[1] // LANES)]


def _rope(x, tab):
    out = [xt * tab[0] + pltpu.roll(xt, LANES - ROPE_HALF, 1) * tab[1] + pltpu.roll(xt, ROPE_HALF, 1) * tab[2]
           for xt in _lane_tiles(x)]
    return out[0] if len(out) == 1 else jnp.concatenate(out, axis=1)


def _rope_bwd(g, tab):
    out = [gt * tab[0] + pltpu.roll(gt * tab[1], ROPE_HALF, 1) + pltpu.roll(gt * tab[2], LANES - ROPE_HALF, 1)
           for gt in _lane_tiles(g)]
    return out[0] if len(out) == 1 else jnp.concatenate(out, axis=1)


def _head(x, h):
    return x[:, h * HEAD_DIM:(h + 1) * HEAD_DIM]


def _stack_heads(x, first, count):
    return jnp.concatenate([_head(x, first + h) for h in range(count)], axis=0)


def _dot_nt(a, b):
    return lax.dot_general(a, b, (((1,), (1,)), ((), ())), preferred_element_type=F32)


def _causal(rows):
    qi = lax.broadcasted_iota(jnp.int32, (rows, BLOCK), 0) % BLOCK
    return lax.broadcasted_iota(jnp.int32, (rows, BLOCK), 1) <= qi


def _band_probs_by_head(qs, k_band, sink_ref, first, count, causal, blk):
    s_band = _dot_nt(qs, k_band)
    p_all, p_band, p_sink = [], [], []
    for h in range(count):
        rows = slice(h * BLOCK, (h + 1) * BLOCK)
        sink = sink_ref[first + h]
        s = jnp.where(causal, s_band[rows, :BLOCK], jnp.where(blk > 0, s_band[rows, BLOCK:], -jnp.inf))
        m = jnp.maximum(jnp.max(s, axis=-1, keepdims=True), sink)
        p = jnp.exp(s - m)
        ps = jnp.exp(sink - m)
        inv = 1.0 / (jnp.sum(p, axis=-1, keepdims=True) + ps)
        p = p * inv
        p_all.append(p)
        p_sink.append(ps * inv)
        p_band.append(_split_band(p, causal))
    cat = lambda parts: jnp.concatenate(parts, axis=0)
    return cat([c for c, _ in p_band]), cat([v for _, v in p_band]), cat(p_all), cat(p_sink)


def _split_band(x, causal):
    return jnp.where(causal, x, 0.0).astype(BF16), jnp.where(causal, 0.0, x).astype(BF16)


def _attn_dims(s, d, kv):
    n_kv = kv // HEAD_DIM
    group = d // kv
    qw = 2 * group * HEAD_DIM
    assert n_kv % 2 == 0 and (d + 2 * kv) % qw == 0 and s % BLOCK == 0
    return group, qw, n_kv // 2, s // BLOCK


def _attention_fwd(proj, tables, sink, kv, after):
    s, d = proj.shape[0], sink.shape[0] * HEAD_DIM
    group, qw, n_pairs, nb = _attn_dims(s, d, kv)

    def body(sink_ref, q_ref, kc_ref, kp_ref, vc_ref, vp_ref, ga_ref, tc_ref, tp_ref, after_ref,
             attn_ref, ain_ref, qrot_ref, krot_ref):
        pair, blk = pl.program_id(0), pl.program_id(1)
        tab_c, tab_p = tc_ref[...], tp_ref[...]
        q = _rope(q_ref[...].astype(F32), tab_c) * ATTN_SCALE
        k_cur, k_prev = _rope(kc_ref[...].astype(F32), tab_c), _rope(kp_ref[...].astype(F32), tab_p)
        qrot_ref[...] = q.astype(BF16)
        krot_ref[...] = k_cur.astype(BF16)
        v_cur, v_prev = vc_ref[...], vp_ref[...]
        causal = _causal(BLOCK)
        outs = []
        for a in range(2):
            k_band = jnp.concatenate([_head(k_cur, a), _head(k_prev, a)], axis=0).astype(BF16)
            vc, vp = _head(v_cur, a).astype(BF16), _head(v_prev, a).astype(BF16)
            qs = _stack_heads(q, a * group, group).astype(BF16)
            p_cur, p_prev, _, _ = _band_probs_by_head(qs, k_band, sink_ref, (2 * pair + a) * group, group, causal, blk)
            o = jnp.dot(p_cur, vc, preferred_element_type=F32) + jnp.dot(p_prev, vp, preferred_element_type=F32)
            outs += [o[h * BLOCK:(h + 1) * BLOCK] for h in range(group)]
        attn = jnp.concatenate(outs, axis=1)
        attn_ref[...] = attn.astype(BF16)
        silu, _ = _silu_and_grad(ga_ref[...].astype(F32))
        ain_ref[...] = (attn * silu).astype(BF16)

    k0, v0, g0 = d // LANES, (d + kv) // LANES, (d + 2 * kv) // qw
    prev = lambda i: jnp.maximum(i - 1, 0)
    in_specs = [
        SMEM_SPEC,
        pl.BlockSpec((BLOCK, qw), lambda p, i: (i, p)),
        pl.BlockSpec((BLOCK, LANES), lambda p, i: (i, k0 + p)),
        pl.BlockSpec((BLOCK, LANES), lambda p, i: (prev(i), k0 + p)),
        pl.BlockSpec((BLOCK, LANES), lambda p, i: (i, v0 + p)),
        pl.BlockSpec((BLOCK, LANES), lambda p, i: (prev(i), v0 + p)),
        pl.BlockSpec((BLOCK, qw), lambda p, i: (i, g0 + p)),
        pl.BlockSpec((3, BLOCK, LANES), lambda p, i: (0, i, 0)),
        pl.BlockSpec((3, BLOCK, LANES), lambda p, i: (0, prev(i), 0)),
        ANY_SPEC,
    ]
    out = pl.BlockSpec((BLOCK, qw), lambda p, i: (i, p))
    k_out = pl.BlockSpec((BLOCK, LANES), lambda p, i: (i, p))
    return pl.pallas_call(
        body, name="attention_fwd", grid=(n_pairs, nb), in_specs=in_specs, out_specs=(out, out, out, k_out),
        out_shape=(_sds((s, d), BF16), _sds((s, d), BF16), _sds((s, d), BF16), _sds((s, kv), BF16)),
        compiler_params=_params("parallel", "parallel"),
    )(sink, proj, proj, proj, proj, proj, proj, tables, tables, after)


def _attention_bwd(proj, q_rot, k_rot, tables, sink, kv, attn, d_ain, after):
    s, d = proj.shape[0], sink.shape[0] * HEAD_DIM
    group, qw, n_pairs, nb = _attn_dims(s, d, kv)

    def body(sink_ref, q_ref, kc_ref, kp_ref, vc_ref, vp_ref, ga_ref, tc_ref, tp_ref, attn_ref, dain_ref, after_ref,
             dq_ref, dk_ref, dv_ref, dga_ref, dsink_ref, carry_k, carry_v, sink_acc):
        pair, blk = pl.program_id(0), pl.program_id(1)

        @pl.when(blk == 0)
        def _():
            carry_k[...] = jnp.zeros_like(carry_k)
            carry_v[...] = jnp.zeros_like(carry_v)
            sink_acc[...] = jnp.zeros_like(sink_acc)

        @pl.when(blk < nb)
        def _():
            tab_c, tab_p = tc_ref[...], tp_ref[...]
            q = q_ref[...].astype(F32)
            k_cur, k_prev = kc_ref[...].astype(F32), kp_ref[...].astype(F32)
            v_cur, v_prev = vc_ref[...], vp_ref[...]
            silu, silu_grad = _silu_and_grad(ga_ref[...].astype(F32))
            d_ain_v = dain_ref[...].astype(F32)
            dga_ref[...] = (d_ain_v * attn_ref[...].astype(F32) * silu_grad).astype(BF16)
            d_attn = d_ain_v * silu
            q_t = q.T
            d_attn_t = d_attn.T
            causal = _causal(BLOCK)
            dq_parts = []
            dk_t = {"cur": [], "prev": []}
            dv_t = {"cur": [], "prev": []}
            lane = lax.broadcasted_iota(jnp.int32, (BLOCK, LANES), 1)
            dsink = jnp.zeros((BLOCK, LANES), F32)
            for a in range(2):
                first = a * group
                qs = _stack_heads(q, first, group).astype(BF16)
                kc, kp = _head(k_cur, a).astype(BF16), _head(k_prev, a).astype(BF16)
                k_band = jnp.concatenate([_head(k_cur, a), _head(k_prev, a)], axis=0).astype(BF16)
                v_band = jnp.concatenate([_head(v_cur, a), _head(v_prev, a)], axis=0).astype(BF16)
                p_cur, p_prev, p, p_sink = _band_probs_by_head(qs, k_band, sink_ref, (2 * pair + a) * group, group,
                                                               causal, blk)
                do = _stack_heads(d_attn, first, group).astype(BF16)
                dp_band = _dot_nt(do, v_band)
                ds_parts, delta = [], []
                for h in range(group):
                    rows = slice(h * BLOCK, (h + 1) * BLOCK)
                    dp = jnp.where(causal, dp_band[rows, :BLOCK], dp_band[rows, BLOCK:])
                    delta.append(jnp.sum(p[rows] * dp, axis=-1, keepdims=True))
                    ds_parts.append(_split_band(p[rows] * (dp - delta[-1]), causal))
                ds_cur = jnp.concatenate([c for c, _ in ds_parts], axis=0)
                ds_prev = jnp.concatenate([v for _, v in ds_parts], axis=0)
                delta = jnp.concatenate(delta, axis=0)
                dqs = (jnp.dot(ds_cur, kc, preferred_element_type=F32)
                       + jnp.dot(ds_prev, kp, preferred_element_type=F32)) * ATTN_SCALE
                dq_parts += [dqs[h * BLOCK:(h + 1) * BLOCK] for h in range(group)]
                rows = lambda t: jnp.concatenate(
                    [t[(first + h) * HEAD_DIM:(first + h + 1) * HEAD_DIM] for h in range(group)], axis=1).astype(BF16)
                qs_t, do_t = rows(q_t), rows(d_attn_t)
                dk_t["cur"].append(jnp.dot(qs_t, ds_cur, preferred_element_type=F32))
                dk_t["prev"].append(jnp.dot(qs_t, ds_prev, preferred_element_type=F32))
                dv_t["cur"].append(jnp.dot(do_t, p_cur, preferred_element_type=F32))
                dv_t["prev"].append(jnp.dot(do_t, p_prev, preferred_element_type=F32))
                ds_sink = -(p_sink * delta)
                for h in range(group):
                    dsink = dsink + jnp.where(lane == first + h, ds_sink[h * BLOCK:(h + 1) * BLOCK], 0.0)
            sink_acc[...] += dsink
            dq_ref[...] = _rope_bwd(jnp.concatenate(dq_parts, axis=1), tab_c).astype(BF16)
            pair_block = lambda parts: jnp.concatenate(parts, axis=0).T
            dk_ref[...] = (carry_k[...] + _rope_bwd(pair_block(dk_t["prev"]), tab_p)).astype(BF16)
            dv_ref[...] = (carry_v[...] + pair_block(dv_t["prev"])).astype(BF16)
            carry_k[...] = _rope_bwd(pair_block(dk_t["cur"]), tab_c)
            carry_v[...] = pair_block(dv_t["cur"])

        @pl.when(blk == nb)
        def _():
            dk_ref[...] = carry_k[...].astype(BF16)
            dv_ref[...] = carry_v[...].astype(BF16)
            dsink_ref[0] = jnp.sum(sink_acc[...], axis=0, keepdims=True)

    v0, g0 = (d + kv) // LANES, (d + 2 * kv) // qw
    cur = lambda i: jnp.minimum(i, nb - 1)
    prev = lambda i: jnp.maximum(cur(i) - 1, 0)
    back = lambda i: jnp.maximum(i - 1, 0)
    q_spec = pl.BlockSpec((BLOCK, qw), lambda p, i: (cur(i), p))
    in_specs = [
        SMEM_SPEC,
        q_spec,
        pl.BlockSpec((BLOCK, LANES), lambda p, i: (cur(i), p)),
        pl.BlockSpec((BLOCK, LANES), lambda p, i: (prev(i), p)),
        pl.BlockSpec((BLOCK, LANES), lambda p, i: (cur(i), v0 + p)),
        pl.BlockSpec((BLOCK, LANES), lambda p, i: (prev(i), v0 + p)),
        pl.BlockSpec((BLOCK, qw), lambda p, i: (cur(i), g0 + p)),
        pl.BlockSpec((3, BLOCK, LANES), lambda p, i: (0, cur(i), 0)),
        pl.BlockSpec((3, BLOCK, LANES), lambda p, i: (0, prev(i), 0)),
        q_spec,
        q_spec,
        ANY_SPEC,
    ]
    kv_out = pl.BlockSpec((BLOCK, LANES), lambda p, i: (back(i), p))
    return pl.pallas_call(
        body, name="attention_bwd", grid=(n_pairs, nb + 1), in_specs=in_specs,
        out_specs=(q_spec, kv_out, kv_out, q_spec, pl.BlockSpec((1, 1, LANES), lambda p, i: (p, 0, 0))),
        out_shape=(_sds((s, d), BF16), _sds((s, kv), BF16), _sds((s, kv), BF16), _sds((s, d), BF16),
                   _sds((n_pairs, 1, LANES), F32)),
        scratch_shapes=[pltpu.VMEM((BLOCK, LANES), F32), pltpu.VMEM((BLOCK, LANES), F32),
                        pltpu.VMEM((BLOCK, LANES), F32)],
        compiler_params=_params("parallel", "arbitrary"),
    )(sink, q_rot, k_rot, k_rot, proj, proj, proj, tables, tables, attn, d_ain, after)


def _gmlp_core(gu, gv, ln_g, ln_b, w_ref, bias_t):
    xc = gv - jnp.mean(gv, axis=-1, keepdims=True)
    rstd = lax.rsqrt(jnp.mean(xc * xc, axis=-1, keepdims=True) + LN_EPS)
    xhat = xc * rstd
    vn = (xhat * ln_g + ln_b).astype(BF16)
    gd = gu.shape[1] // GMLP_GROUPS
    tri = (lax.broadcasted_iota(jnp.int32, (BLOCK, BLOCK), 0) >= lax.broadcasted_iota(jnp.int32, (BLOCK, BLOCK), 1))
    w_tri = [jnp.where(tri, w_ref[g], 0.0).astype(BF16) for g in range(GMLP_GROUPS)]
    mixed = jnp.concatenate(
        [jnp.dot(w_tri[g], vn[:, g * gd:(g + 1) * gd], preferred_element_type=F32) + bias_t[:, g:g + 1]
         for g in range(GMLP_GROUPS)], axis=1)
    return gu, xhat, rstd, vn, w_tri, tri, mixed


def _whole(shape):
    return pl.BlockSpec(shape, lambda i: tuple(0 for _ in shape))


def _gmlp_fwd(proj, col_u, d, w_s, bias_t, ln_g, ln_b, after):
    s = proj.shape[0]
    ns = SEGMENT_SPLIT

    def body(*refs):
        u, vg, gb = _cat(refs[:ns]), _cat(refs[ns:2 * ns]), _cat(refs[2 * ns:3 * ns])
        w_ref, bt_ref, lg_ref, lb_ref, after_ref, o_ref = refs[3 * ns:]
        gu, _, _, _, _, _, mixed = _gmlp_core(_gelu(u), _gelu(vg), lg_ref[...], lb_ref[...], w_ref, bt_ref[...])
        silu, _ = _silu_and_grad(gb)
        o_ref[...] = ((gu * mixed) * silu).astype(BF16)

    segs = [sp for j in range(3) for sp in _segment_specs(BLOCK, d, col_u + j * d)]
    return pl.pallas_call(
        body, name="gmlp_fwd", grid=(s // BLOCK,),
        in_specs=[*segs, _whole(w_s.shape), _whole(bias_t.shape), _whole((1, d)), _whole((1, d)), ANY_SPEC],
        out_specs=pl.BlockSpec((BLOCK, d), lambda i: (i, 0)), out_shape=_sds((s, d), BF16),
        compiler_params=_params("parallel"),
    )(*([proj] * (3 * ns)), w_s, bias_t, ln_g, ln_b, after)


def _gmlp_bwd(proj, col_u, d, w_s, bias_t, ln_g, ln_b, d_bin, after, d_proj):
    s = proj.shape[0]
    gd = d // GMLP_GROUPS
    ns = SEGMENT_SPLIT
    n_steps = s // BLOCK

    def body(*refs):
        u, vg, gb = _cat(refs[:ns]), _cat(refs[ns:2 * ns]), _cat(refs[2 * ns:3 * ns])
        (w_ref, bt_ref, lg_ref, lb_ref, dbin_ref, after_ref, wide_in, wide_ref, dw_ref, dbt_ref, dlg_ref, dlb_ref,
         buf, sems) = refs[3 * ns:]

        @pl.when(pl.program_id(0) == 0)
        def _():
            dw_ref[...] = jnp.zeros_like(dw_ref)
            dbt_ref[...] = jnp.zeros_like(dbt_ref)
            dlg_ref[...] = jnp.zeros_like(dlg_ref)
            dlb_ref[...] = jnp.zeros_like(dlb_ref)

        ln_g = lg_ref[...]
        (gu, gu_grad), (gv, gv_grad) = _gelu_and_grad(u), _gelu_and_grad(vg)
        gu, xhat, rstd, vn, w_tri, tri, mixed = _gmlp_core(gu, gv, ln_g, lb_ref[...], w_ref, bt_ref[...])
        silu, silu_grad = _silu_and_grad(gb)
        d_bin_v = dbin_ref[...].astype(F32)
        d_sg = d_bin_v * silu
        d_gate = (d_bin_v * (gu * mixed) * silu_grad).astype(BF16)
        d_u = (d_sg * mixed * gu_grad).astype(BF16)
        d_mixed = d_sg * gu
        d_mixed_b = d_mixed.astype(BF16)
        d_vn, d_bias = [], []
        for g in range(GMLP_GROUPS):
            dm_g = d_mixed_b[:, g * gd:(g + 1) * gd]
            d_bias.append(jnp.sum(d_mixed[:, g * gd:(g + 1) * gd], axis=-1, keepdims=True))
            dw = lax.dot_general(dm_g, vn[:, g * gd:(g + 1) * gd], (((1,), (1,)), ((), ())),
                                 preferred_element_type=F32)
            dw_ref[g] += jnp.where(tri, dw, 0.0)
            d_vn.append(lax.dot_general(w_tri[g], dm_g, (((0,), (0,)), ((), ())), preferred_element_type=F32))
        dbt_ref[...] += jnp.concatenate(d_bias, axis=1)
        d_vn = jnp.concatenate(d_vn, axis=1)
        dlg_ref[...] += jnp.sum(d_vn * xhat, axis=0, keepdims=True)
        dlb_ref[...] += jnp.sum(d_vn, axis=0, keepdims=True)
        d_xhat = d_vn * ln_g
        d_gv = rstd * (d_xhat - jnp.mean(d_xhat, axis=-1, keepdims=True)
                       - xhat * jnp.mean(d_xhat * xhat, axis=-1, keepdims=True))
        d_v = (d_gv * gv_grad).astype(BF16)

        def fill(out):
            out[:, :d] = d_u
            out[:, d:2 * d] = d_v
            out[:, 2 * d:] = d_gate

        _write_behind(pl.program_id(0), n_steps, buf, sems, wide_ref, BLOCK, col_u, fill)

    segs = [sp for j in range(3) for sp in _segment_specs(BLOCK, d, col_u + j * d)]
    return pl.pallas_call(
        body, name="gmlp_bwd", grid=(n_steps,),
        in_specs=[*segs, _whole(w_s.shape), _whole(bias_t.shape), _whole((1, d)), _whole((1, d)),
                  pl.BlockSpec((BLOCK, d), lambda i: (i, 0)), ANY_SPEC, ANY_SPEC],
        out_specs=(ANY_SPEC, _whole(w_s.shape), _whole(bias_t.shape), _whole((1, d)), _whole((1, d))),
        out_shape=(_sds(d_proj.shape, BF16), _sds(w_s.shape, F32), _sds(bias_t.shape, F32), _sds((1, d), F32),
                   _sds((1, d), F32)),
        input_output_aliases={3 * ns + 6: 0},
        scratch_shapes=[pltpu.VMEM((2, BLOCK, 3 * d), BF16), pltpu.SemaphoreType.DMA((2,))],
        compiler_params=_params("arbitrary"),
    )(*([proj] * (3 * ns)), w_s, bias_t, ln_g, ln_b, d_bin, after, d_proj)


def _pack(parts):
    rows = []
    tile = SUBLANES * LANES
    for p in parts:
        flat = p.astype(F32).reshape(-1)
        padded = -(-flat.shape[0] // tile) * tile
        rows.append(jnp.pad(flat, (0, padded - flat.shape[0])).reshape(-1, LANES))
    return jnp.concatenate(rows, axis=0)


def _unpack(packed, shapes):
    out, row = [], 0
    tile = SUBLANES * LANES
    for shape in shapes:
        size = math.prod(shape)
        n_rows = -(-size // tile) * SUBLANES
        out.append(packed[row:row + n_rows].reshape(-1)[:size].reshape(shape))
        row += n_rows
    return out


def kernel(x, positions, norm_g, w_in, attn_sink, gmlp_ln_g, gmlp_ln_b, w_spatial, b_spatial, w_up_attn, w_up_gmlp, w_out, final_norm_g, loss_target, m_norm_g, m_w_in, m_attn_sink, m_gmlp_ln_g, m_gmlp_ln_b, m_w_spatial, m_b_spatial, m_w_up_attn, m_w_up_gmlp, m_w_out, m_final_norm_g, v_norm_g, v_w_in, v_attn_sink, v_gmlp_ln_g, v_gmlp_ln_b, v_w_spatial, v_b_spatial, v_w_up_attn, v_w_up_gmlp, v_w_out, v_final_norm_g):
    x2d, target = x[0], loss_target[0]
    s, d = x2d.shape
    n_q_heads = attn_sink.shape[1]
    cw = w_in.shape[2]
    rw = w_up_attn.shape[1]
    kv = (cw * N_DEV - 7 * d) // 2
    col_u, col_m = 2 * d + 2 * kv, 5 * d + 2 * kv
    final_g = final_norm_g.reshape(1, d)
    sink = attn_sink[0]
    w_s = w_spatial[0]
    bias_t = b_spatial[0].T
    mx, my, mc = _mesh_pos()
    pos = jnp.stack([mx, my, mc]).astype(jnp.int32)

    def one_block(fn):
        return jnp.reshape(fn(mx, my, mc), (1,)).astype(jnp.int32)

    w_in_b = _cast_into_slot(w_in[0], pos, "cast_w_in")
    to_sibling = [_Copy(0, _slot, 0, _slot, _sibling)]
    ici = [[_Copy(0, _slot, 0, _slot, lambda x, y, c, chip=chip: (*chip(x, y, c), c))] for chip in _ICI_STAGES[:2]]
    relayed = lambda x, y, c: _slot(*_ICI_STAGES[1](x, y, c), c)
    ici.append([_Copy(0, relayed, 0, relayed, lambda x, y, c: (*_ICI_STAGES[0](x, y, c), c))])
    passes = []
    for chip in _ICI_STAGES:
        landed = lambda x, y, c, chip=chip: _slot(*chip(x, y, c), c)
        passes.append([_Copy(0, landed, 0, landed, _sibling)])
    sib_sems = _rdma_start("w_in_sibling_start", [w_in_b], to_sibling)
    ici_sems = _rdma_start("w_in_ici0_start", sib_sems[2], ici[0])
    h = _rmsnorm_fwd(x2d, norm_g, ici_sems[3])
    proj = _project(h, ici_sems[2][0], one_block(_slot), "projection_own")
    w_blocks = _rdma_wait("w_in_sibling_wait", ici_sems[2], sib_sems[0], sib_sems[1], to_sibling, proj)
    proj = _project(h, w_blocks[0], one_block(lambda x, y, c: _slot(x, y, 1 - c)), "projection_sibling", proj=proj)
    w_blocks = _rdma_wait("w_in_ici0_wait", w_blocks, ici_sems[0], ici_sems[1], ici[0], proj)
    first = _gather_first_copies(3)
    for k, chip in enumerate(_ICI_STAGES):
        if k + 1 < len(_ICI_STAGES):
            ici_sems = _rdma_start("w_in_ici%d_start" % (k + 1), w_blocks, ici[k + 1])
            w_blocks = ici_sems[2]
        else:
            send1, recv1, thru, _ = _rdma_start("gather_squares_start", squares + w_blocks, first)
            squares, w_blocks = thru[:3], thru[3:]
        pass_sems = _rdma_start("w_in_pass%d_start" % k, w_blocks, passes[k])
        if k == 0:
            squares = [_cast_into_slot(w[0], pos, "cast_" + nm, after=pass_sems[3])
                       for nm, w in (("w_up_attn", w_up_attn), ("w_up_gmlp", w_up_gmlp), ("w_out", w_out))]
        proj = _project(h, pass_sems[2][0], one_block(lambda x, y, c, chip=chip: _slot(*chip(x, y, c), c)),
                        "projection_ici%d" % k, proj=proj, after=pass_sems[3])
        w_blocks = _rdma_wait("w_in_pass%d_wait" % k, pass_sems[2], pass_sems[0], pass_sems[1], passes[k], proj)
        proj = _project(h, w_blocks[0], one_block(lambda x, y, c, chip=chip: _slot(*chip(x, y, 1 - c), 1 - c)),
                        "projection_pass%d" % k, proj=proj)
        if k + 1 < len(_ICI_STAGES):
            w_blocks = _rdma_wait("w_in_ici%d_wait" % (k + 1), w_blocks, ici_sems[0], ici_sems[1], ici[k + 1], proj)
    w_in_b = w_blocks[0]

    tables = _rope_tables(positions[0])
    attn, a_in, q_rot, k_rot = _attention_fwd(proj, tables, sink, kv, proj)
    squares = _rdma_wait("gather_squares_wait", squares, send1, recv1, first, attn)
    passed = _gather_pass_copies(3)
    send2, recv2, squares, token = _rdma_start("pass_squares_start", squares, passed)
    b_in = _gmlp_fwd(proj, col_u, d, w_s, bias_t, gmlp_ln_g, gmlp_ln_b, token)
    squares = _rdma_wait("pass_squares_wait", squares, send2, recv2, passed, b_in)
    w_ua, w_ug, w_o = [w.reshape(N_DEV * rw, d) for w in squares]
    y_a = _matmul(a_in, w_ua, "nn", BF16, "up_attn")
    y_b = _matmul(b_in, w_ug, "nn", BF16, "up_gmlp")
    merged = _merge_fwd(y_a, y_b, proj, col_m)
    x_out = _matmul(merged, w_o, "nn", F32, "out_proj", res=x2d, tn=512)
    loss_p, d_final_g, dx2_b = _loss_and_final_norm_bwd(x_out, target, final_g)

    d_merged = _matmul(dx2_b, w_o, "nt", BF16, "d_merged")
    g_w_out = _matmul(merged, dx2_b, "tn", BF16, "g_w_out")
    d_ya, d_yb, d_proj = _merge_bwd(d_merged, y_a, y_b, proj, col_m, lax.empty(proj.shape, BF16))
    d_ain = _matmul(d_ya, w_ua, "nt", BF16, "d_a_in")
    g_w_ua = _matmul(a_in, d_ya, "tn", BF16, "g_w_up_attn")
    d_bin = _matmul(d_yb, w_ug, "nt", BF16, "d_b_in")
    g_w_ug = _matmul(b_in, d_yb, "tn", BF16, "g_w_up_gmlp")
    sq_grads = [g.reshape(N_DEV, rw, d) for g in (g_w_ua, g_w_ug, g_w_out)]
    sq_land = [lax.empty((N_CHIPS, rw, d), BF16) for _ in sq_grads]
    pairs_sq = _pair_copies_strided(3)
    arrays = [a for gl in zip(sq_grads, sq_land) for a in gl]
    send3, recv3, arrays, token = _rdma_start("pair_squares_start", arrays, pairs_sq)
    d_q, d_k, d_v, d_ga, d_sink = _attention_bwd(proj, q_rot, k_rot, tables, sink, kv, attn, d_ain, token)
    arrays = _rdma_wait("pair_squares_wait", arrays, send3, recv3, pairs_sq, d_q)
    sq_sums = [_pair_sum(arrays[2 * a], arrays[2 * a + 1], pos, "pair_sum_%d" % a) for a in range(3)]
    sq_land2 = [lax.empty((N_CHIPS - 1, rw, d), BF16) for _ in sq_sums]
    chip_sq = _chip_sum_copies(3)
    arrays = [a for gl in zip(sq_sums, sq_land2) for a in gl]
    send4, recv4, sq_arrays, token = _rdma_start("chip_squares_start", arrays, chip_sq)
    d_proj, d_w_s, d_bias_t, d_ln_g, d_ln_b = _gmlp_bwd(proj, col_u, d, w_s, bias_t, gmlp_ln_g, gmlp_ln_b, d_bin, token,
                                                        d_proj)
    for piece, col0, nm in ((d_q, 0, "d_q"), (d_k, d, "d_k"), (d_v, d + kv, "d_v"), (d_ga, d + 2 * kv, "d_gate")):
        d_proj = _place(d_proj, piece, col0, "place_" + nm)

    half = N_CHIPS // 2
    owners = [*(chip(mx, my) for chip in _OTHER_CHIPS), (mx, my)]

    def blocks_of(j, core):
        return jnp.stack([_slot(*owners[q], core) for q in range(j * half, (j + 1) * half)]).astype(jnp.int32)

    pairs_in = [_Copy(0, lambda x, y, c, q=q: q, 1, lambda x, y, c, q=q: q, _sibling) for q in range(half)]
    sent, token = [], None
    for j in range(2):
        g_sib = _grad_w_in_blocks(h, d_proj, blocks_of(j, 1 - mc), cw, "g_w_in_sibling%d" % j, after=token)
        sent.append(_rdma_start("pair_w_in%d_start" % j, [g_sib, lax.empty((half, d, cw), BF16)], pairs_in))
        token = sent[-1][3]
    in_sums, land_in = None, lax.empty((N_CHIPS - 1, d, cw), BF16)
    chip_in = [[_Copy(0, lambda x, y, c, k=k: k, 1, lambda x, y, c, k=k: k,
                      lambda x, y, c, k=k: (*_OTHER_CHIPS[k](x, y), c)) for k in ks] for ks in ((0, 1), (2,))]
    chip_sent = []
    for j in range(2):
        send5, recv5, arrays, _ = sent[j]
        arrays = _rdma_wait("pair_w_in%d_wait" % j, arrays, send5, recv5, pairs_in, token)
        in_sums = _grad_w_in_blocks(h, d_proj, blocks_of(j, mc), cw, "g_w_in_own%d" % j,
                                    slots=N_CHIPS, slot0=j * half, prev=in_sums, init=arrays[1], tm=512)
        chip_sent.append(_rdma_start("chip_w_in%d_start" % j, [in_sums, land_in], chip_in[j]))
        (in_sums, land_in), token = chip_sent[-1][2], chip_sent[-1][3]
    d_h = _d_hidden(d_proj, w_in_b, after=token)
    grad_x, d_norm_g = _input_grad(d_h, x2d, norm_g, dx2_b)

    sq_arrays = _rdma_wait("chip_squares_wait", sq_arrays, send4, recv4, chip_sq, grad_x)
    big = {}
    for a, (name, w, m, v) in enumerate((("w_up_attn", w_up_attn, m_w_up_attn, v_w_up_attn),
                                         ("w_up_gmlp", w_up_gmlp, m_w_up_gmlp, v_w_up_gmlp),
                                         ("w_out", w_out, m_w_out, v_w_out))):
        big[name] = [r[None] for r in _reduce_adamw(sq_arrays[2 * a], sq_arrays[2 * a + 1], w[0], m[0], v[0], pos,
                                                    "adamw_" + name)]

    heads_per_pair = 2 * n_q_heads // (kv // HEAD_DIM)
    g_sink = d_sink[:, 0, :heads_per_pair].reshape(1, n_q_heads)
    small_w = [norm_g, attn_sink, gmlp_ln_g, gmlp_ln_b, w_spatial, b_spatial, final_norm_g]
    small_m = [m_norm_g, m_attn_sink, m_gmlp_ln_g, m_gmlp_ln_b, m_w_spatial, m_b_spatial, m_final_norm_g]
    small_v = [v_norm_g, v_attn_sink, v_gmlp_ln_g, v_gmlp_ln_b, v_w_spatial, v_b_spatial, v_final_norm_g]
    small_g = [d_norm_g, g_sink, d_ln_g, d_ln_b, d_w_s[None], d_bias_t.T[None], d_final_g.reshape(d)]
    loss_pad = jnp.zeros((1,), F32)
    shapes = [w.shape for w in small_w] + [(1,)]
    packed = _small_allreduce_adamw(_pack(small_g + [loss_p[0, :1]]), _pack(small_w + [loss_pad]),
                                    _pack(small_m + [loss_pad]), _pack(small_v + [loss_pad]))
    sg, sd, sm, sv = [_unpack(p, shapes) for p in packed]
    loss = sg[-1][0]
    in_arrays = [in_sums, land_in]
    for j in range(2):
        in_arrays = _rdma_wait("chip_w_in%d_wait" % j, in_arrays, chip_sent[j][0], chip_sent[j][1], chip_in[j],
                               packed[0])
    big["w_in"] = [r[None] for r in _reduce_adamw(in_arrays[0], in_arrays[1], w_in[0], m_w_in[0], v_w_in[0], pos,
                                                  "adamw_w_in", own_slot=N_CHIPS - 1)]

    names = ["norm_g", "w_in", "attn_sink", "gmlp_ln_g", "gmlp_ln_b", "w_spatial", "b_spatial", "w_up_attn",
             "w_up_gmlp", "w_out", "final_norm_g"]
    small_names = ["norm_g", "attn_sink", "gmlp_ln_g", "gmlp_ln_b", "w_spatial", "b_spatial", "final_norm_g"]
    outs = [[], [], [], []]
    for nm in names:
        for k in range(4):
            if nm in big:
                outs[k].append(big[nm][k])
            else:
                outs[k].append((sg, sd, sm, sv)[k][small_names.index(nm)])
    return (loss, grad_x[None], *outs[0], *outs[1], *outs[2], *outs[3])
```

```python
import math
from typing import Callable, NamedTuple

import jax
import jax.numpy as jnp
from jax import lax
from jax.experimental import pallas as pl
from jax.experimental.pallas import tpu as pltpu

F32 = jnp.float32
BF16 = jnp.bfloat16
MESH = pl.DeviceIdType.MESH

N_DEV = 8
N_CHIPS = 4
HEAD_DIM = 64
BLOCK = 128
ROPE_DIM = 16
ROPE_HALF = ROPE_DIM // 2
ROPE_THETA = 500000.0
GMLP_GROUPS = 8
NORM_EPS = 1e-5
LN_EPS = 1e-5
ATTN_SCALE = HEAD_DIM ** -0.5
LANES = 128
SUBLANES = 8
VMEM_LIMIT = 48 * 1024 * 1024
VMEM_LIMIT_WIDE = 56 * 1024 * 1024
DOT_COLS = 1024
SEGMENT_SPLIT = 4

ADAM_LR = 0.001
ADAM_B1 = 0.9
ADAM_B2 = 0.999
ADAM_EPS = 1e-08
ADAM_WD = 0.01
ADAM_STEP = 10

GELU_C = math.sqrt(2.0 / math.pi)
GELU_K = 0.044715

HBM_SPEC = pl.BlockSpec(memory_space=pltpu.HBM)
ANY_SPEC = pl.BlockSpec(memory_space=pl.ANY)
SEM_SPEC = pl.BlockSpec(memory_space=pltpu.SEMAPHORE)
VMEM_SPEC = pl.BlockSpec(memory_space=pltpu.VMEM)
SMEM_SPEC = pl.BlockSpec(memory_space=pltpu.SMEM)


def _sds(shape, dtype):
    return jax.ShapeDtypeStruct(shape, dtype)


def _params(*sem, vmem=VMEM_LIMIT):
    return pltpu.CompilerParams(dimension_semantics=sem or None, vmem_limit_bytes=vmem)


def _gelu(x):
    return 0.5 * x * (1.0 + jnp.tanh(GELU_C * (x + GELU_K * x * x * x)))


def _gelu_and_grad(x):
    x2 = x * x
    t = jnp.tanh(x * (GELU_C + (GELU_C * GELU_K) * x2))
    half = 0.5 + 0.5 * t
    return x * half, half + (x * (1.0 - t * t)) * (0.5 * GELU_C + (1.5 * GELU_C * GELU_K) * x2)


def _silu_and_grad(x):
    s = jax.nn.sigmoid(x)
    return x * s, s * (1.0 + x * (1.0 - s))


def _adamw(w, g, m, v):
    m = ADAM_B1 * m + (1.0 - ADAM_B1) * g
    v = ADAM_B2 * v + (1.0 - ADAM_B2) * (g * g)
    m_hat = m / (1.0 - ADAM_B1 ** ADAM_STEP)
    v_hat = v / (1.0 - ADAM_B2 ** ADAM_STEP)
    delta = -ADAM_LR * (m_hat / (jnp.sqrt(v_hat) + ADAM_EPS) + ADAM_WD * w)
    return delta, m, v


def _mesh_pos():
    return lax.axis_index("x"), lax.axis_index("y"), lax.axis_index("c")


def _slot(x, y, c):
    return 4 * x + 2 * y + c


def _chip(x, y):
    return 2 * x + y


def _sibling(x, y, c):
    return (x, y, 1 - c)


_OTHER_CHIPS = (lambda x, y: (1 - x, y), lambda x, y: (x, 1 - y), lambda x, y: (1 - x, 1 - y))
_ICI_STAGES = (lambda x, y, c: (x ^ c, y ^ (1 - c)), lambda x, y, c: (x ^ (1 - c), y ^ c),
               lambda x, y, c: (1 - x, 1 - y))


class _Copy(NamedTuple):
    src: int
    src_slot: Callable
    dst: int
    dst_slot: Callable
    peer: Callable


def _descriptor(refs, send_sems, recv_sems, k, cp):
    pos = _mesh_pos()
    return pltpu.make_async_remote_copy(
        src_ref=refs[cp.src].at[cp.src_slot(*pos)], dst_ref=refs[cp.dst].at[cp.dst_slot(*pos)],
        send_sem=send_sems.at[k], recv_sem=recv_sems.at[k], device_id=cp.peer(*pos), device_id_type=MESH)


def _gather_first_copies(n_arrays):
    copies = []
    for a in range(n_arrays):
        copies.append(_Copy(a, _slot, a, _slot, _sibling))
        for chip in _OTHER_CHIPS:
            copies.append(_Copy(a, _slot, a, _slot, lambda x, y, c, chip=chip: (*chip(x, y), c)))
    return copies


def _gather_pass_copies(n_arrays):
    copies = []
    for a in range(n_arrays):
        for chip in _OTHER_CHIPS:
            src = lambda x, y, c, chip=chip: _slot(*chip(x, y), c)
            copies.append(_Copy(a, src, a, src, _sibling))
    return copies


def _pair_copies_strided(n_sets):
    copies = []
    for a in range(n_sets):
        for q in range(N_CHIPS):
            copies.append(_Copy(2 * a, lambda x, y, c, q=q: 2 * q + 1 - c, 2 * a + 1, lambda x, y, c, q=q: q, _sibling))
    return copies


def _chip_sum_copies(n_sets):
    copies = []
    for a in range(n_sets):
        for k, chip in enumerate(_OTHER_CHIPS):
            copies.append(_Copy(2 * a, lambda x, y, c, chip=chip: _chip(*chip(x, y)), 2 * a + 1,
                                lambda x, y, c, k=k: k, lambda x, y, c, chip=chip: (*chip(x, y), c)))
    return copies


def _rdma_start(name, arrays, copies):
    n, nc = len(arrays), len(copies)

    def body(*refs):
        in_refs = refs[:n]
        send_sems, recv_sems = refs[n], refs[n + 1]
        token = refs[2 * n + 2]
        for k, cp in enumerate(copies):
            _descriptor(in_refs, send_sems, recv_sems, k, cp).start()
        token[...] = jnp.zeros_like(token)

    out = pl.pallas_call(
        body, name=name,
        out_shape=(pltpu.SemaphoreType.DMA((nc,)), pltpu.SemaphoreType.DMA((nc,)),
                   *[pltpu.HBM(a.shape, a.dtype) for a in arrays], _sds((SUBLANES, LANES), F32)),
        in_specs=[HBM_SPEC] * n, out_specs=(SEM_SPEC, SEM_SPEC, *([HBM_SPEC] * n), VMEM_SPEC),
        input_output_aliases={i: i + 2 for i in range(n)},
        compiler_params=pltpu.CompilerParams(has_side_effects=pltpu.SideEffectType.DATAFLOW_SIDE_EFFECTING),
    )(*[pltpu.with_memory_space_constraint(a, pltpu.HBM) for a in arrays])
    return out[0], out[1], list(out[2:2 + n]), out[2 + n]


def _rdma_wait(name, arrays, send_sems, recv_sems, copies, after):
    n = len(arrays)

    def body(*refs):
        in_refs = refs[:n]
        send_ref, recv_ref = refs[n], refs[n + 1]
        for k, cp in enumerate(copies):
            d = _descriptor(in_refs, send_ref, recv_ref, k, cp)
            d.wait_send()
            d.wait_recv()

    out = pl.pallas_call(
        body, name=name, out_shape=tuple(pltpu.HBM(a.shape, a.dtype) for a in arrays),
        in_specs=[HBM_SPEC] * n + [SEM_SPEC, SEM_SPEC, ANY_SPEC], out_specs=tuple([HBM_SPEC] * n),
        input_output_aliases={i: i for i in range(n)},
        compiler_params=pltpu.CompilerParams(has_side_effects=pltpu.SideEffectType.DATAFLOW_SIDE_EFFECTING),
    )(*arrays, send_sems, recv_sems, after)
    return list(out)


def _cast_into_slot(w, pos, name, after=None):
    rows, cols = w.shape
    tr = min(rows, 256)

    def body(pos_ref, w_ref, *rest):
        rest[-1][...] = w_ref[...].astype(BF16)

    in_specs, args = [pl.BlockSpec((tr, cols), lambda i, p: (i, 0))], [pos, w]
    if after is not None:
        in_specs.append(ANY_SPEC)
        args.append(after)
    return pl.pallas_call(
        body, name=name,
        grid_spec=pltpu.PrefetchScalarGridSpec(
            num_scalar_prefetch=1, grid=(rows // tr,), in_specs=in_specs,
            out_specs=pl.BlockSpec((None, tr, cols), lambda i, p: (_slot(p[0], p[1], p[2]), i, 0))),
        out_shape=_sds((N_DEV, rows, cols), BF16), compiler_params=_params("parallel"),
    )(*args)


def _pair_sum(g, land, pos, name):
    _, rows, cols = land.shape
    tr = min(rows, 128)
    strided = g.shape[0] == N_DEV

    def body(pos_ref, g_ref, l_ref, o_ref):
        o_ref[...] = (g_ref[...].astype(F32) + l_ref[...].astype(F32)).astype(BF16)

    g_map = (lambda q, i, p: (2 * q + p[2], i, 0)) if strided else (lambda q, i, p: (q, i, 0))
    blk = pl.BlockSpec((None, tr, cols), lambda q, i, p: (q, i, 0))
    return pl.pallas_call(
        body, name=name,
        grid_spec=pltpu.PrefetchScalarGridSpec(
            num_scalar_prefetch=1, grid=(N_CHIPS, rows // tr),
            in_specs=[pl.BlockSpec((None, tr, cols), g_map), blk], out_specs=blk),
        out_shape=_sds((N_CHIPS, rows, cols), BF16), compiler_params=_params("parallel", "parallel"),
    )(pos, g, land)


def _reduce_adamw(sums, land, w, m, v, pos, name, own_slot=None):
    rows, cols = w.shape
    tr = min(rows, 64)
    own = (lambda p: _chip(p[0], p[1])) if own_slot is None else (lambda p: own_slot)

    def body(pos_ref, s_ref, l_ref, w_ref, m_ref, v_ref, g_ref, d_ref, nm_ref, nv_ref):
        g = s_ref[...].astype(F32)
        for k in range(N_CHIPS - 1):
            g = g + l_ref[k].astype(F32)
        delta, nm, nv = _adamw(w_ref[...], g, m_ref[...], v_ref[...])
        g_ref[...] = g
        d_ref[...] = delta
        nm_ref[...] = nm
        nv_ref[...] = nv

    spec = pl.BlockSpec((tr, cols), lambda i, p: (i, 0))
    return pl.pallas_call(
        body, name=name,
        grid_spec=pltpu.PrefetchScalarGridSpec(
            num_scalar_prefetch=1, grid=(rows // tr,),
            in_specs=[pl.BlockSpec((None, tr, cols), lambda i, p: (own(p), i, 0)),
                      pl.BlockSpec((N_CHIPS - 1, tr, cols), lambda i, p: (0, i, 0)), spec, spec, spec],
            out_specs=(spec, spec, spec, spec)),
        out_shape=tuple([_sds((rows, cols), F32)] * 4), compiler_params=_params("parallel"),
    )(pos, sums, land, w, m, v)


def _small_allreduce_adamw(g, w, m, v):
    rows = g.shape[0]

    def body(g_ref, w_ref, m_ref, v_ref, gs_ref, d_ref, nm_ref, nv_ref, all_ref, send_sems, recv_sems):
        x, y, c = _mesh_pos()
        me = _slot(x, y, c)
        copies = []
        for k in range(1, N_DEV):
            peer = (x ^ (k >> 2), y ^ ((k >> 1) & 1), c ^ (k & 1))
            copies.append(pltpu.make_async_remote_copy(
                src_ref=g_ref, dst_ref=all_ref.at[me], send_sem=send_sems.at[k - 1],
                recv_sem=recv_sems.at[k - 1], device_id=peer, device_id_type=MESH))
        for cp in copies:
            cp.start()
        all_ref[me] = g_ref[...]
        for cp in copies:
            cp.wait_recv()
        total = all_ref[0]
        for s in range(1, N_DEV):
            total = total + all_ref[s]
        delta, nm, nv = _adamw(w_ref[...], total, m_ref[...], v_ref[...])
        gs_ref[...] = total
        d_ref[...] = delta
        nm_ref[...] = nm
        nv_ref[...] = nv
        for cp in copies:
            cp.wait_send()

    return pl.pallas_call(
        body, name="small_allreduce_adamw", out_shape=tuple([_sds((rows, LANES), F32)] * 4),
        in_specs=[VMEM_SPEC] * 4, out_specs=tuple([VMEM_SPEC] * 4),
        scratch_shapes=[pltpu.VMEM((N_DEV, rows, LANES), F32), pltpu.SemaphoreType.DMA((7,)),
                        pltpu.SemaphoreType.DMA((7,))],
    )(g, w, m, v)


_DOT_DIMS = {"nn": ((1,), (0,)), "nt": ((1,), (1,)), "tn": ((0,), (0,))}


def _dot(a, b, mode):
    return lax.dot_general(a, b, (_DOT_DIMS[mode], ((), ())), preferred_element_type=F32)


def _col_chunks(cols):
    return [(c0, min(c0 + DOT_COLS, cols)) for c0 in range(0, cols, DOT_COLS)]


def _matmul(a, b, mode, out_dtype, name, *, res=None, tm=1024, tn=1024):
    if mode == "tn":
        kdim, mdim = a.shape
    else:
        mdim, kdim = a.shape
    ndim = b.shape[0] if mode == "nt" else b.shape[1]
    tm, tn = min(tm, mdim), min(tn, ndim)
    assert mdim % tm == 0 and ndim % tn == 0, (name, mdim, ndim)

    def body(*refs):
        out = _dot(refs[0][...], refs[1][...], mode)
        if res is not None:
            out = out + refs[2][...]
        refs[-1][...] = out.astype(out_dtype)

    a_spec = pl.BlockSpec((kdim, tm), lambda i, j: (0, i)) if mode == "tn" else pl.BlockSpec((tm, kdim), lambda i, j: (i, 0))
    b_spec = pl.BlockSpec((tn, kdim), lambda i, j: (j, 0)) if mode == "nt" else pl.BlockSpec((kdim, tn), lambda i, j: (0, j))
    o_spec = pl.BlockSpec((tm, tn), lambda i, j: (i, j))
    in_specs, args = [a_spec, b_spec], [a, b]
    if res is not None:
        in_specs.append(o_spec)
        args.append(res)
    return pl.pallas_call(
        body, name=name, grid=(mdim // tm, ndim // tn), in_specs=in_specs, out_specs=o_spec,
        out_shape=_sds((mdim, ndim), out_dtype), compiler_params=_params("parallel", "parallel"),
    )(*args)


def _project(h, w_blocks, block_ids, name, *, proj=None, after=None, tm=1024, tk=1024):
    s, d = h.shape
    _, _, cw = w_blocks.shape
    n = block_ids.shape[0]
    tm, tk = min(tm, s), min(tk, d)
    nk = d // tk

    def body(ids_ref, h_ref, w_ref, *rest):
        o_ref, acc_ref = rest[-2], rest[-1]
        k = pl.program_id(2)

        @pl.when(k == 0)
        def _():
            acc_ref[...] = jnp.zeros_like(acc_ref)

        for c0, c1 in _col_chunks(cw):
            acc_ref[:, c0:c1] += _dot(h_ref[...], w_ref[:, c0:c1], "nn")

        @pl.when(k == nk - 1)
        def _():
            o_ref[...] = acc_ref[...].astype(BF16)

    in_specs = [pl.BlockSpec((tm, tk), lambda j, i, k, ids: (i, k)),
                pl.BlockSpec((None, tk, cw), lambda j, i, k, ids: (ids[j], k, 0))]
    args = [block_ids, h, w_blocks]
    aliases = {}
    if proj is not None:
        in_specs.append(ANY_SPEC)
        args.append(proj)
        aliases = {3: 0}
    if after is not None:
        in_specs.append(ANY_SPEC)
        args.append(after)
    return pl.pallas_call(
        body, name=name,
        grid_spec=pltpu.PrefetchScalarGridSpec(
            num_scalar_prefetch=1, grid=(n, s // tm, d // tk), in_specs=in_specs,
            out_specs=pl.BlockSpec((tm, cw), lambda j, i, k, ids: (i, ids[j])),
            scratch_shapes=[pltpu.VMEM((tm, cw), F32)]),
        out_shape=_sds((s, N_DEV * cw), BF16), input_output_aliases=aliases,
        compiler_params=_params("arbitrary", "arbitrary", "arbitrary", vmem=VMEM_LIMIT_WIDE),
    )(*args)


def _grad_w_in_blocks(h, d_proj, block_ids, cw, name, *, slots=None, slot0=0, prev=None, init=None, after=None,
                      tm=1024, tk=1024):
    s, d = h.shape
    n = block_ids.shape[0]
    slots = n if slots is None else slots
    tm, tk = min(tm, d), min(tk, s)
    nk = s // tk

    def body(ids_ref, h_ref, g_ref, *rest):
        o_ref, acc_ref = rest[-2], rest[-1]
        k = pl.program_id(2)

        @pl.when(k == 0)
        def _():
            acc_ref[...] = jnp.zeros_like(acc_ref) if init is None else rest[0][...].astype(F32)

        for c0, c1 in _col_chunks(cw):
            acc_ref[:, c0:c1] += _dot(h_ref[...], g_ref[:, c0:c1], "tn")

        @pl.when(k == nk - 1)
        def _():
            o_ref[...] = acc_ref[...].astype(BF16)

    in_specs = [pl.BlockSpec((tk, tm), lambda q, i, k, ids: (k, i)),
                pl.BlockSpec((tk, cw), lambda q, i, k, ids: (k, ids[q]))]
    args = [block_ids, h, d_proj]
    aliases = {}
    if init is not None:
        in_specs.append(pl.BlockSpec((None, tm, cw), lambda q, i, k, ids: (q, i, 0), pipeline_mode=pl.Buffered(1)))
        args.append(init)
    if prev is not None:
        aliases = {len(args): 0}
        in_specs.append(ANY_SPEC)
        args.append(prev)
    if after is not None:
        in_specs.append(ANY_SPEC)
        args.append(after)
    return pl.pallas_call(
        body, name=name,
        grid_spec=pltpu.PrefetchScalarGridSpec(
            num_scalar_prefetch=1, grid=(n, d // tm, nk), in_specs=in_specs,
            out_specs=pl.BlockSpec((None, tm, cw), lambda q, i, k, ids: (slot0 + q, i, 0)),
            scratch_shapes=[pltpu.VMEM((tm, cw), F32)]),
        out_shape=_sds((slots, d, cw), BF16), input_output_aliases=aliases,
        compiler_params=_params("parallel", "parallel", "arbitrary", vmem=VMEM_LIMIT_WIDE),
    )(*args)


def _d_hidden(d_proj, w_blocks, after=None, *, tm=1024, tn=1024):
    s = d_proj.shape[0]
    nb, d, cw = w_blocks.shape
    tm, tn = min(tm, s), min(tn, d)

    def body(g_ref, w_ref, *rest):
        o_ref = rest[-1]
        k = pl.program_id(2)

        @pl.when(k == 0)
        def _():
            o_ref[...] = jnp.zeros_like(o_ref)

        o_ref[...] += _dot(g_ref[...], w_ref[...], "nt")

    in_specs = [pl.BlockSpec((tm, cw), lambda i, j, k: (i, k)),
                pl.BlockSpec((None, tn, cw), lambda i, j, k: (k, j, 0))]
    args = [d_proj, w_blocks]
    if after is not None:
        in_specs.append(ANY_SPEC)
        args.append(after)
    return pl.pallas_call(
        body, name="d_h", grid=(s // tm, d // tn, nb), in_specs=in_specs,
        out_specs=pl.BlockSpec((tm, tn), lambda i, j, k: (i, j)), out_shape=_sds((s, d), F32),
        compiler_params=_params("parallel", "parallel", "arbitrary", vmem=VMEM_LIMIT_WIDE),
    )(*args)


def _row_tile(rows):
    return min(rows, 128)


def _segment_specs(rows, d, col0):
    w = d // SEGMENT_SPLIT
    assert col0 % w == 0
    return [pl.BlockSpec((rows, w), lambda i, t=t: (i, col0 // w + t)) for t in range(SEGMENT_SPLIT)]


def _cat(refs):
    return jnp.concatenate([r[...].astype(F32) for r in refs], axis=1)


def _rmsnorm_fwd(x, g, after):
    s, d = x.shape
    tr = min(s, 2 * _row_tile(s))

    def body(x_ref, g_ref, after_ref, h_ref):
        xv = x_ref[...]
        r = lax.rsqrt(jnp.mean(xv * xv, axis=-1, keepdims=True) + NORM_EPS)
        h_ref[...] = (xv * r * g_ref[...]).astype(BF16)

    row = pl.BlockSpec((tr, d), lambda i: (i, 0))
    vec = pl.BlockSpec((1, d), lambda i: (0, 0))
    return pl.pallas_call(body, name="rmsnorm_fwd", grid=(s // tr,), in_specs=[row, vec, ANY_SPEC], out_specs=row,
                          out_shape=_sds((s, d), BF16), compiler_params=_params("parallel"))(x, g, after)


def _merge_fwd(y_a, y_b, proj, col_m):
    s, d = y_a.shape
    tr = _row_tile(s)
    ns = SEGMENT_SPLIT

    def body(ya_ref, yb_ref, *rest):
        ma, mb, o_ref = _cat(rest[:ns]), _cat(rest[ns:2 * ns]), rest[2 * ns]
        o_ref[...] = (jax.nn.sigmoid(ma) * ya_ref[...].astype(F32)
                      + jax.nn.sigmoid(mb) * yb_ref[...].astype(F32)).astype(BF16)

    row = pl.BlockSpec((tr, d), lambda i: (i, 0))
    return pl.pallas_call(
        body, name="merge_fwd", grid=(s // tr,),
        in_specs=[row, row, *_segment_specs(tr, d, col_m), *_segment_specs(tr, d, col_m + d)],
        out_specs=row, out_shape=_sds((s, d), BF16), compiler_params=_params("parallel"),
    )(y_a, y_b, *([proj] * (2 * ns)))


def _loss_and_final_norm_bwd(x2, target, g):
    s, d = x2.shape
    tr = min(s, 2 * _row_tile(s))

    def body(x_ref, t_ref, g_ref, loss_ref, dg_ref, dxb_ref):
        @pl.when(pl.program_id(0) == 0)
        def _():
            loss_ref[...] = jnp.zeros_like(loss_ref)
            dg_ref[...] = jnp.zeros_like(dg_ref)

        xv, gv = x_ref[...], g_ref[...]
        r = lax.rsqrt(jnp.mean(xv * xv, axis=-1, keepdims=True) + NORM_EPS)
        xhat = xv * r
        err = xhat * gv - t_ref[...]
        loss_ref[...] += 0.5 * jnp.sum(jnp.mean(err * err, axis=-1, keepdims=True))
        dy = err / d
        dg_ref[...] += jnp.sum(dy * xhat, axis=0, keepdims=True)
        dyg = dy * gv
        dxb_ref[...] = (r * (dyg - xhat * jnp.mean(dyg * xhat, axis=-1, keepdims=True))).astype(BF16)

    row = pl.BlockSpec((tr, d), lambda i: (i, 0))
    vec = pl.BlockSpec((1, d), lambda i: (0, 0))
    return pl.pallas_call(
        body, name="loss_final_norm_bwd", grid=(s // tr,), in_specs=[row, row, vec],
        out_specs=(pl.BlockSpec((SUBLANES, LANES), lambda i: (0, 0)), vec, row),
        out_shape=(_sds((SUBLANES, LANES), F32), _sds((1, d), F32), _sds((s, d), BF16)),
        compiler_params=_params("arbitrary"))(x2, target, g)


def _write_behind(step, n_steps, buf, sems, wide_ref, rows, col0, fill):
    cols = buf.shape[2]
    slot = step % 2

    def copy(at_step, at_slot):
        dst = wide_ref.at[pl.ds(pl.multiple_of(at_step * rows, rows), rows), pl.ds(col0, cols)]
        return pltpu.make_async_copy(buf.at[at_slot], dst, sems.at[at_slot])

    @pl.when(step >= 2)
    def _():
        copy(step - 2, slot).wait()

    fill(buf.at[slot])
    copy(step, slot).start()

    @pl.when(step == n_steps - 1)
    def _():
        copy(step, slot).wait()
        if n_steps > 1:
            copy(step - 1, 1 - slot).wait()


def _merge_bwd(d_merged, y_a, y_b, proj, col_m, d_proj):
    s, d = y_a.shape
    tr = _row_tile(s)
    ns = SEGMENT_SPLIT
    n_steps = s // tr

    def body(dm_ref, ya_ref, yb_ref, *rest):
        ma, mb = _cat(rest[:ns]), _cat(rest[ns:2 * ns])
        dya_ref, dyb_ref, wide_ref, buf, sems = rest[2 * ns + 1:]
        dm = dm_ref[...].astype(F32)
        sa = jax.nn.sigmoid(ma)
        sb = jax.nn.sigmoid(mb)
        dya_ref[...] = (dm * sa).astype(BF16)
        dyb_ref[...] = (dm * sb).astype(BF16)

        def fill(out):
            out[:, :d] = (dm * ya_ref[...].astype(F32) * (sa * (1.0 - sa))).astype(BF16)
            out[:, d:] = (dm * yb_ref[...].astype(F32) * (sb * (1.0 - sb))).astype(BF16)

        _write_behind(pl.program_id(0), n_steps, buf, sems, wide_ref, tr, col_m, fill)

    row = pl.BlockSpec((tr, d), lambda i: (i, 0))
    n_in = 3 + 2 * ns
    return pl.pallas_call(
        body, name="merge_bwd", grid=(n_steps,),
        in_specs=[row, row, row, *_segment_specs(tr, d, col_m), *_segment_specs(tr, d, col_m + d), ANY_SPEC],
        out_specs=(row, row, ANY_SPEC),
        out_shape=(_sds((s, d), BF16), _sds((s, d), BF16), _sds(d_proj.shape, BF16)),
        input_output_aliases={n_in: 2},
        scratch_shapes=[pltpu.VMEM((2, tr, 2 * d), BF16), pltpu.SemaphoreType.DMA((2,))],
        compiler_params=_params("arbitrary"))(d_merged, y_a, y_b, *([proj] * (2 * ns)), d_proj)


def _place(d_proj, piece, col0, name):
    s, w = piece.shape
    bw = math.gcd(w, col0) if col0 else w
    tr = min(s, 512)

    def body(p_ref, wide_in, o_ref):
        o_ref[...] = p_ref[...]

    return pl.pallas_call(
        body, name=name, grid=(s // tr, w // bw),
        in_specs=[pl.BlockSpec((tr, bw), lambda i, j: (i, j)), ANY_SPEC],
        out_specs=pl.BlockSpec((tr, bw), lambda i, j: (i, col0 // bw + j)),
        out_shape=_sds(d_proj.shape, d_proj.dtype), input_output_aliases={1: 0},
        compiler_params=_params("parallel", "parallel"))(piece, d_proj)


def _input_grad(d_h, x, g, dx2):
    s, d = x.shape
    tr = min(s, 2 * _row_tile(s))

    def body(dh_ref, x_ref, g_ref, dx2_ref, gx_ref, dg_ref):
        @pl.when(pl.program_id(0) == 0)
        def _():
            dg_ref[...] = jnp.zeros_like(dg_ref)

        xv, dh = x_ref[...], dh_ref[...]
        r = lax.rsqrt(jnp.mean(xv * xv, axis=-1, keepdims=True) + NORM_EPS)
        xhat = xv * r
        dg_ref[...] += jnp.sum(dh * xhat, axis=0, keepdims=True)
        dyg = dh * g_ref[...]
        gx_ref[...] = dx2_ref[...].astype(F32) + r * (dyg - xhat * jnp.mean(dyg * xhat, axis=-1, keepdims=True))

    row = pl.BlockSpec((tr, d), lambda i: (i, 0))
    vec = pl.BlockSpec((1, d), lambda i: (0, 0))
    return pl.pallas_call(
        body, name="input_grad", grid=(s // tr,), in_specs=[row, row, vec, row], out_specs=(row, vec),
        out_shape=(_sds((s, d), F32), _sds((1, d), F32)), compiler_params=_params("arbitrary"))(d_h, x, g, dx2)


def _rope_tables(positions):
    inv_freq = ROPE_THETA ** (-jnp.arange(ROPE_HALF, dtype=F32) * 2.0 / ROPE_DIM)
    ang = positions.astype(F32)[:, None] * inv_freq
    cos, sin = jnp.cos(ang), jnp.sin(ang)
    zero = jnp.zeros((positions.shape[0], HEAD_DIM - ROPE_DIM), F32)
    zero_h = jnp.zeros_like(sin)
    c = jnp.concatenate([cos, cos, zero + 1.0], axis=1)
    up = jnp.concatenate([-sin, zero_h, zero], axis=1)
    down = jnp.concatenate([zero_h, sin, zero], axis=1)
    reps = LANES // HEAD_DIM
    return jnp.stack([jnp.tile(c, (1, reps)), jnp.tile(up, (1, reps)), jnp.tile(down, (1, reps))])


def _lane_tiles(x):
    return [x[:, t * LANES:(t + 1) * LANES] for t in range(x.shape[1] // LANES)]


def _rope(x, tab):
    out = [xt * tab[0] + pltpu.roll(xt, LANES - ROPE_HALF, 1) * tab[1] + pltpu.roll(xt, ROPE_HALF, 1) * tab[2]
           for xt in _lane_tiles(x)]
    return out[0] if len(out) == 1 else jnp.concatenate(out, axis=1)


def _rope_bwd(g, tab):
    out = [gt * tab[0] + pltpu.roll(gt * tab[1], ROPE_HALF, 1) + pltpu.roll(gt * tab[2], LANES - ROPE_HALF, 1)
           for gt in _lane_tiles(g)]
    return out[0] if len(out) == 1 else jnp.concatenate(out, axis=1)


def _head(x, h):
    return x[:, h * HEAD_DIM:(h + 1) * HEAD_DIM]


def _stack_heads(x, first, count):
    return jnp.concatenate([_head(x, first + h) for h in range(count)], axis=0)


def _dot_nt(a, b):
    return lax.dot_general(a, b, (((1,), (1,)), ((), ())), preferred_element_type=F32)


def _causal(rows):
    qi = lax.broadcasted_iota(jnp.int32, (rows, BLOCK), 0) % BLOCK
    return lax.broadcasted_iota(jnp.int32, (rows, BLOCK), 1) <= qi


def _band_probs_by_head(qs, k_band, sink_ref, first, count, causal, blk):
    s_band = _dot_nt(qs, k_band)
    p_all, p_band, p_sink = [], [], []
    for h in range(count):
        rows = slice(h * BLOCK, (h + 1) * BLOCK)
        sink = sink_ref[first + h]
        s = jnp.where(causal, s_band[rows, :BLOCK], jnp.where(blk > 0, s_band[rows, BLOCK:], -jnp.inf))
        m = jnp.maximum(jnp.max(s, axis=-1, keepdims=True), sink)
        p = jnp.exp(s - m)
        ps = jnp.exp(sink - m)
        inv = 1.0 / (jnp.sum(p, axis=-1, keepdims=True) + ps)
        p = p * inv
        p_all.append(p)
        p_sink.append(ps * inv)
        p_band.append(_split_band(p, causal))
    cat = lambda parts: jnp.concatenate(parts, axis=0)
    return cat([c for c, _ in p_band]), cat([v for _, v in p_band]), cat(p_all), cat(p_sink)


def _split_band(x, causal):
    return jnp.where(causal, x, 0.0).astype(BF16), jnp.where(causal, 0.0, x).astype(BF16)


def _attn_dims(s, d, kv):
    n_kv = kv // HEAD_DIM
    group = d // kv
    qw = 2 * group * HEAD_DIM
    assert n_kv % 2 == 0 and (d + 2 * kv) % qw == 0 and s % BLOCK == 0
    return group, qw, n_kv // 2, s // BLOCK


def _attention_fwd(proj, tables, sink, kv, after):
    s, d = proj.shape[0], sink.shape[0] * HEAD_DIM
    group, qw, n_pairs, nb = _attn_dims(s, d, kv)

    def body(sink_ref, q_ref, kc_ref, kp_ref, vc_ref, vp_ref, ga_ref, tc_ref, tp_ref, after_ref,
             attn_ref, ain_ref, qrot_ref, krot_ref):
        pair, blk = pl.program_id(0), pl.program_id(1)
        tab_c, tab_p = tc_ref[...], tp_ref[...]
        q = _rope(q_ref[...].astype(F32), tab_c) * ATTN_SCALE
        k_cur, k_prev = _rope(kc_ref[...].astype(F32), tab_c), _rope(kp_ref[...].astype(F32), tab_p)
        qrot_ref[...] = q.astype(BF16)
        krot_ref[...] = k_cur.astype(BF16)
        v_cur, v_prev = vc_ref[...], vp_ref[...]
        causal = _causal(BLOCK)
        outs = []
        for a in range(2):
            k_band = jnp.concatenate([_head(k_cur, a), _head(k_prev, a)], axis=0).astype(BF16)
            vc, vp = _head(v_cur, a).astype(BF16), _head(v_prev, a).astype(BF16)
            qs = _stack_heads(q, a * group, group).astype(BF16)
            p_cur, p_prev, _, _ = _band_probs_by_head(qs, k_band, sink_ref, (2 * pair + a) * group, group, causal, blk)
            o = jnp.dot(p_cur, vc, preferred_element_type=F32) + jnp.dot(p_prev, vp, preferred_element_type=F32)
            outs += [o[h * BLOCK:(h + 1) * BLOCK] for h in range(group)]
        attn = jnp.concatenate(outs, axis=1)
        attn_ref[...] = attn.astype(BF16)
        silu, _ = _silu_and_grad(ga_ref[...].astype(F32))
        ain_ref[...] = (attn * silu).astype(BF16)

    k0, v0, g0 = d // LANES, (d + kv) // LANES, (d + 2 * kv) // qw
    prev = lambda i: jnp.maximum(i - 1, 0)
    in_specs = [
        SMEM_SPEC,
        pl.BlockSpec((BLOCK, qw), lambda p, i: (i, p)),
        pl.BlockSpec((BLOCK, LANES), lambda p, i: (i, k0 + p)),
        pl.BlockSpec((BLOCK, LANES), lambda p, i: (prev(i), k0 + p)),
        pl.BlockSpec((BLOCK, LANES), lambda p, i: (i, v0 + p)),
        pl.BlockSpec((BLOCK, LANES), lambda p, i: (prev(i), v0 + p)),
        pl.BlockSpec((BLOCK, qw), lambda p, i: (i, g0 + p)),
        pl.BlockSpec((3, BLOCK, LANES), lambda p, i: (0, i, 0)),
        pl.BlockSpec((3, BLOCK, LANES), lambda p, i: (0, prev(i), 0)),
        ANY_SPEC,
    ]
    out = pl.BlockSpec((BLOCK, qw), lambda p, i: (i, p))
    k_out = pl.BlockSpec((BLOCK, LANES), lambda p, i: (i, p))
    return pl.pallas_call(
        body, name="attention_fwd", grid=(n_pairs, nb), in_specs=in_specs, out_specs=(out, out, out, k_out),
        out_shape=(_sds((s, d), BF16), _sds((s, d), BF16), _sds((s, d), BF16), _sds((s, kv), BF16)),
        compiler_params=_params("parallel", "parallel"),
    )(sink, proj, proj, proj, proj, proj, proj, tables, tables, after)


def _attention_bwd(proj, q_rot, k_rot, tables, sink, kv, attn, d_ain, after):
    s, d = proj.shape[0], sink.shape[0] * HEAD_DIM
    group, qw, n_pairs, nb = _attn_dims(s, d, kv)

    def body(sink_ref, q_ref, kc_ref, kp_ref, vc_ref, vp_ref, ga_ref, tc_ref, tp_ref, attn_ref, dain_ref, after_ref,
             dq_ref, dk_ref, dv_ref, dga_ref, dsink_ref, carry_k, carry_v, sink_acc):
        pair, blk = pl.program_id(0), pl.program_id(1)

        @pl.when(blk == 0)
        def _():
            carry_k[...] = jnp.zeros_like(carry_k)
            carry_v[...] = jnp.zeros_like(carry_v)
            sink_acc[...] = jnp.zeros_like(sink_acc)

        @pl.when(blk < nb)
        def _():
            tab_c, tab_p = tc_ref[...], tp_ref[...]
            q = q_ref[...].astype(F32)
            k_cur, k_prev = kc_ref[...].astype(F32), kp_ref[...].astype(F32)
            v_cur, v_prev = vc_ref[...], vp_ref[...]
            silu, silu_grad = _silu_and_grad(ga_ref[...].astype(F32))
            d_ain_v = dain_ref[...].astype(F32)
            dga_ref[...] = (d_ain_v * attn_ref[...].astype(F32) * silu_grad).astype(BF16)
            d_attn = d_ain_v * silu
            q_t = q.T
            d_attn_t = d_attn.T
            causal = _causal(BLOCK)
            dq_parts = []
            dk_t = {"cur": [], "prev": []}
            dv_t = {"cur": [], "prev": []}
            lane = lax.broadcasted_iota(jnp.int32, (BLOCK, LANES), 1)
            dsink = jnp.zeros((BLOCK, LANES), F32)
            for a in range(2):
                first = a * group
                qs = _stack_heads(q, first, group).astype(BF16)
                kc, kp = _head(k_cur, a).astype(BF16), _head(k_prev, a).astype(BF16)
                k_band = jnp.concatenate([_head(k_cur, a), _head(k_prev, a)], axis=0).astype(BF16)
                v_band = jnp.concatenate([_head(v_cur, a), _head(v_prev, a)], axis=0).astype(BF16)
                p_cur, p_prev, p, p_sink = _band_probs_by_head(qs, k_band, sink_ref, (2 * pair + a) * group, group,
                                                               causal, blk)
                do = _stack_heads(d_attn, first, group).astype(BF16)
                dp_band = _dot_nt(do, v_band)
                ds_parts, delta = [], []
                for h in range(group):
                    rows = slice(h * BLOCK, (h + 1) * BLOCK)
                    dp = jnp.where(causal, dp_band[rows, :BLOCK], dp_band[rows, BLOCK:])
                    delta.append(jnp.sum(p[rows] * dp, axis=-1, keepdims=True))
                    ds_parts.append(_split_band(p[rows] * (dp - delta[-1]), causal))
                ds_cur = jnp.concatenate([c for c, _ in ds_parts], axis=0)
                ds_prev = jnp.concatenate([v for _, v in ds_parts], axis=0)
                delta = jnp.concatenate(delta, axis=0)
                dqs = (jnp.dot(ds_cur, kc, preferred_element_type=F32)
                       + jnp.dot(ds_prev, kp, preferred_element_type=F32)) * ATTN_SCALE
                dq_parts += [dqs[h * BLOCK:(h + 1) * BLOCK] for h in range(group)]
                rows = lambda t: jnp.concatenate(
                    [t[(first + h) * HEAD_DIM:(first + h + 1) * HEAD_DIM] for h in range(group)], axis=1).astype(BF16)
                qs_t, do_t = rows(q_t), rows(d_attn_t)
                dk_t["cur"].append(jnp.dot(qs_t, ds_cur, preferred_element_type=F32))
                dk_t["prev"].append(jnp.dot(qs_t, ds_prev, preferred_element_type=F32))
                dv_t["cur"].append(jnp.dot(do_t, p_cur, preferred_element_type=F32))
                dv_t["prev"].append(jnp.dot(do_t, p_prev, preferred_element_type=F32))
                ds_sink = -(p_sink * delta)
                for h in range(group):
                    dsink = dsink + jnp.where(lane == first + h, ds_sink[h * BLOCK:(h + 1) * BLOCK], 0.0)
            sink_acc[...] += dsink
            dq_ref[...] = _rope_bwd(jnp.concatenate(dq_parts, axis=1), tab_c).astype(BF16)
            pair_block = lambda parts: jnp.concatenate(parts, axis=0).T
            dk_ref[...] = (carry_k[...] + _rope_bwd(pair_block(dk_t["prev"]), tab_p)).astype(BF16)
            dv_ref[...] = (carry_v[...] + pair_block(dv_t["prev"])).astype(BF16)
            carry_k[...] = _rope_bwd(pair_block(dk_t["cur"]), tab_c)
            carry_v[...] = pair_block(dv_t["cur"])

        @pl.when(blk == nb)
        def _():
            dk_ref[...] = carry_k[...].astype(BF16)
            dv_ref[...] = carry_v[...].astype(BF16)
            dsink_ref[0] = jnp.sum(sink_acc[...], axis=0, keepdims=True)

    v0, g0 = (d + kv) // LANES, (d + 2 * kv) // qw
    cur = lambda i: jnp.minimum(i, nb - 1)
    prev = lambda i: jnp.maximum(cur(i) - 1, 0)
    back = lambda i: jnp.maximum(i - 1, 0)
    q_spec = pl.BlockSpec((BLOCK, qw), lambda p, i: (cur(i), p))
    in_specs = [
        SMEM_SPEC,
        q_spec,
        pl.BlockSpec((BLOCK, LANES), lambda p, i: (cur(i), p)),
        pl.BlockSpec((BLOCK, LANES), lambda p, i: (prev(i), p)),
        pl.BlockSpec((BLOCK, LANES), lambda p, i: (cur(i), v0 + p)),
        pl.BlockSpec((BLOCK, LANES), lambda p, i: (prev(i), v0 + p)),
        pl.BlockSpec((BLOCK, qw), lambda p, i: (cur(i), g0 + p)),
        pl.BlockSpec((3, BLOCK, LANES), lambda p, i: (0, cur(i), 0)),
        pl.BlockSpec((3, BLOCK, LANES), lambda p, i: (0, prev(i), 0)),
        q_spec,
        q_spec,
        ANY_SPEC,
    ]
    kv_out = pl.BlockSpec((BLOCK, LANES), lambda p, i: (back(i), p))
    return pl.pallas_call(
        body, name="attention_bwd", grid=(n_pairs, nb + 1), in_specs=in_specs,
        out_specs=(q_spec, kv_out, kv_out, q_spec, pl.BlockSpec((1, 1, LANES), lambda p, i: (p, 0, 0))),
        out_shape=(_sds((s, d), BF16), _sds((s, kv), BF16), _sds((s, kv), BF16), _sds((s, d), BF16),
                   _sds((n_pairs, 1, LANES), F32)),
        scratch_shapes=[pltpu.VMEM((BLOCK, LANES), F32), pltpu.VMEM((BLOCK, LANES), F32),
                        pltpu.VMEM((BLOCK, LANES), F32)],
        compiler_params=_params("parallel", "arbitrary"),
    )(sink, q_rot, k_rot, k_rot, proj, proj, proj, tables, tables, attn, d_ain, after)


def _gmlp_core(gu, gv, ln_g, ln_b, w_ref, bias_t):
    xc = gv - jnp.mean(gv, axis=-1, keepdims=True)
    rstd = lax.rsqrt(jnp.mean(xc * xc, axis=-1, keepdims=True) + LN_EPS)
    xhat = xc * rstd
    vn = (xhat * ln_g + ln_b).astype(BF16)
    gd = gu.shape[1] // GMLP_GROUPS
    tri = (lax.broadcasted_iota(jnp.int32, (BLOCK, BLOCK), 0) >= lax.broadcasted_iota(jnp.int32, (BLOCK, BLOCK), 1))
    w_tri = [jnp.where(tri, w_ref[g], 0.0).astype(BF16) for g in range(GMLP_GROUPS)]
    mixed = jnp.concatenate(
        [jnp.dot(w_tri[g], vn[:, g * gd:(g + 1) * gd], preferred_element_type=F32) + bias_t[:, g:g + 1]
         for g in range(GMLP_GROUPS)], axis=1)
    return gu, xhat, rstd, vn, w_tri, tri, mixed


def _whole(shape):
    return pl.BlockSpec(shape, lambda i: tuple(0 for _ in shape))


def _gmlp_fwd(proj, col_u, d, w_s, bias_t, ln_g, ln_b, after):
    s = proj.shape[0]
    ns = SEGMENT_SPLIT

    def body(*refs):
        u, vg, gb = _cat(refs[:ns]), _cat(refs[ns:2 * ns]), _cat(refs[2 * ns:3 * ns])
        w_ref, bt_ref, lg_ref, lb_ref, after_ref, o_ref = refs[3 * ns:]
        gu, _, _, _, _, _, mixed = _gmlp_core(_gelu(u), _gelu(vg), lg_ref[...], lb_ref[...], w_ref, bt_ref[...])
        silu, _ = _silu_and_grad(gb)
        o_ref[...] = ((gu * mixed) * silu).astype(BF16)

    segs = [sp for j in range(3) for sp in _segment_specs(BLOCK, d, col_u + j * d)]
    return pl.pallas_call(
        body, name="gmlp_fwd", grid=(s // BLOCK,),
        in_specs=[*segs, _whole(w_s.shape), _whole(bias_t.shape), _whole((1, d)), _whole((1, d)), ANY_SPEC],
        out_specs=pl.BlockSpec((BLOCK, d), lambda i: (i, 0)), out_shape=_sds((s, d), BF16),
        compiler_params=_params("parallel"),
    )(*([proj] * (3 * ns)), w_s, bias_t, ln_g, ln_b, after)


def _gmlp_bwd(proj, col_u, d, w_s, bias_t, ln_g, ln_b, d_bin, after, d_proj):
    s = proj.shape[0]
    gd = d // GMLP_GROUPS
    ns = SEGMENT_SPLIT
    n_steps = s // BLOCK

    def body(*refs):
        u, vg, gb = _cat(refs[:ns]), _cat(refs[ns:2 * ns]), _cat(refs[2 * ns:3 * ns])
        (w_ref, bt_ref, lg_ref, lb_ref, dbin_ref, after_ref, wide_in, wide_ref, dw_ref, dbt_ref, dlg_ref, dlb_ref,
         buf, sems) = refs[3 * ns:]

        @pl.when(pl.program_id(0) == 0)
        def _():
            dw_ref[...] = jnp.zeros_like(dw_ref)
            dbt_ref[...] = jnp.zeros_like(dbt_ref)
            dlg_ref[...] = jnp.zeros_like(dlg_ref)
            dlb_ref[...] = jnp.zeros_like(dlb_ref)

        ln_g = lg_ref[...]
        (gu, gu_grad), (gv, gv_grad) = _gelu_and_grad(u), _gelu_and_grad(vg)
        gu, xhat, rstd, vn, w_tri, tri, mixed = _gmlp_core(gu, gv, ln_g, lb_ref[...], w_ref, bt_ref[...])
        silu, silu_grad = _silu_and_grad(gb)
        d_bin_v = dbin_ref[...].astype(F32)
        d_sg = d_bin_v * silu
        d_gate = (d_bin_v * (gu * mixed) * silu_grad).astype(BF16)
        d_u = (d_sg * mixed * gu_grad).astype(BF16)
        d_mixed = d_sg * gu
        d_mixed_b = d_mixed.astype(BF16)
        d_vn, d_bias = [], []
        for g in range(GMLP_GROUPS):
            dm_g = d_mixed_b[:, g * gd:(g + 1) * gd]
            d_bias.append(jnp.sum(d_mixed[:, g * gd:(g + 1) * gd], axis=-1, keepdims=True))
            dw = lax.dot_general(dm_g, vn[:, g * gd:(g + 1) * gd], (((1,), (1,)), ((), ())),
                                 preferred_element_type=F32)
            dw_ref[g] += jnp.where(tri, dw, 0.0)
            d_vn.append(lax.dot_general(w_tri[g], dm_g, (((0,), (0,)), ((), ())), preferred_element_type=F32))
        dbt_ref[...] += jnp.concatenate(d_bias, axis=1)
        d_vn = jnp.concatenate(d_vn, axis=1)
        dlg_ref[...] += jnp.sum(d_vn * xhat, axis=0, keepdims=True)
        dlb_ref[...] += jnp.sum(d_vn, axis=0, keepdims=True)
        d_xhat = d_vn * ln_g
        d_gv = rstd * (d_xhat - jnp.mean(d_xhat, axis=-1, keepdims=True)
                       - xhat * jnp.mean(d_xhat * xhat, axis=-1, keepdims=True))
        d_v = (d_gv * gv_grad).astype(BF16)

        def fill(out):
            out[:, :d] = d_u
            out[:, d:2 * d] = d_v
            out[:, 2 * d:] = d_gate

        _write_behind(pl.program_id(0), n_steps, buf, sems, wide_ref, BLOCK, col_u, fill)

    segs = [sp for j in range(3) for sp in _segment_specs(BLOCK, d, col_u + j * d)]
    return pl.pallas_call(
        body, name="gmlp_bwd", grid=(n_steps,),
        in_specs=[*segs, _whole(w_s.shape), _whole(bias_t.shape), _whole((1, d)), _whole((1, d)),
                  pl.BlockSpec((BLOCK, d), lambda i: (i, 0)), ANY_SPEC, ANY_SPEC],
        out_specs=(ANY_SPEC, _whole(w_s.shape), _whole(bias_t.shape), _whole((1, d)), _whole((1, d))),
        out_shape=(_sds(d_proj.shape, BF16), _sds(w_s.shape, F32), _sds(bias_t.shape, F32), _sds((1, d), F32),
                   _sds((1, d), F32)),
        input_output_aliases={3 * ns + 6: 0},
        scratch_shapes=[pltpu.VMEM((2, BLOCK, 3 * d), BF16), pltpu.SemaphoreType.DMA((2,))],
        compiler_params=_params("arbitrary"),
    )(*([proj] * (3 * ns)), w_s, bias_t, ln_g, ln_b, d_bin, after, d_proj)


def _pack(parts):
    rows = []
    tile = SUBLANES * LANES
    for p in parts:
        flat = p.astype(F32).reshape(-1)
        padded = -(-flat.shape[0] // tile) * tile
        rows.append(jnp.pad(flat, (0, padded - flat.shape[0])).reshape(-1, LANES))
    return jnp.concatenate(rows, axis=0)


def _unpack(packed, shapes):
    out, row = [], 0
    tile = SUBLANES * LANES
    for shape in shapes:
        size = math.prod(shape)
        n_rows = -(-size // tile) * SUBLANES
        out.append(packed[row:row + n_rows].reshape(-1)[:size].reshape(shape))
        row += n_rows
    return out


def kernel(x, positions, norm_g, w_in, attn_sink, gmlp_ln_g, gmlp_ln_b, w_spatial, b_spatial, w_up_attn, w_up_gmlp, w_out, final_norm_g, loss_target, m_norm_g, m_w_in, m_attn_sink, m_gmlp_ln_g, m_gmlp_ln_b, m_w_spatial, m_b_spatial, m_w_up_attn, m_w_up_gmlp, m_w_out, m_final_norm_g, v_norm_g, v_w_in, v_attn_sink, v_gmlp_ln_g, v_gmlp_ln_b, v_w_spatial, v_b_spatial, v_w_up_attn, v_w_up_gmlp, v_w_out, v_final_norm_g):
    x2d, target = x[0], loss_target[0]
    s, d = x2d.shape
    n_q_heads = attn_sink.shape[1]
    cw = w_in.shape[2]
    rw = w_up_attn.shape[1]
    kv = (cw * N_DEV - 7 * d) // 2
    col_u, col_m = 2 * d + 2 * kv, 5 * d + 2 * kv
    final_g = final_norm_g.reshape(1, d)
    sink = attn_sink[0]
    w_s = w_spatial[0]
    bias_t = b_spatial[0].T
    mx, my, mc = _mesh_pos()
    pos = jnp.stack([mx, my, mc]).astype(jnp.int32)

    def one_block(fn):
        return jnp.reshape(fn(mx, my, mc), (1,)).astype(jnp.int32)

    w_in_b = _cast_into_slot(w_in[0], pos, "cast_w_in")
    to_sibling = [_Copy(0, _slot, 0, _slot, _sibling)]
    ici = [[_Copy(0, _slot, 0, _slot, lambda x, y, c, chip=chip: (*chip(x, y, c), c))] for chip in _ICI_STAGES[:2]]
    relayed = lambda x, y, c: _slot(*_ICI_STAGES[1](x, y, c), c)
    ici.append([_Copy(0, relayed, 0, relayed, lambda x, y, c: (*_ICI_STAGES[0](x, y, c), c))])
    passes = []
    for chip in _ICI_STAGES:
        landed = lambda x, y, c, chip=chip: _slot(*chip(x, y, c), c)
        passes.append([_Copy(0, landed, 0, landed, _sibling)])
    sib_sems = _rdma_start("w_in_sibling_start", [w_in_b], to_sibling)
    ici_sems = _rdma_start("w_in_ici0_start", sib_sems[2], ici[0])
    h = _rmsnorm_fwd(x2d, norm_g, ici_sems[3])
    proj = _project(h, ici_sems[2][0], one_block(_slot), "projection_own")
    w_blocks = _rdma_wait("w_in_sibling_wait", ici_sems[2], sib_sems[0], sib_sems[1], to_sibling, proj)
    proj = _project(h, w_blocks[0], one_block(lambda x, y, c: _slot(x, y, 1 - c)), "projection_sibling", proj=proj)
    w_blocks = _rdma_wait("w_in_ici0_wait", w_blocks, ici_sems[0], ici_sems[1], ici[0], proj)
    first = _gather_first_copies(3)
    for k, chip in enumerate(_ICI_STAGES):
        if k + 1 < len(_ICI_STAGES):
            ici_sems = _rdma_start("w_in_ici%d_start" % (k + 1), w_blocks, ici[k + 1])
            w_blocks = ici_sems[2]
        else:
            send1, recv1, thru, _ = _rdma_start("gather_squares_start", squares + w_blocks, first)
            squares, w_blocks = thru[:3], thru[3:]
        pass_sems = _rdma_start("w_in_pass%d_start" % k, w_blocks, passes[k])
        if k == 0:
            squares = [_cast_into_slot(w[0], pos, "cast_" + nm, after=pass_sems[3])
                       for nm, w in (("w_up_attn", w_up_attn), ("w_up_gmlp", w_up_gmlp), ("w_out", w_out))]
        proj = _project(h, pass_sems[2][0], one_block(lambda x, y, c, chip=chip: _slot(*chip(x, y, c), c)),
                        "projection_ici%d" % k, proj=proj, after=pass_sems[3])
        w_blocks = _rdma_wait("w_in_pass%d_wait" % k, pass_sems[2], pass_sems[0], pass_sems[1], passes[k], proj)
        proj = _project(h, w_blocks[0], one_block(lambda x, y, c, chip=chip: _slot(*chip(x, y, 1 - c), 1 - c)),
                        "projection_pass%d" % k, proj=proj)
        if k + 1 < len(_ICI_STAGES):
            w_blocks = _rdma_wait("w_in_ici%d_wait" % (k + 1), w_blocks, ici_sems[0], ici_sems[1], ici[k + 1], proj)
    w_in_b = w_blocks[0]

    tables = _rope_tables(positions[0])
    attn, a_in, q_rot, k_rot = _attention_fwd(proj, tables, sink, kv, proj)
    squares = _rdma_wait("gather_squares_wait", squares, send1, recv1, first, attn)
    passed = _gather_pass_copies(3)
    send2, recv2, squares, token = _rdma_start("pass_squares_start", squares, passed)
    b_in = _gmlp_fwd(proj, col_u, d, w_s, bias_t, gmlp_ln_g, gmlp_ln_b, token)
    squares = _rdma_wait("pass_squares_wait", squares, send2, recv2, passed, b_in)
    w_ua, w_ug, w_o = [w.reshape(N_DEV * rw, d) for w in squares]
    y_a = _matmul(a_in, w_ua, "nn", BF16, "up_attn")
    y_b = _matmul(b_in, w_ug, "nn", BF16, "up_gmlp")
    merged = _merge_fwd(y_a, y_b, proj, col_m)
    x_out = _matmul(merged, w_o, "nn", F32, "out_proj", res=x2d, tn=512)
    loss_p, d_final_g, dx2_b = _loss_and_final_norm_bwd(x_out, target, final_g)

    d_merged = _matmul(dx2_b, w_o, "nt", BF16, "d_merged")
    g_w_out = _matmul(merged, dx2_b, "tn", BF16, "g_w_out")
    d_ya, d_yb, d_proj = _merge_bwd(d_merged, y_a, y_b, proj, col_m, lax.empty(proj.shape, BF16))
    d_ain = _matmul(d_ya, w_ua, "nt", BF16, "d_a_in")
    g_w_ua = _matmul(a_in, d_ya, "tn", BF16, "g_w_up_attn")
    d_bin = _matmul(d_yb, w_ug, "nt", BF16, "d_b_in")
    g_w_ug = _matmul(b_in, d_yb, "tn", BF16, "g_w_up_gmlp")
    sq_grads = [g.reshape(N_DEV, rw, d) for g in (g_w_ua, g_w_ug, g_w_out)]
    sq_land = [lax.empty((N_CHIPS, rw, d), BF16) for _ in sq_grads]
    pairs_sq = _pair_copies_strided(3)
    arrays = [a for gl in zip(sq_grads, sq_land) for a in gl]
    send3, recv3, arrays, token = _rdma_start("pair_squares_start", arrays, pairs_sq)
    d_q, d_k, d_v, d_ga, d_sink = _attention_bwd(proj, q_rot, k_rot, tables, sink, kv, attn, d_ain, token)
    arrays = _rdma_wait("pair_squares_wait", arrays, send3, recv3, pairs_sq, d_q)
    sq_sums = [_pair_sum(arrays[2 * a], arrays[2 * a + 1], pos, "pair_sum_%d" % a) for a in range(3)]
    sq_land2 = [lax.empty((N_CHIPS - 1, rw, d), BF16) for _ in sq_sums]
    chip_sq = _chip_sum_copies(3)
    arrays = [a for gl in zip(sq_sums, sq_land2) for a in gl]
    send4, recv4, sq_arrays, token = _rdma_start("chip_squares_start", arrays, chip_sq)
    d_proj, d_w_s, d_bias_t, d_ln_g, d_ln_b = _gmlp_bwd(proj, col_u, d, w_s, bias_t, gmlp_ln_g, gmlp_ln_b, d_bin, token,
                                                        d_proj)
    for piece, col0, nm in ((d_q, 0, "d_q"), (d_k, d, "d_k"), (d_v, d + kv, "d_v"), (d_ga, d + 2 * kv, "d_gate")):
        d_proj = _place(d_proj, piece, col0, "place_" + nm)

    half = N_CHIPS // 2
    owners = [*(chip(mx, my) for chip in _OTHER_CHIPS), (mx, my)]

    def blocks_of(j, core):
        return jnp.stack([_slot(*owners[q], core) for q in range(j * half, (j + 1) * half)]).astype(jnp.int32)

    pairs_in = [_Copy(0, lambda x, y, c, q=q: q, 1, lambda x, y, c, q=q: q, _sibling) for q in range(half)]
    sent, token = [], None
    for j in range(2):
        g_sib = _grad_w_in_blocks(h, d_proj, blocks_of(j, 1 - mc), cw, "g_w_in_sibling%d" % j, after=token)
        sent.append(_rdma_start("pair_w_in%d_start" % j, [g_sib, lax.empty((half, d, cw), BF16)], pairs_in))
        token = sent[-1][3]
    in_sums, land_in = None, lax.empty((N_CHIPS - 1, d, cw), BF16)
    chip_in = [[_Copy(0, lambda x, y, c, k=k: k, 1, lambda x, y, c, k=k: k,
                      lambda x, y, c, k=k: (*_OTHER_CHIPS[k](x, y), c)) for k in ks] for ks in ((0, 1), (2,))]
    chip_sent = []
    for j in range(2):
        send5, recv5, arrays, _ = sent[j]
        arrays = _rdma_wait("pair_w_in%d_wait" % j, arrays, send5, recv5, pairs_in, token)
        in_sums = _grad_w_in_blocks(h, d_proj, blocks_of(j, mc), cw, "g_w_in_own%d" % j,
                                    slots=N_CHIPS, slot0=j * half, prev=in_sums, init=arrays[1], tk=512)
        chip_sent.append(_rdma_start("chip_w_in%d_start" % j, [in_sums, land_in], chip_in[j]))
        (in_sums, land_in), token = chip_sent[-1][2], chip_sent[-1][3]
    d_h = _d_hidden(d_proj, w_in_b, after=token)
    grad_x, d_norm_g = _input_grad(d_h, x2d, norm_g, dx2_b)

    sq_arrays = _rdma_wait("chip_squares_wait", sq_arrays, send4, recv4, chip_sq, grad_x)
    big = {}
    for a, (name, w, m, v) in enumerate((("w_up_attn", w_up_attn, m_w_up_attn, v_w_up_attn),
                                         ("w_up_gmlp", w_up_gmlp, m_w_up_gmlp, v_w_up_gmlp),
                                         ("w_out", w_out, m_w_out, v_w_out))):
        big[name] = [r[None] for r in _reduce_adamw(sq_arrays[2 * a], sq_arrays[2 * a + 1], w[0], m[0], v[0], pos,
                                                    "adamw_" + name)]

    heads_per_pair = 2 * n_q_heads // (kv // HEAD_DIM)
    g_sink = d_sink[:, 0, :heads_per_pair].reshape(1, n_q_heads)
    small_w = [norm_g, attn_sink, gmlp_ln_g, gmlp_ln_b, w_spatial, b_spatial, final_norm_g]
    small_m = [m_norm_g, m_attn_sink, m_gmlp_ln_g, m_gmlp_ln_b, m_w_spatial, m_b_spatial, m_final_norm_g]
    small_v = [v_norm_g, v_attn_sink, v_gmlp_ln_g, v_gmlp_ln_b, v_w_spatial, v_b_spatial, v_final_norm_g]
    small_g = [d_norm_g, g_sink, d_ln_g, d_ln_b, d_w_s[None], d_bias_t.T[None], d_final_g.reshape(d)]
    loss_pad = jnp.zeros((1,), F32)
    shapes = [w.shape for w in small_w] + [(1,)]
    packed = _small_allreduce_adamw(_pack(small_g + [loss_p[0, :1]]), _pack(small_w + [loss_pad]),
                                    _pack(small_m + [loss_pad]), _pack(small_v + [loss_pad]))
    sg, sd, sm, sv = [_unpack(p, shapes) for p in packed]
    loss = sg[-1][0]
    in_arrays = [in_sums, land_in]
    for j in range(2):
        in_arrays = _rdma_wait("chip_w_in%d_wait" % j, in_arrays, chip_sent[j][0], chip_sent[j][1], chip_in[j],
                               packed[0])
    big["w_in"] = [r[None] for r in _reduce_adamw(in_arrays[0], in_arrays[1], w_in[0], m_w_in[0], v_w_in[0], pos,
                                                  "adamw_w_in", own_slot=N_CHIPS - 1)]

    names = ["norm_g", "w_in", "attn_sink", "gmlp_ln_g", "gmlp_ln_b", "w_spatial", "b_spatial", "w_up_attn",
             "w_up_gmlp", "w_out", "final_norm_g"]
    small_names = ["norm_g", "attn_sink", "gmlp_ln_g", "gmlp_ln_b", "w_spatial", "b_spatial", "final_norm_g"]
    outs = [[], [], [], []]
    for nm in names:
        for k in range(4):
            if nm in big:
                outs[k].append(big[nm][k])
            else:
                outs[k].append((sg, sd, sm, sv)[k][small_names.index(nm)])
    return (loss, grad_x[None], *outs[0], *outs[1], *outs[2], *outs[3])
```

```python
import math
from typing import Callable, NamedTuple

import jax
import jax.numpy as jnp
from jax import lax
from jax.experimental import pallas as pl
from jax.experimental.pallas import tpu as pltpu

F32 = jnp.float32
BF16 = jnp.bfloat16
MESH = pl.DeviceIdType.MESH

N_DEV = 8
N_CHIPS = 4
HEAD_DIM = 64
BLOCK = 128
ROPE_DIM = 16
ROPE_HALF = ROPE_DIM // 2
ROPE_THETA = 500000.0
GMLP_GROUPS = 8
NORM_EPS = 1e-5
LN_EPS = 1e-5
ATTN_SCALE = HEAD_DIM ** -0.5
LANES = 128
SUBLANES = 8
VMEM_LIMIT = 48 * 1024 * 1024
VMEM_LIMIT_WIDE = 56 * 1024 * 1024
DOT_COLS = 1024
SEGMENT_SPLIT = 4

ADAM_LR = 0.001
ADAM_B1 = 0.9
ADAM_B2 = 0.999
ADAM_EPS = 1e-08
ADAM_WD = 0.01
ADAM_STEP = 10

GELU_C = math.sqrt(2.0 / math.pi)
GELU_K = 0.044715

HBM_SPEC = pl.BlockSpec(memory_space=pltpu.HBM)
ANY_SPEC = pl.BlockSpec(memory_space=pl.ANY)
SEM_SPEC = pl.BlockSpec(memory_space=pltpu.SEMAPHORE)
VMEM_SPEC = pl.BlockSpec(memory_space=pltpu.VMEM)
SMEM_SPEC = pl.BlockSpec(memory_space=pltpu.SMEM)


def _sds(shape, dtype):
    return jax.ShapeDtypeStruct(shape, dtype)


def _params(*sem, vmem=VMEM_LIMIT):
    return pltpu.CompilerParams(dimension_semantics=sem or None, vmem_limit_bytes=vmem)


def _gelu(x):
    return 0.5 * x * (1.0 + jnp.tanh(GELU_C * (x + GELU_K * x * x * x)))


def _gelu_and_grad(x):
    x2 = x * x
    t = jnp.tanh(x * (GELU_C + (GELU_C * GELU_K) * x2))
    half = 0.5 + 0.5 * t
    return x * half, half + (x * (1.0 - t * t)) * (0.5 * GELU_C + (1.5 * GELU_C * GELU_K) * x2)


def _silu_and_grad(x):
    s = jax.nn.sigmoid(x)
    return x * s, s * (1.0 + x * (1.0 - s))


def _adamw(w, g, m, v):
    m = ADAM_B1 * m + (1.0 - ADAM_B1) * g
    v = ADAM_B2 * v + (1.0 - ADAM_B2) * (g * g)
    m_hat = m / (1.0 - ADAM_B1 ** ADAM_STEP)
    v_hat = v / (1.0 - ADAM_B2 ** ADAM_STEP)
    delta = -ADAM_LR * (m_hat / (jnp.sqrt(v_hat) + ADAM_EPS) + ADAM_WD * w)
    return delta, m, v


def _mesh_pos():
    return lax.axis_index("x"), lax.axis_index("y"), lax.axis_index("c")


def _slot(x, y, c):
    return 4 * x + 2 * y + c


def _chip(x, y):
    return 2 * x + y


def _sibling(x, y, c):
    return (x, y, 1 - c)


_OTHER_CHIPS = (lambda x, y: (1 - x, y), lambda x, y: (x, 1 - y), lambda x, y: (1 - x, 1 - y))
_ICI_STAGES = (lambda x, y, c: (x ^ c, y ^ (1 - c)), lambda x, y, c: (x ^ (1 - c), y ^ c),
               lambda x, y, c: (1 - x, 1 - y))


class _Copy(NamedTuple):
    src: int
    src_slot: Callable
    dst: int
    dst_slot: Callable
    peer: Callable


def _descriptor(refs, send_sems, recv_sems, k, cp):
    pos = _mesh_pos()
    return pltpu.make_async_remote_copy(
        src_ref=refs[cp.src].at[cp.src_slot(*pos)], dst_ref=refs[cp.dst].at[cp.dst_slot(*pos)],
        send_sem=send_sems.at[k], recv_sem=recv_sems.at[k], device_id=cp.peer(*pos), device_id_type=MESH)


def _gather_first_copies(n_arrays):
    copies = []
    for a in range(n_arrays):
        copies.append(_Copy(a, _slot, a, _slot, _sibling))
        for chip in _OTHER_CHIPS:
            copies.append(_Copy(a, _slot, a, _slot, lambda x, y, c, chip=chip: (*chip(x, y), c)))
    return copies


def _gather_pass_copies(n_arrays):
    copies = []
    for a in range(n_arrays):
        for chip in _OTHER_CHIPS:
            src = lambda x, y, c, chip=chip: _slot(*chip(x, y), c)
            copies.append(_Copy(a, src, a, src, _sibling))
    return copies


def _pair_copies_strided(n_sets):
    copies = []
    for a in range(n_sets):
        for q in range(N_CHIPS):
            copies.append(_Copy(2 * a, lambda x, y, c, q=q: 2 * q + 1 - c, 2 * a + 1, lambda x, y, c, q=q: q, _sibling))
    return copies


def _chip_sum_copies(n_sets):
    copies = []
    for a in range(n_sets):
        for k, chip in enumerate(_OTHER_CHIPS):
            copies.append(_Copy(2 * a, lambda x, y, c, chip=chip: _chip(*chip(x, y)), 2 * a + 1,
                                lambda x, y, c, k=k: k, lambda x, y, c, chip=chip: (*chip(x, y), c)))
    return copies


def _rdma_start(name, arrays, copies):
    n, nc = len(arrays), len(copies)

    def body(*refs):
        in_refs = refs[:n]
        send_sems, recv_sems = refs[n], refs[n + 1]
        token = refs[2 * n + 2]
        for k, cp in enumerate(copies):
            _descriptor(in_refs, send_sems, recv_sems, k, cp).start()
        token[...] = jnp.zeros_like(token)

    out = pl.pallas_call(
        body, name=name,
        out_shape=(pltpu.SemaphoreType.DMA((nc,)), pltpu.SemaphoreType.DMA((nc,)),
                   *[pltpu.HBM(a.shape, a.dtype) for a in arrays], _sds((SUBLANES, LANES), F32)),
        in_specs=[HBM_SPEC] * n, out_specs=(SEM_SPEC, SEM_SPEC, *([HBM_SPEC] * n), VMEM_SPEC),
        input_output_aliases={i: i + 2 for i in range(n)},
        compiler_params=pltpu.CompilerParams(has_side_effects=pltpu.SideEffectType.DATAFLOW_SIDE_EFFECTING),
    )(*[pltpu.with_memory_space_constraint(a, pltpu.HBM) for a in arrays])
    return out[0], out[1], list(out[2:2 + n]), out[2 + n]


def _rdma_wait(name, arrays, send_sems, recv_sems, copies, after):
    n = len(arrays)

    def body(*refs):
        in_refs = refs[:n]
        send_ref, recv_ref = refs[n], refs[n + 1]
        for k, cp in enumerate(copies):
            d = _descriptor(in_refs, send_ref, recv_ref, k, cp)
            d.wait_send()
            d.wait_recv()

    out = pl.pallas_call(
        body, name=name, out_shape=tuple(pltpu.HBM(a.shape, a.dtype) for a in arrays),
        in_specs=[HBM_SPEC] * n + [SEM_SPEC, SEM_SPEC, ANY_SPEC], out_specs=tuple([HBM_SPEC] * n),
        input_output_aliases={i: i for i in range(n)},
        compiler_params=pltpu.CompilerParams(has_side_effects=pltpu.SideEffectType.DATAFLOW_SIDE_EFFECTING),
    )(*arrays, send_sems, recv_sems, after)
    return list(out)


def _cast_into_slot(w, pos, name, after=None):
    rows, cols = w.shape
    tr = min(rows, 256)

    def body(pos_ref, w_ref, *rest):
        rest[-1][...] = w_ref[...].astype(BF16)

    in_specs, args = [pl.BlockSpec((tr, cols), lambda i, p: (i, 0))], [pos, w]
    if after is not None:
        in_specs.append(ANY_SPEC)
        args.append(after)
    return pl.pallas_call(
        body, name=name,
        grid_spec=pltpu.PrefetchScalarGridSpec(
            num_scalar_prefetch=1, grid=(rows // tr,), in_specs=in_specs,
            out_specs=pl.BlockSpec((None, tr, cols), lambda i, p: (_slot(p[0], p[1], p[2]), i, 0))),
        out_shape=_sds((N_DEV, rows, cols), BF16), compiler_params=_params("parallel"),
    )(*args)


def _pair_sum(g, land, pos, name):
    _, rows, cols = land.shape
    tr = min(rows, 128)
    strided = g.shape[0] == N_DEV

    def body(pos_ref, g_ref, l_ref, o_ref):
        o_ref[...] = (g_ref[...].astype(F32) + l_ref[...].astype(F32)).astype(BF16)

    g_map = (lambda q, i, p: (2 * q + p[2], i, 0)) if strided else (lambda q, i, p: (q, i, 0))
    blk = pl.BlockSpec((None, tr, cols), lambda q, i, p: (q, i, 0))
    return pl.pallas_call(
        body, name=name,
        grid_spec=pltpu.PrefetchScalarGridSpec(
            num_scalar_prefetch=1, grid=(N_CHIPS, rows // tr),
            in_specs=[pl.BlockSpec((None, tr, cols), g_map), blk], out_specs=blk),
        out_shape=_sds((N_CHIPS, rows, cols), BF16), compiler_params=_params("parallel", "parallel"),
    )(pos, g, land)


def _reduce_adamw(sums, land, w, m, v, pos, name, own_slot=None):
    rows, cols = w.shape
    tr = min(rows, 64)
    own = (lambda p: _chip(p[0], p[1])) if own_slot is None else (lambda p: own_slot)

    def body(pos_ref, s_ref, l_ref, w_ref, m_ref, v_ref, g_ref, d_ref, nm_ref, nv_ref):
        g = s_ref[...].astype(F32)
        for k in range(N_CHIPS - 1):
            g = g + l_ref[k].astype(F32)
        delta, nm, nv = _adamw(w_ref[...], g, m_ref[...], v_ref[...])
        g_ref[...] = g
        d_ref[...] = delta
        nm_ref[...] = nm
        nv_ref[...] = nv

    spec = pl.BlockSpec((tr, cols), lambda i, p: (i, 0))
    return pl.pallas_call(
        body, name=name,
        grid_spec=pltpu.PrefetchScalarGridSpec(
            num_scalar_prefetch=1, grid=(rows // tr,),
            in_specs=[pl.BlockSpec((None, tr, cols), lambda i, p: (own(p), i, 0)),
                      pl.BlockSpec((N_CHIPS - 1, tr, cols), lambda i, p: (0, i, 0)), spec, spec, spec],
            out_specs=(spec, spec, spec, spec)),
        out_shape=tuple([_sds((rows, cols), F32)] * 4), compiler_params=_params("parallel"),
    )(pos, sums, land, w, m, v)


def _small_allreduce_adamw(g, w, m, v):
    rows = g.shape[0]

    def body(g_ref, w_ref, m_ref, v_ref, gs_ref, d_ref, nm_ref, nv_ref, all_ref, send_sems, recv_sems):
        x, y, c = _mesh_pos()
        me = _slot(x, y, c)
        copies = []
        for k in range(1, N_DEV):
            peer = (x ^ (k >> 2), y ^ ((k >> 1) & 1), c ^ (k & 1))
            copies.append(pltpu.make_async_remote_copy(
                src_ref=g_ref, dst_ref=all_ref.at[me], send_sem=send_sems.at[k - 1],
                recv_sem=recv_sems.at[k - 1], device_id=peer, device_id_type=MESH))
        for cp in copies:
            cp.start()
        all_ref[me] = g_ref[...]
        for cp in copies:
            cp.wait_recv()
        total = all_ref[0]
        for s in range(1, N_DEV):
            total = total + all_ref[s]
        delta, nm, nv = _adamw(w_ref[...], total, m_ref[...], v_ref[...])
        gs_ref[...] = total
        d_ref[...] = delta
        nm_ref[...] = nm
        nv_ref[...] = nv
        for cp in copies:
            cp.wait_send()

    return pl.pallas_call(
        body, name="small_allreduce_adamw", out_shape=tuple([_sds((rows, LANES), F32)] * 4),
        in_specs=[VMEM_SPEC] * 4, out_specs=tuple([VMEM_SPEC] * 4),
        scratch_shapes=[pltpu.VMEM((N_DEV, rows, LANES), F32), pltpu.SemaphoreType.DMA((7,)),
                        pltpu.SemaphoreType.DMA((7,))],
    )(g, w, m, v)


_DOT_DIMS = {"nn": ((1,), (0,)), "nt": ((1,), (1,)), "tn": ((0,), (0,))}


def _dot(a, b, mode):
    return lax.dot_general(a, b, (_DOT_DIMS[mode], ((), ())), preferred_element_type=F32)


def _col_chunks(cols):
    return [(c0, min(c0 + DOT_COLS, cols)) for c0 in range(0, cols, DOT_COLS)]


def _matmul(a, b, mode, out_dtype, name, *, res=None, tm=1024, tn=1024):
    if mode == "tn":
        kdim, mdim = a.shape
    else:
        mdim, kdim = a.shape
    ndim = b.shape[0] if mode == "nt" else b.shape[1]
    tm, tn = min(tm, mdim), min(tn, ndim)
    assert mdim % tm == 0 and ndim % tn == 0, (name, mdim, ndim)

    def body(*refs):
        out = _dot(refs[0][...], refs[1][...], mode)
        if res is not None:
            out = out + refs[2][...]
        refs[-1][...] = out.astype(out_dtype)

    a_spec = pl.BlockSpec((kdim, tm), lambda i, j: (0, i)) if mode == "tn" else pl.BlockSpec((tm, kdim), lambda i, j: (i, 0))
    b_spec = pl.BlockSpec((tn, kdim), lambda i, j: (j, 0)) if mode == "nt" else pl.BlockSpec((kdim, tn), lambda i, j: (0, j))
    o_spec = pl.BlockSpec((tm, tn), lambda i, j: (i, j))
    in_specs, args = [a_spec, b_spec], [a, b]
    if res is not None:
        in_specs.append(o_spec)
        args.append(res)
    return pl.pallas_call(
        body, name=name, grid=(mdim // tm, ndim // tn), in_specs=in_specs, out_specs=o_spec,
        out_shape=_sds((mdim, ndim), out_dtype), compiler_params=_params("parallel", "parallel"),
    )(*args)


def _project(h, w_blocks, block_ids, name, *, proj=None, after=None, tm=1024, tk=1024):
    s, d = h.shape
    _, _, cw = w_blocks.shape
    n = block_ids.shape[0]
    tm, tk = min(tm, s), min(tk, d)
    nk = d // tk

    def body(ids_ref, h_ref, w_ref, *rest):
        o_ref, acc_ref = rest[-2], rest[-1]
        k = pl.program_id(2)

        @pl.when(k == 0)
        def _():
            acc_ref[...] = jnp.zeros_like(acc_ref)

        for c0, c1 in _col_chunks(cw):
            acc_ref[:, c0:c1] += _dot(h_ref[...], w_ref[:, c0:c1], "nn")

        @pl.when(k == nk - 1)
        def _():
            o_ref[...] = acc_ref[...].astype(BF16)

    in_specs = [pl.BlockSpec((tm, tk), lambda j, i, k, ids: (i, k)),
                pl.BlockSpec((None, tk, cw), lambda j, i, k, ids: (ids[j], k, 0))]
    args = [block_ids, h, w_blocks]
    aliases = {}
    if proj is not None:
        in_specs.append(ANY_SPEC)
        args.append(proj)
        aliases = {3: 0}
    if after is not None:
        in_specs.append(ANY_SPEC)
        args.append(after)
    return pl.pallas_call(
        body, name=name,
        grid_spec=pltpu.PrefetchScalarGridSpec(
            num_scalar_prefetch=1, grid=(n, s // tm, d // tk), in_specs=in_specs,
            out_specs=pl.BlockSpec((tm, cw), lambda j, i, k, ids: (i, ids[j])),
            scratch_shapes=[pltpu.VMEM((tm, cw), F32)]),
        out_shape=_sds((s, N_DEV * cw), BF16), input_output_aliases=aliases,
        compiler_params=_params("arbitrary", "arbitrary", "arbitrary", vmem=VMEM_LIMIT_WIDE),
    )(*args)


def _grad_w_in_blocks(h, d_proj, block_ids, cw, name, *, slots=None, slot0=0, prev=None, init=None, after=None,
                      tm=1024, tk=1024):
    s, d = h.shape
    n = block_ids.shape[0]
    slots = n if slots is None else slots
    tm, tk = min(tm, d), min(tk, s)
    nk = s // tk

    def body(ids_ref, h_ref, g_ref, *rest):
        o_ref, acc_ref = rest[-2], rest[-1]
        k = pl.program_id(2)

        @pl.when(k == 0)
        def _():
            acc_ref[...] = jnp.zeros_like(acc_ref) if init is None else rest[0][...].astype(F32)

        for c0, c1 in _col_chunks(cw):
            acc_ref[:, c0:c1] += _dot(h_ref[...], g_ref[:, c0:c1], "tn")

        @pl.when(k == nk - 1)
        def _():
            o_ref[...] = acc_ref[...].astype(BF16)

    in_specs = [pl.BlockSpec((tk, tm), lambda q, i, k, ids: (k, i)),
                pl.BlockSpec((tk, cw), lambda q, i, k, ids: (k, ids[q]))]
    args = [block_ids, h, d_proj]
    aliases = {}
    if init is not None:
        in_specs.append(pl.BlockSpec((None, tm, cw), lambda q, i, k, ids: (q, i, 0)))
        args.append(init)
    if prev is not None:
        aliases = {len(args): 0}
        in_specs.append(ANY_SPEC)
        args.append(prev)
    if after is not None:
        in_specs.append(ANY_SPEC)
        args.append(after)
    return pl.pallas_call(
        body, name=name,
        grid_spec=pltpu.PrefetchScalarGridSpec(
            num_scalar_prefetch=1, grid=(n, d // tm, nk), in_specs=in_specs,
            out_specs=pl.BlockSpec((None, tm, cw), lambda q, i, k, ids: (slot0 + q, i, 0)),
            scratch_shapes=[pltpu.VMEM((tm, cw), F32)]),
        out_shape=_sds((slots, d, cw), BF16), input_output_aliases=aliases,
        compiler_params=_params("parallel", "parallel", "arbitrary", vmem=VMEM_LIMIT_WIDE),
    )(*args)


def _d_hidden(d_proj, w_blocks, after=None, *, tm=1024, tn=1024):
    s = d_proj.shape[0]
    nb, d, cw = w_blocks.shape
    tm, tn = min(tm, s), min(tn, d)

    def body(g_ref, w_ref, *rest):
        o_ref = rest[-1]
        k = pl.program_id(2)

        @pl.when(k == 0)
        def _():
            o_ref[...] = jnp.zeros_like(o_ref)

        o_ref[...] += _dot(g_ref[...], w_ref[...], "nt")

    in_specs = [pl.BlockSpec((tm, cw), lambda i, j, k: (i, k)),
                pl.BlockSpec((None, tn, cw), lambda i, j, k: (k, j, 0))]
    args = [d_proj, w_blocks]
    if after is not None:
        in_specs.append(ANY_SPEC)
        args.append(after)
    return pl.pallas_call(
        body, name="d_h", grid=(s // tm, d // tn, nb), in_specs=in_specs,
        out_specs=pl.BlockSpec((tm, tn), lambda i, j, k: (i, j)), out_shape=_sds((s, d), F32),
        compiler_params=_params("parallel", "parallel", "arbitrary", vmem=VMEM_LIMIT_WIDE),
    )(*args)


def _row_tile(rows):
    return min(rows, 128)


def _segment_specs(rows, d, col0):
    w = d // SEGMENT_SPLIT
    assert col0 % w == 0
    return [pl.BlockSpec((rows, w), lambda i, t=t: (i, col0 // w + t)) for t in range(SEGMENT_SPLIT)]


def _cat(refs):
    return jnp.concatenate([r[...].astype(F32) for r in refs], axis=1)


def _rmsnorm_fwd(x, g, after):
    s, d = x.shape
    tr = min(s, 2 * _row_tile(s))

    def body(x_ref, g_ref, after_ref, h_ref):
        xv = x_ref[...]
        r = lax.rsqrt(jnp.mean(xv * xv, axis=-1, keepdims=True) + NORM_EPS)
        h_ref[...] = (xv * r * g_ref[...]).astype(BF16)

    row = pl.BlockSpec((tr, d), lambda i: (i, 0))
    vec = pl.BlockSpec((1, d), lambda i: (0, 0))
    return pl.pallas_call(body, name="rmsnorm_fwd", grid=(s // tr,), in_specs=[row, vec, ANY_SPEC], out_specs=row,
                          out_shape=_sds((s, d), BF16), compiler_params=_params("parallel"))(x, g, after)


def _merge_fwd(y_a, y_b, proj, col_m):
    s, d = y_a.shape
    tr = _row_tile(s)
    ns = SEGMENT_SPLIT

    def body(ya_ref, yb_ref, *rest):
        ma, mb, o_ref = _cat(rest[:ns]), _cat(rest[ns:2 * ns]), rest[2 * ns]
        o_ref[...] = (jax.nn.sigmoid(ma) * ya_ref[...].astype(F32)
                      + jax.nn.sigmoid(mb) * yb_ref[...].astype(F32)).astype(BF16)

    row = pl.BlockSpec((tr, d), lambda i: (i, 0))
    return pl.pallas_call(
        body, name="merge_fwd", grid=(s // tr,),
        in_specs=[row, row, *_segment_specs(tr, d, col_m), *_segment_specs(tr, d, col_m + d)],
        out_specs=row, out_shape=_sds((s, d), BF16), compiler_params=_params("parallel"),
    )(y_a, y_b, *([proj] * (2 * ns)))


def _loss_and_final_norm_bwd(x2, target, g):
    s, d = x2.shape
    tr = min(s, 2 * _row_tile(s))

    def body(x_ref, t_ref, g_ref, loss_ref, dg_ref, dxb_ref):
        @pl.when(pl.program_id(0) == 0)
        def _():
            loss_ref[...] = jnp.zeros_like(loss_ref)
            dg_ref[...] = jnp.zeros_like(dg_ref)

        xv, gv = x_ref[...], g_ref[...]
        r = lax.rsqrt(jnp.mean(xv * xv, axis=-1, keepdims=True) + NORM_EPS)
        xhat = xv * r
        err = xhat * gv - t_ref[...]
        loss_ref[...] += 0.5 * jnp.sum(jnp.mean(err * err, axis=-1, keepdims=True))
        dy = err / d
        dg_ref[...] += jnp.sum(dy * xhat, axis=0, keepdims=True)
        dyg = dy * gv
        dxb_ref[...] = (r * (dyg - xhat * jnp.mean(dyg * xhat, axis=-1, keepdims=True))).astype(BF16)

    row = pl.BlockSpec((tr, d), lambda i: (i, 0))
    vec = pl.BlockSpec((1, d), lambda i: (0, 0))
    return pl.pallas_call(
        body, name="loss_final_norm_bwd", grid=(s // tr,), in_specs=[row, row, vec],
        out_specs=(pl.BlockSpec((SUBLANES, LANES), lambda i: (0, 0)), vec, row),
        out_shape=(_sds((SUBLANES, LANES), F32), _sds((1, d), F32), _sds((s, d), BF16)),
        compiler_params=_params("arbitrary"))(x2, target, g)


def _write_behind(step, n_steps, buf, sems, wide_ref, rows, col0, fill):
    cols = buf.shape[2]
    slot = step % 2

    def copy(at_step, at_slot):
        dst = wide_ref.at[pl.ds(pl.multiple_of(at_step * rows, rows), rows), pl.ds(col0, cols)]
        return pltpu.make_async_copy(buf.at[at_slot], dst, sems.at[at_slot])

    @pl.when(step >= 2)
    def _():
        copy(step - 2, slot).wait()

    fill(buf.at[slot])
    copy(step, slot).start()

    @pl.when(step == n_steps - 1)
    def _():
        copy(step, slot).wait()
        if n_steps > 1:
            copy(step - 1, 1 - slot).wait()


def _merge_bwd(d_merged, y_a, y_b, proj, col_m, d_proj):
    s, d = y_a.shape
    tr = _row_tile(s)
    ns = SEGMENT_SPLIT
    n_steps = s // tr

    def body(dm_ref, ya_ref, yb_ref, *rest):
        ma, mb = _cat(rest[:ns]), _cat(rest[ns:2 * ns])
        dya_ref, dyb_ref, wide_ref, buf, sems = rest[2 * ns + 1:]
        dm = dm_ref[...].astype(F32)
        sa = jax.nn.sigmoid(ma)
        sb = jax.nn.sigmoid(mb)
        dya_ref[...] = (dm * sa).astype(BF16)
        dyb_ref[...] = (dm * sb).astype(BF16)

        def fill(out):
            out[:, :d] = (dm * ya_ref[...].astype(F32) * (sa * (1.0 - sa))).astype(BF16)
            out[:, d:] = (dm * yb_ref[...].astype(F32) * (sb * (1.0 - sb))).astype(BF16)

        _write_behind(pl.program_id(0), n_steps, buf, sems, wide_ref, tr, col_m, fill)

    row = pl.BlockSpec((tr, d), lambda i: (i, 0))
    n_in = 3 + 2 * ns
    return pl.pallas_call(
        body, name="merge_bwd", grid=(n_steps,),
        in_specs=[row, row, row, *_segment_specs(tr, d, col_m), *_segment_specs(tr, d, col_m + d), ANY_SPEC],
        out_specs=(row, row, ANY_SPEC),
        out_shape=(_sds((s, d), BF16), _sds((s, d), BF16), _sds(d_proj.shape, BF16)),
        input_output_aliases={n_in: 2},
        scratch_shapes=[pltpu.VMEM((2, tr, 2 * d), BF16), pltpu.SemaphoreType.DMA((2,))],
        compiler_params=_params("arbitrary"))(d_merged, y_a, y_b, *([proj] * (2 * ns)), d_proj)


def _place(d_proj, piece, col0, name):
    s, w = piece.shape
    bw = math.gcd(w, col0) if col0 else w
    tr = min(s, 512)

    def body(p_ref, wide_in, o_ref):
        o_ref[...] = p_ref[...]

    return pl.pallas_call(
        body, name=name, grid=(s // tr, w // bw),
        in_specs=[pl.BlockSpec((tr, bw), lambda i, j: (i, j)), ANY_SPEC],
        out_specs=pl.BlockSpec((tr, bw), lambda i, j: (i, col0 // bw + j)),
        out_shape=_sds(d_proj.shape, d_proj.dtype), input_output_aliases={1: 0},
        compiler_params=_params("parallel", "parallel"))(piece, d_proj)


def _input_grad(d_h, x, g, dx2):
    s, d = x.shape
    tr = min(s, 2 * _row_tile(s))

    def body(dh_ref, x_ref, g_ref, dx2_ref, gx_ref, dg_ref):
        @pl.when(pl.program_id(0) == 0)
        def _():
            dg_ref[...] = jnp.zeros_like(dg_ref)

        xv, dh = x_ref[...], dh_ref[...]
        r = lax.rsqrt(jnp.mean(xv * xv, axis=-1, keepdims=True) + NORM_EPS)
        xhat = xv * r
        dg_ref[...] += jnp.sum(dh * xhat, axis=0, keepdims=True)
        dyg = dh * g_ref[...]
        gx_ref[...] = dx2_ref[...].astype(F32) + r * (dyg - xhat * jnp.mean(dyg * xhat, axis=-1, keepdims=True))

    row = pl.BlockSpec((tr, d), lambda i: (i, 0))
    vec = pl.BlockSpec((1, d), lambda i: (0, 0))
    return pl.pallas_call(
        body, name="input_grad", grid=(s // tr,), in_specs=[row, row, vec, row], out_specs=(row, vec),
        out_shape=(_sds((s, d), F32), _sds((1, d), F32)), compiler_params=_params("arbitrary"))(d_h, x, g, dx2)


def _rope_tables(positions):
    inv_freq = ROPE_THETA ** (-jnp.arange(ROPE_HALF, dtype=F32) * 2.0 / ROPE_DIM)
    ang = positions.astype(F32)[:, None] * inv_freq
    cos, sin = jnp.cos(ang), jnp.sin(ang)
    zero = jnp.zeros((positions.shape[0], HEAD_DIM - ROPE_DIM), F32)
    zero_h = jnp.zeros_like(sin)
    c = jnp.concatenate([cos, cos, zero + 1.0], axis=1)
    up = jnp.concatenate([-sin, zero_h, zero], axis=1)
    down = jnp.concatenate([zero_h, sin, zero], axis=1)
    reps = LANES // HEAD_DIM
    return jnp.stack([jnp.tile(c, (1, reps)), jnp.tile(up, (1, reps)), jnp.tile(down, (1, reps))])


def _lane_tiles(x):
    return [x[:, t * LANES:(t + 1) * LANES] for t in range(x.shape[1] // LANES)]


def _rope(x, tab):
    out = [xt * tab[0] + pltpu.roll(xt, LANES - ROPE_HALF, 1) * tab[1] + pltpu.roll(xt, ROPE_HALF, 1) * tab[2]
           for xt in _lane_tiles(x)]
    return out[0] if len(out) == 1 else jnp.concatenate(out, axis=1)


def _rope_bwd(g, tab):
    out = [gt * tab[0] + pltpu.roll(gt * tab[1], ROPE_HALF, 1) + pltpu.roll(gt * tab[2], LANES - ROPE_HALF, 1)
           for gt in _lane_tiles(g)]
    return out[0] if len(out) == 1 else jnp.concatenate(out, axis=1)


def _head(x, h):
    return x[:, h * HEAD_DIM:(h + 1) * HEAD_DIM]


def _stack_heads(x, first, count):
    return jnp.concatenate([_head(x, first + h) for h in range(count)], axis=0)


def _dot_nt(a, b):
    return lax.dot_general(a, b, (((1,), (1,)), ((), ())), preferred_element_type=F32)


def _causal(rows):
    qi = lax.broadcasted_iota(jnp.int32, (rows, BLOCK), 0) % BLOCK
    return lax.broadcasted_iota(jnp.int32, (rows, BLOCK), 1) <= qi


def _band_probs_by_head(qs, k_band, sink_ref, first, count, causal, blk):
    s_band = _dot_nt(qs, k_band)
    p_all, p_band, p_sink = [], [], []
    for h in range(count):
        rows = slice(h * BLOCK, (h + 1) * BLOCK)
        sink = sink_ref[first + h]
        s = jnp.where(causal, s_band[rows, :BLOCK], jnp.where(blk > 0, s_band[rows, BLOCK:], -jnp.inf))
        m = jnp.maximum(jnp.max(s, axis=-1, keepdims=True), sink)
        p = jnp.exp(s - m)
        ps = jnp.exp(sink - m)
        inv = 1.0 / (jnp.sum(p, axis=-1, keepdims=True) + ps)
        p = p * inv
        p_all.append(p)
        p_sink.append(ps * inv)
        p_band.append(_split_band(p, causal))
    cat = lambda parts: jnp.concatenate(parts, axis=0)
    return cat([c for c, _ in p_band]), cat([v for _, v in p_band]), cat(p_all), cat(p_sink)


def _split_band(x, causal):
    return jnp.where(causal, x, 0.0).astype(BF16), jnp.where(causal, 0.0, x).astype(BF16)


def _attn_dims(s, d, kv):
    n_kv = kv // HEAD_DIM
    group = d // kv
    qw = 2 * group * HEAD_DIM
    assert n_kv % 2 == 0 and (d + 2 * kv) % qw == 0 and s % BLOCK == 0
    return group, qw, n_kv // 2, s // BLOCK


def _attention_fwd(proj, tables, sink, kv, after):
    s, d = proj.shape[0], sink.shape[0] * HEAD_DIM
    group, qw, n_pairs, nb = _attn_dims(s, d, kv)

    def body(sink_ref, q_ref, kc_ref, kp_ref, vc_ref, vp_ref, ga_ref, tc_ref, tp_ref, after_ref,
             attn_ref, ain_ref, qrot_ref, krot_ref):
        pair, blk = pl.program_id(0), pl.program_id(1)
        tab_c, tab_p = tc_ref[...], tp_ref[...]
        q = _rope(q_ref[...].astype(F32), tab_c) * ATTN_SCALE
        k_cur, k_prev = _rope(kc_ref[...].astype(F32), tab_c), _rope(kp_ref[...].astype(F32), tab_p)
        qrot_ref[...] = q.astype(BF16)
        krot_ref[...] = k_cur.astype(BF16)
        v_cur, v_prev = vc_ref[...], vp_ref[...]
        causal = _causal(BLOCK)
        outs = []
        for a in range(2):
            k_band = jnp.concatenate([_head(k_cur, a), _head(k_prev, a)], axis=0).astype(BF16)
            vc, vp = _head(v_cur, a).astype(BF16), _head(v_prev, a).astype(BF16)
            qs = _stack_heads(q, a * group, group).astype(BF16)
            p_cur, p_prev, _, _ = _band_probs_by_head(qs, k_band, sink_ref, (2 * pair + a) * group, group, causal, blk)
            o = jnp.dot(p_cur, vc, preferred_element_type=F32) + jnp.dot(p_prev, vp, preferred_element_type=F32)
            outs += [o[h * BLOCK:(h + 1) * BLOCK] for h in range(group)]
        attn = jnp.concatenate(outs, axis=1)
        attn_ref[...] = attn.astype(BF16)
        silu, _ = _silu_and_grad(ga_ref[...].astype(F32))
        ain_ref[...] = (attn * silu).astype(BF16)

    k0, v0, g0 = d // LANES, (d + kv) // LANES, (d + 2 * kv) // qw
    prev = lambda i: jnp.maximum(i - 1, 0)
    in_specs = [
        SMEM_SPEC,
        pl.BlockSpec((BLOCK, qw), lambda p, i: (i, p)),
        pl.BlockSpec((BLOCK, LANES), lambda p, i: (i, k0 + p)),
        pl.BlockSpec((BLOCK, LANES), lambda p, i: (prev(i), k0 + p)),
        pl.BlockSpec((BLOCK, LANES), lambda p, i: (i, v0 + p)),
        pl.BlockSpec((BLOCK, LANES), lambda p, i: (prev(i), v0 + p)),
        pl.BlockSpec((BLOCK, qw), lambda p, i: (i, g0 + p)),
        pl.BlockSpec((3, BLOCK, LANES), lambda p, i: (0, i, 0)),
        pl.BlockSpec((3, BLOCK, LANES), lambda p, i: (0, prev(i), 0)),
        ANY_SPEC,
    ]
    out = pl.BlockSpec((BLOCK, qw), lambda p, i: (i, p))
    k_out = pl.BlockSpec((BLOCK, LANES), lambda p, i: (i, p))
    return pl.pallas_call(
        body, name="attention_fwd", grid=(n_pairs, nb), in_specs=in_specs, out_specs=(out, out, out, k_out),
        out_shape=(_sds((s, d), BF16), _sds((s, d), BF16), _sds((s, d), BF16), _sds((s, kv), BF16)),
        compiler_params=_params("parallel", "parallel"),
    )(sink, proj, proj, proj, proj, proj, proj, tables, tables, after)


def _attention_bwd(proj, q_rot, k_rot, tables, sink, kv, attn, d_ain, after, d_proj):
    s, d = proj.shape[0], sink.shape[0] * HEAD_DIM
    group, qw, n_pairs, nb = _attn_dims(s, d, kv)

    def body(sink_ref, q_ref, kc_ref, kp_ref, vc_ref, vp_ref, ga_ref, tc_ref, tp_ref, attn_ref, dain_ref, after_ref,
             wide_in, dq_ref, dk_ref, dv_ref, dga_ref, dsink_ref, carry_k, carry_v, sink_acc):
        pair, blk = pl.program_id(0), pl.program_id(1)

        @pl.when(blk == 0)
        def _():
            carry_k[...] = jnp.zeros_like(carry_k)
            carry_v[...] = jnp.zeros_like(carry_v)
            sink_acc[...] = jnp.zeros_like(sink_acc)

        @pl.when(blk < nb)
        def _():
            tab_c, tab_p = tc_ref[...], tp_ref[...]
            q = q_ref[...].astype(F32)
            k_cur, k_prev = kc_ref[...].astype(F32), kp_ref[...].astype(F32)
            v_cur, v_prev = vc_ref[...], vp_ref[...]
            silu, silu_grad = _silu_and_grad(ga_ref[...].astype(F32))
            d_ain_v = dain_ref[...].astype(F32)
            dga_ref[...] = (d_ain_v * attn_ref[...].astype(F32) * silu_grad).astype(BF16)
            d_attn = d_ain_v * silu
            q_t = q.T
            d_attn_t = d_attn.T
            causal = _causal(BLOCK)
            dq_parts = []
            dk_t = {"cur": [], "prev": []}
            dv_t = {"cur": [], "prev": []}
            lane = lax.broadcasted_iota(jnp.int32, (BLOCK, LANES), 1)
            dsink = jnp.zeros((BLOCK, LANES), F32)
            for a in range(2):
                first = a * group
                qs = _stack_heads(q, first, group).astype(BF16)
                kc, kp = _head(k_cur, a).astype(BF16), _head(k_prev, a).astype(BF16)
                k_band = jnp.concatenate([_head(k_cur, a), _head(k_prev, a)], axis=0).astype(BF16)
                v_band = jnp.concatenate([_head(v_cur, a), _head(v_prev, a)], axis=0).astype(BF16)
                p_cur, p_prev, p, p_sink = _band_probs_by_head(qs, k_band, sink_ref, (2 * pair + a) * group, group,
                                                               causal, blk)
                do = _stack_heads(d_attn, first, group).astype(BF16)
                dp_band = _dot_nt(do, v_band)
                ds_parts, delta = [], []
                for h in range(group):
                    rows = slice(h * BLOCK, (h + 1) * BLOCK)
                    dp = jnp.where(causal, dp_band[rows, :BLOCK], dp_band[rows, BLOCK:])
                    delta.append(jnp.sum(p[rows] * dp, axis=-1, keepdims=True))
                    ds_parts.append(_split_band(p[rows] * (dp - delta[-1]), causal))
                ds_cur = jnp.concatenate([c for c, _ in ds_parts], axis=0)
                ds_prev = jnp.concatenate([v for _, v in ds_parts], axis=0)
                delta = jnp.concatenate(delta, axis=0)
                dqs = (jnp.dot(ds_cur, kc, preferred_element_type=F32)
                       + jnp.dot(ds_prev, kp, preferred_element_type=F32)) * ATTN_SCALE
                dq_parts += [dqs[h * BLOCK:(h + 1) * BLOCK] for h in range(group)]
                rows = lambda t: jnp.concatenate(
                    [t[(first + h) * HEAD_DIM:(first + h + 1) * HEAD_DIM] for h in range(group)], axis=1).astype(BF16)
                qs_t, do_t = rows(q_t), rows(d_attn_t)
                dk_t["cur"].append(jnp.dot(qs_t, ds_cur, preferred_element_type=F32))
                dk_t["prev"].append(jnp.dot(qs_t, ds_prev, preferred_element_type=F32))
                dv_t["cur"].append(jnp.dot(do_t, p_cur, preferred_element_type=F32))
                dv_t["prev"].append(jnp.dot(do_t, p_prev, preferred_element_type=F32))
                ds_sink = -(p_sink * delta)
                for h in range(group):
                    dsink = dsink + jnp.where(lane == first + h, ds_sink[h * BLOCK:(h + 1) * BLOCK], 0.0)
            sink_acc[...] += dsink
            dq_ref[...] = _rope_bwd(jnp.concatenate(dq_parts, axis=1), tab_c).astype(BF16)
            pair_block = lambda parts: jnp.concatenate(parts, axis=0).T
            dk_ref[...] = (carry_k[...] + _rope_bwd(pair_block(dk_t["prev"]), tab_p)).astype(BF16)
            dv_ref[...] = (carry_v[...] + pair_block(dv_t["prev"])).astype(BF16)
            carry_k[...] = _rope_bwd(pair_block(dk_t["cur"]), tab_c)
            carry_v[...] = pair_block(dv_t["cur"])

        @pl.when(blk == nb)
        def _():
            dk_ref[...] = carry_k[...].astype(BF16)
            dv_ref[...] = carry_v[...].astype(BF16)
            dsink_ref[0] = jnp.sum(sink_acc[...], axis=0, keepdims=True)

    v0, g0 = (d + kv) // LANES, (d + 2 * kv) // qw
    cur = lambda i: jnp.minimum(i, nb - 1)
    prev = lambda i: jnp.maximum(cur(i) - 1, 0)
    back = lambda i: jnp.maximum(i - 1, 0)
    q_spec = pl.BlockSpec((BLOCK, qw), lambda p, i: (cur(i), p))
    in_specs = [
        SMEM_SPEC,
        q_spec,
        pl.BlockSpec((BLOCK, LANES), lambda p, i: (cur(i), p)),
        pl.BlockSpec((BLOCK, LANES), lambda p, i: (prev(i), p)),
        pl.BlockSpec((BLOCK, LANES), lambda p, i: (cur(i), v0 + p)),
        pl.BlockSpec((BLOCK, LANES), lambda p, i: (prev(i), v0 + p)),
        pl.BlockSpec((BLOCK, qw), lambda p, i: (cur(i), g0 + p)),
        pl.BlockSpec((3, BLOCK, LANES), lambda p, i: (0, cur(i), 0)),
        pl.BlockSpec((3, BLOCK, LANES), lambda p, i: (0, prev(i), 0)),
        q_spec,
        q_spec,
        ANY_SPEC,
        ANY_SPEC,
    ]
    kv_out = pl.BlockSpec((BLOCK, LANES), lambda p, i: (back(i), p))
    gate_out = pl.BlockSpec((BLOCK, qw), lambda p, i: (cur(i), g0 + p))
    return pl.pallas_call(
        body, name="attention_bwd", grid=(n_pairs, nb + 1), in_specs=in_specs,
        out_specs=(q_spec, kv_out, kv_out, gate_out, pl.BlockSpec((1, 1, LANES), lambda p, i: (p, 0, 0))),
        out_shape=(_sds((s, d), BF16), _sds((s, kv), BF16), _sds((s, kv), BF16), _sds(d_proj.shape, BF16),
                   _sds((n_pairs, 1, LANES), F32)),
        input_output_aliases={len(in_specs) - 1: 3},
        scratch_shapes=[pltpu.VMEM((BLOCK, LANES), F32), pltpu.VMEM((BLOCK, LANES), F32),
                        pltpu.VMEM((BLOCK, LANES), F32)],
        compiler_params=_params("parallel", "arbitrary"),
    )(sink, q_rot, k_rot, k_rot, proj, proj, proj, tables, tables, attn, d_ain, after, d_proj)


def _gmlp_core(gu, gv, ln_g, ln_b, w_ref, bias_t):
    xc = gv - jnp.mean(gv, axis=-1, keepdims=True)
    rstd = lax.rsqrt(jnp.mean(xc * xc, axis=-1, keepdims=True) + LN_EPS)
    xhat = xc * rstd
    vn = (xhat * ln_g + ln_b).astype(BF16)
    gd = gu.shape[1] // GMLP_GROUPS
    tri = (lax.broadcasted_iota(jnp.int32, (BLOCK, BLOCK), 0) >= lax.broadcasted_iota(jnp.int32, (BLOCK, BLOCK), 1))
    w_tri = [jnp.where(tri, w_ref[g], 0.0).astype(BF16) for g in range(GMLP_GROUPS)]
    mixed = jnp.concatenate(
        [jnp.dot(w_tri[g], vn[:, g * gd:(g + 1) * gd], preferred_element_type=F32) + bias_t[:, g:g + 1]
         for g in range(GMLP_GROUPS)], axis=1)
    return gu, xhat, rstd, vn, w_tri, tri, mixed


def _whole(shape):
    return pl.BlockSpec(shape, lambda i: tuple(0 for _ in shape))


def _gmlp_fwd(proj, col_u, d, w_s, bias_t, ln_g, ln_b, after):
    s = proj.shape[0]
    ns = SEGMENT_SPLIT

    def body(*refs):
        u, vg, gb = _cat(refs[:ns]), _cat(refs[ns:2 * ns]), _cat(refs[2 * ns:3 * ns])
        w_ref, bt_ref, lg_ref, lb_ref, after_ref, o_ref = refs[3 * ns:]
        gu, _, _, _, _, _, mixed = _gmlp_core(_gelu(u), _gelu(vg), lg_ref[...], lb_ref[...], w_ref, bt_ref[...])
        silu, _ = _silu_and_grad(gb)
        o_ref[...] = ((gu * mixed) * silu).astype(BF16)

    segs = [sp for j in range(3) for sp in _segment_specs(BLOCK, d, col_u + j * d)]
    return pl.pallas_call(
        body, name="gmlp_fwd", grid=(s // BLOCK,),
        in_specs=[*segs, _whole(w_s.shape), _whole(bias_t.shape), _whole((1, d)), _whole((1, d)), ANY_SPEC],
        out_specs=pl.BlockSpec((BLOCK, d), lambda i: (i, 0)), out_shape=_sds((s, d), BF16),
        compiler_params=_params("parallel"),
    )(*([proj] * (3 * ns)), w_s, bias_t, ln_g, ln_b, after)


def _gmlp_bwd(proj, col_u, d, w_s, bias_t, ln_g, ln_b, d_bin, after, d_proj):
    s = proj.shape[0]
    gd = d // GMLP_GROUPS
    ns = SEGMENT_SPLIT
    n_steps = s // BLOCK

    def body(*refs):
        u, vg, gb = _cat(refs[:ns]), _cat(refs[ns:2 * ns]), _cat(refs[2 * ns:3 * ns])
        (w_ref, bt_ref, lg_ref, lb_ref, dbin_ref, after_ref, wide_in, wide_ref, dw_ref, dbt_ref, dlg_ref, dlb_ref,
         buf, sems) = refs[3 * ns:]

        @pl.when(pl.program_id(0) == 0)
        def _():
            dw_ref[...] = jnp.zeros_like(dw_ref)
            dbt_ref[...] = jnp.zeros_like(dbt_ref)
            dlg_ref[...] = jnp.zeros_like(dlg_ref)
            dlb_ref[...] = jnp.zeros_like(dlb_ref)

        ln_g = lg_ref[...]
        (gu, gu_grad), (gv, gv_grad) = _gelu_and_grad(u), _gelu_and_grad(vg)
        gu, xhat, rstd, vn, w_tri, tri, mixed = _gmlp_core(gu, gv, ln_g, lb_ref[...], w_ref, bt_ref[...])
        silu, silu_grad = _silu_and_grad(gb)
        d_bin_v = dbin_ref[...].astype(F32)
        d_sg = d_bin_v * silu
        d_gate = (d_bin_v * (gu * mixed) * silu_grad).astype(BF16)
        d_u = (d_sg * mixed * gu_grad).astype(BF16)
        d_mixed = d_sg * gu
        d_mixed_b = d_mixed.astype(BF16)
        d_vn, d_bias = [], []
        for g in range(GMLP_GROUPS):
            dm_g = d_mixed_b[:, g * gd:(g + 1) * gd]
            d_bias.append(jnp.sum(d_mixed[:, g * gd:(g + 1) * gd], axis=-1, keepdims=True))
            dw = lax.dot_general(dm_g, vn[:, g * gd:(g + 1) * gd], (((1,), (1,)), ((), ())),
                                 preferred_element_type=F32)
            dw_ref[g] += jnp.where(tri, dw, 0.0)
            d_vn.append(lax.dot_general(w_tri[g], dm_g, (((0,), (0,)), ((), ())), preferred_element_type=F32))
        dbt_ref[...] += jnp.concatenate(d_bias, axis=1)
        d_vn = jnp.concatenate(d_vn, axis=1)
        dlg_ref[...] += jnp.sum(d_vn * xhat, axis=0, keepdims=True)
        dlb_ref[...] += jnp.sum(d_vn, axis=0, keepdims=True)
        d_xhat = d_vn * ln_g
        d_gv = rstd * (d_xhat - jnp.mean(d_xhat, axis=-1, keepdims=True)
                       - xhat * jnp.mean(d_xhat * xhat, axis=-1, keepdims=True))
        d_v = (d_gv * gv_grad).astype(BF16)

        def fill(out):
            out[:, :d] = d_u
            out[:, d:2 * d] = d_v
            out[:, 2 * d:] = d_gate

        _write_behind(pl.program_id(0), n_steps, buf, sems, wide_ref, BLOCK, col_u, fill)

    segs = [sp for j in range(3) for sp in _segment_specs(BLOCK, d, col_u + j * d)]
    return pl.pallas_call(
        body, name="gmlp_bwd", grid=(n_steps,),
        in_specs=[*segs, _whole(w_s.shape), _whole(bias_t.shape), _whole((1, d)), _whole((1, d)),
                  pl.BlockSpec((BLOCK, d), lambda i: (i, 0)), ANY_SPEC, ANY_SPEC],
        out_specs=(ANY_SPEC, _whole(w_s.shape), _whole(bias_t.shape), _whole((1, d)), _whole((1, d))),
        out_shape=(_sds(d_proj.shape, BF16), _sds(w_s.shape, F32), _sds(bias_t.shape, F32), _sds((1, d), F32),
                   _sds((1, d), F32)),
        input_output_aliases={3 * ns + 6: 0},
        scratch_shapes=[pltpu.VMEM((2, BLOCK, 3 * d), BF16), pltpu.SemaphoreType.DMA((2,))],
        compiler_params=_params("arbitrary"),
    )(*([proj] * (3 * ns)), w_s, bias_t, ln_g, ln_b, d_bin, after, d_proj)


def _pack(parts):
    rows = []
    tile = SUBLANES * LANES
    for p in parts:
        flat = p.astype(F32).reshape(-1)
        padded = -(-flat.shape[0] // tile) * tile
        rows.append(jnp.pad(flat, (0, padded - flat.shape[0])).reshape(-1, LANES))
    return jnp.concatenate(rows, axis=0)


def _unpack(packed, shapes):
    out, row = [], 0
    tile = SUBLANES * LANES
    for shape in shapes:
        size = math.prod(shape)
        n_rows = -(-size // tile) * SUBLANES
        out.append(packed[row:row + n_rows].reshape(-1)[:size].reshape(shape))
        row += n_rows
    return out


def kernel(x, positions, norm_g, w_in, attn_sink, gmlp_ln_g, gmlp_ln_b, w_spatial, b_spatial, w_up_attn, w_up_gmlp, w_out, final_norm_g, loss_target, m_norm_g, m_w_in, m_attn_sink, m_gmlp_ln_g, m_gmlp_ln_b, m_w_spatial, m_b_spatial, m_w_up_attn, m_w_up_gmlp, m_w_out, m_final_norm_g, v_norm_g, v_w_in, v_attn_sink, v_gmlp_ln_g, v_gmlp_ln_b, v_w_spatial, v_b_spatial, v_w_up_attn, v_w_up_gmlp, v_w_out, v_final_norm_g):
    x2d, target = x[0], loss_target[0]
    s, d = x2d.shape
    n_q_heads = attn_sink.shape[1]
    cw = w_in.shape[2]
    rw = w_up_attn.shape[1]
    kv = (cw * N_DEV - 7 * d) // 2
    col_u, col_m = 2 * d + 2 * kv, 5 * d + 2 * kv
    final_g = final_norm_g.reshape(1, d)
    sink = attn_sink[0]
    w_s = w_spatial[0]
    bias_t = b_spatial[0].T
    mx, my, mc = _mesh_pos()
    pos = jnp.stack([mx, my, mc]).astype(jnp.int32)

    def one_block(fn):
        return jnp.reshape(fn(mx, my, mc), (1,)).astype(jnp.int32)

    w_in_b = _cast_into_slot(w_in[0], pos, "cast_w_in")
    to_sibling = [_Copy(0, _slot, 0, _slot, _sibling)]
    ici = [[_Copy(0, _slot, 0, _slot, lambda x, y, c, chip=chip: (*chip(x, y, c), c))] for chip in _ICI_STAGES[:2]]
    relayed = lambda x, y, c: _slot(*_ICI_STAGES[1](x, y, c), c)
    ici.append([_Copy(0, relayed, 0, relayed, lambda x, y, c: (*_ICI_STAGES[0](x, y, c), c))])
    passes = []
    for chip in _ICI_STAGES:
        landed = lambda x, y, c, chip=chip: _slot(*chip(x, y, c), c)
        passes.append([_Copy(0, landed, 0, landed, _sibling)])
    sib_sems = _rdma_start("w_in_sibling_start", [w_in_b], to_sibling)
    ici_sems = _rdma_start("w_in_ici0_start", sib_sems[2], ici[0])
    h = _rmsnorm_fwd(x2d, norm_g, ici_sems[3])
    proj = _project(h, ici_sems[2][0], one_block(_slot), "projection_own")
    w_blocks = _rdma_wait("w_in_sibling_wait", ici_sems[2], sib_sems[0], sib_sems[1], to_sibling, proj)
    proj = _project(h, w_blocks[0], one_block(lambda x, y, c: _slot(x, y, 1 - c)), "projection_sibling", proj=proj)
    w_blocks = _rdma_wait("w_in_ici0_wait", w_blocks, ici_sems[0], ici_sems[1], ici[0], proj)
    first = _gather_first_copies(3)
    for k, chip in enumerate(_ICI_STAGES):
        if k + 1 < len(_ICI_STAGES):
            ici_sems = _rdma_start("w_in_ici%d_start" % (k + 1), w_blocks, ici[k + 1])
            w_blocks = ici_sems[2]
        else:
            send1, recv1, thru, _ = _rdma_start("gather_squares_start", squares + w_blocks, first)
            squares, w_blocks = thru[:3], thru[3:]
        pass_sems = _rdma_start("w_in_pass%d_start" % k, w_blocks, passes[k])
        if k == 0:
            squares = [_cast_into_slot(w[0], pos, "cast_" + nm, after=pass_sems[3])
                       for nm, w in (("w_up_attn", w_up_attn), ("w_up_gmlp", w_up_gmlp), ("w_out", w_out))]
        proj = _project(h, pass_sems[2][0], one_block(lambda x, y, c, chip=chip: _slot(*chip(x, y, c), c)),
                        "projection_ici%d" % k, proj=proj, after=pass_sems[3])
        w_blocks = _rdma_wait("w_in_pass%d_wait" % k, pass_sems[2], pass_sems[0], pass_sems[1], passes[k], proj)
        proj = _project(h, w_blocks[0], one_block(lambda x, y, c, chip=chip: _slot(*chip(x, y, 1 - c), 1 - c)),
                        "projection_pass%d" % k, proj=proj)
        if k + 1 < len(_ICI_STAGES):
            w_blocks = _rdma_wait("w_in_ici%d_wait" % (k + 1), w_blocks, ici_sems[0], ici_sems[1], ici[k + 1], proj)
    w_in_b = w_blocks[0]

    tables = _rope_tables(positions[0])
    attn, a_in, q_rot, k_rot = _attention_fwd(proj, tables, sink, kv, proj)
    squares = _rdma_wait("gather_squares_wait", squares, send1, recv1, first, attn)
    passed = _gather_pass_copies(3)
    send2, recv2, squares, token = _rdma_start("pass_squares_start", squares, passed)
    b_in = _gmlp_fwd(proj, col_u, d, w_s, bias_t, gmlp_ln_g, gmlp_ln_b, token)
    squares = _rdma_wait("pass_squares_wait", squares, send2, recv2, passed, b_in)
    w_ua, w_ug, w_o = [w.reshape(N_DEV * rw, d) for w in squares]
    y_a = _matmul(a_in, w_ua, "nn", BF16, "up_attn")
    y_b = _matmul(b_in, w_ug, "nn", BF16, "up_gmlp")
    merged = _merge_fwd(y_a, y_b, proj, col_m)
    x_out = _matmul(merged, w_o, "nn", F32, "out_proj", res=x2d, tn=512)
    loss_p, d_final_g, dx2_b = _loss_and_final_norm_bwd(x_out, target, final_g)

    d_merged = _matmul(dx2_b, w_o, "nt", BF16, "d_merged")
    g_w_out = _matmul(merged, dx2_b, "tn", BF16, "g_w_out")
    d_ya, d_yb, d_proj = _merge_bwd(d_merged, y_a, y_b, proj, col_m, lax.empty(proj.shape, BF16))
    d_ain = _matmul(d_ya, w_ua, "nt", BF16, "d_a_in")
    g_w_ua = _matmul(a_in, d_ya, "tn", BF16, "g_w_up_attn")
    d_bin = _matmul(d_yb, w_ug, "nt", BF16, "d_b_in")
    g_w_ug = _matmul(b_in, d_yb, "tn", BF16, "g_w_up_gmlp")
    sq_grads = [g.reshape(N_DEV, rw, d) for g in (g_w_ua, g_w_ug, g_w_out)]
    sq_land = [lax.empty((N_CHIPS, rw, d), BF16) for _ in sq_grads]
    pairs_sq = _pair_copies_strided(3)
    arrays = [a for gl in zip(sq_grads, sq_land) for a in gl]
    send3, recv3, arrays, token = _rdma_start("pair_squares_start", arrays, pairs_sq)
    d_q, d_k, d_v, d_proj, d_sink = _attention_bwd(proj, q_rot, k_rot, tables, sink, kv, attn, d_ain, token, d_proj)
    arrays = _rdma_wait("pair_squares_wait", arrays, send3, recv3, pairs_sq, d_q)
    sq_sums = [_pair_sum(arrays[2 * a], arrays[2 * a + 1], pos, "pair_sum_%d" % a) for a in range(3)]
    sq_land2 = [lax.empty((N_CHIPS - 1, rw, d), BF16) for _ in sq_sums]
    chip_sq = _chip_sum_copies(3)
    arrays = [a for gl in zip(sq_sums, sq_land2) for a in gl]
    send4, recv4, sq_arrays, token = _rdma_start("chip_squares_start", arrays, chip_sq)
    d_proj, d_w_s, d_bias_t, d_ln_g, d_ln_b = _gmlp_bwd(proj, col_u, d, w_s, bias_t, gmlp_ln_g, gmlp_ln_b, d_bin, token,
                                                        d_proj)
    for piece, col0, nm in ((d_q, 0, "d_q"), (d_k, d, "d_k"), (d_v, d + kv, "d_v")):
        d_proj = _place(d_proj, piece, col0, "place_" + nm)

    half = N_CHIPS // 2
    owners = [*(chip(mx, my) for chip in _OTHER_CHIPS), (mx, my)]

    def blocks_of(j, core):
        return jnp.stack([_slot(*owners[q], core) for q in range(j * half, (j + 1) * half)]).astype(jnp.int32)

    pairs_in = [_Copy(0, lambda x, y, c, q=q: q, 1, lambda x, y, c, q=q: q, _sibling) for q in range(half)]
    sent, token = [], None
    for j in range(2):
        g_sib = _grad_w_in_blocks(h, d_proj, blocks_of(j, 1 - mc), cw, "g_w_in_sibling%d" % j, after=token)
        sent.append(_rdma_start("pair_w_in%d_start" % j, [g_sib, lax.empty((half, d, cw), BF16)], pairs_in))
        token = sent[-1][3]
    in_sums, land_in = None, lax.empty((N_CHIPS - 1, d, cw), BF16)
    chip_in = [[_Copy(0, lambda x, y, c, k=k: k, 1, lambda x, y, c, k=k: k,
                      lambda x, y, c, k=k: (*_OTHER_CHIPS[k](x, y), c)) for k in ks] for ks in ((0, 1), (2,))]
    chip_sent = []
    for j in range(2):
        send5, recv5, arrays, _ = sent[j]
        arrays = _rdma_wait("pair_w_in%d_wait" % j, arrays, send5, recv5, pairs_in, token)
        in_sums = _grad_w_in_blocks(h, d_proj, blocks_of(j, mc), cw, "g_w_in_own%d" % j,
                                    slots=N_CHIPS, slot0=j * half, prev=in_sums, init=arrays[1], tm=512)
        chip_sent.append(_rdma_start("chip_w_in%d_start" % j, [in_sums, land_in], chip_in[j]))
        (in_sums, land_in), token = chip_sent[-1][2], chip_sent[-1][3]
    d_h = _d_hidden(d_proj, w_in_b, after=token)
    grad_x, d_norm_g = _input_grad(d_h, x2d, norm_g, dx2_b)

    sq_arrays = _rdma_wait("chip_squares_wait", sq_arrays, send4, recv4, chip_sq, grad_x)
    big = {}
    for a, (name, w, m, v) in enumerate((("w_up_attn", w_up_attn, m_w_up_attn, v_w_up_attn),
                                         ("w_up_gmlp", w_up_gmlp, m_w_up_gmlp, v_w_up_gmlp),
                                         ("w_out", w_out, m_w_out, v_w_out))):
        big[name] = [r[None] for r in _reduce_adamw(sq_arrays[2 * a], sq_arrays[2 * a + 1], w[0], m[0], v[0], pos,
                                                    "adamw_" + name)]

    heads_per_pair = 2 * n_q_heads // (kv // HEAD_DIM)
    g_sink = d_sink[:, 0, :heads_per_pair].reshape(1, n_q_heads)
    small_w = [norm_g, attn_sink, gmlp_ln_g, gmlp_ln_b, w_spatial, b_spatial, final_norm_g]
    small_m = [m_norm_g, m_attn_sink, m_gmlp_ln_g, m_gmlp_ln_b, m_w_spatial, m_b_spatial, m_final_norm_g]
    small_v = [v_norm_g, v_attn_sink, v_gmlp_ln_g, v_gmlp_ln_b, v_w_spatial, v_b_spatial, v_final_norm_g]
    small_g = [d_norm_g, g_sink, d_ln_g, d_ln_b, d_w_s[None], d_bias_t.T[None], d_final_g.reshape(d)]
    loss_pad = jnp.zeros((1,), F32)
    shapes = [w.shape for w in small_w] + [(1,)]
    packed = _small_allreduce_adamw(_pack(small_g + [loss_p[0, :1]]), _pack(small_w + [loss_pad]),
                                    _pack(small_m + [loss_pad]), _pack(small_v + [loss_pad]))
    sg, sd, sm, sv = [_unpack(p, shapes) for p in packed]
    loss = sg[-1][0]
    in_arrays = [in_sums, land_in]
    for j in range(2):
        in_arrays = _rdma_wait("chip_w_in%d_wait" % j, in_arrays, chip_sent[j][0], chip_sent[j][1], chip_in[j],
                               packed[0])
    big["w_in"] = [r[None] for r in _reduce_adamw(in_arrays[0], in_arrays[1], w_in[0], m_w_in[0], v_w_in[0], pos,
                                                  "adamw_w_in", own_slot=N_CHIPS - 1)]

    names = ["norm_g", "w_in", "attn_sink", "gmlp_ln_g", "gmlp_ln_b", "w_spatial", "b_spatial", "w_up_attn",
             "w_up_gmlp", "w_out", "final_norm_g"]
    small_names = ["norm_g", "attn_sink", "gmlp_ln_g", "gmlp_ln_b", "w_spatial", "b_spatial", "final_norm_g"]
    outs = [[], [], [], []]
    for nm in names:
        for k in range(4):
            if nm in big:
                outs[k].append(big[nm][k])
            else:
                outs[k].append((sg, sd, sm, sv)[k][small_names.index(nm)])
    return (loss, grad_x[None], *outs[0], *outs[1], *outs[2], *outs[3])
```

```python
import math
from typing import Callable, NamedTuple

import jax
import jax.numpy as jnp
from jax import lax
from jax.experimental import pallas as pl
from jax.experimental.pallas import tpu as pltpu

F32 = jnp.float32
BF16 = jnp.bfloat16
MESH = pl.DeviceIdType.MESH

N_DEV = 8
N_CHIPS = 4
HEAD_DIM = 64
BLOCK = 128
ROPE_DIM = 16
ROPE_HALF = ROPE_DIM // 2
ROPE_THETA = 500000.0
GMLP_GROUPS = 8
NORM_EPS = 1e-5
LN_EPS = 1e-5
ATTN_SCALE = HEAD_DIM ** -0.5
LANES = 128
SUBLANES = 8
VMEM_LIMIT = 48 * 1024 * 1024
VMEM_LIMIT_WIDE = 56 * 1024 * 1024
DOT_COLS = 1024
SEGMENT_SPLIT = 4

ADAM_LR = 0.001
ADAM_B1 = 0.9
ADAM_B2 = 0.999
ADAM_EPS = 1e-08
ADAM_WD = 0.01
ADAM_STEP = 10

GELU_C = math.sqrt(2.0 / math.pi)
GELU_K = 0.044715

HBM_SPEC = pl.BlockSpec(memory_space=pltpu.HBM)
ANY_SPEC = pl.BlockSpec(memory_space=pl.ANY)
SEM_SPEC = pl.BlockSpec(memory_space=pltpu.SEMAPHORE)
VMEM_SPEC = pl.BlockSpec(memory_space=pltpu.VMEM)
SMEM_SPEC = pl.BlockSpec(memory_space=pltpu.SMEM)


def _sds(shape, dtype):
    return jax.ShapeDtypeStruct(shape, dtype)


def _params(*sem, vmem=VMEM_LIMIT):
    return pltpu.CompilerParams(dimension_semantics=sem or None, vmem_limit_bytes=vmem)


def _gelu(x):
    return 0.5 * x * (1.0 + jnp.tanh(GELU_C * (x + GELU_K * x * x * x)))


def _gelu_and_grad(x):
    x2 = x * x
    t = jnp.tanh(x * (GELU_C + (GELU_C * GELU_K) * x2))
    half = 0.5 + 0.5 * t
    return x * half, half + (x * (1.0 - t * t)) * (0.5 * GELU_C + (1.5 * GELU_C * GELU_K) * x2)


def _silu_and_grad(x):
    s = jax.nn.sigmoid(x)
    return x * s, s * (1.0 + x * (1.0 - s))


def _adamw(w, g, m, v):
    m = ADAM_B1 * m + (1.0 - ADAM_B1) * g
    v = ADAM_B2 * v + (1.0 - ADAM_B2) * (g * g)
    m_hat = m / (1.0 - ADAM_B1 ** ADAM_STEP)
    v_hat = v / (1.0 - ADAM_B2 ** ADAM_STEP)
    delta = -ADAM_LR * (m_hat / (jnp.sqrt(v_hat) + ADAM_EPS) + ADAM_WD * w)
    return delta, m, v


def _mesh_pos():
    return lax.axis_index("x"), lax.axis_index("y"), lax.axis_index("c")


def _slot(x, y, c):
    return 4 * x + 2 * y + c


def _chip(x, y):
    return 2 * x + y


def _sibling(x, y, c):
    return (x, y, 1 - c)


_OTHER_CHIPS = (lambda x, y: (1 - x, y), lambda x, y: (x, 1 - y), lambda x, y: (1 - x, 1 - y))
_ICI_STAGES = (lambda x, y, c: (x ^ c, y ^ (1 - c)), lambda x, y, c: (x ^ (1 - c), y ^ c),
               lambda x, y, c: (1 - x, 1 - y))


class _Copy(NamedTuple):
    src: int
    src_slot: Callable
    dst: int
    dst_slot: Callable
    peer: Callable


def _descriptor(refs, send_sems, recv_sems, k, cp):
    pos = _mesh_pos()
    return pltpu.make_async_remote_copy(
        src_ref=refs[cp.src].at[cp.src_slot(*pos)], dst_ref=refs[cp.dst].at[cp.dst_slot(*pos)],
        send_sem=send_sems.at[k], recv_sem=recv_sems.at[k], device_id=cp.peer(*pos), device_id_type=MESH)


def _gather_first_copies(n_arrays):
    copies = []
    for a in range(n_arrays):
        copies.append(_Copy(a, _slot, a, _slot, _sibling))
        for chip in _OTHER_CHIPS:
            copies.append(_Copy(a, _slot, a, _slot, lambda x, y, c, chip=chip: (*chip(x, y), c)))
    return copies


def _gather_pass_copies(n_arrays):
    copies = []
    for a in range(n_arrays):
        for chip in _OTHER_CHIPS:
            src = lambda x, y, c, chip=chip: _slot(*chip(x, y), c)
            copies.append(_Copy(a, src, a, src, _sibling))
    return copies


def _pair_copies_strided(n_sets):
    copies = []
    for a in range(n_sets):
        for q in range(N_CHIPS):
            copies.append(_Copy(2 * a, lambda x, y, c, q=q: 2 * q + 1 - c, 2 * a + 1, lambda x, y, c, q=q: q, _sibling))
    return copies


def _chip_sum_copies(n_sets):
    copies = []
    for a in range(n_sets):
        for k, chip in enumerate(_OTHER_CHIPS):
            copies.append(_Copy(2 * a, lambda x, y, c, chip=chip: _chip(*chip(x, y)), 2 * a + 1,
                                lambda x, y, c, k=k: k, lambda x, y, c, chip=chip: (*chip(x, y), c)))
    return copies


def _rdma_start(name, arrays, copies):
    n, nc = len(arrays), len(copies)

    def body(*refs):
        in_refs = refs[:n]
        send_sems, recv_sems = refs[n], refs[n + 1]
        token = refs[2 * n + 2]
        for k, cp in enumerate(copies):
            _descriptor(in_refs, send_sems, recv_sems, k, cp).start()
        token[...] = jnp.zeros_like(token)

    out = pl.pallas_call(
        body, name=name,
        out_shape=(pltpu.SemaphoreType.DMA((nc,)), pltpu.SemaphoreType.DMA((nc,)),
                   *[pltpu.HBM(a.shape, a.dtype) for a in arrays], _sds((SUBLANES, LANES), F32)),
        in_specs=[HBM_SPEC] * n, out_specs=(SEM_SPEC, SEM_SPEC, *([HBM_SPEC] * n), VMEM_SPEC),
        input_output_aliases={i: i + 2 for i in range(n)},
        compiler_params=pltpu.CompilerParams(has_side_effects=pltpu.SideEffectType.DATAFLOW_SIDE_EFFECTING),
    )(*[pltpu.with_memory_space_constraint(a, pltpu.HBM) for a in arrays])
    return out[0], out[1], list(out[2:2 + n]), out[2 + n]


def _rdma_wait(name, arrays, send_sems, recv_sems, copies, after):
    n = len(arrays)

    def body(*refs):
        in_refs = refs[:n]
        send_ref, recv_ref = refs[n], refs[n + 1]
        for k, cp in enumerate(copies):
            d = _descriptor(in_refs, send_ref, recv_ref, k, cp)
            d.wait_send()
            d.wait_recv()

    out = pl.pallas_call(
        body, name=name, out_shape=tuple(pltpu.HBM(a.shape, a.dtype) for a in arrays),
        in_specs=[HBM_SPEC] * n + [SEM_SPEC, SEM_SPEC, ANY_SPEC], out_specs=tuple([HBM_SPEC] * n),
        input_output_aliases={i: i for i in range(n)},
        compiler_params=pltpu.CompilerParams(has_side_effects=pltpu.SideEffectType.DATAFLOW_SIDE_EFFECTING),
    )(*arrays, send_sems, recv_sems, after)
    return list(out)


def _cast_into_slot(w, pos, name, after=None):
    rows, cols = w.shape
    tr = min(rows, 256)

    def body(pos_ref, w_ref, *rest):
        rest[-1][...] = w_ref[...].astype(BF16)

    in_specs, args = [pl.BlockSpec((tr, cols), lambda i, p: (i, 0))], [pos, w]
    if after is not None:
        in_specs.append(ANY_SPEC)
        args.append(after)
    return pl.pallas_call(
        body, name=name,
        grid_spec=pltpu.PrefetchScalarGridSpec(
            num_scalar_prefetch=1, grid=(rows // tr,), in_specs=in_specs,
            out_specs=pl.BlockSpec((None, tr, cols), lambda i, p: (_slot(p[0], p[1], p[2]), i, 0))),
        out_shape=_sds((N_DEV, rows, cols), BF16), compiler_params=_params("parallel"),
    )(*args)


def _pair_sum(g, land, pos, name):
    _, rows, cols = land.shape
    tr = min(rows, 128)
    strided = g.shape[0] == N_DEV

    def body(pos_ref, g_ref, l_ref, o_ref):
        o_ref[...] = (g_ref[...].astype(F32) + l_ref[...].astype(F32)).astype(BF16)

    g_map = (lambda q, i, p: (2 * q + p[2], i, 0)) if strided else (lambda q, i, p: (q, i, 0))
    blk = pl.BlockSpec((None, tr, cols), lambda q, i, p: (q, i, 0))
    return pl.pallas_call(
        body, name=name,
        grid_spec=pltpu.PrefetchScalarGridSpec(
            num_scalar_prefetch=1, grid=(N_CHIPS, rows // tr),
            in_specs=[pl.BlockSpec((None, tr, cols), g_map), blk], out_specs=blk),
        out_shape=_sds((N_CHIPS, rows, cols), BF16), compiler_params=_params("parallel", "parallel"),
    )(pos, g, land)


def _reduce_adamw(sums, land, w, m, v, pos, name, own_slot=None):
    rows, cols = w.shape
    tr = min(rows, 64)
    own = (lambda p: _chip(p[0], p[1])) if own_slot is None else (lambda p: own_slot)

    def body(pos_ref, s_ref, l_ref, w_ref, m_ref, v_ref, g_ref, d_ref, nm_ref, nv_ref):
        g = s_ref[...].astype(F32)
        for k in range(N_CHIPS - 1):
            g = g + l_ref[k].astype(F32)
        delta, nm, nv = _adamw(w_ref[...], g, m_ref[...], v_ref[...])
        g_ref[...] = g
        d_ref[...] = delta
        nm_ref[...] = nm
        nv_ref[...] = nv

    spec = pl.BlockSpec((tr, cols), lambda i, p: (i, 0))
    return pl.pallas_call(
        body, name=name,
        grid_spec=pltpu.PrefetchScalarGridSpec(
            num_scalar_prefetch=1, grid=(rows // tr,),
            in_specs=[pl.BlockSpec((None, tr, cols), lambda i, p: (own(p), i, 0)),
                      pl.BlockSpec((N_CHIPS - 1, tr, cols), lambda i, p: (0, i, 0)), spec, spec, spec],
            out_specs=(spec, spec, spec, spec)),
        out_shape=tuple([_sds((rows, cols), F32)] * 4), compiler_params=_params("parallel"),
    )(pos, sums, land, w, m, v)


def _small_allreduce_adamw(g, w, m, v):
    rows = g.shape[0]

    def body(g_ref, w_ref, m_ref, v_ref, gs_ref, d_ref, nm_ref, nv_ref, all_ref, send_sems, recv_sems):
        x, y, c = _mesh_pos()
        me = _slot(x, y, c)
        copies = []
        for k in range(1, N_DEV):
            peer = (x ^ (k >> 2), y ^ ((k >> 1) & 1), c ^ (k & 1))
            copies.append(pltpu.make_async_remote_copy(
                src_ref=g_ref, dst_ref=all_ref.at[me], send_sem=send_sems.at[k - 1],
                recv_sem=recv_sems.at[k - 1], device_id=peer, device_id_type=MESH))
        for cp in copies:
            cp.start()
        all_ref[me] = g_ref[...]
        for cp in copies:
            cp.wait_recv()
        total = all_ref[0]
        for s in range(1, N_DEV):
            total = total + all_ref[s]
        delta, nm, nv = _adamw(w_ref[...], total, m_ref[...], v_ref[...])
        gs_ref[...] = total
        d_ref[...] = delta
        nm_ref[...] = nm
        nv_ref[...] = nv
        for cp in copies:
            cp.wait_send()

    return pl.pallas_call(
        body, name="small_allreduce_adamw", out_shape=tuple([_sds((rows, LANES), F32)] * 4),
        in_specs=[VMEM_SPEC] * 4, out_specs=tuple([VMEM_SPEC] * 4),
        scratch_shapes=[pltpu.VMEM((N_DEV, rows, LANES), F32), pltpu.SemaphoreType.DMA((7,)),
                        pltpu.SemaphoreType.DMA((7,))],
    )(g, w, m, v)


_DOT_DIMS = {"nn": ((1,), (0,)), "nt": ((1,), (1,)), "tn": ((0,), (0,))}


def _dot(a, b, mode):
    return lax.dot_general(a, b, (_DOT_DIMS[mode], ((), ())), preferred_element_type=F32)


def _col_chunks(cols):
    return [(c0, min(c0 + DOT_COLS, cols)) for c0 in range(0, cols, DOT_COLS)]


def _matmul(a, b, mode, out_dtype, name, *, res=None, tm=1024, tn=1024):
    if mode == "tn":
        kdim, mdim = a.shape
    else:
        mdim, kdim = a.shape
    ndim = b.shape[0] if mode == "nt" else b.shape[1]
    tm, tn = min(tm, mdim), min(tn, ndim)
    assert mdim % tm == 0 and ndim % tn == 0, (name, mdim, ndim)

    def body(*refs):
        out = _dot(refs[0][...], refs[1][...], mode)
        if res is not None:
            out = out + refs[2][...]
        refs[-1][...] = out.astype(out_dtype)

    a_spec = pl.BlockSpec((kdim, tm), lambda i, j: (0, i)) if mode == "tn" else pl.BlockSpec((tm, kdim), lambda i, j: (i, 0))
    b_spec = pl.BlockSpec((tn, kdim), lambda i, j: (j, 0)) if mode == "nt" else pl.BlockSpec((kdim, tn), lambda i, j: (0, j))
    o_spec = pl.BlockSpec((tm, tn), lambda i, j: (i, j))
    in_specs, args = [a_spec, b_spec], [a, b]
    if res is not None:
        in_specs.append(o_spec)
        args.append(res)
    return pl.pallas_call(
        body, name=name, grid=(mdim // tm, ndim // tn), in_specs=in_specs, out_specs=o_spec,
        out_shape=_sds((mdim, ndim), out_dtype), compiler_params=_params("parallel", "parallel"),
    )(*args)


def _project(h, w_blocks, block_ids, name, *, proj=None, after=None, tm=1024, tk=1024):
    s, d = h.shape
    _, _, cw = w_blocks.shape
    n = block_ids.shape[0]
    tm, tk = min(tm, s), min(tk, d)
    nk = d // tk

    def body(ids_ref, h_ref, w_ref, *rest):
        o_ref, acc_ref = rest[-2], rest[-1]
        k = pl.program_id(2)

        @pl.when(k == 0)
        def _():
            acc_ref[...] = jnp.zeros_like(acc_ref)

        for c0, c1 in _col_chunks(cw):
            acc_ref[:, c0:c1] += _dot(h_ref[...], w_ref[:, c0:c1], "nn")

        @pl.when(k == nk - 1)
        def _():
            o_ref[...] = acc_ref[...].astype(BF16)

    in_specs = [pl.BlockSpec((tm, tk), lambda j, i, k, ids: (i, k)),
                pl.BlockSpec((None, tk, cw), lambda j, i, k, ids: (ids[j], k, 0))]
    args = [block_ids, h, w_blocks]
    aliases = {}
    if proj is not None:
        in_specs.append(ANY_SPEC)
        args.append(proj)
        aliases = {3: 0}
    if after is not None:
        in_specs.append(ANY_SPEC)
        args.append(after)
    return pl.pallas_call(
        body, name=name,
        grid_spec=pltpu.PrefetchScalarGridSpec(
            num_scalar_prefetch=1, grid=(n, s // tm, d // tk), in_specs=in_specs,
            out_specs=pl.BlockSpec((tm, cw), lambda j, i, k, ids: (i, ids[j])),
            scratch_shapes=[pltpu.VMEM((tm, cw), F32)]),
        out_shape=_sds((s, N_DEV * cw), BF16), input_output_aliases=aliases,
        compiler_params=_params("arbitrary", "arbitrary", "arbitrary", vmem=VMEM_LIMIT_WIDE),
    )(*args)


def _grad_w_in_blocks(h, d_proj, block_ids, cw, name, *, slots=None, slot0=0, prev=None, init=None, after=None,
                      tm=1024, tk=1024):
    s, d = h.shape
    n = block_ids.shape[0]
    slots = n if slots is None else slots
    tm, tk = min(tm, d), min(tk, s)
    nk = s // tk

    def body(ids_ref, h_ref, g_ref, *rest):
        o_ref, acc_ref = rest[-2], rest[-1]
        k = pl.program_id(2)

        @pl.when(k == 0)
        def _():
            acc_ref[...] = jnp.zeros_like(acc_ref) if init is None else rest[0][...].astype(F32)

        for c0, c1 in _col_chunks(cw):
            acc_ref[:, c0:c1] += _dot(h_ref[...], g_ref[:, c0:c1], "tn")

        @pl.when(k == nk - 1)
        def _():
            o_ref[...] = acc_ref[...].astype(BF16)

    in_specs = [pl.BlockSpec((tk, tm), lambda q, i, k, ids: (k, i)),
                pl.BlockSpec((tk, cw), lambda q, i, k, ids: (k, ids[q]))]
    args = [block_ids, h, d_proj]
    aliases = {}
    if init is not None:
        in_specs.append(pl.BlockSpec((None, tm, cw), lambda q, i, k, ids: (q, i, 0)))
        args.append(init)
    if prev is not None:
        aliases = {len(args): 0}
        in_specs.append(ANY_SPEC)
        args.append(prev)
    if after is not None:
        in_specs.append(ANY_SPEC)
        args.append(after)
    return pl.pallas_call(
        body, name=name,
        grid_spec=pltpu.PrefetchScalarGridSpec(
            num_scalar_prefetch=1, grid=(n, d // tm, nk), in_specs=in_specs,
            out_specs=pl.BlockSpec((None, tm, cw), lambda q, i, k, ids: (slot0 + q, i, 0)),
            scratch_shapes=[pltpu.VMEM((tm, cw), F32)]),
        out_shape=_sds((slots, d, cw), BF16), input_output_aliases=aliases,
        compiler_params=_params("parallel", "parallel", "arbitrary", vmem=VMEM_LIMIT_WIDE),
    )(*args)


def _d_hidden(d_proj, w_blocks, after=None, *, tm=1024, tn=1024):
    s = d_proj.shape[0]
    nb, d, cw = w_blocks.shape
    tm, tn = min(tm, s), min(tn, d)

    def body(g_ref, w_ref, *rest):
        o_ref = rest[-1]
        k = pl.program_id(2)

        @pl.when(k == 0)
        def _():
            o_ref[...] = jnp.zeros_like(o_ref)

        o_ref[...] += _dot(g_ref[...], w_ref[...], "nt")

    in_specs = [pl.BlockSpec((tm, cw), lambda i, j, k: (i, k)),
                pl.BlockSpec((None, tn, cw), lambda i, j, k: (k, j, 0))]
    args = [d_proj, w_blocks]
    if after is not None:
        in_specs.append(ANY_SPEC)
        args.append(after)
    return pl.pallas_call(
        body, name="d_h", grid=(s // tm, d // tn, nb), in_specs=in_specs,
        out_specs=pl.BlockSpec((tm, tn), lambda i, j, k: (i, j)), out_shape=_sds((s, d), F32),
        compiler_params=_params("parallel", "parallel", "arbitrary", vmem=VMEM_LIMIT_WIDE),
    )(*args)


def _row_tile(rows):
    return min(rows, 128)


def _segment_specs(rows, d, col0):
    w = d // SEGMENT_SPLIT
    assert col0 % w == 0
    return [pl.BlockSpec((rows, w), lambda i, t=t: (i, col0 // w + t)) for t in range(SEGMENT_SPLIT)]


def _cat(refs):
    return jnp.concatenate([r[...].astype(F32) for r in refs], axis=1)


def _rmsnorm_fwd(x, g, after):
    s, d = x.shape
    tr = min(s, 2 * _row_tile(s))

    def body(x_ref, g_ref, after_ref, h_ref):
        xv = x_ref[...]
        r = lax.rsqrt(jnp.mean(xv * xv, axis=-1, keepdims=True) + NORM_EPS)
        h_ref[...] = (xv * r * g_ref[...]).astype(BF16)

    row = pl.BlockSpec((tr, d), lambda i: (i, 0))
    vec = pl.BlockSpec((1, d), lambda i: (0, 0))
    return pl.pallas_call(body, name="rmsnorm_fwd", grid=(s // tr,), in_specs=[row, vec, ANY_SPEC], out_specs=row,
                          out_shape=_sds((s, d), BF16), compiler_params=_params("parallel"))(x, g, after)


def _merge_fwd(y_a, y_b, proj, col_m):
    s, d = y_a.shape
    tr = _row_tile(s)
    ns = SEGMENT_SPLIT

    def body(ya_ref, yb_ref, *rest):
        ma, mb, o_ref = _cat(rest[:ns]), _cat(rest[ns:2 * ns]), rest[2 * ns]
        o_ref[...] = (jax.nn.sigmoid(ma) * ya_ref[...].astype(F32)
                      + jax.nn.sigmoid(mb) * yb_ref[...].astype(F32)).astype(BF16)

    row = pl.BlockSpec((tr, d), lambda i: (i, 0))
    return pl.pallas_call(
        body, name="merge_fwd", grid=(s // tr,),
        in_specs=[row, row, *_segment_specs(tr, d, col_m), *_segment_specs(tr, d, col_m + d)],
        out_specs=row, out_shape=_sds((s, d), BF16), compiler_params=_params("parallel"),
    )(y_a, y_b, *([proj] * (2 * ns)))


def _loss_and_final_norm_bwd(x2, target, g):
    s, d = x2.shape
    tr = min(s, 2 * _row_tile(s))

    def body(x_ref, t_ref, g_ref, loss_ref, dg_ref, dxb_ref):
        @pl.when(pl.program_id(0) == 0)
        def _():
            loss_ref[...] = jnp.zeros_like(loss_ref)
            dg_ref[...] = jnp.zeros_like(dg_ref)

        xv, gv = x_ref[...], g_ref[...]
        r = lax.rsqrt(jnp.mean(xv * xv, axis=-1, keepdims=True) + NORM_EPS)
        xhat = xv * r
        err = xhat * gv - t_ref[...]
        loss_ref[...] += 0.5 * jnp.sum(jnp.mean(err * err, axis=-1, keepdims=True))
        dy = err / d
        dg_ref[...] += jnp.sum(dy * xhat, axis=0, keepdims=True)
        dyg = dy * gv
        dxb_ref[...] = (r * (dyg - xhat * jnp.mean(dyg * xhat, axis=-1, keepdims=True))).astype(BF16)

    row = pl.BlockSpec((tr, d), lambda i: (i, 0))
    vec = pl.BlockSpec((1, d), lambda i: (0, 0))
    return pl.pallas_call(
        body, name="loss_final_norm_bwd", grid=(s // tr,), in_specs=[row, row, vec],
        out_specs=(pl.BlockSpec((SUBLANES, LANES), lambda i: (0, 0)), vec, row),
        out_shape=(_sds((SUBLANES, LANES), F32), _sds((1, d), F32), _sds((s, d), BF16)),
        compiler_params=_params("arbitrary"))(x2, target, g)


def _write_behind(step, n_steps, buf, sems, wide_ref, rows, col0, fill, side=(), side_sems=None):
    cols = buf.shape[2]
    slot = step % 2

    def copies(at_step, at_slot):
        row0 = pl.multiple_of(at_step * rows, rows)
        out = [pltpu.make_async_copy(buf.at[at_slot], wide_ref.at[pl.ds(row0, rows), pl.ds(col0, cols)],
                                     sems.at[at_slot])]
        for j, (src, col) in enumerate(side):
            out.append(pltpu.make_async_copy(src.at[pl.ds(row0, rows), :],
                                             wide_ref.at[pl.ds(row0, rows), pl.ds(col, src.shape[1])],
                                             side_sems.at[at_slot, j]))
        return out

    @pl.when(step >= 2)
    def _():
        for cp in copies(step - 2, slot):
            cp.wait()

    fill(buf.at[slot])
    for cp in copies(step, slot):
        cp.start()

    @pl.when(step == n_steps - 1)
    def _():
        for cp in copies(step, slot):
            cp.wait()
        if n_steps > 1:
            for cp in copies(step - 1, 1 - slot):
                cp.wait()


def _merge_bwd(d_merged, y_a, y_b, proj, col_m, d_proj):
    s, d = y_a.shape
    tr = _row_tile(s)
    ns = SEGMENT_SPLIT
    n_steps = s // tr

    def body(dm_ref, ya_ref, yb_ref, *rest):
        ma, mb = _cat(rest[:ns]), _cat(rest[ns:2 * ns])
        dya_ref, dyb_ref, wide_ref, buf, sems = rest[2 * ns + 1:]
        dm = dm_ref[...].astype(F32)
        sa = jax.nn.sigmoid(ma)
        sb = jax.nn.sigmoid(mb)
        dya_ref[...] = (dm * sa).astype(BF16)
        dyb_ref[...] = (dm * sb).astype(BF16)

        def fill(out):
            out[:, :d] = (dm * ya_ref[...].astype(F32) * (sa * (1.0 - sa))).astype(BF16)
            out[:, d:] = (dm * yb_ref[...].astype(F32) * (sb * (1.0 - sb))).astype(BF16)

        _write_behind(pl.program_id(0), n_steps, buf, sems, wide_ref, tr, col_m, fill)

    row = pl.BlockSpec((tr, d), lambda i: (i, 0))
    n_in = 3 + 2 * ns
    return pl.pallas_call(
        body, name="merge_bwd", grid=(n_steps,),
        in_specs=[row, row, row, *_segment_specs(tr, d, col_m), *_segment_specs(tr, d, col_m + d), ANY_SPEC],
        out_specs=(row, row, ANY_SPEC),
        out_shape=(_sds((s, d), BF16), _sds((s, d), BF16), _sds(d_proj.shape, BF16)),
        input_output_aliases={n_in: 2},
        scratch_shapes=[pltpu.VMEM((2, tr, 2 * d), BF16), pltpu.SemaphoreType.DMA((2,))],
        compiler_params=_params("arbitrary"))(d_merged, y_a, y_b, *([proj] * (2 * ns)), d_proj)


def _input_grad(d_h, x, g, dx2):
    s, d = x.shape
    tr = min(s, 2 * _row_tile(s))

    def body(dh_ref, x_ref, g_ref, dx2_ref, gx_ref, dg_ref):
        @pl.when(pl.program_id(0) == 0)
        def _():
            dg_ref[...] = jnp.zeros_like(dg_ref)

        xv, dh = x_ref[...], dh_ref[...]
        r = lax.rsqrt(jnp.mean(xv * xv, axis=-1, keepdims=True) + NORM_EPS)
        xhat = xv * r
        dg_ref[...] += jnp.sum(dh * xhat, axis=0, keepdims=True)
        dyg = dh * g_ref[...]
        gx_ref[...] = dx2_ref[...].astype(F32) + r * (dyg - xhat * jnp.mean(dyg * xhat, axis=-1, keepdims=True))

    row = pl.BlockSpec((tr, d), lambda i: (i, 0))
    vec = pl.BlockSpec((1, d), lambda i: (0, 0))
    return pl.pallas_call(
        body, name="input_grad", grid=(s // tr,), in_specs=[row, row, vec, row], out_specs=(row, vec),
        out_shape=(_sds((s, d), F32), _sds((1, d), F32)), compiler_params=_params("arbitrary"))(d_h, x, g, dx2)


def _rope_tables(positions):
    inv_freq = ROPE_THETA ** (-jnp.arange(ROPE_HALF, dtype=F32) * 2.0 / ROPE_DIM)
    ang = positions.astype(F32)[:, None] * inv_freq
    cos, sin = jnp.cos(ang), jnp.sin(ang)
    zero = jnp.zeros((positions.shape[0], HEAD_DIM - ROPE_DIM), F32)
    zero_h = jnp.zeros_like(sin)
    c = jnp.concatenate([cos, cos, zero + 1.0], axis=1)
    up = jnp.concatenate([-sin, zero_h, zero], axis=1)
    down = jnp.concatenate([zero_h, sin, zero], axis=1)
    reps = LANES // HEAD_DIM
    return jnp.stack([jnp.tile(c, (1, reps)), jnp.tile(up, (1, reps)), jnp.tile(down, (1, reps))])


def _lane_tiles(x):
    return [x[:, t * LANES:(t + 1) * LANES] for t in range(x.shape[1] // LANES)]


def _rope(x, tab):
    out = [xt * tab[0] + pltpu.roll(xt, LANES - ROPE_HALF, 1) * tab[1] + pltpu.roll(xt, ROPE_HALF, 1) * tab[2]
           for xt in _lane_tiles(x)]
    return out[0] if len(out) == 1 else jnp.concatenate(out, axis=1)


def _rope_bwd(g, tab):
    out = [gt * tab[0] + pltpu.roll(gt * tab[1], ROPE_HALF, 1) + pltpu.roll(gt * tab[2], LANES - ROPE_HALF, 1)
           for gt in _lane_tiles(g)]
    return out[0] if len(out) == 1 else jnp.concatenate(out, axis=1)


def _head(x, h):
    return x[:, h * HEAD_DIM:(h + 1) * HEAD_DIM]


def _stack_heads(x, first, count):
    return jnp.concatenate([_head(x, first + h) for h in range(count)], axis=0)


def _dot_nt(a, b):
    return lax.dot_general(a, b, (((1,), (1,)), ((), ())), preferred_element_type=F32)


def _causal(rows):
    qi = lax.broadcasted_iota(jnp.int32, (rows, BLOCK), 0) % BLOCK
    return lax.broadcasted_iota(jnp.int32, (rows, BLOCK), 1) <= qi


def _band_probs_by_head(qs, k_band, sink_ref, first, count, causal, blk):
    s_band = _dot_nt(qs, k_band)
    p_all, p_band, p_sink = [], [], []
    for h in range(count):
        rows = slice(h * BLOCK, (h + 1) * BLOCK)
        sink = sink_ref[first + h]
        s = jnp.where(causal, s_band[rows, :BLOCK], jnp.where(blk > 0, s_band[rows, BLOCK:], -jnp.inf))
        m = jnp.maximum(jnp.max(s, axis=-1, keepdims=True), sink)
        p = jnp.exp(s - m)
        ps = jnp.exp(sink - m)
        inv = 1.0 / (jnp.sum(p, axis=-1, keepdims=True) + ps)
        p = p * inv
        p_all.append(p)
        p_sink.append(ps * inv)
        p_band.append(_split_band(p, causal))
    cat = lambda parts: jnp.concatenate(parts, axis=0)
    return cat([c for c, _ in p_band]), cat([v for _, v in p_band]), cat(p_all), cat(p_sink)


def _split_band(x, causal):
    return jnp.where(causal, x, 0.0).astype(BF16), jnp.where(causal, 0.0, x).astype(BF16)


def _attn_dims(s, d, kv):
    n_kv = kv // HEAD_DIM
    group = d // kv
    qw = 2 * group * HEAD_DIM
    assert n_kv % 2 == 0 and (d + 2 * kv) % qw == 0 and s % BLOCK == 0
    return group, qw, n_kv // 2, s // BLOCK


def _attention_fwd(proj, tables, sink, kv, after):
    s, d = proj.shape[0], sink.shape[0] * HEAD_DIM
    group, qw, n_pairs, nb = _attn_dims(s, d, kv)

    def body(sink_ref, q_ref, kc_ref, kp_ref, vc_ref, vp_ref, ga_ref, tc_ref, tp_ref, after_ref,
             attn_ref, ain_ref, qrot_ref, krot_ref):
        pair, blk = pl.program_id(0), pl.program_id(1)
        tab_c, tab_p = tc_ref[...], tp_ref[...]
        q = _rope(q_ref[...].astype(F32), tab_c) * ATTN_SCALE
        k_cur, k_prev = _rope(kc_ref[...].astype(F32), tab_c), _rope(kp_ref[...].astype(F32), tab_p)
        qrot_ref[...] = q.astype(BF16)
        krot_ref[...] = k_cur.astype(BF16)
        v_cur, v_prev = vc_ref[...], vp_ref[...]
        causal = _causal(BLOCK)
        outs = []
        for a in range(2):
            k_band = jnp.concatenate([_head(k_cur, a), _head(k_prev, a)], axis=0).astype(BF16)
            vc, vp = _head(v_cur, a).astype(BF16), _head(v_prev, a).astype(BF16)
            qs = _stack_heads(q, a * group, group).astype(BF16)
            p_cur, p_prev, _, _ = _band_probs_by_head(qs, k_band, sink_ref, (2 * pair + a) * group, group, causal, blk)
            o = jnp.dot(p_cur, vc, preferred_element_type=F32) + jnp.dot(p_prev, vp, preferred_element_type=F32)
            outs += [o[h * BLOCK:(h + 1) * BLOCK] for h in range(group)]
        attn = jnp.concatenate(outs, axis=1)
        attn_ref[...] = attn.astype(BF16)
        silu, _ = _silu_and_grad(ga_ref[...].astype(F32))
        ain_ref[...] = (attn * silu).astype(BF16)

    k0, v0, g0 = d // LANES, (d + kv) // LANES, (d + 2 * kv) // qw
    prev = lambda i: jnp.maximum(i - 1, 0)
    in_specs = [
        SMEM_SPEC,
        pl.BlockSpec((BLOCK, qw), lambda p, i: (i, p)),
        pl.BlockSpec((BLOCK, LANES), lambda p, i: (i, k0 + p)),
        pl.BlockSpec((BLOCK, LANES), lambda p, i: (prev(i), k0 + p)),
        pl.BlockSpec((BLOCK, LANES), lambda p, i: (i, v0 + p)),
        pl.BlockSpec((BLOCK, LANES), lambda p, i: (prev(i), v0 + p)),
        pl.BlockSpec((BLOCK, qw), lambda p, i: (i, g0 + p)),
        pl.BlockSpec((3, BLOCK, LANES), lambda p, i: (0, i, 0)),
        pl.BlockSpec((3, BLOCK, LANES), lambda p, i: (0, prev(i), 0)),
        ANY_SPEC,
    ]
    out = pl.BlockSpec((BLOCK, qw), lambda p, i: (i, p))
    k_out = pl.BlockSpec((BLOCK, LANES), lambda p, i: (i, p))
    return pl.pallas_call(
        body, name="attention_fwd", grid=(n_pairs, nb), in_specs=in_specs, out_specs=(out, out, out, k_out),
        out_shape=(_sds((s, d), BF16), _sds((s, d), BF16), _sds((s, d), BF16), _sds((s, kv), BF16)),
        compiler_params=_params("parallel", "parallel"),
    )(sink, proj, proj, proj, proj, proj, proj, tables, tables, after)


def _attention_bwd(proj, q_rot, k_rot, tables, sink, kv, attn, d_ain, after, d_proj):
    s, d = proj.shape[0], sink.shape[0] * HEAD_DIM
    group, qw, n_pairs, nb = _attn_dims(s, d, kv)

    def body(sink_ref, q_ref, kc_ref, kp_ref, vc_ref, vp_ref, ga_ref, tc_ref, tp_ref, attn_ref, dain_ref, after_ref,
             wide_in, dq_ref, dk_ref, dv_ref, dga_ref, dsink_ref, carry_k, carry_v, sink_acc):
        pair, blk = pl.program_id(0), pl.program_id(1)

        @pl.when(blk == 0)
        def _():
            carry_k[...] = jnp.zeros_like(carry_k)
            carry_v[...] = jnp.zeros_like(carry_v)
            sink_acc[...] = jnp.zeros_like(sink_acc)

        @pl.when(blk < nb)
        def _():
            tab_c, tab_p = tc_ref[...], tp_ref[...]
            q = q_ref[...].astype(F32)
            k_cur, k_prev = kc_ref[...].astype(F32), kp_ref[...].astype(F32)
            v_cur, v_prev = vc_ref[...], vp_ref[...]
            silu, silu_grad = _silu_and_grad(ga_ref[...].astype(F32))
            d_ain_v = dain_ref[...].astype(F32)
            dga_ref[...] = (d_ain_v * attn_ref[...].astype(F32) * silu_grad).astype(BF16)
            d_attn = d_ain_v * silu
            q_t = q.T
            d_attn_t = d_attn.T
            causal = _causal(BLOCK)
            dq_parts = []
            dk_t = {"cur": [], "prev": []}
            dv_t = {"cur": [], "prev": []}
            lane = lax.broadcasted_iota(jnp.int32, (BLOCK, LANES), 1)
            dsink = jnp.zeros((BLOCK, LANES), F32)
            for a in range(2):
                first = a * group
                qs = _stack_heads(q, first, group).astype(BF16)
                kc, kp = _head(k_cur, a).astype(BF16), _head(k_prev, a).astype(BF16)
                k_band = jnp.concatenate([_head(k_cur, a), _head(k_prev, a)], axis=0).astype(BF16)
                v_band = jnp.concatenate([_head(v_cur, a), _head(v_prev, a)], axis=0).astype(BF16)
                p_cur, p_prev, p, p_sink = _band_probs_by_head(qs, k_band, sink_ref, (2 * pair + a) * group, group,
                                                               causal, blk)
                do = _stack_heads(d_attn, first, group).astype(BF16)
                dp_band = _dot_nt(do, v_band)
                ds_parts, delta = [], []
                for h in range(group):
                    rows = slice(h * BLOCK, (h + 1) * BLOCK)
                    dp = jnp.where(causal, dp_band[rows, :BLOCK], dp_band[rows, BLOCK:])
                    delta.append(jnp.sum(p[rows] * dp, axis=-1, keepdims=True))
                    ds_parts.append(_split_band(p[rows] * (dp - delta[-1]), causal))
                ds_cur = jnp.concatenate([c for c, _ in ds_parts], axis=0)
                ds_prev = jnp.concatenate([v for _, v in ds_parts], axis=0)
                delta = jnp.concatenate(delta, axis=0)
                dqs = (jnp.dot(ds_cur, kc, preferred_element_type=F32)
                       + jnp.dot(ds_prev, kp, preferred_element_type=F32)) * ATTN_SCALE
                dq_parts += [dqs[h * BLOCK:(h + 1) * BLOCK] for h in range(group)]
                rows = lambda t: jnp.concatenate(
                    [t[(first + h) * HEAD_DIM:(first + h + 1) * HEAD_DIM] for h in range(group)], axis=1).astype(BF16)
                qs_t, do_t = rows(q_t), rows(d_attn_t)
                dk_t["cur"].append(jnp.dot(qs_t, ds_cur, preferred_element_type=F32))
                dk_t["prev"].append(jnp.dot(qs_t, ds_prev, preferred_element_type=F32))
                dv_t["cur"].append(jnp.dot(do_t, p_cur, preferred_element_type=F32))
                dv_t["prev"].append(jnp.dot(do_t, p_prev, preferred_element_type=F32))
                ds_sink = -(p_sink * delta)
                for h in range(group):
                    dsink = dsink + jnp.where(lane == first + h, ds_sink[h * BLOCK:(h + 1) * BLOCK], 0.0)
            sink_acc[...] += dsink
            dq_ref[...] = _rope_bwd(jnp.concatenate(dq_parts, axis=1), tab_c).astype(BF16)
            pair_block = lambda parts: jnp.concatenate(parts, axis=0).T
            dk_ref[...] = (carry_k[...] + _rope_bwd(pair_block(dk_t["prev"]), tab_p)).astype(BF16)
            dv_ref[...] = (carry_v[...] + pair_block(dv_t["prev"])).astype(BF16)
            carry_k[...] = _rope_bwd(pair_block(dk_t["cur"]), tab_c)
            carry_v[...] = pair_block(dv_t["cur"])

        @pl.when(blk == nb)
        def _():
            dk_ref[...] = carry_k[...].astype(BF16)
            dv_ref[...] = carry_v[...].astype(BF16)
            dsink_ref[0] = jnp.sum(sink_acc[...], axis=0, keepdims=True)

    v0, g0 = (d + kv) // LANES, (d + 2 * kv) // qw
    cur = lambda i: jnp.minimum(i, nb - 1)
    prev = lambda i: jnp.maximum(cur(i) - 1, 0)
    back = lambda i: jnp.maximum(i - 1, 0)
    q_spec = pl.BlockSpec((BLOCK, qw), lambda p, i: (cur(i), p))
    in_specs = [
        SMEM_SPEC,
        q_spec,
        pl.BlockSpec((BLOCK, LANES), lambda p, i: (cur(i), p)),
        pl.BlockSpec((BLOCK, LANES), lambda p, i: (prev(i), p)),
        pl.BlockSpec((BLOCK, LANES), lambda p, i: (cur(i), v0 + p)),
        pl.BlockSpec((BLOCK, LANES), lambda p, i: (prev(i), v0 + p)),
        pl.BlockSpec((BLOCK, qw), lambda p, i: (cur(i), g0 + p)),
        pl.BlockSpec((3, BLOCK, LANES), lambda p, i: (0, cur(i), 0)),
        pl.BlockSpec((3, BLOCK, LANES), lambda p, i: (0, prev(i), 0)),
        q_spec,
        q_spec,
        ANY_SPEC,
        ANY_SPEC,
    ]
    kv_out = pl.BlockSpec((BLOCK, LANES), lambda p, i: (back(i), p))
    gate_out = pl.BlockSpec((BLOCK, qw), lambda p, i: (cur(i), g0 + p))
    return pl.pallas_call(
        body, name="attention_bwd", grid=(n_pairs, nb + 1), in_specs=in_specs,
        out_specs=(q_spec, kv_out, kv_out, gate_out, pl.BlockSpec((1, 1, LANES), lambda p, i: (p, 0, 0))),
        out_shape=(_sds((s, d), BF16), _sds((s, kv), BF16), _sds((s, kv), BF16), _sds(d_proj.shape, BF16),
                   _sds((n_pairs, 1, LANES), F32)),
        input_output_aliases={len(in_specs) - 1: 3},
        scratch_shapes=[pltpu.VMEM((BLOCK, LANES), F32), pltpu.VMEM((BLOCK, LANES), F32),
                        pltpu.VMEM((BLOCK, LANES), F32)],
        compiler_params=_params("parallel", "arbitrary"),
    )(sink, q_rot, k_rot, k_rot, proj, proj, proj, tables, tables, attn, d_ain, after, d_proj)


def _gmlp_core(gu, gv, ln_g, ln_b, w_ref, bias_t):
    xc = gv - jnp.mean(gv, axis=-1, keepdims=True)
    rstd = lax.rsqrt(jnp.mean(xc * xc, axis=-1, keepdims=True) + LN_EPS)
    xhat = xc * rstd
    vn = (xhat * ln_g + ln_b).astype(BF16)
    gd = gu.shape[1] // GMLP_GROUPS
    tri = (lax.broadcasted_iota(jnp.int32, (BLOCK, BLOCK), 0) >= lax.broadcasted_iota(jnp.int32, (BLOCK, BLOCK), 1))
    w_tri = [jnp.where(tri, w_ref[g], 0.0).astype(BF16) for g in range(GMLP_GROUPS)]
    mixed = jnp.concatenate(
        [jnp.dot(w_tri[g], vn[:, g * gd:(g + 1) * gd], preferred_element_type=F32) + bias_t[:, g:g + 1]
         for g in range(GMLP_GROUPS)], axis=1)
    return gu, xhat, rstd, vn, w_tri, tri, mixed


def _whole(shape):
    return pl.BlockSpec(shape, lambda i: tuple(0 for _ in shape))


def _gmlp_fwd(proj, col_u, d, w_s, bias_t, ln_g, ln_b, after):
    s = proj.shape[0]
    ns = SEGMENT_SPLIT

    def body(*refs):
        u, vg, gb = _cat(refs[:ns]), _cat(refs[ns:2 * ns]), _cat(refs[2 * ns:3 * ns])
        w_ref, bt_ref, lg_ref, lb_ref, after_ref, o_ref = refs[3 * ns:]
        gu, _, _, _, _, _, mixed = _gmlp_core(_gelu(u), _gelu(vg), lg_ref[...], lb_ref[...], w_ref, bt_ref[...])
        silu, _ = _silu_and_grad(gb)
        o_ref[...] = ((gu * mixed) * silu).astype(BF16)

    segs = [sp for j in range(3) for sp in _segment_specs(BLOCK, d, col_u + j * d)]
    return pl.pallas_call(
        body, name="gmlp_fwd", grid=(s // BLOCK,),
        in_specs=[*segs, _whole(w_s.shape), _whole(bias_t.shape), _whole((1, d)), _whole((1, d)), ANY_SPEC],
        out_specs=pl.BlockSpec((BLOCK, d), lambda i: (i, 0)), out_shape=_sds((s, d), BF16),
        compiler_params=_params("parallel"),
    )(*([proj] * (3 * ns)), w_s, bias_t, ln_g, ln_b, after)


def _gmlp_bwd(proj, col_u, d, w_s, bias_t, ln_g, ln_b, d_bin, after, d_proj, pieces):
    s = proj.shape[0]
    gd = d // GMLP_GROUPS
    ns = SEGMENT_SPLIT
    n_steps = s // BLOCK
    n_side = len(pieces)

    def body(*refs):
        u, vg, gb = _cat(refs[:ns]), _cat(refs[ns:2 * ns]), _cat(refs[2 * ns:3 * ns])
        w_ref, bt_ref, lg_ref, lb_ref, dbin_ref, after_ref, wide_in = refs[3 * ns:3 * ns + 7]
        side_refs = refs[3 * ns + 7:3 * ns + 7 + n_side]
        wide_ref, dw_ref, dbt_ref, dlg_ref, dlb_ref, buf, sems, side_sems = refs[3 * ns + 7 + n_side:]

        @pl.when(pl.program_id(0) == 0)
        def _():
            dw_ref[...] = jnp.zeros_like(dw_ref)
            dbt_ref[...] = jnp.zeros_like(dbt_ref)
            dlg_ref[...] = jnp.zeros_like(dlg_ref)
            dlb_ref[...] = jnp.zeros_like(dlb_ref)

        ln_g = lg_ref[...]
        (gu, gu_grad), (gv, gv_grad) = _gelu_and_grad(u), _gelu_and_grad(vg)
        gu, xhat, rstd, vn, w_tri, tri, mixed = _gmlp_core(gu, gv, ln_g, lb_ref[...], w_ref, bt_ref[...])
        silu, silu_grad = _silu_and_grad(gb)
        d_bin_v = dbin_ref[...].astype(F32)
        d_sg = d_bin_v * silu
        d_gate = (d_bin_v * (gu * mixed) * silu_grad).astype(BF16)
        d_u = (d_sg * mixed * gu_grad).astype(BF16)
        d_mixed = d_sg * gu
        d_mixed_b = d_mixed.astype(BF16)
        d_vn, d_bias = [], []
        for g in range(GMLP_GROUPS):
            dm_g = d_mixed_b[:, g * gd:(g + 1) * gd]
            d_bias.append(jnp.sum(d_mixed[:, g * gd:(g + 1) * gd], axis=-1, keepdims=True))
            dw = lax.dot_general(dm_g, vn[:, g * gd:(g + 1) * gd], (((1,), (1,)), ((), ())),
                                 preferred_element_type=F32)
            dw_ref[g] += jnp.where(tri, dw, 0.0)
            d_vn.append(lax.dot_general(w_tri[g], dm_g, (((0,), (0,)), ((), ())), preferred_element_type=F32))
        dbt_ref[...] += jnp.concatenate(d_bias, axis=1)
        d_vn = jnp.concatenate(d_vn, axis=1)
        dlg_ref[...] += jnp.sum(d_vn * xhat, axis=0, keepdims=True)
        dlb_ref[...] += jnp.sum(d_vn, axis=0, keepdims=True)
        d_xhat = d_vn * ln_g
        d_gv = rstd * (d_xhat - jnp.mean(d_xhat, axis=-1, keepdims=True)
                       - xhat * jnp.mean(d_xhat * xhat, axis=-1, keepdims=True))
        d_v = (d_gv * gv_grad).astype(BF16)

        def fill(out):
            out[:, :d] = d_u
            out[:, d:2 * d] = d_v
            out[:, 2 * d:] = d_gate

        _write_behind(pl.program_id(0), n_steps, buf, sems, wide_ref, BLOCK, col_u, fill,
                      side=[(r, col) for r, (_, col) in zip(side_refs, pieces)], side_sems=side_sems)

    segs = [sp for j in range(3) for sp in _segment_specs(BLOCK, d, col_u + j * d)]
    return pl.pallas_call(
        body, name="gmlp_bwd", grid=(n_steps,),
        in_specs=[*segs, _whole(w_s.shape), _whole(bias_t.shape), _whole((1, d)), _whole((1, d)),
                  pl.BlockSpec((BLOCK, d), lambda i: (i, 0)), ANY_SPEC, ANY_SPEC, *([ANY_SPEC] * n_side)],
        out_specs=(ANY_SPEC, _whole(w_s.shape), _whole(bias_t.shape), _whole((1, d)), _whole((1, d))),
        out_shape=(_sds(d_proj.shape, BF16), _sds(w_s.shape, F32), _sds(bias_t.shape, F32), _sds((1, d), F32),
                   _sds((1, d), F32)),
        input_output_aliases={3 * ns + 6: 0},
        scratch_shapes=[pltpu.VMEM((2, BLOCK, 3 * d), BF16), pltpu.SemaphoreType.DMA((2,)),
                        pltpu.SemaphoreType.DMA((2, max(n_side, 1)))],
        compiler_params=_params("arbitrary"),
    )(*([proj] * (3 * ns)), w_s, bias_t, ln_g, ln_b, d_bin, after, d_proj, *[p for p, _ in pieces])


def _pack(parts):
    rows = []
    tile = SUBLANES * LANES
    for p in parts:
        flat = p.astype(F32).reshape(-1)
        padded = -(-flat.shape[0] // tile) * tile
        rows.append(jnp.pad(flat, (0, padded - flat.shape[0])).reshape(-1, LANES))
    return jnp.concatenate(rows, axis=0)


def _unpack(packed, shapes):
    out, row = [], 0
    tile = SUBLANES * LANES
    for shape in shapes:
        size = math.prod(shape)
        n_rows = -(-size // tile) * SUBLANES
        out.append(packed[row:row + n_rows].reshape(-1)[:size].reshape(shape))
        row += n_rows
    return out


def kernel(x, positions, norm_g, w_in, attn_sink, gmlp_ln_g, gmlp_ln_b, w_spatial, b_spatial, w_up_attn, w_up_gmlp, w_out, final_norm_g, loss_target, m_norm_g, m_w_in, m_attn_sink, m_gmlp_ln_g, m_gmlp_ln_b, m_w_spatial, m_b_spatial, m_w_up_attn, m_w_up_gmlp, m_w_out, m_final_norm_g, v_norm_g, v_w_in, v_attn_sink, v_gmlp_ln_g, v_gmlp_ln_b, v_w_spatial, v_b_spatial, v_w_up_attn, v_w_up_gmlp, v_w_out, v_final_norm_g):
    x2d, target = x[0], loss_target[0]
    s, d = x2d.shape
    n_q_heads = attn_sink.shape[1]
    cw = w_in.shape[2]
    rw = w_up_attn.shape[1]
    kv = (cw * N_DEV - 7 * d) // 2
    col_u, col_m = 2 * d + 2 * kv, 5 * d + 2 * kv
    final_g = final_norm_g.reshape(1, d)
    sink = attn_sink[0]
    w_s = w_spatial[0]
    bias_t = b_spatial[0].T
    mx, my, mc = _mesh_pos()
    pos = jnp.stack([mx, my, mc]).astype(jnp.int32)

    def one_block(fn):
        return jnp.reshape(fn(mx, my, mc), (1,)).astype(jnp.int32)

    w_in_b = _cast_into_slot(w_in[0], pos, "cast_w_in")
    to_sibling = [_Copy(0, _slot, 0, _slot, _sibling)]
    ici = [[_Copy(0, _slot, 0, _slot, lambda x, y, c, chip=chip: (*chip(x, y, c), c))] for chip in _ICI_STAGES[:2]]
    relayed = lambda x, y, c: _slot(*_ICI_STAGES[1](x, y, c), c)
    ici.append([_Copy(0, relayed, 0, relayed, lambda x, y, c: (*_ICI_STAGES[0](x, y, c), c))])
    passes = []
    for chip in _ICI_STAGES:
        landed = lambda x, y, c, chip=chip: _slot(*chip(x, y, c), c)
        passes.append([_Copy(0, landed, 0, landed, _sibling)])
    sib_sems = _rdma_start("w_in_sibling_start", [w_in_b], to_sibling)
    ici_sems = _rdma_start("w_in_ici0_start", sib_sems[2], ici[0])
    h = _rmsnorm_fwd(x2d, norm_g, ici_sems[3])
    proj = _project(h, ici_sems[2][0], one_block(_slot), "projection_own")
    w_blocks = _rdma_wait("w_in_sibling_wait", ici_sems[2], sib_sems[0], sib_sems[1], to_sibling, proj)
    proj = _project(h, w_blocks[0], one_block(lambda x, y, c: _slot(x, y, 1 - c)), "projection_sibling", proj=proj)
    w_blocks = _rdma_wait("w_in_ici0_wait", w_blocks, ici_sems[0], ici_sems[1], ici[0], proj)
    first = _gather_first_copies(3)
    for k, chip in enumerate(_ICI_STAGES):
        if k + 1 < len(_ICI_STAGES):
            ici_sems = _rdma_start("w_in_ici%d_start" % (k + 1), w_blocks, ici[k + 1])
            w_blocks = ici_sems[2]
        else:
            send1, recv1, thru, _ = _rdma_start("gather_squares_start", squares + w_blocks, first)
            squares, w_blocks = thru[:3], thru[3:]
        pass_sems = _rdma_start("w_in_pass%d_start" % k, w_blocks, passes[k])
        if k == 0:
            squares = [_cast_into_slot(w[0], pos, "cast_" + nm, after=pass_sems[3])
                       for nm, w in (("w_up_attn", w_up_attn), ("w_up_gmlp", w_up_gmlp), ("w_out", w_out))]
        proj = _project(h, pass_sems[2][0], one_block(lambda x, y, c, chip=chip: _slot(*chip(x, y, c), c)),
                        "projection_ici%d" % k, proj=proj, after=pass_sems[3])
        w_blocks = _rdma_wait("w_in_pass%d_wait" % k, pass_sems[2], pass_sems[0], pass_sems[1], passes[k], proj)
        proj = _project(h, w_blocks[0], one_block(lambda x, y, c, chip=chip: _slot(*chip(x, y, 1 - c), 1 - c)),
                        "projection_pass%d" % k, proj=proj)
        if k + 1 < len(_ICI_STAGES):
            w_blocks = _rdma_wait("w_in_ici%d_wait" % (k + 1), w_blocks, ici_sems[0], ici_sems[1], ici[k + 1], proj)
    w_in_b = w_blocks[0]

    tables = _rope_tables(positions[0])
    attn, a_in, q_rot, k_rot = _attention_fwd(proj, tables, sink, kv, proj)
    squares = _rdma_wait("gather_squares_wait", squares, send1, recv1, first, attn)
    passed = _gather_pass_copies(3)
    send2, recv2, squares, token = _rdma_start("pass_squares_start", squares, passed)
    b_in = _gmlp_fwd(proj, col_u, d, w_s, bias_t, gmlp_ln_g, gmlp_ln_b, token)
    squares = _rdma_wait("pass_squares_wait", squares, send2, recv2, passed, b_in)
    w_ua, w_ug, w_o = [w.reshape(N_DEV * rw, d) for w in squares]
    y_a = _matmul(a_in, w_ua, "nn", BF16, "up_attn")
    y_b = _matmul(b_in, w_ug, "nn", BF16, "up_gmlp")
    merged = _merge_fwd(y_a, y_b, proj, col_m)
    x_out = _matmul(merged, w_o, "nn", F32, "out_proj", res=x2d, tn=512)
    loss_p, d_final_g, dx2_b = _loss_and_final_norm_bwd(x_out, target, final_g)

    d_merged = _matmul(dx2_b, w_o, "nt", BF16, "d_merged")
    g_w_out = _matmul(merged, dx2_b, "tn", BF16, "g_w_out")
    d_ya, d_yb, d_proj = _merge_bwd(d_merged, y_a, y_b, proj, col_m, lax.empty(proj.shape, BF16))
    d_ain = _matmul(d_ya, w_ua, "nt", BF16, "d_a_in")
    g_w_ua = _matmul(a_in, d_ya, "tn", BF16, "g_w_up_attn")
    d_bin = _matmul(d_yb, w_ug, "nt", BF16, "d_b_in")
    g_w_ug = _matmul(b_in, d_yb, "tn", BF16, "g_w_up_gmlp")
    sq_grads = [g.reshape(N_DEV, rw, d) for g in (g_w_ua, g_w_ug, g_w_out)]
    sq_land = [lax.empty((N_CHIPS, rw, d), BF16) for _ in sq_grads]
    pairs_sq = _pair_copies_strided(3)
    arrays = [a for gl in zip(sq_grads, sq_land) for a in gl]
    send3, recv3, arrays, token = _rdma_start("pair_squares_start", arrays, pairs_sq)
    d_q, d_k, d_v, d_proj, d_sink = _attention_bwd(proj, q_rot, k_rot, tables, sink, kv, attn, d_ain, token, d_proj)
    arrays = _rdma_wait("pair_squares_wait", arrays, send3, recv3, pairs_sq, d_q)
    sq_sums = [_pair_sum(arrays[2 * a], arrays[2 * a + 1], pos, "pair_sum_%d" % a) for a in range(3)]
    sq_land2 = [lax.empty((N_CHIPS - 1, rw, d), BF16) for _ in sq_sums]
    chip_sq = _chip_sum_copies(3)
    arrays = [a for gl in zip(sq_sums, sq_land2) for a in gl]
    send4, recv4, sq_arrays, token = _rdma_start("chip_squares_start", arrays, chip_sq)
    d_proj, d_w_s, d_bias_t, d_ln_g, d_ln_b = _gmlp_bwd(proj, col_u, d, w_s, bias_t, gmlp_ln_g, gmlp_ln_b, d_bin, token,
                                                        d_proj, [(d_q, 0), (d_k, d), (d_v, d + kv)])

    half = N_CHIPS // 2
    owners = [*(chip(mx, my) for chip in _OTHER_CHIPS), (mx, my)]

    def blocks_of(j, core):
        return jnp.stack([_slot(*owners[q], core) for q in range(j * half, (j + 1) * half)]).astype(jnp.int32)

    pairs_in = [_Copy(0, lambda x, y, c, q=q: q, 1, lambda x, y, c, q=q: q, _sibling) for q in range(half)]
    sent, token = [], None
    for j in range(2):
        g_sib = _grad_w_in_blocks(h, d_proj, blocks_of(j, 1 - mc), cw, "g_w_in_sibling%d" % j, after=token)
        sent.append(_rdma_start("pair_w_in%d_start" % j, [g_sib, lax.empty((half, d, cw), BF16)], pairs_in))
        token = sent[-1][3]
    in_sums, land_in = None, lax.empty((N_CHIPS - 1, d, cw), BF16)
    chip_in = [[_Copy(0, lambda x, y, c, k=k: k, 1, lambda x, y, c, k=k: k,
                      lambda x, y, c, k=k: (*_OTHER_CHIPS[k](x, y), c)) for k in ks] for ks in ((0, 1), (2,))]
    chip_sent = []
    for j in range(2):
        send5, recv5, arrays, _ = sent[j]
        arrays = _rdma_wait("pair_w_in%d_wait" % j, arrays, send5, recv5, pairs_in, token)
        in_sums = _grad_w_in_blocks(h, d_proj, blocks_of(j, mc), cw, "g_w_in_own%d" % j,
                                    slots=N_CHIPS, slot0=j * half, prev=in_sums, init=arrays[1], tm=512)
        chip_sent.append(_rdma_start("chip_w_in%d_start" % j, [in_sums, land_in], chip_in[j]))
        (in_sums, land_in), token = chip_sent[-1][2], chip_sent[-1][3]
    d_h = _d_hidden(d_proj, w_in_b, after=token)
    grad_x, d_norm_g = _input_grad(d_h, x2d, norm_g, dx2_b)

    sq_arrays = _rdma_wait("chip_squares_wait", sq_arrays, send4, recv4, chip_sq, grad_x)
    big = {}
    for a, (name, w, m, v) in enumerate((("w_up_attn", w_up_attn, m_w_up_attn, v_w_up_attn),
                                         ("w_up_gmlp", w_up_gmlp, m_w_up_gmlp, v_w_up_gmlp),
                                         ("w_out", w_out, m_w_out, v_w_out))):
        big[name] = [r[None] for r in _reduce_adamw(sq_arrays[2 * a], sq_arrays[2 * a + 1], w[0], m[0], v[0], pos,
                                                    "adamw_" + name)]

    heads_per_pair = 2 * n_q_heads // (kv // HEAD_DIM)
    g_sink = d_sink[:, 0, :heads_per_pair].reshape(1, n_q_heads)
    small_w = [norm_g, attn_sink, gmlp_ln_g, gmlp_ln_b, w_spatial, b_spatial, final_norm_g]
    small_m = [m_norm_g, m_attn_sink, m_gmlp_ln_g, m_gmlp_ln_b, m_w_spatial, m_b_spatial, m_final_norm_g]
    small_v = [v_norm_g, v_attn_sink, v_gmlp_ln_g, v_gmlp_ln_b, v_w_spatial, v_b_spatial, v_final_norm_g]
    small_g = [d_norm_g, g_sink, d_ln_g, d_ln_b, d_w_s[None], d_bias_t.T[None], d_final_g.reshape(d)]
    loss_pad = jnp.zeros((1,), F32)
    shapes = [w.shape for w in small_w] + [(1,)]
    packed = _small_allreduce_adamw(_pack(small_g + [loss_p[0, :1]]), _pack(small_w + [loss_pad]),
                                    _pack(small_m + [loss_pad]), _pack(small_v + [loss_pad]))
    sg, sd, sm, sv = [_unpack(p, shapes) for p in packed]
    loss = sg[-1][0]
    in_arrays = [in_sums, land_in]
    for j in range(2):
        in_arrays = _rdma_wait("chip_w_in%d_wait" % j, in_arrays, chip_sent[j][0], chip_sent[j][1], chip_in[j],
                               packed[0])
    big["w_in"] = [r[None] for r in _reduce_adamw(in_arrays[0], in_arrays[1], w_in[0], m_w_in[0], v_w_in[0], pos,
                                                  "adamw_w_in", own_slot=N_CHIPS - 1)]

    names = ["norm_g", "w_in", "attn_sink", "gmlp_ln_g", "gmlp_ln_b", "w_spatial", "b_spatial", "w_up_attn",
             "w_up_gmlp", "w_out", "final_norm_g"]
    small_names = ["norm_g", "attn_sink", "gmlp_ln_g", "gmlp_ln_b", "w_spatial", "b_spatial", "final_norm_g"]
    outs = [[], [], [], []]
    for nm in names:
        for k in range(4):
            if nm in big:
                outs[k].append(big[nm][k])
            else:
                outs[k].append((sg, sd, sm, sv)[k][small_names.index(nm)])
    return (loss, grad_x[None], *outs[0], *outs[1], *outs[2], *outs[3])
```
